```python
import jax, jax.numpy as jnp
from jax import lax
import numpy as np

D_MODEL = 1024
BATCH = 8
SEQ = 8192
DEPTH = 1
DEC_BATCH = 32
DEC_SEQ = 64
PAST_LEN = 1024

CHUNK = 64
N_HEADS = 8
N_KV_HEADS = 2
HEAD_DIM = 64
Q_GROUP = N_HEADS // N_KV_HEADS
ATTN_WIDTH = N_HEADS * HEAD_DIM
KV_WIDTH = N_KV_HEADS * HEAD_DIM
WINDOW = 128
WINDOW_CHUNKS = WINDOW // CHUNK
GMLP_CHUNK = 128
GMLP_GROUPS = 4
GMLP_GROUP_DIM = 128
GMLP_WIDTH = GMLP_GROUPS * GMLP_GROUP_DIM
MIX_WIDTH = ATTN_WIDTH + GMLP_WIDTH
IN_WIDTH = ATTN_WIDTH + 2 * KV_WIDTH + 2 * GMLP_WIDTH
SPLITS = [ATTN_WIDTH, ATTN_WIDTH + KV_WIDTH, ATTN_WIDTH + 2 * KV_WIDTH,
          ATTN_WIDTH + 2 * KV_WIDTH + GMLP_WIDTH]
N_EXPERTS = 32
TOP_K = 4
D_FF = 1024
SWIGLU_LIMIT = 7.0
SWIGLU_ALPHA = 1.702
ROUTE_BLOCK = 256
DN_ALPHA = (2 * DEPTH) ** 0.25
DN_BETA = (8 * DEPTH) ** -0.25
LN_EPS = 1e-5
NEG_INF = -1e30

kernel_name = "hybrid_swa_sink_gmlp_moe_stream_step"


def layer_norm(x, g, b):
    x32 = x.astype(jnp.float32)
    mu = jnp.mean(x32, axis=-1, keepdims=True)
    var = jnp.mean(jnp.square(x32 - mu), axis=-1, keepdims=True)
    return ((x32 - mu) * lax.rsqrt(var + LN_EPS) * g + b).astype(x.dtype)


def sink_attention(q, k, v, sinks, valid=None):
    scores = jnp.einsum('...qhgd,...shd->...hgqs', q, k).astype(jnp.float32) * (HEAD_DIM ** -0.5)
    if valid is not None:
        scores = jnp.where(valid[..., None, None, None, :], scores, NEG_INF)
    sink = jnp.broadcast_to(sinks.reshape(N_KV_HEADS, Q_GROUP)[:, :, None, None].astype(jnp.float32),
                            scores.shape[:-1] + (1,))
    probs = jax.nn.softmax(jnp.concatenate([scores, sink], axis=-1), axis=-1)[..., :-1]
    return jnp.einsum('...hgqs,...shd->...qhgd', probs.astype(v.dtype), v)


def attn_prompt(q, k, v, sinks):
    b, s = q.shape[:2]
    nc = s // CHUNK
    qc = q.reshape(b, nc, CHUNK, N_KV_HEADS, Q_GROUP, HEAD_DIM)
    pad = ((0, 0), (WINDOW_CHUNKS * CHUNK, 0), (0, 0), (0, 0))
    kp = jnp.pad(k, pad).reshape(b, nc + WINDOW_CHUNKS, CHUNK, N_KV_HEADS, HEAD_DIM)
    vp = jnp.pad(v, pad).reshape(b, nc + WINDOW_CHUNKS, CHUNK, N_KV_HEADS, HEAD_DIM)
    band = lambda t: jnp.concatenate([t[:, j:j + nc] for j in range(WINDOW_CHUNKS + 1)], axis=2)
    kb, vb = band(kp), band(vp)
    key_chunk = jnp.arange(nc)[:, None] + jnp.arange(WINDOW_CHUNKS + 1)[None, :] - WINDOW_CHUNKS
    valid = jnp.repeat(key_chunk >= 0, CHUNK, axis=1)[None]
    out = sink_attention(qc, kb, vb, sinks, valid)
    return out.reshape(b, s, ATTN_WIDTH)


def masked_spatial(w_s):
    pos = jnp.arange(GMLP_CHUNK)
    mask = (pos[None, :] // CHUNK) <= (pos[:, None] // CHUNK)
    return jnp.where(mask[None], w_s, jnp.zeros_like(w_s))


def project_in(x, w_in, b_in, ln_g, ln_b):
    lead = x.shape[:-1]
    z = x @ w_in + b_in
    q, k, v, u, gv = jnp.split(z, SPLITS, axis=-1)
    q = q.reshape(lead + (N_KV_HEADS, Q_GROUP, HEAD_DIM))
    k = k.reshape(lead + (N_KV_HEADS, HEAD_DIM))
    v = v.reshape(lead + (N_KV_HEADS, HEAD_DIM))
    u = jax.nn.gelu(u).reshape(lead + (GMLP_GROUPS, GMLP_GROUP_DIM))
    gv = layer_norm(jax.nn.gelu(gv).reshape(lead + (GMLP_GROUPS, GMLP_GROUP_DIM)), ln_g, ln_b)
    return q, k, v, u, gv


def mix_prompt(x, w_in, b_in, sinks, ln_g, ln_b, w_s, b_s):
    b, s, _ = x.shape
    q, k, v, u, gv = project_in(x, w_in, b_in, ln_g, ln_b)
    att = attn_prompt(q, k, v, sinks)
    nc = s // GMLP_CHUNK
    shp = (b, nc, GMLP_CHUNK, GMLP_GROUPS, GMLP_GROUP_DIM)
    sp = jnp.einsum('gij,bnjgc->bnigc', masked_spatial(w_s), gv.reshape(shp)) + b_s.T[:, :, None]
    gm = (u.reshape(shp) * sp).reshape(b, s, GMLP_WIDTH)
    keep = min(WINDOW, s)
    return jnp.concatenate([att, gm], axis=-1), k[:, s - keep:], v[:, s - keep:]


def mix_sample(x, cache_k, cache_v, w_in, b_in, sinks, ln_g, ln_b, w_s, b_s):
    b, t, _ = x.shape
    q, k, v, u, gv = project_in(x, w_in, b_in, ln_g, ln_b)
    kk = jnp.concatenate([cache_k, k], axis=1)
    vv = jnp.concatenate([cache_v, v], axis=1)
    att = sink_attention(q, kk, vv, sinks).reshape(b, t, ATTN_WIDTH)
    ws = masked_spatial(w_s)[:, :t, :t]
    sp = jnp.einsum('gij,bjgc->bigc', ws, gv) + b_s[:, :t].T[:, :, None]
    gm = (u * sp).reshape(b, t, GMLP_WIDTH)
    return jnp.concatenate([att, gm], axis=-1), k, v, gv


def moe(x2, w_router, b_router, w_gu, b_gu, w_down, b_down):
    t = x2.shape[0]
    n_assign = t * TOP_K
    n_blocks = -(-n_assign // ROUTE_BLOCK) + N_EXPERTS
    n_rows = n_blocks * ROUTE_BLOCK
    logits = (x2 @ w_router + b_router).astype(jnp.float32)
    top_logit, top_e = lax.top_k(logits, TOP_K)
    gates = jax.nn.softmax(top_logit, axis=-1).reshape(-1)
    flat_e = top_e.reshape(-1).astype(jnp.int32)
    flat_tok = jnp.arange(n_assign, dtype=jnp.int32) // TOP_K
    order = jnp.argsort(flat_e).astype(jnp.int32)
    sorted_e = flat_e[order]
    counts = jnp.bincount(flat_e, length=N_EXPERTS)
    padded = (counts + ROUTE_BLOCK - 1) // ROUTE_BLOCK * ROUTE_BLOCK
    start = jnp.cumsum(counts) - counts
    pend = jnp.cumsum(padded)
    pstart = pend - padded
    dest = pstart[sorted_e] + jnp.arange(n_assign, dtype=jnp.int32) - start[sorted_e]
    row_assign = jnp.full((n_rows,), n_assign, jnp.int32).at[dest].set(order)
    valid = row_assign < n_assign
    safe = jnp.minimum(row_assign, n_assign - 1)
    row_tok = flat_tok[safe]
    row_gate = jnp.where(valid, gates[safe], 0.0)
    block_start = jnp.arange(n_blocks) * ROUTE_BLOCK
    block_e = jnp.minimum(jnp.sum(pend[None, :] <= block_start[:, None], axis=1), N_EXPERTS - 1)
    xb = x2[row_tok].reshape(n_blocks, ROUTE_BLOCK, x2.shape[-1])

    def expert_block(args):
        xblk, e = args
        h = xblk @ w_gu[e] + b_gu[e]
        gate = jnp.minimum(h[:, :D_FF], SWIGLU_LIMIT)
        up = jnp.clip(h[:, D_FF:], -SWIGLU_LIMIT, SWIGLU_LIMIT)
        act = (up + 1.0) * gate * jax.nn.sigmoid(SWIGLU_ALPHA * gate)
        return act @ w_down[e] + b_down[e]

    yb = lax.map(expert_block, (xb, block_e)).reshape(n_rows, -1)
    return jax.ops.segment_sum(yb * row_gate[:, None].astype(yb.dtype), row_tok, num_segments=t)


def post_mix(x, mix, w_o, b_o, ln1_g, ln1_b, w_router, b_router, w_gu, b_gu, w_down, b_down, ln2_g, ln2_b):
    h = layer_norm(DN_ALPHA * x + (mix @ w_o + b_o), ln1_g, ln1_b)
    f = moe(h.reshape(-1, D_MODEL), w_router, b_router, w_gu, b_gu, w_down, b_down).reshape(h.shape)
    return layer_norm(DN_ALPHA * h + f, ln2_g, ln2_b)


def setup_inputs(seed: int = 0) -> dict:
    key = jax.random.key(seed)
    ks = jax.random.split(key, 24)
    n = lambda i, shape, scale: jax.random.normal(ks[i], shape, jnp.float32) * scale
    cache_rows = min(WINDOW, PAST_LEN)
    return {
        "x_prompt": n(0, (BATCH, SEQ, D_MODEL), 1.0),
        "x_sample": n(1, (DEC_BATCH, DEC_SEQ, D_MODEL), 1.0),
        "cache_k": n(2, (DEPTH, DEC_BATCH, cache_rows, N_KV_HEADS, HEAD_DIM), 1.0),
        "cache_v": n(3, (DEPTH, DEC_BATCH, cache_rows, N_KV_HEADS, HEAD_DIM), 1.0),
        "w_in": n(4, (DEPTH, D_MODEL, IN_WIDTH), D_MODEL ** -0.5),
        "b_in": n(5, (DEPTH, IN_WIDTH), 0.01),
        "attn_sinks": n(6, (DEPTH, N_HEADS), 1.0),
        "gmlp_ln_g": 1.0 + n(7, (DEPTH, GMLP_GROUPS, GMLP_GROUP_DIM), 0.01),
        "gmlp_ln_b": n(8, (DEPTH, GMLP_GROUPS, GMLP_GROUP_DIM), 0.01),
        "w_spatial": n(9, (DEPTH, GMLP_GROUPS, GMLP_CHUNK, GMLP_CHUNK), GMLP_CHUNK ** -0.5),
        "b_spatial": 1.0 + n(10, (DEPTH, GMLP_GROUPS, GMLP_CHUNK), 0.01),
        "w_o": n(11, (DEPTH, MIX_WIDTH, D_MODEL), MIX_WIDTH ** -0.5 * DN_BETA),
        "b_o": n(12, (DEPTH, D_MODEL), 0.01),
        "ln1_g": 1.0 + n(13, (DEPTH, D_MODEL), 0.01),
        "ln1_b": n(14, (DEPTH, D_MODEL), 0.01),
        "w_router": n(15, (DEPTH, D_MODEL, N_EXPERTS), D_MODEL ** -0.5),
        "b_router": n(16, (DEPTH, N_EXPERTS), 0.01),
        "w_gate_up": n(17, (DEPTH, N_EXPERTS, D_MODEL, 2 * D_FF), D_MODEL ** -0.5),
        "b_gate_up": n(18, (DEPTH, N_EXPERTS, 2 * D_FF), 0.01),
        "w_down": n(19, (DEPTH, N_EXPERTS, D_FF, D_MODEL), D_FF ** -0.5 * DN_BETA),
        "b_down": n(20, (DEPTH, N_EXPERTS, D_MODEL), 0.01),
        "ln2_g": 1.0 + n(21, (DEPTH, D_MODEL), 0.01),
        "ln2_b": n(22, (DEPTH, D_MODEL), 0.01),
    }


def reference(x_prompt, x_sample, cache_k, cache_v, w_in, b_in, attn_sinks, gmlp_ln_g, gmlp_ln_b,
              w_spatial, b_spatial, w_o, b_o, ln1_g, ln1_b, w_router, b_router, w_gate_up, b_gate_up,
              w_down, b_down, ln2_g, ln2_b):
    xp, xs = x_prompt, x_sample
    kp_list, vp_list, ks_list, vs_list, gs_list = [], [], [], [], []
    for l in range(DEPTH):
        mix_p, kp, vp = mix_prompt(xp, w_in[l], b_in[l], attn_sinks[l], gmlp_ln_g[l], gmlp_ln_b[l],
                                   w_spatial[l], b_spatial[l])
        mix_s, kn, vn, gvn = mix_sample(xs, cache_k[l], cache_v[l], w_in[l], b_in[l], attn_sinks[l],
                                        gmlp_ln_g[l], gmlp_ln_b[l], w_spatial[l], b_spatial[l])
        ffn_args = (w_o[l], b_o[l], ln1_g[l], ln1_b[l], w_router[l], b_router[l], w_gate_up[l],
                    b_gate_up[l], w_down[l], b_down[l], ln2_g[l], ln2_b[l])
        xp = post_mix(xp, mix_p, *ffn_args)
        xs = post_mix(xs, mix_s, *ffn_args)
        kp_list.append(kp)
        vp_list.append(vp)
        ks_list.append(kn)
        vs_list.append(vn)
        gs_list.append(gvn)
    new_k_prompt = jnp.stack(kp_list)
    new_v_prompt = jnp.stack(vp_list)
    new_k_sample = jnp.stack(ks_list)
    new_v_sample = jnp.stack(vs_list)
    new_gmlp_v_sample = jnp.stack(gs_list)
    return (xp, xs, new_k_prompt, new_v_prompt, new_k_sample, new_v_sample, new_gmlp_v_sample)
```

```python
import functools

import jax
import jax.numpy as jnp
from jax import lax
from jax.experimental import pallas as pl
from jax.experimental.pallas import tpu as pltpu

F32 = jnp.float32
BF16 = jnp.bfloat16
I32 = jnp.int32

D_MODEL = 1024
CHUNK = 64
N_HEADS = 8
N_KV_HEADS = 2
HEAD_DIM = 64
Q_GROUP = N_HEADS // N_KV_HEADS
KV_WIDTH = N_KV_HEADS * HEAD_DIM
WINDOW = 128
KEYS = WINDOW + CHUNK
GMLP_GROUPS = 4
GMLP_GROUP_DIM = 128
GMLP_WIDTH = GMLP_GROUPS * GMLP_GROUP_DIM
GMLP_CHUNK = 128
ATTN_WIDTH = N_HEADS * HEAD_DIM
N_EXPERTS = 32
TOP_K = 4
D_FF = 1024
SWIGLU_LIMIT = 7.0
SWIGLU_ALPHA = 1.702
DEPTH = 1
DN_ALPHA = (2 * DEPTH) ** 0.25
LN_EPS = 1e-5
NEG_INF = -1e30

LANES = 128
QPAD_WIDTH = N_HEADS * LANES
K_OFF = QPAD_WIDTH
V_OFF = K_OFF + KV_WIDTH
U_OFF = V_OFF + KV_WIDTH
GV_OFF = U_OFF + GMLP_WIDTH
ZPAD_WIDTH = GV_OFF + GMLP_WIDTH

MIX_TILE = 512
ROW_TILE = 512
MOVE_TILE = 256
VMEM_LIMIT = 56 * 1024 * 1024

META_E, META_R, META_G = 0, 4, 8


def _gelu_tanh(x):
    return 0.5 * x * (1.0 + jnp.tanh(0.7978845608028654 * (x + 0.044715 * x * x * x)))


def _layer_norm(x, g, b):
    mu = jnp.mean(x, axis=-1, keepdims=True)
    xc = x - mu
    var = jnp.mean(xc * xc, axis=-1, keepdims=True)
    return xc * lax.rsqrt(var + LN_EPS) * g + b


def _dot(a, b):
    return jnp.dot(a, b, preferred_element_type=F32)


def _mixer_kernel(*refs, is_prompt, tt):
    n_chunks = tt // CHUNK
    it = iter(refs)
    sinks_ref = next(it)
    x_ref = next(it)
    if not is_prompt:
        ck_ref, cv_ref, base0_ref = next(it), next(it), next(it)
    (win_ref, bin_ref, lng_ref, lnb_ref, ws_ref, bs_ref, woa_ref, wog_ref, bo_ref, l1g_ref, l1b_ref,
     wrh_ref, wrl_ref, br_ref, tri_ref) = (next(it) for _ in range(15))
    h_ref, meta_ref, cnt_ref = next(it), next(it), next(it)
    if is_prompt:
        kt_ref, vt_ref = next(it), next(it)
    else:
        ko_ref, vo_ref, gvo_ref = next(it), next(it), next(it)
    z_ref, kext_ref, vext_ref, att_ref, gm_ref, base_ref = (next(it) for _ in range(6))

    if is_prompt:
        first = (pl.program_id(0) == 0) & (pl.program_id(1) == 0)
        tile_in_seq = pl.program_id(1)

        @pl.when(first)
        def _():
            base_ref[...] = jnp.zeros_like(base_ref)

        @pl.when(tile_in_seq == 0)
        def _():
            kext_ref[0:WINDOW, :] = jnp.zeros((WINDOW, KV_WIDTH), BF16)
            vext_ref[0:WINDOW, :] = jnp.zeros((WINDOW, KV_WIDTH), BF16)
    else:
        @pl.when(pl.program_id(0) == 0)
        def _():
            base_ref[...] = base0_ref[...]

    x = x_ref[...]
    z_ref[...] = _dot(x.astype(BF16), win_ref[...]) + bin_ref[...]

    k = z_ref[:, K_OFF:K_OFF + KV_WIDTH]
    v = z_ref[:, V_OFF:V_OFF + KV_WIDTH]
    if is_prompt:
        kext_ref[WINDOW:WINDOW + tt, :] = k.astype(BF16)
        vext_ref[WINDOW:WINDOW + tt, :] = v.astype(BF16)
        kt_ref[...] = k[tt - WINDOW:, :]
        vt_ref[...] = v[tt - WINDOW:, :]
        key_stride = CHUNK
    else:
        ko_ref[...] = k
        vo_ref[...] = v
        for c in range(n_chunks):
            kext_ref[KEYS * c:KEYS * c + WINDOW, :] = ck_ref[WINDOW * c:WINDOW * (c + 1), :].astype(BF16)
            vext_ref[KEYS * c:KEYS * c + WINDOW, :] = cv_ref[WINDOW * c:WINDOW * (c + 1), :].astype(BF16)
            kext_ref[KEYS * c + WINDOW:KEYS * (c + 1), :] = k[CHUNK * c:CHUNK * (c + 1), :].astype(BF16)
            vext_ref[KEYS * c + WINDOW:KEYS * (c + 1), :] = v[CHUNK * c:CHUNK * (c + 1), :].astype(BF16)
        key_stride = KEYS

    rows = Q_GROUP * CHUNK
    row_i = lax.broadcasted_iota(I32, (rows, 1), 0)
    key_i = lax.broadcasted_iota(I32, (rows, KEYS), 1)
    sink_cols = []
    for hk in range(N_KV_HEADS):
        s = [sinks_ref[hk * Q_GROUP + g] for g in range(Q_GROUP)]
        sink_cols.append(jnp.where(row_i < CHUNK, s[0],
                                   jnp.where(row_i < 2 * CHUNK, s[1],
                                             jnp.where(row_i < 3 * CHUNK, s[2], s[3]))))
    for c in range(n_chunks):
        kc = kext_ref[key_stride * c:key_stride * c + KEYS, :]
        vc = vext_ref[key_stride * c:key_stride * c + KEYS, :]
        outs = []
        for hk in range(N_KV_HEADS):
            qs = jnp.concatenate(
                [z_ref[CHUNK * c:CHUNK * (c + 1), (hk * Q_GROUP + g) * LANES:(hk * Q_GROUP + g + 1) * LANES]
                 for g in range(Q_GROUP)], axis=0).astype(BF16)
            sc = lax.dot_general(qs, kc, (((1,), (1,)), ((), ())), preferred_element_type=F32)
            sc = sc * (HEAD_DIM ** -0.5)
            if is_prompt and c < WINDOW // CHUNK:
                valid = (key_i >= WINDOW - CHUNK * c) | (tile_in_seq > 0)
                sc = jnp.where(valid, sc, NEG_INF)
            sink = sink_cols[hk]
            m = jnp.maximum(jnp.max(sc, axis=-1, keepdims=True), sink)
            p = jnp.exp(sc - m)
            den = jnp.sum(p, axis=-1, keepdims=True) + jnp.exp(sink - m)
            o = _dot(p.astype(BF16), vc) / den
            outs.extend(o[CHUNK * g:CHUNK * (g + 1), :] for g in range(Q_GROUP))
        att_ref[CHUNK * c:CHUNK * (c + 1), :] = jnp.concatenate(outs, axis=1).astype(BF16)

    if is_prompt:
        kext_ref[0:WINDOW, :] = kext_ref[tt:tt + WINDOW, :]
        vext_ref[0:WINDOW, :] = vext_ref[tt:tt + WINDOW, :]

    gc = GMLP_CHUNK if is_prompt else CHUNK
    ri = lax.broadcasted_iota(I32, (gc, gc), 0)
    ci = lax.broadcasted_iota(I32, (gc, gc), 1)
    causal = (ci // CHUNK) <= (ri // CHUNK)
    for g in range(GMLP_GROUPS):
        lo, hi = g * GMLP_GROUP_DIM, (g + 1) * GMLP_GROUP_DIM
        u = _gelu_tanh(z_ref[:, U_OFF + lo:U_OFF + hi])
        gv = _layer_norm(_gelu_tanh(z_ref[:, GV_OFF + lo:GV_OFF + hi]), lng_ref[:, lo:hi], lnb_ref[:, lo:hi])
        if not is_prompt:
            gvo_ref[:, lo:hi] = gv
        gvb = gv.astype(BF16)
        wm = jnp.where(causal, ws_ref[g, 0:gc, 0:gc], 0.0).astype(BF16)
        bcol = bs_ref[0:gc, g:g + 1]
        for n in range(tt // gc):
            sp = _dot(wm, gvb[gc * n:gc * (n + 1), :]) + bcol
            gm_ref[gc * n:gc * (n + 1), lo:hi] = (u[gc * n:gc * (n + 1), :] * sp).astype(BF16)

    y = _dot(att_ref[...], woa_ref[...]) + _dot(gm_ref[...], wog_ref[...]) + bo_ref[...]
    h = _layer_norm(DN_ALPHA * x + y, l1g_ref[...], l1b_ref[...])
    h_ref[...] = h

    h_hi = h.astype(BF16)
    h_lo = (h - h_hi.astype(F32)).astype(BF16)
    logits = _dot(h_hi, wrh_ref[...]) + _dot(h_lo, wrh_ref[...]) + _dot(h_hi, wrl_ref[...]) + br_ref[...]
    lane = lax.broadcasted_iota(I32, (tt, LANES), 1)
    lane_f = lane.astype(F32)
    l = jnp.where(lane < N_EXPERTS, logits, -jnp.inf)
    tops, idxs, hots = [], [], []
    for _ in range(TOP_K):
        m = jnp.max(l, axis=-1, keepdims=True)
        idx = jnp.min(jnp.where(l == m, lane_f, float(LANES)), axis=-1, keepdims=True)
        hot = lane_f == idx
        l = jnp.where(hot, -jnp.inf, l)
        tops.append(m)
        idxs.append(idx)
        hots.append(hot)
    es = [jnp.exp(t - tops[0]) for t in tops]
    esum = es[0] + es[1] + es[2] + es[3]
    chosen = jnp.where(hots[0] | hots[1] | hots[2] | hots[3], 1.0, 0.0)
    before = _dot(tri_ref[...], chosen.astype(BF16)) + base_ref[...]
    meta = jnp.zeros((tt, LANES), F32)
    for kk in range(TOP_K):
        rank = jnp.sum(jnp.where(hots[kk], before, 0.0), axis=-1, keepdims=True)
        meta = jnp.where(lane == META_E + kk, idxs[kk], meta)
        meta = jnp.where(lane == META_R + kk, rank, meta)
        meta = jnp.where(lane == META_G + kk, es[kk] / esum, meta)
    meta_ref[...] = meta
    base_ref[...] = base_ref[...] + jnp.sum(chosen, axis=0, keepdims=True)
    cnt_ref[...] = base_ref[...]


def _const_spec(shape):
    nd = len(shape)
    return pl.BlockSpec(shape, lambda *_: (0,) * nd)


def _mixer_weight_specs(tt):
    return [
        _const_spec((D_MODEL, ZPAD_WIDTH)), _const_spec((1, ZPAD_WIDTH)),
        _const_spec((1, GMLP_WIDTH)), _const_spec((1, GMLP_WIDTH)),
        _const_spec((GMLP_GROUPS, GMLP_CHUNK, GMLP_CHUNK)), _const_spec((GMLP_CHUNK, GMLP_GROUPS)),
        _const_spec((QPAD_WIDTH, D_MODEL)), _const_spec((GMLP_WIDTH, D_MODEL)), _const_spec((1, D_MODEL)),
        _const_spec((1, D_MODEL)), _const_spec((1, D_MODEL)),
        _const_spec((D_MODEL, LANES)), _const_spec((D_MODEL, LANES)), _const_spec((1, LANES)),
        _const_spec((tt, tt)),
    ]


def _mixer_scratch(tt, kext_rows):
    return [
        pltpu.VMEM((tt, ZPAD_WIDTH), F32),
        pltpu.VMEM((kext_rows, KV_WIDTH), BF16), pltpu.VMEM((kext_rows, KV_WIDTH), BF16),
        pltpu.VMEM((tt, QPAD_WIDTH), BF16), pltpu.VMEM((tt, GMLP_WIDTH), BF16),
        pltpu.VMEM((1, LANES), F32),
    ]


def _mix_prompt(sinks, x, weights):
    batch, seq, _ = x.shape
    tt = MIX_TILE
    n_tiles = seq // tt
    tok = batch * seq
    smem = pl.BlockSpec(memory_space=pltpu.SMEM)
    return pl.pallas_call(
        functools.partial(_mixer_kernel, is_prompt=True, tt=tt),
        grid=(batch, n_tiles),
        in_specs=[smem, pl.BlockSpec((None, tt, D_MODEL), lambda b, i: (b, i, 0))] + _mixer_weight_specs(tt),
        out_specs=[
            pl.BlockSpec((tt, D_MODEL), lambda b, i: (b * n_tiles + i, 0)),
            pl.BlockSpec((tt, LANES), lambda b, i: (b * n_tiles + i, 0)),
            pl.BlockSpec((1, LANES), lambda b, i: (0, 0)),
            pl.BlockSpec((WINDOW, KV_WIDTH), lambda b, i: (b, 0)),
            pl.BlockSpec((WINDOW, KV_WIDTH), lambda b, i: (b, 0)),
        ],
        out_shape=[
            jax.ShapeDtypeStruct((tok, D_MODEL), F32), jax.ShapeDtypeStruct((tok, LANES), F32),
            jax.ShapeDtypeStruct((1, LANES), F32),
            jax.ShapeDtypeStruct((batch * WINDOW, KV_WIDTH), F32),
            jax.ShapeDtypeStruct((batch * WINDOW, KV_WIDTH), F32),
        ],
        scratch_shapes=_mixer_scratch(tt, WINDOW + tt),
        compiler_params=pltpu.CompilerParams(
            dimension_semantics=("arbitrary", "arbitrary"), vmem_limit_bytes=VMEM_LIMIT),
        name="mix_prompt",
    )(sinks, x, *weights)


def _mix_sample(sinks, x2, ck, cv, base0, weights):
    tok = x2.shape[0]
    tt = MIX_TILE
    n_chunks = tt // CHUNK
    cache_rows = n_chunks * WINDOW
    smem = pl.BlockSpec(memory_space=pltpu.SMEM)
    row = lambda w: pl.BlockSpec((tt, w), lambda i: (i, 0))
    return pl.pallas_call(
        functools.partial(_mixer_kernel, is_prompt=False, tt=tt),
        grid=(tok // tt,),
        in_specs=[smem, row(D_MODEL),
                  pl.BlockSpec((cache_rows, KV_WIDTH), lambda i: (i, 0)),
                  pl.BlockSpec((cache_rows, KV_WIDTH), lambda i: (i, 0)),
                  _const_spec((1, LANES))] + _mixer_weight_specs(tt),
        out_specs=[row(D_MODEL), row(LANES), pl.BlockSpec((1, LANES), lambda i: (0, 0)),
                   row(KV_WIDTH), row(KV_WIDTH), row(GMLP_WIDTH)],
        out_shape=[
            jax.ShapeDtypeStruct((tok, D_MODEL), F32), jax.ShapeDtypeStruct((tok, LANES), F32),
            jax.ShapeDtypeStruct((1, LANES), F32),
            jax.ShapeDtypeStruct((tok, KV_WIDTH), F32), jax.ShapeDtypeStruct((tok, KV_WIDTH), F32),
            jax.ShapeDtypeStruct((tok, GMLP_WIDTH), F32),
        ],
        scratch_shapes=_mixer_scratch(tt, n_chunks * KEYS),
        compiler_params=pltpu.CompilerParams(
            dimension_semantics=("arbitrary",), vmem_limit_bytes=VMEM_LIMIT),
        name="mix_sample",
    )(sinks, x2, ck, cv, base0, *weights)


def _dispatch_kernel(*refs, td, aliased):
    if aliased:
        dest_ref, h_hbm, _, xs_hbm, sem = refs
    else:
        dest_ref, h_hbm, xs_hbm, sem = refs
    i = pl.program_id(0)
    base = i * td

    def wait_tile():
        for _ in range(TOP_K):
            pltpu.make_async_copy(h_hbm.at[pl.ds(0, td)], xs_hbm.at[pl.ds(0, td)], sem).wait()

    def body(t, carry):
        for kk in range(TOP_K):
            d = dest_ref[0, 0, kk * td + t]
            pltpu.make_async_copy(h_hbm.at[pl.ds(base + t, 1)], xs_hbm.at[pl.ds(d, 1)], sem).start()
        return carry

    lax.fori_loop(0, td, body, 0)

    @pl.when(i > 0)
    def _():
        wait_tile()

    @pl.when(i == pl.num_programs(0) - 1)
    def _():
        wait_tile()


def _dispatch(dest_tiles, h, xs_or_rows):
    td = MOVE_TILE
    n_tiles = h.shape[0] // td
    aliased = not isinstance(xs_or_rows, int)
    any_spec = pl.BlockSpec(memory_space=pl.ANY)
    in_specs = [pl.BlockSpec((1, 1, TOP_K * td), lambda i: (i, 0, 0), memory_space=pltpu.SMEM), any_spec]
    args = [dest_tiles, h]
    if aliased:
        in_specs.append(any_spec)
        args.append(xs_or_rows)
        n_rows = xs_or_rows.shape[0]
    else:
        n_rows = xs_or_rows
    return pl.pallas_call(
        functools.partial(_dispatch_kernel, td=td, aliased=aliased),
        grid=(n_tiles,),
        in_specs=in_specs,
        out_specs=any_spec,
        out_shape=jax.ShapeDtypeStruct((n_rows, D_MODEL), F32),
        scratch_shapes=[pltpu.SemaphoreType.DMA(())],
        input_output_aliases={2: 0} if aliased else {},
        compiler_params=pltpu.CompilerParams(dimension_semantics=("arbitrary",)),
        name="dispatch_update" if aliased else "dispatch",
    )(*args)


def _expert_kernel(te_ref, na_ref, x_ref, wgu_ref, bgu_ref, wd_ref, bd_ref, y_ref):
    del te_ref

    @pl.when(pl.program_id(0) < na_ref[0])
    def _():
        x = x_ref[...].astype(BF16)
        hmid = _dot(x, wgu_ref[0]) + bgu_ref[0]
        gate = jnp.minimum(hmid[:, :D_FF], SWIGLU_LIMIT)
        up = jnp.clip(hmid[:, D_FF:], -SWIGLU_LIMIT, SWIGLU_LIMIT)
        act = (up + 1.0) * gate * jax.nn.sigmoid(SWIGLU_ALPHA * gate)
        y_ref[...] = _dot(act.astype(BF16), wd_ref[0]) + bd_ref[0]


def _experts(tile_e, n_active, xs, wgu, bgu, wd, bd):
    tm = ROW_TILE
    n_tiles = xs.shape[0] // tm
    row_map = lambda i, te, na: (jnp.minimum(i, na[0] - 1), 0)
    w_map = lambda i, te, na: (te[i], 0, 0)
    return pl.pallas_call(
        _expert_kernel,
        grid_spec=pltpu.PrefetchScalarGridSpec(
            num_scalar_prefetch=2,
            grid=(n_tiles,),
            in_specs=[
                pl.BlockSpec((tm, D_MODEL), row_map),
                pl.BlockSpec((1, D_MODEL, 2 * D_FF), w_map),
                pl.BlockSpec((1, 1, 2 * D_FF), w_map),
                pl.BlockSpec((1, D_FF, D_MODEL), w_map),
                pl.BlockSpec((1, 1, D_MODEL), w_map),
            ],
            out_specs=pl.BlockSpec((tm, D_MODEL), row_map),
        ),
        out_shape=jax.ShapeDtypeStruct(xs.shape, F32),
        compiler_params=pltpu.CompilerParams(
            dimension_semantics=("arbitrary",), vmem_limit_bytes=VMEM_LIMIT),
        name="experts",
    )(tile_e, n_active, xs, wgu, bgu, wd, bd)


def _combine_kernel(dcur_ref, dnxt_ref, h_ref, meta_ref, g_ref, b_ref, ys_hbm, out_ref, buf, sem, *, tc):
    i = pl.program_id(0)
    n = pl.num_programs(0)
    slot = i % 2

    def issue(dref, s):
        def body(t, carry):
            for kk in range(TOP_K):
                d = dref[0, 0, kk * tc + t]
                pltpu.make_async_copy(ys_hbm.at[pl.ds(d, 1)], buf.at[s, kk, pl.ds(t, 1)], sem.at[s]).start()
            return carry
        lax.fori_loop(0, tc, body, 0)

    @pl.when(i == 0)
    def _():
        issue(dcur_ref, 0)

    @pl.when(i + 1 < n)
    def _():
        issue(dnxt_ref, 1 - slot)

    for kk in range(TOP_K):
        pltpu.make_async_copy(ys_hbm.at[pl.ds(0, tc)], buf.at[slot, kk], sem.at[slot]).wait()

    meta = meta_ref[...]
    f = meta[:, META_G:META_G + 1] * buf[slot, 0]
    for kk in range(1, TOP_K):
        f = f + meta[:, META_G + kk:META_G + kk + 1] * buf[slot, kk]
    out_ref[...] = _layer_norm(DN_ALPHA * h_ref[...] + f, g_ref[...], b_ref[...])


def _combine(dest_tiles, h, meta, ln_g, ln_b, ys):
    tc = MOVE_TILE
    tok = h.shape[0]
    n_tiles = tok // tc
    dspec = lambda f: pl.BlockSpec((1, 1, TOP_K * tc), f, memory_space=pltpu.SMEM)
    return pl.pallas_call(
        functools.partial(_combine_kernel, tc=tc),
        grid=(n_tiles,),
        in_specs=[
            dspec(lambda i: (i, 0, 0)),
            dspec(lambda i: (jnp.minimum(i + 1, n_tiles - 1), 0, 0)),
            pl.BlockSpec((tc, D_MODEL), lambda i: (i, 0)),
            pl.BlockSpec((tc, LANES), lambda i: (i, 0)),
            _const_spec((1, D_MODEL)), _const_spec((1, D_MODEL)),
            pl.BlockSpec(memory_space=pl.ANY),
        ],
        out_specs=pl.BlockSpec((tc, D_MODEL), lambda i: (i, 0)),
        out_shape=jax.ShapeDtypeStruct((tok, D_MODEL), F32),
        scratch_shapes=[pltpu.VMEM((2, TOP_K, tc, D_MODEL), F32), pltpu.SemaphoreType.DMA((2,))],
        compiler_params=pltpu.CompilerParams(
            dimension_semantics=("arbitrary",), vmem_limit_bytes=VMEM_LIMIT),
        name="combine",
    )(dest_tiles, dest_tiles, h, meta, ln_g, ln_b, ys)


def _pad_heads_last(w):
    lead = w.shape[:-1]
    w = w.reshape(lead + (N_KV_HEADS, Q_GROUP, HEAD_DIM))
    zero = jnp.zeros_like(w[..., 0, :, :])
    parts = []
    for hk in range(N_KV_HEADS):
        cols = [zero] * N_KV_HEADS
        cols[hk] = w[..., hk, :, :]
        parts.append(jnp.concatenate(cols, axis=-1))
    return jnp.stack(parts, axis=-3).reshape(lead + (QPAD_WIDTH,))


def _mixer_weights(w_in, b_in, ln_g, ln_b, w_s, b_s, w_o, b_o, ln1_g, ln1_b, w_router, b_router):
    win = jnp.concatenate([_pad_heads_last(w_in[:, :ATTN_WIDTH]), w_in[:, ATTN_WIDTH:]], axis=1).astype(BF16)
    bin_ = jnp.concatenate([_pad_heads_last(b_in[:ATTN_WIDTH]), b_in[ATTN_WIDTH:]])[None, :]
    woa = _pad_heads_last(w_o[:ATTN_WIDTH].T).T.astype(BF16)
    wog = w_o[ATTN_WIDTH:].astype(BF16)
    wr = jnp.pad(w_router, ((0, 0), (0, LANES - N_EXPERTS)))
    wrh = wr.astype(BF16)
    wrl = (wr - wrh.astype(F32)).astype(BF16)
    br = jnp.pad(b_router, (0, LANES - N_EXPERTS))[None, :]
    tri = (lax.broadcasted_iota(I32, (MIX_TILE, MIX_TILE), 1)
           < lax.broadcasted_iota(I32, (MIX_TILE, MIX_TILE), 0)).astype(BF16)
    return (win, bin_, ln_g.reshape(1, GMLP_WIDTH), ln_b.reshape(1, GMLP_WIDTH), w_s, b_s.T,
            woa, wog, b_o[None, :], ln1_g[None, :], ln1_b[None, :], wrh, wrl, br, tri)


def _dest_tiles(meta, pstart, tile):
    e = meta[:, META_E:META_E + TOP_K].astype(I32)
    r = meta[:, META_R:META_R + TOP_K].astype(I32)
    dest = pstart[e] + r
    n_tiles = meta.shape[0] // tile
    return dest.reshape(n_tiles, tile, TOP_K).transpose(0, 2, 1).reshape(n_tiles, 1, TOP_K * tile)


def kernel(x_prompt, x_sample, cache_k, cache_v, w_in, b_in, attn_sinks, gmlp_ln_g, gmlp_ln_b, w_spatial, b_spatial, w_o, b_o, ln1_g, ln1_b, w_router, b_router, w_gate_up, b_gate_up, w_down, b_down, ln2_g, ln2_b):
    assert w_in.shape[0] == DEPTH
    batch, seq, _ = x_prompt.shape
    dec_batch, dec_seq, _ = x_sample.shape
    assert dec_seq == CHUNK and seq % MIX_TILE == 0 and (dec_batch * dec_seq) % MIX_TILE == 0
    tok_p, tok_s = batch * seq, dec_batch * dec_seq

    weights = _mixer_weights(w_in[0], b_in[0], gmlp_ln_g[0], gmlp_ln_b[0], w_spatial[0], b_spatial[0],
                             w_o[0], b_o[0], ln1_g[0], ln1_b[0], w_router[0], b_router[0])
    sinks = attn_sinks[0]
    h_p, meta_p, cnt_p, kt, vt = _mix_prompt(sinks, x_prompt, weights)
    h_s, meta_s, cnt, ks, vs, gvs = _mix_sample(
        sinks, x_sample.reshape(tok_s, D_MODEL),
        cache_k[0].reshape(dec_batch * WINDOW, KV_WIDTH), cache_v[0].reshape(dec_batch * WINDOW, KV_WIDTH),
        cnt_p, weights)

    n_assign = (tok_p + tok_s) * TOP_K
    n_tiles = (n_assign + N_EXPERTS * (ROW_TILE - 1)) // ROW_TILE
    counts = cnt[0, :N_EXPERTS].astype(I32)
    padded = (counts + ROW_TILE - 1) // ROW_TILE * ROW_TILE
    pend = jnp.cumsum(padded)
    pstart = pend - padded
    n_active = (pend[-1:] // ROW_TILE).astype(I32)
    tile_start = jnp.arange(n_tiles, dtype=I32) * ROW_TILE
    tile_e = jnp.minimum(jnp.sum(pend[None, :] <= tile_start[:, None], axis=1), N_EXPERTS - 1).astype(I32)

    dest_p = _dest_tiles(meta_p, pstart, MOVE_TILE)
    dest_s = _dest_tiles(meta_s, pstart, MOVE_TILE)
    xs = _dispatch(dest_p, h_p, n_tiles * ROW_TILE)
    xs = _dispatch(dest_s, h_s, xs)

    ys = _experts(tile_e, n_active, xs, w_gate_up[0].astype(BF16), b_gate_up[0][:, None, :],
                  w_down[0].astype(BF16), b_down[0][:, None, :])

    g2, b2 = ln2_g[0][None, :], ln2_b[0][None, :]
    y_p = _combine(dest_p, h_p, meta_p, g2, b2, ys).reshape(batch, seq, D_MODEL)
    y_s = _combine(dest_s, h_s, meta_s, g2, b2, ys).reshape(dec_batch, dec_seq, D_MODEL)

    kv5 = lambda a, nb, rows: a.reshape(DEPTH, nb, rows, N_KV_HEADS, HEAD_DIM)
    return (y_p, y_s, kv5(kt, batch, WINDOW), kv5(vt, batch, WINDOW),
            kv5(ks, dec_batch, dec_seq), kv5(vs, dec_batch, dec_seq),
            gvs.reshape(DEPTH, dec_batch, dec_seq, GMLP_GROUPS, GMLP_GROUP_DIM))
```

```python
import functools

import jax
import jax.numpy as jnp
from jax import lax
from jax.experimental import pallas as pl
from jax.experimental.pallas import tpu as pltpu

F32 = jnp.float32
BF16 = jnp.bfloat16
I32 = jnp.int32

D_MODEL = 1024
CHUNK = 64
N_HEADS = 8
N_KV_HEADS = 2
HEAD_DIM = 64
Q_GROUP = N_HEADS // N_KV_HEADS
KV_WIDTH = N_KV_HEADS * HEAD_DIM
WINDOW = 128
KEYS = WINDOW + CHUNK
GMLP_GROUPS = 4
GMLP_GROUP_DIM = 128
GMLP_WIDTH = GMLP_GROUPS * GMLP_GROUP_DIM
GMLP_CHUNK = 128
ATTN_WIDTH = N_HEADS * HEAD_DIM
N_EXPERTS = 32
TOP_K = 4
D_FF = 1024
SWIGLU_LIMIT = 7.0
SWIGLU_ALPHA = 1.702
DEPTH = 1
DN_ALPHA = (2 * DEPTH) ** 0.25
LN_EPS = 1e-5
NEG_INF = -1e30

LANES = 128
QPAD_WIDTH = N_HEADS * LANES
K_OFF = QPAD_WIDTH
V_OFF = K_OFF + KV_WIDTH
U_OFF = V_OFF + KV_WIDTH
GV_OFF = U_OFF + GMLP_WIDTH
ZPAD_WIDTH = GV_OFF + GMLP_WIDTH

MIX_TILE = 512
ROW_TILE = 512
DISPATCH_TILE = 1024
COMBINE_TILE = 256
VMEM_LIMIT = 56 * 1024 * 1024

META_E, META_R, META_G = 0, 4, 8


def _gelu_tanh(x):
    return 0.5 * x * (1.0 + jnp.tanh(0.7978845608028654 * (x + 0.044715 * x * x * x)))


def _layer_norm(x, g, b):
    mu = jnp.mean(x, axis=-1, keepdims=True)
    xc = x - mu
    var = jnp.mean(xc * xc, axis=-1, keepdims=True)
    return xc * lax.rsqrt(var + LN_EPS) * g + b


def _dot(a, b):
    return jnp.dot(a, b, preferred_element_type=F32)


def _mixer_kernel(*refs, is_prompt, tt):
    n_chunks = tt // CHUNK
    it = iter(refs)
    sinks_ref = next(it)
    x_ref = next(it)
    if not is_prompt:
        ck_ref, cv_ref, base0_ref = next(it), next(it), next(it)
    (win_ref, bin_ref, lng_ref, lnb_ref, ws_ref, bs_ref, woa_ref, wog_ref, bo_ref, l1g_ref, l1b_ref,
     wrh_ref, wrl_ref, br_ref, tri_ref) = (next(it) for _ in range(15))
    h_ref, meta_ref, cnt_ref = next(it), next(it), next(it)
    if is_prompt:
        kt_ref, vt_ref = next(it), next(it)
    else:
        ko_ref, vo_ref, gvo_ref = next(it), next(it), next(it)
    z_ref, kext_ref, vext_ref, att_ref, gm_ref, base_ref = (next(it) for _ in range(6))

    if is_prompt:
        first = (pl.program_id(0) == 0) & (pl.program_id(1) == 0)
        tile_in_seq = pl.program_id(1)

        @pl.when(first)
        def _():
            base_ref[...] = jnp.zeros_like(base_ref)

        @pl.when(tile_in_seq == 0)
        def _():
            kext_ref[0:WINDOW, :] = jnp.zeros((WINDOW, KV_WIDTH), BF16)
            vext_ref[0:WINDOW, :] = jnp.zeros((WINDOW, KV_WIDTH), BF16)
    else:
        @pl.when(pl.program_id(0) == 0)
        def _():
            base_ref[...] = base0_ref[...]

    x = x_ref[...]
    z_ref[...] = _dot(x.astype(BF16), win_ref[...]) + bin_ref[...]

    k = z_ref[:, K_OFF:K_OFF + KV_WIDTH]
    v = z_ref[:, V_OFF:V_OFF + KV_WIDTH]
    if is_prompt:
        kext_ref[WINDOW:WINDOW + tt, :] = k.astype(BF16)
        vext_ref[WINDOW:WINDOW + tt, :] = v.astype(BF16)
        kt_ref[...] = k[tt - WINDOW:, :]
        vt_ref[...] = v[tt - WINDOW:, :]
        key_stride = CHUNK
    else:
        ko_ref[...] = k
        vo_ref[...] = v
        for c in range(n_chunks):
            kext_ref[KEYS * c:KEYS * c + WINDOW, :] = ck_ref[WINDOW * c:WINDOW * (c + 1), :].astype(BF16)
            vext_ref[KEYS * c:KEYS * c + WINDOW, :] = cv_ref[WINDOW * c:WINDOW * (c + 1), :].astype(BF16)
            kext_ref[KEYS * c + WINDOW:KEYS * (c + 1), :] = k[CHUNK * c:CHUNK * (c + 1), :].astype(BF16)
            vext_ref[KEYS * c + WINDOW:KEYS * (c + 1), :] = v[CHUNK * c:CHUNK * (c + 1), :].astype(BF16)
        key_stride = KEYS

    rows = Q_GROUP * CHUNK
    row_i = lax.broadcasted_iota(I32, (rows, 1), 0)
    key_i = lax.broadcasted_iota(I32, (rows, KEYS), 1)
    sink_cols = []
    for hk in range(N_KV_HEADS):
        s = [sinks_ref[hk * Q_GROUP + g] for g in range(Q_GROUP)]
        sink_cols.append(jnp.where(row_i < CHUNK, s[0],
                                   jnp.where(row_i < 2 * CHUNK, s[1],
                                             jnp.where(row_i < 3 * CHUNK, s[2], s[3]))))
    for c in range(n_chunks):
        kc = kext_ref[key_stride * c:key_stride * c + KEYS, :]
        vc = vext_ref[key_stride * c:key_stride * c + KEYS, :]
        outs = []
        for hk in range(N_KV_HEADS):
            qs = jnp.concatenate(
                [z_ref[CHUNK * c:CHUNK * (c + 1), (hk * Q_GROUP + g) * LANES:(hk * Q_GROUP + g + 1) * LANES]
                 for g in range(Q_GROUP)], axis=0).astype(BF16)
            sc = lax.dot_general(qs, kc, (((1,), (1,)), ((), ())), preferred_element_type=F32)
            sc = sc * (HEAD_DIM ** -0.5)
            if is_prompt and c < WINDOW // CHUNK:
                valid = (key_i >= WINDOW - CHUNK * c) | (tile_in_seq > 0)
                sc = jnp.where(valid, sc, NEG_INF)
            sink = sink_cols[hk]
            m = jnp.maximum(jnp.max(sc, axis=-1, keepdims=True), sink)
            p = jnp.exp(sc - m)
            den = jnp.sum(p, axis=-1, keepdims=True) + jnp.exp(sink - m)
            o = _dot(p.astype(BF16), vc) / den
            outs.extend(o[CHUNK * g:CHUNK * (g + 1), :] for g in range(Q_GROUP))
        att_ref[CHUNK * c:CHUNK * (c + 1), :] = jnp.concatenate(outs, axis=1).astype(BF16)

    if is_prompt:
        kext_ref[0:WINDOW, :] = kext_ref[tt:tt + WINDOW, :]
        vext_ref[0:WINDOW, :] = vext_ref[tt:tt + WINDOW, :]

    gc = GMLP_CHUNK if is_prompt else CHUNK
    ri = lax.broadcasted_iota(I32, (gc, gc), 0)
    ci = lax.broadcasted_iota(I32, (gc, gc), 1)
    causal = (ci // CHUNK) <= (ri // CHUNK)
    for g in range(GMLP_GROUPS):
        lo, hi = g * GMLP_GROUP_DIM, (g + 1) * GMLP_GROUP_DIM
        u = _gelu_tanh(z_ref[:, U_OFF + lo:U_OFF + hi])
        gv = _layer_norm(_gelu_tanh(z_ref[:, GV_OFF + lo:GV_OFF + hi]), lng_ref[:, lo:hi], lnb_ref[:, lo:hi])
        if not is_prompt:
            gvo_ref[:, lo:hi] = gv
        gvb = gv.astype(BF16)
        wm = jnp.where(causal, ws_ref[g, 0:gc, 0:gc], 0.0).astype(BF16)
        bcol = bs_ref[0:gc, g:g + 1]
        for n in range(tt // gc):
            sp = _dot(wm, gvb[gc * n:gc * (n + 1), :]) + bcol
            gm_ref[gc * n:gc * (n + 1), lo:hi] = (u[gc * n:gc * (n + 1), :] * sp).astype(BF16)

    y = _dot(att_ref[...], woa_ref[...]) + _dot(gm_ref[...], wog_ref[...]) + bo_ref[...]
    h = _layer_norm(DN_ALPHA * x + y, l1g_ref[...], l1b_ref[...])
    h_ref[...] = h

    h_hi = h.astype(BF16)
    h_lo = (h - h_hi.astype(F32)).astype(BF16)
    logits = _dot(h_hi, wrh_ref[...]) + _dot(h_lo, wrh_ref[...]) + _dot(h_hi, wrl_ref[...]) + br_ref[...]
    lane = lax.broadcasted_iota(I32, (tt, LANES), 1)
    lane_f = lane.astype(F32)
    l = jnp.where(lane < N_EXPERTS, logits, -jnp.inf)
    tops, idxs, hots = [], [], []
    for _ in range(TOP_K):
        m = jnp.max(l, axis=-1, keepdims=True)
        idx = jnp.min(jnp.where(l == m, lane_f, float(LANES)), axis=-1, keepdims=True)
        hot = lane_f == idx
        l = jnp.where(hot, -jnp.inf, l)
        tops.append(m)
        idxs.append(idx)
        hots.append(hot)
    es = [jnp.exp(t - tops[0]) for t in tops]
    esum = es[0] + es[1] + es[2] + es[3]
    chosen = jnp.where(hots[0] | hots[1] | hots[2] | hots[3], 1.0, 0.0)
    before = _dot(tri_ref[...], chosen.astype(BF16)) + base_ref[...]
    meta = jnp.zeros((tt, LANES), F32)
    for kk in range(TOP_K):
        rank = jnp.sum(jnp.where(hots[kk], before, 0.0), axis=-1, keepdims=True)
        meta = jnp.where(lane == META_E + kk, idxs[kk], meta)
        meta = jnp.where(lane == META_R + kk, rank, meta)
        meta = jnp.where(lane == META_G + kk, es[kk] / esum, meta)
    meta_ref[...] = meta
    base_ref[...] = base_ref[...] + jnp.sum(chosen, axis=0, keepdims=True)
    cnt_ref[...] = base_ref[...]


def _const_spec(shape):
    nd = len(shape)
    return pl.BlockSpec(shape, lambda *_: (0,) * nd)


def _mixer_weight_specs(tt):
    return [
        _const_spec((D_MODEL, ZPAD_WIDTH)), _const_spec((1, ZPAD_WIDTH)),
        _const_spec((1, GMLP_WIDTH)), _const_spec((1, GMLP_WIDTH)),
        _const_spec((GMLP_GROUPS, GMLP_CHUNK, GMLP_CHUNK)), _const_spec((GMLP_CHUNK, GMLP_GROUPS)),
        _const_spec((QPAD_WIDTH, D_MODEL)), _const_spec((GMLP_WIDTH, D_MODEL)), _const_spec((1, D_MODEL)),
        _const_spec((1, D_MODEL)), _const_spec((1, D_MODEL)),
        _const_spec((D_MODEL, LANES)), _const_spec((D_MODEL, LANES)), _const_spec((1, LANES)),
        _const_spec((tt, tt)),
    ]


def _mixer_scratch(tt, kext_rows):
    return [
        pltpu.VMEM((tt, ZPAD_WIDTH), F32),
        pltpu.VMEM((kext_rows, KV_WIDTH), BF16), pltpu.VMEM((kext_rows, KV_WIDTH), BF16),
        pltpu.VMEM((tt, QPAD_WIDTH), BF16), pltpu.VMEM((tt, GMLP_WIDTH), BF16),
        pltpu.VMEM((1, LANES), F32),
    ]


def _mix_prompt(sinks, x, weights):
    batch, seq, _ = x.shape
    tt = MIX_TILE
    n_tiles = seq // tt
    tok = batch * seq
    smem = pl.BlockSpec(memory_space=pltpu.SMEM)
    return pl.pallas_call(
        functools.partial(_mixer_kernel, is_prompt=True, tt=tt),
        grid=(batch, n_tiles),
        in_specs=[smem, pl.BlockSpec((None, tt, D_MODEL), lambda b, i: (b, i, 0))] + _mixer_weight_specs(tt),
        out_specs=[
            pl.BlockSpec((tt, D_MODEL), lambda b, i: (b * n_tiles + i, 0)),
            pl.BlockSpec((tt, LANES), lambda b, i: (b * n_tiles + i, 0)),
            pl.BlockSpec((1, LANES), lambda b, i: (0, 0)),
            pl.BlockSpec((WINDOW, KV_WIDTH), lambda b, i: (b, 0)),
            pl.BlockSpec((WINDOW, KV_WIDTH), lambda b, i: (b, 0)),
        ],
        out_shape=[
            jax.ShapeDtypeStruct((tok, D_MODEL), F32), jax.ShapeDtypeStruct((tok, LANES), F32),
            jax.ShapeDtypeStruct((1, LANES), F32),
            jax.ShapeDtypeStruct((batch * WINDOW, KV_WIDTH), F32),
            jax.ShapeDtypeStruct((batch * WINDOW, KV_WIDTH), F32),
        ],
        scratch_shapes=_mixer_scratch(tt, WINDOW + tt),
        compiler_params=pltpu.CompilerParams(
            dimension_semantics=("arbitrary", "arbitrary"), vmem_limit_bytes=VMEM_LIMIT),
        name="mix_prompt",
    )(sinks, x, *weights)


def _mix_sample(sinks, x2, ck, cv, base0, weights):
    tok = x2.shape[0]
    tt = MIX_TILE
    n_chunks = tt // CHUNK
    cache_rows = n_chunks * WINDOW
    smem = pl.BlockSpec(memory_space=pltpu.SMEM)
    row = lambda w: pl.BlockSpec((tt, w), lambda i: (i, 0))
    return pl.pallas_call(
        functools.partial(_mixer_kernel, is_prompt=False, tt=tt),
        grid=(tok // tt,),
        in_specs=[smem, row(D_MODEL),
                  pl.BlockSpec((cache_rows, KV_WIDTH), lambda i: (i, 0)),
                  pl.BlockSpec((cache_rows, KV_WIDTH), lambda i: (i, 0)),
                  _const_spec((1, LANES))] + _mixer_weight_specs(tt),
        out_specs=[row(D_MODEL), row(LANES), pl.BlockSpec((1, LANES), lambda i: (0, 0)),
                   row(KV_WIDTH), row(KV_WIDTH), row(GMLP_WIDTH)],
        out_shape=[
            jax.ShapeDtypeStruct((tok, D_MODEL), F32), jax.ShapeDtypeStruct((tok, LANES), F32),
            jax.ShapeDtypeStruct((1, LANES), F32),
            jax.ShapeDtypeStruct((tok, KV_WIDTH), F32), jax.ShapeDtypeStruct((tok, KV_WIDTH), F32),
            jax.ShapeDtypeStruct((tok, GMLP_WIDTH), F32),
        ],
        scratch_shapes=_mixer_scratch(tt, n_chunks * KEYS),
        compiler_params=pltpu.CompilerParams(
            dimension_semantics=("arbitrary",), vmem_limit_bytes=VMEM_LIMIT),
        name="mix_sample",
    )(sinks, x2, ck, cv, base0, *weights)


def _dispatch_kernel(*refs, td, aliased):
    if aliased:
        dest_ref, h_ref, _, xs_hbm, sem = refs
    else:
        dest_ref, h_ref, xs_hbm, sem = refs

    def body(t, carry):
        for kk in range(TOP_K):
            d = dest_ref[0, 0, kk * td + t]
            pltpu.make_async_copy(h_ref.at[pl.ds(t, 1)], xs_hbm.at[pl.ds(d, 1)], sem).start(priority=kk % 2)
        return carry

    lax.fori_loop(0, td, body, 0)

    for _ in range(TOP_K):
        pltpu.make_async_copy(h_ref, xs_hbm.at[pl.ds(0, td)], sem).wait()


def _dispatch(dest_tiles, h, xs_or_rows):
    td = DISPATCH_TILE
    n_tiles = h.shape[0] // td
    aliased = not isinstance(xs_or_rows, int)
    any_spec = pl.BlockSpec(memory_space=pl.ANY)
    in_specs = [pl.BlockSpec((1, 1, TOP_K * td), lambda i: (i, 0, 0), memory_space=pltpu.SMEM),
                pl.BlockSpec((td, D_MODEL), lambda i: (i, 0))]
    args = [dest_tiles, h]
    if aliased:
        in_specs.append(any_spec)
        args.append(xs_or_rows)
        n_rows = xs_or_rows.shape[0]
    else:
        n_rows = xs_or_rows
    return pl.pallas_call(
        functools.partial(_dispatch_kernel, td=td, aliased=aliased),
        grid=(n_tiles,),
        in_specs=in_specs,
        out_specs=any_spec,
        out_shape=jax.ShapeDtypeStruct((n_rows, D_MODEL), F32),
        scratch_shapes=[pltpu.SemaphoreType.DMA(())],
        input_output_aliases={2: 0} if aliased else {},
        compiler_params=pltpu.CompilerParams(dimension_semantics=("arbitrary",)),
        name="dispatch_update" if aliased else "dispatch",
    )(*args)


def _expert_kernel(te_ref, na_ref, x_ref, wgu_ref, bgu_ref, wd_ref, bd_ref, y_ref):
    del te_ref

    @pl.when(pl.program_id(0) < na_ref[0])
    def _():
        x = x_ref[...].astype(BF16)
        hmid = _dot(x, wgu_ref[0]) + bgu_ref[0]
        gate = jnp.minimum(hmid[:, :D_FF], SWIGLU_LIMIT)
        up = jnp.clip(hmid[:, D_FF:], -SWIGLU_LIMIT, SWIGLU_LIMIT)
        act = (up + 1.0) * gate * jax.nn.sigmoid(SWIGLU_ALPHA * gate)
        y_ref[...] = _dot(act.astype(BF16), wd_ref[0]) + bd_ref[0]


def _experts(tile_e, n_active, xs, wgu, bgu, wd, bd):
    tm = ROW_TILE
    n_tiles = xs.shape[0] // tm
    row_map = lambda i, te, na: (jnp.minimum(i, na[0] - 1), 0)
    w_map = lambda i, te, na: (te[i], 0, 0)
    return pl.pallas_call(
        _expert_kernel,
        grid_spec=pltpu.PrefetchScalarGridSpec(
            num_scalar_prefetch=2,
            grid=(n_tiles,),
            in_specs=[
                pl.BlockSpec((tm, D_MODEL), row_map),
                pl.BlockSpec((1, D_MODEL, 2 * D_FF), w_map),
                pl.BlockSpec((1, 1, 2 * D_FF), w_map),
                pl.BlockSpec((1, D_FF, D_MODEL), w_map),
                pl.BlockSpec((1, 1, D_MODEL), w_map),
            ],
            out_specs=pl.BlockSpec((tm, D_MODEL), row_map),
        ),
        out_shape=jax.ShapeDtypeStruct(xs.shape, F32),
        compiler_params=pltpu.CompilerParams(
            dimension_semantics=("arbitrary",), vmem_limit_bytes=VMEM_LIMIT),
        name="experts",
    )(tile_e, n_active, xs, wgu, bgu, wd, bd)


def _combine_kernel(dcur_ref, dnxt_ref, h_ref, meta_ref, g_ref, b_ref, ys_hbm, out_ref, buf, sem, *, tc):
    i = pl.program_id(0)
    n = pl.num_programs(0)
    slot = i % 2

    def issue(dref, s):
        def body(t, carry):
            for kk in range(TOP_K):
                d = dref[0, 0, kk * tc + t]
                pltpu.make_async_copy(ys_hbm.at[pl.ds(d, 1)], buf.at[s, kk, pl.ds(t, 1)], sem.at[s]).start(priority=kk % 2)
            return carry
        lax.fori_loop(0, tc, body, 0)

    @pl.when(i == 0)
    def _():
        issue(dcur_ref, 0)

    @pl.when(i + 1 < n)
    def _():
        issue(dnxt_ref, 1 - slot)

    for kk in range(TOP_K):
        pltpu.make_async_copy(ys_hbm.at[pl.ds(0, tc)], buf.at[slot, kk], sem.at[slot]).wait()

    meta = meta_ref[...]
    f = meta[:, META_G:META_G + 1] * buf[slot, 0]
    for kk in range(1, TOP_K):
        f = f + meta[:, META_G + kk:META_G + kk + 1] * buf[slot, kk]
    out_ref[...] = _layer_norm(DN_ALPHA * h_ref[...] + f, g_ref[...], b_ref[...])


def _combine(dest_tiles, h, meta, ln_g, ln_b, ys):
    tc = COMBINE_TILE
    tok = h.shape[0]
    n_tiles = tok // tc
    dspec = lambda f: pl.BlockSpec((1, 1, TOP_K * tc), f, memory_space=pltpu.SMEM)
    return pl.pallas_call(
        functools.partial(_combine_kernel, tc=tc),
        grid=(n_tiles,),
        in_specs=[
            dspec(lambda i: (i, 0, 0)),
            dspec(lambda i: (jnp.minimum(i + 1, n_tiles - 1), 0, 0)),
            pl.BlockSpec((tc, D_MODEL), lambda i: (i, 0)),
            pl.BlockSpec((tc, LANES), lambda i: (i, 0)),
            _const_spec((1, D_MODEL)), _const_spec((1, D_MODEL)),
            pl.BlockSpec(memory_space=pl.ANY),
        ],
        out_specs=pl.BlockSpec((tc, D_MODEL), lambda i: (i, 0)),
        out_shape=jax.ShapeDtypeStruct((tok, D_MODEL), F32),
        scratch_shapes=[pltpu.VMEM((2, TOP_K, tc, D_MODEL), F32), pltpu.SemaphoreType.DMA((2,))],
        compiler_params=pltpu.CompilerParams(
            dimension_semantics=("arbitrary",), vmem_limit_bytes=VMEM_LIMIT),
        name="combine",
    )(dest_tiles, dest_tiles, h, meta, ln_g, ln_b, ys)


def _pad_heads_last(w):
    lead = w.shape[:-1]
    w = w.reshape(lead + (N_KV_HEADS, Q_GROUP, HEAD_DIM))
    zero = jnp.zeros_like(w[..., 0, :, :])
    parts = []
    for hk in range(N_KV_HEADS):
        cols = [zero] * N_KV_HEADS
        cols[hk] = w[..., hk, :, :]
        parts.append(jnp.concatenate(cols, axis=-1))
    return jnp.stack(parts, axis=-3).reshape(lead + (QPAD_WIDTH,))


def _mixer_weights(w_in, b_in, ln_g, ln_b, w_s, b_s, w_o, b_o, ln1_g, ln1_b, w_router, b_router):
    win = jnp.concatenate([_pad_heads_last(w_in[:, :ATTN_WIDTH]), w_in[:, ATTN_WIDTH:]], axis=1).astype(BF16)
    bin_ = jnp.concatenate([_pad_heads_last(b_in[:ATTN_WIDTH]), b_in[ATTN_WIDTH:]])[None, :]
    woa = _pad_heads_last(w_o[:ATTN_WIDTH].T).T.astype(BF16)
    wog = w_o[ATTN_WIDTH:].astype(BF16)
    wr = jnp.pad(w_router, ((0, 0), (0, LANES - N_EXPERTS)))
    wrh = wr.astype(BF16)
    wrl = (wr - wrh.astype(F32)).astype(BF16)
    br = jnp.pad(b_router, (0, LANES - N_EXPERTS))[None, :]
    tri = (lax.broadcasted_iota(I32, (MIX_TILE, MIX_TILE), 1)
           < lax.broadcasted_iota(I32, (MIX_TILE, MIX_TILE), 0)).astype(BF16)
    return (win, bin_, ln_g.reshape(1, GMLP_WIDTH), ln_b.reshape(1, GMLP_WIDTH), w_s, b_s.T,
            woa, wog, b_o[None, :], ln1_g[None, :], ln1_b[None, :], wrh, wrl, br, tri)


def _dest_tiles(meta, pstart, tile):
    e = meta[:, META_E:META_E + TOP_K].astype(I32)
    r = meta[:, META_R:META_R + TOP_K].astype(I32)
    dest = pstart[e] + r
    n_tiles = meta.shape[0] // tile
    return dest.reshape(n_tiles, tile, TOP_K).transpose(0, 2, 1).reshape(n_tiles, 1, TOP_K * tile)


def kernel(x_prompt, x_sample, cache_k, cache_v, w_in, b_in, attn_sinks, gmlp_ln_g, gmlp_ln_b, w_spatial, b_spatial, w_o, b_o, ln1_g, ln1_b, w_router, b_router, w_gate_up, b_gate_up, w_down, b_down, ln2_g, ln2_b):
    assert w_in.shape[0] == DEPTH
    batch, seq, _ = x_prompt.shape
    dec_batch, dec_seq, _ = x_sample.shape
    tok_p, tok_s = batch * seq, dec_batch * dec_seq
    assert dec_seq == CHUNK and seq % MIX_TILE == 0
    assert all(t % DISPATCH_TILE == 0 and t % COMBINE_TILE == 0 and t % MIX_TILE == 0 for t in (tok_p, tok_s))

    weights = _mixer_weights(w_in[0], b_in[0], gmlp_ln_g[0], gmlp_ln_b[0], w_spatial[0], b_spatial[0],
                             w_o[0], b_o[0], ln1_g[0], ln1_b[0], w_router[0], b_router[0])
    sinks = attn_sinks[0]
    h_p, meta_p, cnt_p, kt, vt = _mix_prompt(sinks, x_prompt, weights)
    h_s, meta_s, cnt, ks, vs, gvs = _mix_sample(
        sinks, x_sample.reshape(tok_s, D_MODEL),
        cache_k[0].reshape(dec_batch * WINDOW, KV_WIDTH), cache_v[0].reshape(dec_batch * WINDOW, KV_WIDTH),
        cnt_p, weights)

    n_assign = (tok_p + tok_s) * TOP_K
    n_tiles = (n_assign + N_EXPERTS * (ROW_TILE - 1)) // ROW_TILE
    counts = cnt[0, :N_EXPERTS].astype(I32)
    padded = (counts + ROW_TILE - 1) // ROW_TILE * ROW_TILE
    pend = jnp.cumsum(padded)
    pstart = pend - padded
    n_active = (pend[-1:] // ROW_TILE).astype(I32)
    tile_start = jnp.arange(n_tiles, dtype=I32) * ROW_TILE
    tile_e = jnp.minimum(jnp.sum(pend[None, :] <= tile_start[:, None], axis=1), N_EXPERTS - 1).astype(I32)

    xs = _dispatch(_dest_tiles(meta_p, pstart, DISPATCH_TILE), h_p, n_tiles * ROW_TILE)
    xs = _dispatch(_dest_tiles(meta_s, pstart, DISPATCH_TILE), h_s, xs)
    dest_p = _dest_tiles(meta_p, pstart, COMBINE_TILE)
    dest_s = _dest_tiles(meta_s, pstart, COMBINE_TILE)

    ys = _experts(tile_e, n_active, xs, w_gate_up[0].astype(BF16), b_gate_up[0][:, None, :],
                  w_down[0].astype(BF16), b_down[0][:, None, :])

    g2, b2 = ln2_g[0][None, :], ln2_b[0][None, :]
    y_p = _combine(dest_p, h_p, meta_p, g2, b2, ys).reshape(batch, seq, D_MODEL)
    y_s = _combine(dest_s, h_s, meta_s, g2, b2, ys).reshape(dec_batch, dec_seq, D_MODEL)

    kv5 = lambda a, nb, rows: a.reshape(DEPTH, nb, rows, N_KV_HEADS, HEAD_DIM)
    return (y_p, y_s, kv5(kt, batch, WINDOW), kv5(vt, batch, WINDOW),
            kv5(ks, dec_batch, dec_seq), kv5(vs, dec_batch, dec_seq),
            gvs.reshape(DEPTH, dec_batch, dec_seq, GMLP_GROUPS, GMLP_GROUP_DIM))
```

```python
import functools

import jax
import jax.numpy as jnp
from jax import lax
from jax.experimental import pallas as pl
from jax.experimental.pallas import tpu as pltpu
from jax.experimental.pallas import tpu_sc as plsc

F32 = jnp.float32
BF16 = jnp.bfloat16
I32 = jnp.int32

D_MODEL = 1024
CHUNK = 64
N_HEADS = 8
N_KV_HEADS = 2
HEAD_DIM = 64
Q_GROUP = N_HEADS // N_KV_HEADS
KV_WIDTH = N_KV_HEADS * HEAD_DIM
WINDOW = 128
KEYS = WINDOW + CHUNK
GMLP_GROUPS = 4
GMLP_GROUP_DIM = 128
GMLP_WIDTH = GMLP_GROUPS * GMLP_GROUP_DIM
GMLP_CHUNK = 128
ATTN_WIDTH = N_HEADS * HEAD_DIM
N_EXPERTS = 32
TOP_K = 4
D_FF = 1024
SWIGLU_LIMIT = 7.0
SWIGLU_ALPHA = 1.702
DEPTH = 1
DN_ALPHA = (2 * DEPTH) ** 0.25
LN_EPS = 1e-5
NEG_INF = -1e30

LANES = 128
QPAD_WIDTH = N_HEADS * LANES
K_OFF = QPAD_WIDTH
V_OFF = K_OFF + KV_WIDTH
U_OFF = V_OFF + KV_WIDTH
GV_OFF = U_OFF + GMLP_WIDTH
ZPAD_WIDTH = GV_OFF + GMLP_WIDTH

MIX_TILE = 512
ROW_TILE = 512
DISPATCH_TILE = 1024
COMBINE_TILE = 256
SC_WINDOW = 128
SC_SLABS = 4
VMEM_LIMIT = 56 * 1024 * 1024

META_E, META_R, META_G = 0, 4, 8


def _gelu_tanh(x):
    return 0.5 * x * (1.0 + jnp.tanh(0.7978845608028654 * (x + 0.044715 * x * x * x)))


def _layer_norm(x, g, b):
    mu = jnp.mean(x, axis=-1, keepdims=True)
    xc = x - mu
    var = jnp.mean(xc * xc, axis=-1, keepdims=True)
    return xc * lax.rsqrt(var + LN_EPS) * g + b


def _dot(a, b):
    return jnp.dot(a, b, preferred_element_type=F32)


def _mixer_kernel(*refs, is_prompt, tt):
    n_chunks = tt // CHUNK
    it = iter(refs)
    sinks_ref = next(it)
    x_ref = next(it)
    if not is_prompt:
        ck_ref, cv_ref, base0_ref = next(it), next(it), next(it)
    (win_ref, bin_ref, lng_ref, lnb_ref, ws_ref, bs_ref, woa_ref, wog_ref, bo_ref, l1g_ref, l1b_ref,
     wrh_ref, wrl_ref, br_ref, tri_ref) = (next(it) for _ in range(15))
    h_ref, meta_ref, cnt_ref = next(it), next(it), next(it)
    if is_prompt:
        kt_ref, vt_ref = next(it), next(it)
    else:
        ko_ref, vo_ref, gvo_ref = next(it), next(it), next(it)
    z_ref, kext_ref, vext_ref, att_ref, gm_ref, base_ref = (next(it) for _ in range(6))

    if is_prompt:
        first = (pl.program_id(0) == 0) & (pl.program_id(1) == 0)
        tile_in_seq = pl.program_id(1)

        @pl.when(first)
        def _():
            base_ref[...] = jnp.zeros_like(base_ref)

        @pl.when(tile_in_seq == 0)
        def _():
            kext_ref[0:WINDOW, :] = jnp.zeros((WINDOW, KV_WIDTH), BF16)
            vext_ref[0:WINDOW, :] = jnp.zeros((WINDOW, KV_WIDTH), BF16)
    else:
        @pl.when(pl.program_id(0) == 0)
        def _():
            base_ref[...] = base0_ref[...]

    x = x_ref[...]
    z_ref[...] = _dot(x.astype(BF16), win_ref[...]) + bin_ref[...]

    k = z_ref[:, K_OFF:K_OFF + KV_WIDTH]
    v = z_ref[:, V_OFF:V_OFF + KV_WIDTH]
    if is_prompt:
        kext_ref[WINDOW:WINDOW + tt, :] = k.astype(BF16)
        vext_ref[WINDOW:WINDOW + tt, :] = v.astype(BF16)
        kt_ref[...] = k[tt - WINDOW:, :]
        vt_ref[...] = v[tt - WINDOW:, :]
        key_stride = CHUNK
    else:
        ko_ref[...] = k
        vo_ref[...] = v
        for c in range(n_chunks):
            kext_ref[KEYS * c:KEYS * c + WINDOW, :] = ck_ref[WINDOW * c:WINDOW * (c + 1), :].astype(BF16)
            vext_ref[KEYS * c:KEYS * c + WINDOW, :] = cv_ref[WINDOW * c:WINDOW * (c + 1), :].astype(BF16)
            kext_ref[KEYS * c + WINDOW:KEYS * (c + 1), :] = k[CHUNK * c:CHUNK * (c + 1), :].astype(BF16)
            vext_ref[KEYS * c + WINDOW:KEYS * (c + 1), :] = v[CHUNK * c:CHUNK * (c + 1), :].astype(BF16)
        key_stride = KEYS

    rows = Q_GROUP * CHUNK
    row_i = lax.broadcasted_iota(I32, (rows, 1), 0)
    key_i = lax.broadcasted_iota(I32, (rows, KEYS), 1)
    sink_cols = []
    for hk in range(N_KV_HEADS):
        s = [sinks_ref[hk * Q_GROUP + g] for g in range(Q_GROUP)]
        sink_cols.append(jnp.where(row_i < CHUNK, s[0],
                                   jnp.where(row_i < 2 * CHUNK, s[1],
                                             jnp.where(row_i < 3 * CHUNK, s[2], s[3]))))
    for c in range(n_chunks):
        kc = kext_ref[key_stride * c:key_stride * c + KEYS, :]
        vc = vext_ref[key_stride * c:key_stride * c + KEYS, :]
        outs = []
        for hk in range(N_KV_HEADS):
            qs = jnp.concatenate(
                [z_ref[CHUNK * c:CHUNK * (c + 1), (hk * Q_GROUP + g) * LANES:(hk * Q_GROUP + g + 1) * LANES]
                 for g in range(Q_GROUP)], axis=0).astype(BF16)
            sc = lax.dot_general(qs, kc, (((1,), (1,)), ((), ())), preferred_element_type=F32)
            sc = sc * (HEAD_DIM ** -0.5)
            if is_prompt and c < WINDOW // CHUNK:
                valid = (key_i >= WINDOW - CHUNK * c) | (tile_in_seq > 0)
                sc = jnp.where(valid, sc, NEG_INF)
            sink = sink_cols[hk]
            m = jnp.maximum(jnp.max(sc, axis=-1, keepdims=True), sink)
            p = jnp.exp(sc - m)
            den = jnp.sum(p, axis=-1, keepdims=True) + jnp.exp(sink - m)
            o = _dot(p.astype(BF16), vc) / den
            outs.extend(o[CHUNK * g:CHUNK * (g + 1), :] for g in range(Q_GROUP))
        att_ref[CHUNK * c:CHUNK * (c + 1), :] = jnp.concatenate(outs, axis=1).astype(BF16)

    if is_prompt:
        kext_ref[0:WINDOW, :] = kext_ref[tt:tt + WINDOW, :]
        vext_ref[0:WINDOW, :] = vext_ref[tt:tt + WINDOW, :]

    gc = GMLP_CHUNK if is_prompt else CHUNK
    ri = lax.broadcasted_iota(I32, (gc, gc), 0)
    ci = lax.broadcasted_iota(I32, (gc, gc), 1)
    causal = (ci // CHUNK) <= (ri // CHUNK)
    for g in range(GMLP_GROUPS):
        lo, hi = g * GMLP_GROUP_DIM, (g + 1) * GMLP_GROUP_DIM
        u = _gelu_tanh(z_ref[:, U_OFF + lo:U_OFF + hi])
        gv = _layer_norm(_gelu_tanh(z_ref[:, GV_OFF + lo:GV_OFF + hi]), lng_ref[:, lo:hi], lnb_ref[:, lo:hi])
        if not is_prompt:
            gvo_ref[:, lo:hi] = gv
        gvb = gv.astype(BF16)
        wm = jnp.where(causal, ws_ref[g, 0:gc, 0:gc], 0.0).astype(BF16)
        bcol = bs_ref[0:gc, g:g + 1]
        for n in range(tt // gc):
            sp = _dot(wm, gvb[gc * n:gc * (n + 1), :]) + bcol
            gm_ref[gc * n:gc * (n + 1), lo:hi] = (u[gc * n:gc * (n + 1), :] * sp).astype(BF16)

    y = _dot(att_ref[...], woa_ref[...]) + _dot(gm_ref[...], wog_ref[...]) + bo_ref[...]
    h = _layer_norm(DN_ALPHA * x + y, l1g_ref[...], l1b_ref[...])
    h_ref[...] = h

    h_hi = h.astype(BF16)
    h_lo = (h - h_hi.astype(F32)).astype(BF16)
    logits = _dot(h_hi, wrh_ref[...]) + _dot(h_lo, wrh_ref[...]) + _dot(h_hi, wrl_ref[...]) + br_ref[...]
    lane = lax.broadcasted_iota(I32, (tt, LANES), 1)
    lane_f = lane.astype(F32)
    l = jnp.where(lane < N_EXPERTS, logits, -jnp.inf)
    tops, idxs, hots = [], [], []
    for _ in range(TOP_K):
        m = jnp.max(l, axis=-1, keepdims=True)
        idx = jnp.min(jnp.where(l == m, lane_f, float(LANES)), axis=-1, keepdims=True)
        hot = lane_f == idx
        l = jnp.where(hot, -jnp.inf, l)
        tops.append(m)
        idxs.append(idx)
        hots.append(hot)
    es = [jnp.exp(t - tops[0]) for t in tops]
    esum = es[0] + es[1] + es[2] + es[3]
    chosen = jnp.where(hots[0] | hots[1] | hots[2] | hots[3], 1.0, 0.0)
    before = _dot(tri_ref[...], chosen.astype(BF16)) + base_ref[...]
    meta = jnp.zeros((tt, LANES), F32)
    for kk in range(TOP_K):
        rank = jnp.sum(jnp.where(hots[kk], before, 0.0), axis=-1, keepdims=True)
        meta = jnp.where(lane == META_E + kk, idxs[kk], meta)
        meta = jnp.where(lane == META_R + kk, rank, meta)
        meta = jnp.where(lane == META_G + kk, es[kk] / esum, meta)
    meta_ref[...] = meta
    base_ref[...] = base_ref[...] + jnp.sum(chosen, axis=0, keepdims=True)
    cnt_ref[...] = base_ref[...]


def _const_spec(shape):
    nd = len(shape)
    return pl.BlockSpec(shape, lambda *_: (0,) * nd)


def _mixer_weight_specs(tt):
    return [
        _const_spec((D_MODEL, ZPAD_WIDTH)), _const_spec((1, ZPAD_WIDTH)),
        _const_spec((1, GMLP_WIDTH)), _const_spec((1, GMLP_WIDTH)),
        _const_spec((GMLP_GROUPS, GMLP_CHUNK, GMLP_CHUNK)), _const_spec((GMLP_CHUNK, GMLP_GROUPS)),
        _const_spec((QPAD_WIDTH, D_MODEL)), _const_spec((GMLP_WIDTH, D_MODEL)), _const_spec((1, D_MODEL)),
        _const_spec((1, D_MODEL)), _const_spec((1, D_MODEL)),
        _const_spec((D_MODEL, LANES)), _const_spec((D_MODEL, LANES)), _const_spec((1, LANES)),
        _const_spec((tt, tt)),
    ]


def _mixer_scratch(tt, kext_rows):
    return [
        pltpu.VMEM((tt, ZPAD_WIDTH), F32),
        pltpu.VMEM((kext_rows, KV_WIDTH), BF16), pltpu.VMEM((kext_rows, KV_WIDTH), BF16),
        pltpu.VMEM((tt, QPAD_WIDTH), BF16), pltpu.VMEM((tt, GMLP_WIDTH), BF16),
        pltpu.VMEM((1, LANES), F32),
    ]


def _mix_prompt(sinks, x, weights):
    batch, seq, _ = x.shape
    tt = MIX_TILE
    n_tiles = seq // tt
    tok = batch * seq
    smem = pl.BlockSpec(memory_space=pltpu.SMEM)
    return pl.pallas_call(
        functools.partial(_mixer_kernel, is_prompt=True, tt=tt),
        grid=(batch, n_tiles),
        in_specs=[smem, pl.BlockSpec((None, tt, D_MODEL), lambda b, i: (b, i, 0))] + _mixer_weight_specs(tt),
        out_specs=[
            pl.BlockSpec((tt, D_MODEL), lambda b, i: (b * n_tiles + i, 0)),
            pl.BlockSpec((tt, LANES), lambda b, i: (b * n_tiles + i, 0)),
            pl.BlockSpec((1, LANES), lambda b, i: (0, 0)),
            pl.BlockSpec((WINDOW, KV_WIDTH), lambda b, i: (b, 0)),
            pl.BlockSpec((WINDOW, KV_WIDTH), lambda b, i: (b, 0)),
        ],
        out_shape=[
            jax.ShapeDtypeStruct((tok, D_MODEL), F32), jax.ShapeDtypeStruct((tok, LANES), F32),
            jax.ShapeDtypeStruct((1, LANES), F32),
            jax.ShapeDtypeStruct((batch * WINDOW, KV_WIDTH), F32),
            jax.ShapeDtypeStruct((batch * WINDOW, KV_WIDTH), F32),
        ],
        scratch_shapes=_mixer_scratch(tt, WINDOW + tt),
        compiler_params=pltpu.CompilerParams(
            dimension_semantics=("arbitrary", "arbitrary"), vmem_limit_bytes=VMEM_LIMIT),
        name="mix_prompt",
    )(sinks, x, *weights)


def _mix_sample(sinks, x2, ck, cv, base0, weights):
    tok = x2.shape[0]
    tt = MIX_TILE
    n_chunks = tt // CHUNK
    cache_rows = n_chunks * WINDOW
    smem = pl.BlockSpec(memory_space=pltpu.SMEM)
    row = lambda w: pl.BlockSpec((tt, w), lambda i: (i, 0))
    return pl.pallas_call(
        functools.partial(_mixer_kernel, is_prompt=False, tt=tt),
        grid=(tok // tt,),
        in_specs=[smem, row(D_MODEL),
                  pl.BlockSpec((cache_rows, KV_WIDTH), lambda i: (i, 0)),
                  pl.BlockSpec((cache_rows, KV_WIDTH), lambda i: (i, 0)),
                  _const_spec((1, LANES))] + _mixer_weight_specs(tt),
        out_specs=[row(D_MODEL), row(LANES), pl.BlockSpec((1, LANES), lambda i: (0, 0)),
                   row(KV_WIDTH), row(KV_WIDTH), row(GMLP_WIDTH)],
        out_shape=[
            jax.ShapeDtypeStruct((tok, D_MODEL), F32), jax.ShapeDtypeStruct((tok, LANES), F32),
            jax.ShapeDtypeStruct((1, LANES), F32),
            jax.ShapeDtypeStruct((tok, KV_WIDTH), F32), jax.ShapeDtypeStruct((tok, KV_WIDTH), F32),
            jax.ShapeDtypeStruct((tok, GMLP_WIDTH), F32),
        ],
        scratch_shapes=_mixer_scratch(tt, n_chunks * KEYS),
        compiler_params=pltpu.CompilerParams(
            dimension_semantics=("arbitrary",), vmem_limit_bytes=VMEM_LIMIT),
        name="mix_sample",
    )(sinks, x2, ck, cv, base0, *weights)


def _dispatch_kernel(*refs, td, aliased):
    if aliased:
        dest_ref, h_ref, _, xs_hbm, sem = refs
    else:
        dest_ref, h_ref, xs_hbm, sem = refs

    def body(t, carry):
        for kk in range(TOP_K):
            d = dest_ref[0, 0, kk * td + t]
            pltpu.make_async_copy(h_ref.at[pl.ds(t, 1)], xs_hbm.at[pl.ds(d, 1)], sem).start(priority=kk % 2)
        return carry

    lax.fori_loop(0, td, body, 0)

    for _ in range(TOP_K):
        pltpu.make_async_copy(h_ref, xs_hbm.at[pl.ds(0, td)], sem).wait()


def _dispatch(dest_tiles, h, xs_or_rows):
    td = DISPATCH_TILE
    n_tiles = h.shape[0] // td
    aliased = not isinstance(xs_or_rows, int)
    any_spec = pl.BlockSpec(memory_space=pl.ANY)
    in_specs = [pl.BlockSpec((1, 1, TOP_K * td), lambda i: (i, 0, 0), memory_space=pltpu.SMEM),
                pl.BlockSpec((td, D_MODEL), lambda i: (i, 0))]
    args = [dest_tiles, h]
    if aliased:
        in_specs.append(any_spec)
        args.append(xs_or_rows)
        n_rows = xs_or_rows.shape[0]
    else:
        n_rows = xs_or_rows
    return pl.pallas_call(
        functools.partial(_dispatch_kernel, td=td, aliased=aliased),
        grid=(n_tiles,),
        in_specs=in_specs,
        out_specs=any_spec,
        out_shape=jax.ShapeDtypeStruct((n_rows, D_MODEL), F32),
        scratch_shapes=[pltpu.SemaphoreType.DMA(())],
        input_output_aliases={2: 0} if aliased else {},
        compiler_params=pltpu.CompilerParams(dimension_semantics=("arbitrary",)),
        name="dispatch_update" if aliased else "dispatch",
    )(*args)


def _sc_dispatch(dests_p, h_p, dests_s, h_s, n_rows):
    win, wid = SC_WINDOW, D_MODEL // SC_SLABS
    mesh = plsc.VectorSubcoreMesh(core_axis_name="core", subcore_axis_name="subcore")

    @functools.partial(pl.kernel, out_type=[jax.ShapeDtypeStruct((n_rows, wid), F32)] * SC_SLABS, mesh=mesh,
                       scratch_types=[], name="sc_dispatch")
    def run(*refs):
        hp_hbm, hs_hbm = refs[0], refs[1]
        ip_hbm, is_hbm = refs[2:2 + TOP_K], refs[2 + TOP_K:2 + 2 * TOP_K]
        xs_hbm = refs[2 + 2 * TOP_K:]

        for h_hbm, i_hbm in ((hp_hbm, ip_hbm), (hs_hbm, is_hbm)):
            for q in range(SC_SLABS):
                def body(x_vmem, *i_vmem, q=q):
                    for kk in range(TOP_K):
                        pltpu.sync_copy(x_vmem, xs_hbm[q].at[i_vmem[kk].at[0]])

                pltpu.emit_pipeline(
                    body,
                    grid=(h_hbm.shape[0] // win,),
                    in_specs=[pl.BlockSpec((win, wid), lambda i, q=q: (i, q))]
                    + [pl.BlockSpec((1, win), lambda i: (0, i))] * TOP_K,
                    out_specs=[],
                    core_axis_name=("core", "subcore"),
                    dimension_semantics=(pltpu.PARALLEL,),
                )(h_hbm, *i_hbm)

    return run(h_p, h_s, *dests_p, *dests_s)


def _expert_kernel(te_ref, na_ref, *refs):
    del te_ref
    x_refs = refs[:SC_SLABS]
    wgu_ref, bgu_ref, wd_ref, bd_ref, y_ref = refs[SC_SLABS:]

    @pl.when(pl.program_id(0) < na_ref[0])
    def _():
        x = jnp.concatenate([r[...] for r in x_refs], axis=1).astype(BF16)
        hmid = _dot(x, wgu_ref[0]) + bgu_ref[0]
        gate = jnp.minimum(hmid[:, :D_FF], SWIGLU_LIMIT)
        up = jnp.clip(hmid[:, D_FF:], -SWIGLU_LIMIT, SWIGLU_LIMIT)
        act = (up + 1.0) * gate * jax.nn.sigmoid(SWIGLU_ALPHA * gate)
        y_ref[...] = _dot(act.astype(BF16), wd_ref[0]) + bd_ref[0]


def _experts(tile_e, n_active, xs, wgu, bgu, wd, bd):
    tm = ROW_TILE
    n_tiles = xs[0].shape[0] // tm
    row_map = lambda i, te, na: (jnp.minimum(i, na[0] - 1), 0)
    w_map = lambda i, te, na: (te[i], 0, 0)
    return pl.pallas_call(
        _expert_kernel,
        grid_spec=pltpu.PrefetchScalarGridSpec(
            num_scalar_prefetch=2,
            grid=(n_tiles,),
            in_specs=[pl.BlockSpec((tm, D_MODEL // SC_SLABS), row_map)] * SC_SLABS + [
                pl.BlockSpec((1, D_MODEL, 2 * D_FF), w_map),
                pl.BlockSpec((1, 1, 2 * D_FF), w_map),
                pl.BlockSpec((1, D_FF, D_MODEL), w_map),
                pl.BlockSpec((1, 1, D_MODEL), w_map),
            ],
            out_specs=pl.BlockSpec((tm, D_MODEL), row_map),
        ),
        out_shape=jax.ShapeDtypeStruct((n_tiles * tm, D_MODEL), F32),
        compiler_params=pltpu.CompilerParams(
            dimension_semantics=("arbitrary",), vmem_limit_bytes=VMEM_LIMIT),
        name="experts",
    )(tile_e, n_active, *xs, wgu, bgu, wd, bd)


def _combine_kernel(dcur_ref, dnxt_ref, h_ref, meta_ref, g_ref, b_ref, ys_hbm, out_ref, buf, sem, *, tc):
    i = pl.program_id(0)
    n = pl.num_programs(0)
    slot = i % 2

    def issue(dref, s):
        def body(t, carry):
            for kk in range(TOP_K):
                d = dref[0, 0, kk * tc + t]
                pltpu.make_async_copy(ys_hbm.at[pl.ds(d, 1)], buf.at[s, kk, pl.ds(t, 1)], sem.at[s]).start(priority=kk % 2)
            return carry
        lax.fori_loop(0, tc, body, 0)

    @pl.when(i == 0)
    def _():
        issue(dcur_ref, 0)

    @pl.when(i + 1 < n)
    def _():
        issue(dnxt_ref, 1 - slot)

    for kk in range(TOP_K):
        pltpu.make_async_copy(ys_hbm.at[pl.ds(0, tc)], buf.at[slot, kk], sem.at[slot]).wait()

    meta = meta_ref[...]
    f = meta[:, META_G:META_G + 1] * buf[slot, 0]
    for kk in range(1, TOP_K):
        f = f + meta[:, META_G + kk:META_G + kk + 1] * buf[slot, kk]
    out_ref[...] = _layer_norm(DN_ALPHA * h_ref[...] + f, g_ref[...], b_ref[...])


def _combine(dest_tiles, h, meta, ln_g, ln_b, ys):
    tc = COMBINE_TILE
    tok = h.shape[0]
    n_tiles = tok // tc
    dspec = lambda f: pl.BlockSpec((1, 1, TOP_K * tc), f, memory_space=pltpu.SMEM)
    return pl.pallas_call(
        functools.partial(_combine_kernel, tc=tc),
        grid=(n_tiles,),
        in_specs=[
            dspec(lambda i: (i, 0, 0)),
            dspec(lambda i: (jnp.minimum(i + 1, n_tiles - 1), 0, 0)),
            pl.BlockSpec((tc, D_MODEL), lambda i: (i, 0)),
            pl.BlockSpec((tc, LANES), lambda i: (i, 0)),
            _const_spec((1, D_MODEL)), _const_spec((1, D_MODEL)),
            pl.BlockSpec(memory_space=pl.ANY),
        ],
        out_specs=pl.BlockSpec((tc, D_MODEL), lambda i: (i, 0)),
        out_shape=jax.ShapeDtypeStruct((tok, D_MODEL), F32),
        scratch_shapes=[pltpu.VMEM((2, TOP_K, tc, D_MODEL), F32), pltpu.SemaphoreType.DMA((2,))],
        compiler_params=pltpu.CompilerParams(
            dimension_semantics=("arbitrary",), vmem_limit_bytes=VMEM_LIMIT),
        name="combine",
    )(dest_tiles, dest_tiles, h, meta, ln_g, ln_b, ys)


def _pad_heads_last(w):
    lead = w.shape[:-1]
    w = w.reshape(lead + (N_KV_HEADS, Q_GROUP, HEAD_DIM))
    zero = jnp.zeros_like(w[..., 0, :, :])
    parts = []
    for hk in range(N_KV_HEADS):
        cols = [zero] * N_KV_HEADS
        cols[hk] = w[..., hk, :, :]
        parts.append(jnp.concatenate(cols, axis=-1))
    return jnp.stack(parts, axis=-3).reshape(lead + (QPAD_WIDTH,))


def _mixer_weights(w_in, b_in, ln_g, ln_b, w_s, b_s, w_o, b_o, ln1_g, ln1_b, w_router, b_router):
    win = jnp.concatenate([_pad_heads_last(w_in[:, :ATTN_WIDTH]), w_in[:, ATTN_WIDTH:]], axis=1).astype(BF16)
    bin_ = jnp.concatenate([_pad_heads_last(b_in[:ATTN_WIDTH]), b_in[ATTN_WIDTH:]])[None, :]
    woa = _pad_heads_last(w_o[:ATTN_WIDTH].T).T.astype(BF16)
    wog = w_o[ATTN_WIDTH:].astype(BF16)
    wr = jnp.pad(w_router, ((0, 0), (0, LANES - N_EXPERTS)))
    wrh = wr.astype(BF16)
    wrl = (wr - wrh.astype(F32)).astype(BF16)
    br = jnp.pad(b_router, (0, LANES - N_EXPERTS))[None, :]
    tri = (lax.broadcasted_iota(I32, (MIX_TILE, MIX_TILE), 1)
           < lax.broadcasted_iota(I32, (MIX_TILE, MIX_TILE), 0)).astype(BF16)
    return (win, bin_, ln_g.reshape(1, GMLP_WIDTH), ln_b.reshape(1, GMLP_WIDTH), w_s, b_s.T,
            woa, wog, b_o[None, :], ln1_g[None, :], ln1_b[None, :], wrh, wrl, br, tri)


def _dest_rows(meta, pstart):
    e = meta[:, META_E:META_E + TOP_K].astype(I32)
    r = meta[:, META_R:META_R + TOP_K].astype(I32)
    return pstart[e] + r


def _dest_tiles(dest, tile):
    n_tiles = dest.shape[0] // tile
    return dest.reshape(n_tiles, tile, TOP_K).transpose(0, 2, 1).reshape(n_tiles, 1, TOP_K * tile)


def kernel(x_prompt, x_sample, cache_k, cache_v, w_in, b_in, attn_sinks, gmlp_ln_g, gmlp_ln_b, w_spatial, b_spatial, w_o, b_o, ln1_g, ln1_b, w_router, b_router, w_gate_up, b_gate_up, w_down, b_down, ln2_g, ln2_b):
    assert w_in.shape[0] == DEPTH
    batch, seq, _ = x_prompt.shape
    dec_batch, dec_seq, _ = x_sample.shape
    tok_p, tok_s = batch * seq, dec_batch * dec_seq
    assert dec_seq == CHUNK and seq % MIX_TILE == 0
    assert all(t % DISPATCH_TILE == 0 and t % COMBINE_TILE == 0 and t % MIX_TILE == 0 for t in (tok_p, tok_s))

    weights = _mixer_weights(w_in[0], b_in[0], gmlp_ln_g[0], gmlp_ln_b[0], w_spatial[0], b_spatial[0],
                             w_o[0], b_o[0], ln1_g[0], ln1_b[0], w_router[0], b_router[0])
    sinks = attn_sinks[0]
    h_p, meta_p, cnt_p, kt, vt = _mix_prompt(sinks, x_prompt, weights)
    h_s, meta_s, cnt, ks, vs, gvs = _mix_sample(
        sinks, x_sample.reshape(tok_s, D_MODEL),
        cache_k[0].reshape(dec_batch * WINDOW, KV_WIDTH), cache_v[0].reshape(dec_batch * WINDOW, KV_WIDTH),
        cnt_p, weights)

    n_assign = (tok_p + tok_s) * TOP_K
    n_tiles = (n_assign + N_EXPERTS * (ROW_TILE - 1)) // ROW_TILE
    counts = cnt[0, :N_EXPERTS].astype(I32)
    padded = (counts + ROW_TILE - 1) // ROW_TILE * ROW_TILE
    pend = jnp.cumsum(padded)
    pstart = pend - padded
    n_active = (pend[-1:] // ROW_TILE).astype(I32)
    tile_start = jnp.arange(n_tiles, dtype=I32) * ROW_TILE
    tile_e = jnp.minimum(jnp.sum(pend[None, :] <= tile_start[:, None], axis=1), N_EXPERTS - 1).astype(I32)

    rows_p, rows_s = _dest_rows(meta_p, pstart), _dest_rows(meta_s, pstart)
    lists = lambda d: [d[:, kk][None, :] for kk in range(TOP_K)]
    xs = _sc_dispatch(lists(rows_p), h_p, lists(rows_s), h_s, n_tiles * ROW_TILE)
    dest_p = _dest_tiles(rows_p, COMBINE_TILE)
    dest_s = _dest_tiles(rows_s, COMBINE_TILE)

    ys = _experts(tile_e, n_active, xs, w_gate_up[0].astype(BF16), b_gate_up[0][:, None, :],
                  w_down[0].astype(BF16), b_down[0][:, None, :])

    g2, b2 = ln2_g[0][None, :], ln2_b[0][None, :]
    y_p = _combine(dest_p, h_p, meta_p, g2, b2, ys).reshape(batch, seq, D_MODEL)
    y_s = _combine(dest_s, h_s, meta_s, g2, b2, ys).reshape(dec_batch, dec_seq, D_MODEL)

    kv5 = lambda a, nb, rows: a.reshape(DEPTH, nb, rows, N_KV_HEADS, HEAD_DIM)
    return (y_p, y_s, kv5(kt, batch, WINDOW), kv5(vt, batch, WINDOW),
            kv5(ks, dec_batch, dec_seq), kv5(vs, dec_batch, dec_seq),
            gvs.reshape(DEPTH, dec_batch, dec_seq, GMLP_GROUPS, GMLP_GROUP_DIM))
```

```python
import functools

import jax
import jax.numpy as jnp
from jax import lax
from jax.experimental import pallas as pl
from jax.experimental.pallas import tpu as pltpu
from jax.experimental.pallas import tpu_sc as plsc

F32 = jnp.float32
BF16 = jnp.bfloat16
I32 = jnp.int32

D_MODEL = 1024
CHUNK = 64
N_HEADS = 8
N_KV_HEADS = 2
HEAD_DIM = 64
Q_GROUP = N_HEADS // N_KV_HEADS
KV_WIDTH = N_KV_HEADS * HEAD_DIM
WINDOW = 128
KEYS = WINDOW + CHUNK
GMLP_GROUPS = 4
GMLP_GROUP_DIM = 128
GMLP_WIDTH = GMLP_GROUPS * GMLP_GROUP_DIM
GMLP_CHUNK = 128
ATTN_WIDTH = N_HEADS * HEAD_DIM
N_EXPERTS = 32
TOP_K = 4
D_FF = 1024
SWIGLU_LIMIT = 7.0
SWIGLU_ALPHA = 1.702
DEPTH = 1
DN_ALPHA = (2 * DEPTH) ** 0.25
LN_EPS = 1e-5
NEG_INF = -1e30

LANES = 128
QPAD_WIDTH = N_HEADS * LANES
K_OFF = QPAD_WIDTH
V_OFF = K_OFF + KV_WIDTH
U_OFF = V_OFF + KV_WIDTH
GV_OFF = U_OFF + GMLP_WIDTH
ZPAD_WIDTH = GV_OFF + GMLP_WIDTH

MIX_TILE = 512
ROW_TILE = 512
FINAL_TILE = 512
SC_WINDOW = 128
SC_SLABS = 4
SLAB_WIDTH = D_MODEL // SC_SLABS
PACK_WIDTH = D_MODEL // 2
PACK_SLABS = PACK_WIDTH // SLAB_WIDTH
VMEM_LIMIT = 56 * 1024 * 1024

META_E, META_R = 0, 4


def _gelu_tanh(x):
    return 0.5 * x * (1.0 + jnp.tanh(0.7978845608028654 * (x + 0.044715 * x * x * x)))


def _layer_norm(x, g, b):
    mu = jnp.mean(x, axis=-1, keepdims=True)
    xc = x - mu
    var = jnp.mean(xc * xc, axis=-1, keepdims=True)
    return xc * lax.rsqrt(var + LN_EPS) * g + b


def _dot(a, b):
    return jnp.dot(a, b, preferred_element_type=F32)


def _mixer_kernel(*refs, is_prompt, tt):
    n_chunks = tt // CHUNK
    it = iter(refs)
    sinks_ref = next(it)
    x_ref = next(it)
    if not is_prompt:
        ck_ref, cv_ref, base0_ref = next(it), next(it), next(it)
    (win_ref, bin_ref, lng_ref, lnb_ref, ws_ref, bs_ref, woa_ref, wog_ref, bo_ref, l1g_ref, l1b_ref,
     wrh_ref, wrl_ref, br_ref, tri_ref) = (next(it) for _ in range(15))
    h_ref, meta_ref, gate_ref, cnt_ref = next(it), next(it), next(it), next(it)
    if is_prompt:
        kt_ref, vt_ref = next(it), next(it)
    else:
        ko_ref, vo_ref, gvo_ref = next(it), next(it), next(it)
    z_ref, kext_ref, vext_ref, att_ref, gm_ref, base_ref = (next(it) for _ in range(6))

    if is_prompt:
        first = (pl.program_id(0) == 0) & (pl.program_id(1) == 0)
        tile_in_seq = pl.program_id(1)

        @pl.when(first)
        def _():
            base_ref[...] = jnp.zeros_like(base_ref)

        @pl.when(tile_in_seq == 0)
        def _():
            kext_ref[0:WINDOW, :] = jnp.zeros((WINDOW, KV_WIDTH), BF16)
            vext_ref[0:WINDOW, :] = jnp.zeros((WINDOW, KV_WIDTH), BF16)
    else:
        @pl.when(pl.program_id(0) == 0)
        def _():
            base_ref[...] = base0_ref[...]

    x = x_ref[...]
    z_ref[...] = _dot(x.astype(BF16), win_ref[...]) + bin_ref[...]

    k = z_ref[:, K_OFF:K_OFF + KV_WIDTH]
    v = z_ref[:, V_OFF:V_OFF + KV_WIDTH]
    if is_prompt:
        kext_ref[WINDOW:WINDOW + tt, :] = k.astype(BF16)
        vext_ref[WINDOW:WINDOW + tt, :] = v.astype(BF16)
        kt_ref[...] = k[tt - WINDOW:, :]
        vt_ref[...] = v[tt - WINDOW:, :]
        key_stride = CHUNK
    else:
        ko_ref[...] = k
        vo_ref[...] = v
        for c in range(n_chunks):
            kext_ref[KEYS * c:KEYS * c + WINDOW, :] = ck_ref[WINDOW * c:WINDOW * (c + 1), :].astype(BF16)
            vext_ref[KEYS * c:KEYS * c + WINDOW, :] = cv_ref[WINDOW * c:WINDOW * (c + 1), :].astype(BF16)
            kext_ref[KEYS * c + WINDOW:KEYS * (c + 1), :] = k[CHUNK * c:CHUNK * (c + 1), :].astype(BF16)
            vext_ref[KEYS * c + WINDOW:KEYS * (c + 1), :] = v[CHUNK * c:CHUNK * (c + 1), :].astype(BF16)
        key_stride = KEYS

    rows = Q_GROUP * CHUNK
    row_i = lax.broadcasted_iota(I32, (rows, 1), 0)
    key_i = lax.broadcasted_iota(I32, (rows, KEYS), 1)
    sink_cols = []
    for hk in range(N_KV_HEADS):
        s = [sinks_ref[hk * Q_GROUP + g] for g in range(Q_GROUP)]
        sink_cols.append(jnp.where(row_i < CHUNK, s[0],
                                   jnp.where(row_i < 2 * CHUNK, s[1],
                                             jnp.where(row_i < 3 * CHUNK, s[2], s[3]))))
    for c in range(n_chunks):
        kc = kext_ref[key_stride * c:key_stride * c + KEYS, :]
        vc = vext_ref[key_stride * c:key_stride * c + KEYS, :]
        outs = []
        for hk in range(N_KV_HEADS):
            qs = jnp.concatenate(
                [z_ref[CHUNK * c:CHUNK * (c + 1), (hk * Q_GROUP + g) * LANES:(hk * Q_GROUP + g + 1) * LANES]
                 for g in range(Q_GROUP)], axis=0).astype(BF16)
            sc = lax.dot_general(qs, kc, (((1,), (1,)), ((), ())), preferred_element_type=F32)
            sc = sc * (HEAD_DIM ** -0.5)
            if is_prompt and c < WINDOW // CHUNK:
                valid = (key_i >= WINDOW - CHUNK * c) | (tile_in_seq > 0)
                sc = jnp.where(valid, sc, NEG_INF)
            sink = sink_cols[hk]
            m = jnp.maximum(jnp.max(sc, axis=-1, keepdims=True), sink)
            p = jnp.exp(sc - m)
            den = jnp.sum(p, axis=-1, keepdims=True) + jnp.exp(sink - m)
            o = _dot(p.astype(BF16), vc) / den
            outs.extend(o[CHUNK * g:CHUNK * (g + 1), :] for g in range(Q_GROUP))
        att_ref[CHUNK * c:CHUNK * (c + 1), :] = jnp.concatenate(outs, axis=1).astype(BF16)

    if is_prompt:
        kext_ref[0:WINDOW, :] = kext_ref[tt:tt + WINDOW, :]
        vext_ref[0:WINDOW, :] = vext_ref[tt:tt + WINDOW, :]

    gc = GMLP_CHUNK if is_prompt else CHUNK
    ri = lax.broadcasted_iota(I32, (gc, gc), 0)
    ci = lax.broadcasted_iota(I32, (gc, gc), 1)
    causal = (ci // CHUNK) <= (ri // CHUNK)
    for g in range(GMLP_GROUPS):
        lo, hi = g * GMLP_GROUP_DIM, (g + 1) * GMLP_GROUP_DIM
        u = _gelu_tanh(z_ref[:, U_OFF + lo:U_OFF + hi])
        gv = _layer_norm(_gelu_tanh(z_ref[:, GV_OFF + lo:GV_OFF + hi]), lng_ref[:, lo:hi], lnb_ref[:, lo:hi])
        if not is_prompt:
            gvo_ref[:, lo:hi] = gv
        gvb = gv.astype(BF16)
        wm = jnp.where(causal, ws_ref[g, 0:gc, 0:gc], 0.0).astype(BF16)
        bcol = bs_ref[0:gc, g:g + 1]
        for n in range(tt // gc):
            sp = _dot(wm, gvb[gc * n:gc * (n + 1), :]) + bcol
            gm_ref[gc * n:gc * (n + 1), lo:hi] = (u[gc * n:gc * (n + 1), :] * sp).astype(BF16)

    y = _dot(att_ref[...], woa_ref[...]) + _dot(gm_ref[...], wog_ref[...]) + bo_ref[...]
    h = _layer_norm(DN_ALPHA * x + y, l1g_ref[...], l1b_ref[...])
    h_ref[...] = h

    h_hi = h.astype(BF16)
    h_lo = (h - h_hi.astype(F32)).astype(BF16)
    logits = _dot(h_hi, wrh_ref[...]) + _dot(h_lo, wrh_ref[...]) + _dot(h_hi, wrl_ref[...]) + br_ref[...]
    lane = lax.broadcasted_iota(I32, (tt, LANES), 1)
    lane_f = lane.astype(F32)
    l = jnp.where(lane < N_EXPERTS, logits, -jnp.inf)
    tops, idxs, hots = [], [], []
    for _ in range(TOP_K):
        m = jnp.max(l, axis=-1, keepdims=True)
        idx = jnp.min(jnp.where(l == m, lane_f, float(LANES)), axis=-1, keepdims=True)
        hot = lane_f == idx
        l = jnp.where(hot, -jnp.inf, l)
        tops.append(m)
        idxs.append(idx)
        hots.append(hot)
    es = [jnp.exp(t - tops[0]) for t in tops]
    esum = es[0] + es[1] + es[2] + es[3]
    chosen = jnp.where(hots[0] | hots[1] | hots[2] | hots[3], 1.0, 0.0)
    before = _dot(tri_ref[...], chosen.astype(BF16)) + base_ref[...]
    meta = jnp.zeros((tt, LANES), F32)
    for kk in range(TOP_K):
        rank = jnp.sum(jnp.where(hots[kk], before, 0.0), axis=-1, keepdims=True)
        meta = jnp.where(lane == META_E + kk, idxs[kk], meta)
        meta = jnp.where(lane == META_R + kk, rank, meta)
        gate_ref[:, kk * LANES:(kk + 1) * LANES] = jnp.broadcast_to(es[kk] / esum, (tt, LANES))
    meta_ref[...] = meta
    base_ref[...] = base_ref[...] + jnp.sum(chosen, axis=0, keepdims=True)
    cnt_ref[...] = base_ref[...]


def _const_spec(shape):
    nd = len(shape)
    return pl.BlockSpec(shape, lambda *_: (0,) * nd)


def _mixer_weight_specs(tt):
    return [
        _const_spec((D_MODEL, ZPAD_WIDTH)), _const_spec((1, ZPAD_WIDTH)),
        _const_spec((1, GMLP_WIDTH)), _const_spec((1, GMLP_WIDTH)),
        _const_spec((GMLP_GROUPS, GMLP_CHUNK, GMLP_CHUNK)), _const_spec((GMLP_CHUNK, GMLP_GROUPS)),
        _const_spec((QPAD_WIDTH, D_MODEL)), _const_spec((GMLP_WIDTH, D_MODEL)), _const_spec((1, D_MODEL)),
        _const_spec((1, D_MODEL)), _const_spec((1, D_MODEL)),
        _const_spec((D_MODEL, LANES)), _const_spec((D_MODEL, LANES)), _const_spec((1, LANES)),
        _const_spec((tt, tt)),
    ]


def _mixer_scratch(tt, kext_rows):
    return [
        pltpu.VMEM((tt, ZPAD_WIDTH), F32),
        pltpu.VMEM((kext_rows, KV_WIDTH), BF16), pltpu.VMEM((kext_rows, KV_WIDTH), BF16),
        pltpu.VMEM((tt, QPAD_WIDTH), BF16), pltpu.VMEM((tt, GMLP_WIDTH), BF16),
        pltpu.VMEM((1, LANES), F32),
    ]


def _mix_prompt(sinks, x, weights):
    batch, seq, _ = x.shape
    tt = MIX_TILE
    n_tiles = seq // tt
    tok = batch * seq
    smem = pl.BlockSpec(memory_space=pltpu.SMEM)
    return pl.pallas_call(
        functools.partial(_mixer_kernel, is_prompt=True, tt=tt),
        grid=(batch, n_tiles),
        in_specs=[smem, pl.BlockSpec((None, tt, D_MODEL), lambda b, i: (b, i, 0))] + _mixer_weight_specs(tt),
        out_specs=[
            pl.BlockSpec((tt, D_MODEL), lambda b, i: (b * n_tiles + i, 0)),
            pl.BlockSpec((tt, LANES), lambda b, i: (b * n_tiles + i, 0)),
            pl.BlockSpec((tt, TOP_K * LANES), lambda b, i: (b * n_tiles + i, 0)),
            pl.BlockSpec((1, LANES), lambda b, i: (0, 0)),
            pl.BlockSpec((WINDOW, KV_WIDTH), lambda b, i: (b, 0)),
            pl.BlockSpec((WINDOW, KV_WIDTH), lambda b, i: (b, 0)),
        ],
        out_shape=[
            jax.ShapeDtypeStruct((tok, D_MODEL), F32), jax.ShapeDtypeStruct((tok, LANES), F32),
            jax.ShapeDtypeStruct((tok, TOP_K * LANES), F32),
            jax.ShapeDtypeStruct((1, LANES), F32),
            jax.ShapeDtypeStruct((batch * WINDOW, KV_WIDTH), F32),
            jax.ShapeDtypeStruct((batch * WINDOW, KV_WIDTH), F32),
        ],
        scratch_shapes=_mixer_scratch(tt, WINDOW + tt),
        compiler_params=pltpu.CompilerParams(
            dimension_semantics=("arbitrary", "arbitrary"), vmem_limit_bytes=VMEM_LIMIT),
        name="mix_prompt",
    )(sinks, x, *weights)


def _mix_sample(sinks, x2, ck, cv, base0, weights):
    tok = x2.shape[0]
    tt = MIX_TILE
    n_chunks = tt // CHUNK
    cache_rows = n_chunks * WINDOW
    smem = pl.BlockSpec(memory_space=pltpu.SMEM)
    row = lambda w: pl.BlockSpec((tt, w), lambda i: (i, 0))
    return pl.pallas_call(
        functools.partial(_mixer_kernel, is_prompt=False, tt=tt),
        grid=(tok // tt,),
        in_specs=[smem, row(D_MODEL),
                  pl.BlockSpec((cache_rows, KV_WIDTH), lambda i: (i, 0)),
                  pl.BlockSpec((cache_rows, KV_WIDTH), lambda i: (i, 0)),
                  _const_spec((1, LANES))] + _mixer_weight_specs(tt),
        out_specs=[row(D_MODEL), row(LANES), row(TOP_K * LANES), pl.BlockSpec((1, LANES), lambda i: (0, 0)),
                   row(KV_WIDTH), row(KV_WIDTH), row(GMLP_WIDTH)],
        out_shape=[
            jax.ShapeDtypeStruct((tok, D_MODEL), F32), jax.ShapeDtypeStruct((tok, LANES), F32),
            jax.ShapeDtypeStruct((tok, TOP_K * LANES), F32),
            jax.ShapeDtypeStruct((1, LANES), F32),
            jax.ShapeDtypeStruct((tok, KV_WIDTH), F32), jax.ShapeDtypeStruct((tok, KV_WIDTH), F32),
            jax.ShapeDtypeStruct((tok, GMLP_WIDTH), F32),
        ],
        scratch_shapes=_mixer_scratch(tt, n_chunks * KEYS),
        compiler_params=pltpu.CompilerParams(
            dimension_semantics=("arbitrary",), vmem_limit_bytes=VMEM_LIMIT),
        name="mix_sample",
    )(sinks, x2, ck, cv, base0, *weights)


def _sc_pipeline(body, n_tok, in_specs, out_specs):
    return pltpu.emit_pipeline(
        body, grid=(n_tok // SC_WINDOW,), in_specs=in_specs, out_specs=out_specs,
        core_axis_name=("core", "subcore"), dimension_semantics=(pltpu.PARALLEL,))


def _index_specs():
    return [pl.BlockSpec((1, SC_WINDOW), lambda i: (0, i))] * TOP_K


def _sc_dispatch(dests_p, h_p, gates_p, dests_s, h_s, gates_s, n_rows):
    win, wid = SC_WINDOW, D_MODEL // SC_SLABS
    mesh = plsc.VectorSubcoreMesh(core_axis_name="core", subcore_axis_name="subcore")
    out_type = [jax.ShapeDtypeStruct((n_rows, wid), F32)] * SC_SLABS + [jax.ShapeDtypeStruct((n_rows, LANES), F32)]

    @functools.partial(pl.kernel, out_type=out_type, mesh=mesh, scratch_types=[], name="sc_dispatch")
    def run(*refs):
        hp_hbm, gp_hbm, hs_hbm, gs_in_hbm = refs[:4]
        ip_hbm, is_hbm = refs[4:4 + TOP_K], refs[4 + TOP_K:4 + 2 * TOP_K]
        xs_hbm = refs[4 + 2 * TOP_K:4 + 2 * TOP_K + SC_SLABS]
        gs_hbm = refs[4 + 2 * TOP_K + SC_SLABS]

        for h_hbm, g_hbm, i_hbm in ((hp_hbm, gp_hbm, ip_hbm), (hs_hbm, gs_in_hbm, is_hbm)):
            n_tok = h_hbm.shape[0]
            for q in range(SC_SLABS):
                def rows_body(x_vmem, *i_vmem, q=q):
                    for kk in range(TOP_K):
                        pltpu.sync_copy(x_vmem, xs_hbm[q].at[i_vmem[kk].at[0]])

                _sc_pipeline(rows_body, n_tok, [pl.BlockSpec((win, wid), lambda i, q=q: (i, q))] + _index_specs(),
                             [])(h_hbm, *i_hbm)
            for kk in range(TOP_K):
                def gate_body(g_vmem, i_vmem):
                    pltpu.sync_copy(g_vmem, gs_hbm.at[i_vmem.at[0]])

                _sc_pipeline(gate_body, n_tok,
                             [pl.BlockSpec((win, LANES), lambda i, kk=kk: (i, kk)), _index_specs()[0]],
                             [])(g_hbm, i_hbm[kk])

    outs = run(h_p, gates_p, h_s, gates_s, *dests_p, *dests_s)
    return outs[:SC_SLABS], outs[SC_SLABS]


def _expert_kernel(te_ref, na_ref, *refs):
    del te_ref
    x_refs, (gs_ref, wgu_ref, bgu_ref, wd_ref, bd_ref) = refs[:SC_SLABS], refs[SC_SLABS:SC_SLABS + 5]
    y_refs = refs[SC_SLABS + 5:]

    @pl.when(pl.program_id(0) < na_ref[0])
    def _():
        x = jnp.concatenate([r[...] for r in x_refs], axis=1).astype(BF16)
        hmid = _dot(x, wgu_ref[0]) + bgu_ref[0]
        gate = jnp.minimum(hmid[:, :D_FF], SWIGLU_LIMIT)
        up = jnp.clip(hmid[:, D_FF:], -SWIGLU_LIMIT, SWIGLU_LIMIT)
        act = (up + 1.0) * gate * jax.nn.sigmoid(SWIGLU_ALPHA * gate)
        y = _dot(act.astype(BF16), wd_ref[0]) + bd_ref[0]
        y = y * jnp.concatenate([gs_ref[...]] * (D_MODEL // LANES), axis=1)
        packed = pltpu.pack_elementwise([y[:, :PACK_WIDTH], y[:, PACK_WIDTH:]], packed_dtype=BF16)
        for q in range(PACK_SLABS):
            y_refs[q][...] = packed[:, q * SLAB_WIDTH:(q + 1) * SLAB_WIDTH]


def _experts(tile_e, n_active, xs, gs, wgu, bgu, wd, bd):
    tm = ROW_TILE
    wid = D_MODEL // SC_SLABS
    n_tiles = gs.shape[0] // tm
    row_map = lambda i, te, na: (jnp.minimum(i, na[0] - 1), 0)
    w_map = lambda i, te, na: (te[i], 0, 0)
    return pl.pallas_call(
        _expert_kernel,
        grid_spec=pltpu.PrefetchScalarGridSpec(
            num_scalar_prefetch=2,
            grid=(n_tiles,),
            in_specs=[pl.BlockSpec((tm, wid), row_map)] * SC_SLABS + [
                pl.BlockSpec((tm, LANES), row_map),
                pl.BlockSpec((1, D_MODEL, 2 * D_FF), w_map),
                pl.BlockSpec((1, 1, 2 * D_FF), w_map),
                pl.BlockSpec((1, D_FF, D_MODEL), w_map),
                pl.BlockSpec((1, 1, D_MODEL), w_map),
            ],
            out_specs=[pl.BlockSpec((tm, SLAB_WIDTH), row_map)] * PACK_SLABS,
        ),
        out_shape=[jax.ShapeDtypeStruct((n_tiles * tm, SLAB_WIDTH), jnp.uint32)] * PACK_SLABS,
        compiler_params=pltpu.CompilerParams(
            dimension_semantics=("arbitrary",), vmem_limit_bytes=VMEM_LIMIT),
        name="experts",
    )(tile_e, n_active, *xs, gs, wgu, bgu, wd, bd)


def _sc_combine(ys, dests_p, dests_s):
    n_out = TOP_K * PACK_SLABS
    tok_p, tok_s = dests_p[0].shape[1], dests_s[0].shape[1]
    mesh = plsc.VectorSubcoreMesh(core_axis_name="core", subcore_axis_name="subcore")
    out_type = ([jax.ShapeDtypeStruct((tok_p, SLAB_WIDTH), jnp.uint32)] * n_out
                + [jax.ShapeDtypeStruct((tok_s, SLAB_WIDTH), jnp.uint32)] * n_out)

    @functools.partial(pl.kernel, out_type=out_type, mesh=mesh, scratch_types=[], name="sc_combine")
    def run(*refs):
        ys_hbm = refs[:PACK_SLABS]
        ip_hbm = refs[PACK_SLABS:PACK_SLABS + TOP_K]
        is_hbm = refs[PACK_SLABS + TOP_K:PACK_SLABS + 2 * TOP_K]
        fp_hbm = refs[PACK_SLABS + 2 * TOP_K:PACK_SLABS + 2 * TOP_K + n_out]
        fs_hbm = refs[PACK_SLABS + 2 * TOP_K + n_out:]

        for i_hbm, f_hbm, n_tok in ((ip_hbm, fp_hbm, tok_p), (is_hbm, fs_hbm, tok_s)):
            for kk in range(TOP_K):
                for q in range(PACK_SLABS):
                    def body(i_vmem, o_vmem, q=q):
                        pltpu.sync_copy(ys_hbm[q].at[i_vmem.at[0]], o_vmem)

                    _sc_pipeline(body, n_tok, [_index_specs()[0]],
                                 [pl.BlockSpec((SC_WINDOW, SLAB_WIDTH), lambda i: (i, 0))]
                                 )(i_hbm[kk], f_hbm[kk * PACK_SLABS + q])

    outs = run(*ys, *dests_p, *dests_s)
    return outs[:n_out], outs[n_out:]


def _final_kernel(*refs):
    n_in = TOP_K * PACK_SLABS
    h_ref, p_refs, (g_ref, b_ref, out_ref) = refs[0], refs[1:1 + n_in], refs[1 + n_in:]
    halves = []
    for index in range(2):
        for q in range(PACK_SLABS):
            parts = [pltpu.unpack_elementwise(p_refs[kk * PACK_SLABS + q][...], index=index,
                                              packed_dtype=BF16, unpacked_dtype=F32) for kk in range(TOP_K)]
            halves.append((parts[0] + parts[1]) + (parts[2] + parts[3]))
    f = jnp.concatenate(halves, axis=1)
    out_ref[...] = _layer_norm(DN_ALPHA * h_ref[...] + f, g_ref[...], b_ref[...])


def _final_norm(h, picked, ln_g, ln_b):
    tt = FINAL_TILE
    tok = h.shape[0]
    row = lambda w: pl.BlockSpec((tt, w), lambda i: (i, 0))
    return pl.pallas_call(
        _final_kernel,
        grid=(tok // tt,),
        in_specs=[row(D_MODEL)] + [row(SLAB_WIDTH)] * (TOP_K * PACK_SLABS)
        + [_const_spec((1, D_MODEL)), _const_spec((1, D_MODEL))],
        out_specs=row(D_MODEL),
        out_shape=jax.ShapeDtypeStruct((tok, D_MODEL), F32),
        compiler_params=pltpu.CompilerParams(dimension_semantics=("arbitrary",)),
        name="final_norm",
    )(h, *picked, ln_g, ln_b)


def _pad_heads_last(w):
    lead = w.shape[:-1]
    w = w.reshape(lead + (N_KV_HEADS, Q_GROUP, HEAD_DIM))
    zero = jnp.zeros_like(w[..., 0, :, :])
    parts = []
    for hk in range(N_KV_HEADS):
        cols = [zero] * N_KV_HEADS
        cols[hk] = w[..., hk, :, :]
        parts.append(jnp.concatenate(cols, axis=-1))
    return jnp.stack(parts, axis=-3).reshape(lead + (QPAD_WIDTH,))


def _mixer_weights(w_in, b_in, ln_g, ln_b, w_s, b_s, w_o, b_o, ln1_g, ln1_b, w_router, b_router):
    win = jnp.concatenate([_pad_heads_last(w_in[:, :ATTN_WIDTH]), w_in[:, ATTN_WIDTH:]], axis=1).astype(BF16)
    bin_ = jnp.concatenate([_pad_heads_last(b_in[:ATTN_WIDTH]), b_in[ATTN_WIDTH:]])[None, :]
    woa = _pad_heads_last(w_o[:ATTN_WIDTH].T).T.astype(BF16)
    wog = w_o[ATTN_WIDTH:].astype(BF16)
    wr = jnp.pad(w_router, ((0, 0), (0, LANES - N_EXPERTS)))
    wrh = wr.astype(BF16)
    wrl = (wr - wrh.astype(F32)).astype(BF16)
    br = jnp.pad(b_router, (0, LANES - N_EXPERTS))[None, :]
    tri = (lax.broadcasted_iota(I32, (MIX_TILE, MIX_TILE), 1)
           < lax.broadcasted_iota(I32, (MIX_TILE, MIX_TILE), 0)).astype(BF16)
    return (win, bin_, ln_g.reshape(1, GMLP_WIDTH), ln_b.reshape(1, GMLP_WIDTH), w_s, b_s.T,
            woa, wog, b_o[None, :], ln1_g[None, :], ln1_b[None, :], wrh, wrl, br, tri)


def _dest_rows(meta, pstart):
    e = meta[:, META_E:META_E + TOP_K].astype(I32)
    r = meta[:, META_R:META_R + TOP_K].astype(I32)
    return pstart[e] + r


def kernel(x_prompt, x_sample, cache_k, cache_v, w_in, b_in, attn_sinks, gmlp_ln_g, gmlp_ln_b, w_spatial, b_spatial, w_o, b_o, ln1_g, ln1_b, w_router, b_router, w_gate_up, b_gate_up, w_down, b_down, ln2_g, ln2_b):
    assert w_in.shape[0] == DEPTH
    batch, seq, _ = x_prompt.shape
    dec_batch, dec_seq, _ = x_sample.shape
    tok_p, tok_s = batch * seq, dec_batch * dec_seq
    assert dec_seq == CHUNK and seq % MIX_TILE == 0
    assert all(t % MIX_TILE == 0 and t % FINAL_TILE == 0 and t % SC_WINDOW == 0 for t in (tok_p, tok_s))

    weights = _mixer_weights(w_in[0], b_in[0], gmlp_ln_g[0], gmlp_ln_b[0], w_spatial[0], b_spatial[0],
                             w_o[0], b_o[0], ln1_g[0], ln1_b[0], w_router[0], b_router[0])
    sinks = attn_sinks[0]
    h_p, meta_p, gates_p, cnt_p, kt, vt = _mix_prompt(sinks, x_prompt, weights)
    h_s, meta_s, gates_s, cnt, ks, vs, gvs = _mix_sample(
        sinks, x_sample.reshape(tok_s, D_MODEL),
        cache_k[0].reshape(dec_batch * WINDOW, KV_WIDTH), cache_v[0].reshape(dec_batch * WINDOW, KV_WIDTH),
        cnt_p, weights)

    n_assign = (tok_p + tok_s) * TOP_K
    n_tiles = (n_assign + N_EXPERTS * (ROW_TILE - 1)) // ROW_TILE
    counts = cnt[0, :N_EXPERTS].astype(I32)
    padded = (counts + ROW_TILE - 1) // ROW_TILE * ROW_TILE
    pend = jnp.cumsum(padded)
    pstart = pend - padded
    n_active = (pend[-1:] // ROW_TILE).astype(I32)
    tile_start = jnp.arange(n_tiles, dtype=I32) * ROW_TILE
    tile_e = jnp.minimum(jnp.sum(pend[None, :] <= tile_start[:, None], axis=1), N_EXPERTS - 1).astype(I32)

    rows_p, rows_s = _dest_rows(meta_p, pstart), _dest_rows(meta_s, pstart)
    lists = lambda d: [d[:, kk][None, :] for kk in range(TOP_K)]
    dests_p, dests_s = lists(rows_p), lists(rows_s)
    xs, gs = _sc_dispatch(dests_p, h_p, gates_p, dests_s, h_s, gates_s, n_tiles * ROW_TILE)

    ys = _experts(tile_e, n_active, xs, gs, w_gate_up[0].astype(BF16), b_gate_up[0][:, None, :],
                  w_down[0].astype(BF16), b_down[0][:, None, :])

    f_p, f_s = _sc_combine(ys, dests_p, dests_s)
    g2, b2 = ln2_g[0][None, :], ln2_b[0][None, :]
    y_p = _final_norm(h_p, f_p, g2, b2).reshape(batch, seq, D_MODEL)
    y_s = _final_norm(h_s, f_s, g2, b2).reshape(dec_batch, dec_seq, D_MODEL)

    kv5 = lambda a, nb, rows: a.reshape(DEPTH, nb, rows, N_KV_HEADS, HEAD_DIM)
    return (y_p, y_s, kv5(kt, batch, WINDOW), kv5(vt, batch, WINDOW),
            kv5(ks, dec_batch, dec_seq), kv5(vs, dec_batch, dec_seq),
            gvs.reshape(DEPTH, dec_batch, dec_seq, GMLP_GROUPS, GMLP_GROUP_DIM))
```

```python
import functools

import jax
import jax.numpy as jnp
from jax import lax
from jax.experimental import pallas as pl
from jax.experimental.pallas import tpu as pltpu
from jax.experimental.pallas import tpu_sc as plsc

F32 = jnp.float32
BF16 = jnp.bfloat16
I32 = jnp.int32

D_MODEL = 1024
CHUNK = 64
N_HEADS = 8
N_KV_HEADS = 2
HEAD_DIM = 64
Q_GROUP = N_HEADS // N_KV_HEADS
KV_WIDTH = N_KV_HEADS * HEAD_DIM
WINDOW = 128
KEYS = WINDOW + CHUNK
GMLP_GROUPS = 4
GMLP_GROUP_DIM = 128
GMLP_WIDTH = GMLP_GROUPS * GMLP_GROUP_DIM
GMLP_CHUNK = 128
ATTN_WIDTH = N_HEADS * HEAD_DIM
N_EXPERTS = 32
TOP_K = 4
D_FF = 1024
SWIGLU_LIMIT = 7.0
SWIGLU_ALPHA = 1.702
DEPTH = 1
DN_ALPHA = (2 * DEPTH) ** 0.25
LN_EPS = 1e-5
NEG_INF = -1e30

LANES = 128
QPAD_WIDTH = N_HEADS * LANES
K_OFF = QPAD_WIDTH
V_OFF = K_OFF + KV_WIDTH
U_OFF = V_OFF + KV_WIDTH
GV_OFF = U_OFF + GMLP_WIDTH
ZPAD_WIDTH = GV_OFF + GMLP_WIDTH

MIX_TILE = 512
ROW_TILE = 512
FINAL_TILE = 512
SC_WINDOW = 128
SC_SLABS = 4
SLAB_WIDTH = D_MODEL // SC_SLABS
PACK_WIDTH = D_MODEL // 2
PACK_SLABS = PACK_WIDTH // SLAB_WIDTH
VMEM_LIMIT = 56 * 1024 * 1024

META_E, META_R = 0, 4


def _gelu_tanh(x):
    return 0.5 * x * (1.0 + jnp.tanh(0.7978845608028654 * (x + 0.044715 * x * x * x)))


def _layer_norm(x, g, b):
    mu = jnp.mean(x, axis=-1, keepdims=True)
    xc = x - mu
    var = jnp.mean(xc * xc, axis=-1, keepdims=True)
    return xc * lax.rsqrt(var + LN_EPS) * g + b


def _dot(a, b):
    return jnp.dot(a, b, preferred_element_type=F32)


def _mixer_kernel(*refs, is_prompt, tt):
    n_chunks = tt // CHUNK
    it = iter(refs)
    sinks_ref = next(it)
    x_ref = next(it)
    if not is_prompt:
        ck_ref, cv_ref, base0_ref = next(it), next(it), next(it)
    (win_ref, bin_ref, lng_ref, lnb_ref, ws_ref, bs_ref, woa_ref, wog_ref, bo_ref, l1g_ref, l1b_ref,
     wrh_ref, wrl_ref, br_ref, tri_ref) = (next(it) for _ in range(15))
    h_ref, hw_ref, meta_ref, gate_ref, cnt_ref = (next(it) for _ in range(5))
    if is_prompt:
        kt_ref, vt_ref = next(it), next(it)
    else:
        ko_ref, vo_ref, gvo_ref = next(it), next(it), next(it)
    z_ref, kext_ref, vext_ref, att_ref, gm_ref, base_ref = (next(it) for _ in range(6))

    if is_prompt:
        first = (pl.program_id(0) == 0) & (pl.program_id(1) == 0)
        tile_in_seq = pl.program_id(1)

        @pl.when(first)
        def _():
            base_ref[...] = jnp.zeros_like(base_ref)

        @pl.when(tile_in_seq == 0)
        def _():
            kext_ref[0:WINDOW, :] = jnp.zeros((WINDOW, KV_WIDTH), BF16)
            vext_ref[0:WINDOW, :] = jnp.zeros((WINDOW, KV_WIDTH), BF16)
    else:
        @pl.when(pl.program_id(0) == 0)
        def _():
            base_ref[...] = base0_ref[...]

    x = x_ref[...]
    z_ref[...] = _dot(x.astype(BF16), win_ref[...]) + bin_ref[...]

    k = z_ref[:, K_OFF:K_OFF + KV_WIDTH]
    v = z_ref[:, V_OFF:V_OFF + KV_WIDTH]
    if is_prompt:
        kext_ref[WINDOW:WINDOW + tt, :] = k.astype(BF16)
        vext_ref[WINDOW:WINDOW + tt, :] = v.astype(BF16)
        kt_ref[...] = k[tt - WINDOW:, :]
        vt_ref[...] = v[tt - WINDOW:, :]
        key_stride = CHUNK
    else:
        ko_ref[...] = k
        vo_ref[...] = v
        for c in range(n_chunks):
            kext_ref[KEYS * c:KEYS * c + WINDOW, :] = ck_ref[WINDOW * c:WINDOW * (c + 1), :].astype(BF16)
            vext_ref[KEYS * c:KEYS * c + WINDOW, :] = cv_ref[WINDOW * c:WINDOW * (c + 1), :].astype(BF16)
            kext_ref[KEYS * c + WINDOW:KEYS * (c + 1), :] = k[CHUNK * c:CHUNK * (c + 1), :].astype(BF16)
            vext_ref[KEYS * c + WINDOW:KEYS * (c + 1), :] = v[CHUNK * c:CHUNK * (c + 1), :].astype(BF16)
        key_stride = KEYS

    rows = Q_GROUP * CHUNK
    row_i = lax.broadcasted_iota(I32, (rows, 1), 0)
    key_i = lax.broadcasted_iota(I32, (rows, KEYS), 1)
    sink_cols = []
    for hk in range(N_KV_HEADS):
        s = [sinks_ref[hk * Q_GROUP + g] for g in range(Q_GROUP)]
        sink_cols.append(jnp.where(row_i < CHUNK, s[0],
                                   jnp.where(row_i < 2 * CHUNK, s[1],
                                             jnp.where(row_i < 3 * CHUNK, s[2], s[3]))))
    for c in range(n_chunks):
        kc = kext_ref[key_stride * c:key_stride * c + KEYS, :]
        vc = vext_ref[key_stride * c:key_stride * c + KEYS, :]
        outs = []
        for hk in range(N_KV_HEADS):
            qs = jnp.concatenate(
                [z_ref[CHUNK * c:CHUNK * (c + 1), (hk * Q_GROUP + g) * LANES:(hk * Q_GROUP + g + 1) * LANES]
                 for g in range(Q_GROUP)], axis=0).astype(BF16)
            sc = lax.dot_general(qs, kc, (((1,), (1,)), ((), ())), preferred_element_type=F32)
            sc = sc * (HEAD_DIM ** -0.5)
            if is_prompt and c < WINDOW // CHUNK:
                valid = (key_i >= WINDOW - CHUNK * c) | (tile_in_seq > 0)
                sc = jnp.where(valid, sc, NEG_INF)
            sink = sink_cols[hk]
            m = jnp.maximum(jnp.max(sc, axis=-1, keepdims=True), sink)
            p = jnp.exp(sc - m)
            den = jnp.sum(p, axis=-1, keepdims=True) + jnp.exp(sink - m)
            o = _dot(p.astype(BF16), vc) / den
            outs.extend(o[CHUNK * g:CHUNK * (g + 1), :] for g in range(Q_GROUP))
        att_ref[CHUNK * c:CHUNK * (c + 1), :] = jnp.concatenate(outs, axis=1).astype(BF16)

    if is_prompt:
        kext_ref[0:WINDOW, :] = kext_ref[tt:tt + WINDOW, :]
        vext_ref[0:WINDOW, :] = vext_ref[tt:tt + WINDOW, :]

    gc = GMLP_CHUNK if is_prompt else CHUNK
    ri = lax.broadcasted_iota(I32, (gc, gc), 0)
    ci = lax.broadcasted_iota(I32, (gc, gc), 1)
    causal = (ci // CHUNK) <= (ri // CHUNK)
    for g in range(GMLP_GROUPS):
        lo, hi = g * GMLP_GROUP_DIM, (g + 1) * GMLP_GROUP_DIM
        u = _gelu_tanh(z_ref[:, U_OFF + lo:U_OFF + hi])
        gv = _layer_norm(_gelu_tanh(z_ref[:, GV_OFF + lo:GV_OFF + hi]), lng_ref[:, lo:hi], lnb_ref[:, lo:hi])
        if not is_prompt:
            gvo_ref[:, lo:hi] = gv
        gvb = gv.astype(BF16)
        wm = jnp.where(causal, ws_ref[g, 0:gc, 0:gc], 0.0).astype(BF16)
        bcol = bs_ref[0:gc, g:g + 1]
        for n in range(tt // gc):
            sp = _dot(wm, gvb[gc * n:gc * (n + 1), :]) + bcol
            gm_ref[gc * n:gc * (n + 1), lo:hi] = (u[gc * n:gc * (n + 1), :] * sp).astype(BF16)

    y = _dot(att_ref[...], woa_ref[...]) + _dot(gm_ref[...], wog_ref[...]) + bo_ref[...]
    h = _layer_norm(DN_ALPHA * x + y, l1g_ref[...], l1b_ref[...])
    h_ref[...] = h
    hw_ref[...] = pltpu.pack_elementwise([h[:, :PACK_WIDTH], h[:, PACK_WIDTH:]], packed_dtype=BF16)

    h_hi = h.astype(BF16)
    h_lo = (h - h_hi.astype(F32)).astype(BF16)
    logits = _dot(h_hi, wrh_ref[...]) + _dot(h_lo, wrh_ref[...]) + _dot(h_hi, wrl_ref[...]) + br_ref[...]
    lane = lax.broadcasted_iota(I32, (tt, LANES), 1)
    lane_f = lane.astype(F32)
    l = jnp.where(lane < N_EXPERTS, logits, -jnp.inf)
    tops, idxs, hots = [], [], []
    for _ in range(TOP_K):
        m = jnp.max(l, axis=-1, keepdims=True)
        idx = jnp.min(jnp.where(l == m, lane_f, float(LANES)), axis=-1, keepdims=True)
        hot = lane_f == idx
        l = jnp.where(hot, -jnp.inf, l)
        tops.append(m)
        idxs.append(idx)
        hots.append(hot)
    es = [jnp.exp(t - tops[0]) for t in tops]
    esum = es[0] + es[1] + es[2] + es[3]
    chosen = jnp.where(hots[0] | hots[1] | hots[2] | hots[3], 1.0, 0.0)
    before = _dot(tri_ref[...], chosen.astype(BF16)) + base_ref[...]
    meta = jnp.zeros((tt, LANES), F32)
    for kk in range(TOP_K):
        rank = jnp.sum(jnp.where(hots[kk], before, 0.0), axis=-1, keepdims=True)
        meta = jnp.where(lane == META_E + kk, idxs[kk], meta)
        meta = jnp.where(lane == META_R + kk, rank, meta)
        gate_ref[:, kk * LANES:(kk + 1) * LANES] = jnp.broadcast_to(es[kk] / esum, (tt, LANES))
    meta_ref[...] = meta
    base_ref[...] = base_ref[...] + jnp.sum(chosen, axis=0, keepdims=True)
    cnt_ref[...] = base_ref[...]


def _const_spec(shape):
    nd = len(shape)
    return pl.BlockSpec(shape, lambda *_: (0,) * nd)


def _mixer_weight_specs(tt):
    return [
        _const_spec((D_MODEL, ZPAD_WIDTH)), _const_spec((1, ZPAD_WIDTH)),
        _const_spec((1, GMLP_WIDTH)), _const_spec((1, GMLP_WIDTH)),
        _const_spec((GMLP_GROUPS, GMLP_CHUNK, GMLP_CHUNK)), _const_spec((GMLP_CHUNK, GMLP_GROUPS)),
        _const_spec((QPAD_WIDTH, D_MODEL)), _const_spec((GMLP_WIDTH, D_MODEL)), _const_spec((1, D_MODEL)),
        _const_spec((1, D_MODEL)), _const_spec((1, D_MODEL)),
        _const_spec((D_MODEL, LANES)), _const_spec((D_MODEL, LANES)), _const_spec((1, LANES)),
        _const_spec((tt, tt)),
    ]


def _mixer_scratch(tt, kext_rows):
    return [
        pltpu.VMEM((tt, ZPAD_WIDTH), F32),
        pltpu.VMEM((kext_rows, KV_WIDTH), BF16), pltpu.VMEM((kext_rows, KV_WIDTH), BF16),
        pltpu.VMEM((tt, QPAD_WIDTH), BF16), pltpu.VMEM((tt, GMLP_WIDTH), BF16),
        pltpu.VMEM((1, LANES), F32),
    ]


def _mix_prompt(sinks, x, weights):
    batch, seq, _ = x.shape
    tt = MIX_TILE
    n_tiles = seq // tt
    tok = batch * seq
    smem = pl.BlockSpec(memory_space=pltpu.SMEM)
    return pl.pallas_call(
        functools.partial(_mixer_kernel, is_prompt=True, tt=tt),
        grid=(batch, n_tiles),
        in_specs=[smem, pl.BlockSpec((None, tt, D_MODEL), lambda b, i: (b, i, 0))] + _mixer_weight_specs(tt),
        out_specs=[
            pl.BlockSpec((tt, D_MODEL), lambda b, i: (b * n_tiles + i, 0)),
            pl.BlockSpec((tt, PACK_WIDTH), lambda b, i: (b * n_tiles + i, 0)),
            pl.BlockSpec((tt, LANES), lambda b, i: (b * n_tiles + i, 0)),
            pl.BlockSpec((tt, TOP_K * LANES), lambda b, i: (b * n_tiles + i, 0)),
            pl.BlockSpec((1, LANES), lambda b, i: (0, 0)),
            pl.BlockSpec((WINDOW, KV_WIDTH), lambda b, i: (b, 0)),
            pl.BlockSpec((WINDOW, KV_WIDTH), lambda b, i: (b, 0)),
        ],
        out_shape=[
            jax.ShapeDtypeStruct((tok, D_MODEL), F32), jax.ShapeDtypeStruct((tok, PACK_WIDTH), jnp.uint32),
            jax.ShapeDtypeStruct((tok, LANES), F32), jax.ShapeDtypeStruct((tok, TOP_K * LANES), F32),
            jax.ShapeDtypeStruct((1, LANES), F32),
            jax.ShapeDtypeStruct((batch * WINDOW, KV_WIDTH), F32),
            jax.ShapeDtypeStruct((batch * WINDOW, KV_WIDTH), F32),
        ],
        scratch_shapes=_mixer_scratch(tt, WINDOW + tt),
        compiler_params=pltpu.CompilerParams(
            dimension_semantics=("arbitrary", "arbitrary"), vmem_limit_bytes=VMEM_LIMIT),
        name="mix_prompt",
    )(sinks, x, *weights)


def _mix_sample(sinks, x2, ck, cv, base0, weights):
    tok = x2.shape[0]
    tt = MIX_TILE
    n_chunks = tt // CHUNK
    cache_rows = n_chunks * WINDOW
    smem = pl.BlockSpec(memory_space=pltpu.SMEM)
    row = lambda w: pl.BlockSpec((tt, w), lambda i: (i, 0))
    return pl.pallas_call(
        functools.partial(_mixer_kernel, is_prompt=False, tt=tt),
        grid=(tok // tt,),
        in_specs=[smem, row(D_MODEL),
                  pl.BlockSpec((cache_rows, KV_WIDTH), lambda i: (i, 0)),
                  pl.BlockSpec((cache_rows, KV_WIDTH), lambda i: (i, 0)),
                  _const_spec((1, LANES))] + _mixer_weight_specs(tt),
        out_specs=[row(D_MODEL), row(PACK_WIDTH), row(LANES), row(TOP_K * LANES),
                   pl.BlockSpec((1, LANES), lambda i: (0, 0)),
                   row(KV_WIDTH), row(KV_WIDTH), row(GMLP_WIDTH)],
        out_shape=[
            jax.ShapeDtypeStruct((tok, D_MODEL), F32), jax.ShapeDtypeStruct((tok, PACK_WIDTH), jnp.uint32),
            jax.ShapeDtypeStruct((tok, LANES), F32), jax.ShapeDtypeStruct((tok, TOP_K * LANES), F32),
            jax.ShapeDtypeStruct((1, LANES), F32),
            jax.ShapeDtypeStruct((tok, KV_WIDTH), F32), jax.ShapeDtypeStruct((tok, KV_WIDTH), F32),
            jax.ShapeDtypeStruct((tok, GMLP_WIDTH), F32),
        ],
        scratch_shapes=_mixer_scratch(tt, n_chunks * KEYS),
        compiler_params=pltpu.CompilerParams(
            dimension_semantics=("arbitrary",), vmem_limit_bytes=VMEM_LIMIT),
        name="mix_sample",
    )(sinks, x2, ck, cv, base0, *weights)


def _sc_pipeline(body, n_tok, in_specs, out_specs):
    return pltpu.emit_pipeline(
        body, grid=(n_tok // SC_WINDOW,), in_specs=in_specs, out_specs=out_specs,
        core_axis_name=("core", "subcore"), dimension_semantics=(pltpu.PARALLEL,))


def _index_specs():
    return [pl.BlockSpec((1, SC_WINDOW), lambda i: (0, i))] * TOP_K


def _sc_dispatch(dests_p, h_p, gates_p, dests_s, h_s, gates_s, n_rows):
    win, wid = SC_WINDOW, SLAB_WIDTH
    mesh = plsc.VectorSubcoreMesh(core_axis_name="core", subcore_axis_name="subcore")
    out_type = ([jax.ShapeDtypeStruct((n_rows, wid), jnp.uint32)] * PACK_SLABS
                + [jax.ShapeDtypeStruct((n_rows, LANES), F32)])

    @functools.partial(pl.kernel, out_type=out_type, mesh=mesh, scratch_types=[], name="sc_dispatch")
    def run(*refs):
        hp_hbm, gp_hbm, hs_hbm, gs_in_hbm = refs[:4]
        ip_hbm, is_hbm = refs[4:4 + TOP_K], refs[4 + TOP_K:4 + 2 * TOP_K]
        xs_hbm = refs[4 + 2 * TOP_K:4 + 2 * TOP_K + PACK_SLABS]
        gs_hbm = refs[4 + 2 * TOP_K + PACK_SLABS]

        for h_hbm, g_hbm, i_hbm in ((hp_hbm, gp_hbm, ip_hbm), (hs_hbm, gs_in_hbm, is_hbm)):
            n_tok = h_hbm.shape[0]
            for q in range(PACK_SLABS):
                def rows_body(x_vmem, *i_vmem, q=q):
                    for kk in range(TOP_K):
                        pltpu.sync_copy(x_vmem, xs_hbm[q].at[i_vmem[kk].at[0]])

                _sc_pipeline(rows_body, n_tok, [pl.BlockSpec((win, wid), lambda i, q=q: (i, q))] + _index_specs(),
                             [])(h_hbm, *i_hbm)
            for kk in range(TOP_K):
                def gate_body(g_vmem, i_vmem):
                    pltpu.sync_copy(g_vmem, gs_hbm.at[i_vmem.at[0]])

                _sc_pipeline(gate_body, n_tok,
                             [pl.BlockSpec((win, LANES), lambda i, kk=kk: (i, kk)), _index_specs()[0]],
                             [])(g_hbm, i_hbm[kk])

    outs = run(h_p, gates_p, h_s, gates_s, *dests_p, *dests_s)
    return outs[:PACK_SLABS], outs[PACK_SLABS]


def _expert_kernel(te_ref, na_ref, *refs):
    del te_ref
    x_refs, (gs_ref, wgu_ref, bgu_ref, wd_ref, bd_ref) = refs[:PACK_SLABS], refs[PACK_SLABS:PACK_SLABS + 5]
    y_refs = refs[PACK_SLABS + 5:]

    @pl.when(pl.program_id(0) < na_ref[0])
    def _():
        words = [r[...] for r in x_refs]
        x = jnp.concatenate(
            [pltpu.unpack_elementwise(w, index=index, packed_dtype=BF16, unpacked_dtype=F32)
             for index in range(2) for w in words], axis=1).astype(BF16)
        hmid = _dot(x, wgu_ref[0]) + bgu_ref[0]
        gate = jnp.minimum(hmid[:, :D_FF], SWIGLU_LIMIT)
        up = jnp.clip(hmid[:, D_FF:], -SWIGLU_LIMIT, SWIGLU_LIMIT)
        act = (up + 1.0) * gate * jax.nn.sigmoid(SWIGLU_ALPHA * gate)
        y = _dot(act.astype(BF16), wd_ref[0]) + bd_ref[0]
        y = y * jnp.concatenate([gs_ref[...]] * (D_MODEL // LANES), axis=1)
        packed = pltpu.pack_elementwise([y[:, :PACK_WIDTH], y[:, PACK_WIDTH:]], packed_dtype=BF16)
        for q in range(PACK_SLABS):
            y_refs[q][...] = packed[:, q * SLAB_WIDTH:(q + 1) * SLAB_WIDTH]


def _experts(tile_e, n_active, xs, gs, wgu, bgu, wd, bd):
    tm = ROW_TILE
    n_tiles = gs.shape[0] // tm
    row_map = lambda i, te, na: (jnp.minimum(i, na[0] - 1), 0)
    w_map = lambda i, te, na: (te[i], 0, 0)
    return pl.pallas_call(
        _expert_kernel,
        grid_spec=pltpu.PrefetchScalarGridSpec(
            num_scalar_prefetch=2,
            grid=(n_tiles,),
            in_specs=[pl.BlockSpec((tm, SLAB_WIDTH), row_map)] * PACK_SLABS + [
                pl.BlockSpec((tm, LANES), row_map),
                pl.BlockSpec((1, D_MODEL, 2 * D_FF), w_map),
                pl.BlockSpec((1, 1, 2 * D_FF), w_map),
                pl.BlockSpec((1, D_FF, D_MODEL), w_map),
                pl.BlockSpec((1, 1, D_MODEL), w_map),
            ],
            out_specs=[pl.BlockSpec((tm, SLAB_WIDTH), row_map)] * PACK_SLABS,
        ),
        out_shape=[jax.ShapeDtypeStruct((n_tiles * tm, SLAB_WIDTH), jnp.uint32)] * PACK_SLABS,
        compiler_params=pltpu.CompilerParams(
            dimension_semantics=("arbitrary",), vmem_limit_bytes=VMEM_LIMIT),
        name="experts",
    )(tile_e, n_active, *xs, gs, wgu, bgu, wd, bd)


def _sc_combine(ys, dests_p, dests_s):
    n_out = TOP_K * PACK_SLABS
    tok_p, tok_s = dests_p[0].shape[1], dests_s[0].shape[1]
    mesh = plsc.VectorSubcoreMesh(core_axis_name="core", subcore_axis_name="subcore")
    out_type = ([jax.ShapeDtypeStruct((tok_p, SLAB_WIDTH), jnp.uint32)] * n_out
                + [jax.ShapeDtypeStruct((tok_s, SLAB_WIDTH), jnp.uint32)] * n_out)

    @functools.partial(pl.kernel, out_type=out_type, mesh=mesh, scratch_types=[], name="sc_combine")
    def run(*refs):
        ys_hbm = refs[:PACK_SLABS]
        ip_hbm = refs[PACK_SLABS:PACK_SLABS + TOP_K]
        is_hbm = refs[PACK_SLABS + TOP_K:PACK_SLABS + 2 * TOP_K]
        fp_hbm = refs[PACK_SLABS + 2 * TOP_K:PACK_SLABS + 2 * TOP_K + n_out]
        fs_hbm = refs[PACK_SLABS + 2 * TOP_K + n_out:]

        for i_hbm, f_hbm, n_tok in ((ip_hbm, fp_hbm, tok_p), (is_hbm, fs_hbm, tok_s)):
            for kk in range(TOP_K):
                for q in range(PACK_SLABS):
                    def body(i_vmem, o_vmem, q=q):
                        pltpu.sync_copy(ys_hbm[q].at[i_vmem.at[0]], o_vmem)

                    _sc_pipeline(body, n_tok, [_index_specs()[0]],
                                 [pl.BlockSpec((SC_WINDOW, SLAB_WIDTH), lambda i: (i, 0))]
                                 )(i_hbm[kk], f_hbm[kk * PACK_SLABS + q])

    outs = run(*ys, *dests_p, *dests_s)
    return outs[:n_out], outs[n_out:]


def _final_kernel(*refs):
    n_in = TOP_K * PACK_SLABS
    h_ref, p_refs, (g_ref, b_ref, out_ref) = refs[0], refs[1:1 + n_in], refs[1 + n_in:]
    halves = []
    for index in range(2):
        for q in range(PACK_SLABS):
            parts = [pltpu.unpack_elementwise(p_refs[kk * PACK_SLABS + q][...], index=index,
                                              packed_dtype=BF16, unpacked_dtype=F32) for kk in range(TOP_K)]
            halves.append((parts[0] + parts[1]) + (parts[2] + parts[3]))
    f = jnp.concatenate(halves, axis=1)
    out_ref[...] = _layer_norm(DN_ALPHA * h_ref[...] + f, g_ref[...], b_ref[...])


def _final_norm(h, picked, ln_g, ln_b):
    tt = FINAL_TILE
    tok = h.shape[0]
    row = lambda w: pl.BlockSpec((tt, w), lambda i: (i, 0))
    return pl.pallas_call(
        _final_kernel,
        grid=(tok // tt,),
        in_specs=[row(D_MODEL)] + [row(SLAB_WIDTH)] * (TOP_K * PACK_SLABS)
        + [_const_spec((1, D_MODEL)), _const_spec((1, D_MODEL))],
        out_specs=row(D_MODEL),
        out_shape=jax.ShapeDtypeStruct((tok, D_MODEL), F32),
        compiler_params=pltpu.CompilerParams(dimension_semantics=("arbitrary",)),
        name="final_norm",
    )(h, *picked, ln_g, ln_b)


def _pad_heads_last(w):
    lead = w.shape[:-1]
    w = w.reshape(lead + (N_KV_HEADS, Q_GROUP, HEAD_DIM))
    zero = jnp.zeros_like(w[..., 0, :, :])
    parts = []
    for hk in range(N_KV_HEADS):
        cols = [zero] * N_KV_HEADS
        cols[hk] = w[..., hk, :, :]
        parts.append(jnp.concatenate(cols, axis=-1))
    return jnp.stack(parts, axis=-3).reshape(lead + (QPAD_WIDTH,))


def _mixer_weights(w_in, b_in, ln_g, ln_b, w_s, b_s, w_o, b_o, ln1_g, ln1_b, w_router, b_router):
    win = jnp.concatenate([_pad_heads_last(w_in[:, :ATTN_WIDTH]), w_in[:, ATTN_WIDTH:]], axis=1).astype(BF16)
    bin_ = jnp.concatenate([_pad_heads_last(b_in[:ATTN_WIDTH]), b_in[ATTN_WIDTH:]])[None, :]
    woa = _pad_heads_last(w_o[:ATTN_WIDTH].T).T.astype(BF16)
    wog = w_o[ATTN_WIDTH:].astype(BF16)
    wr = jnp.pad(w_router, ((0, 0), (0, LANES - N_EXPERTS)))
    wrh = wr.astype(BF16)
    wrl = (wr - wrh.astype(F32)).astype(BF16)
    br = jnp.pad(b_router, (0, LANES - N_EXPERTS))[None, :]
    tri = (lax.broadcasted_iota(I32, (MIX_TILE, MIX_TILE), 1)
           < lax.broadcasted_iota(I32, (MIX_TILE, MIX_TILE), 0)).astype(BF16)
    return (win, bin_, ln_g.reshape(1, GMLP_WIDTH), ln_b.reshape(1, GMLP_WIDTH), w_s, b_s.T,
            woa, wog, b_o[None, :], ln1_g[None, :], ln1_b[None, :], wrh, wrl, br, tri)


def _dest_rows(meta, pstart):
    e = meta[:, META_E:META_E + TOP_K].astype(I32)
    r = meta[:, META_R:META_R + TOP_K].astype(I32)
    hit = e[..., None] == jnp.arange(N_EXPERTS, dtype=I32)
    return jnp.sum(jnp.where(hit, pstart, 0), axis=-1) + r


def kernel(x_prompt, x_sample, cache_k, cache_v, w_in, b_in, attn_sinks, gmlp_ln_g, gmlp_ln_b, w_spatial, b_spatial, w_o, b_o, ln1_g, ln1_b, w_router, b_router, w_gate_up, b_gate_up, w_down, b_down, ln2_g, ln2_b):
    assert w_in.shape[0] == DEPTH
    batch, seq, _ = x_prompt.shape
    dec_batch, dec_seq, _ = x_sample.shape
    tok_p, tok_s = batch * seq, dec_batch * dec_seq
    assert dec_seq == CHUNK and seq % MIX_TILE == 0
    assert all(t % MIX_TILE == 0 and t % FINAL_TILE == 0 and t % SC_WINDOW == 0 for t in (tok_p, tok_s))

    weights = _mixer_weights(w_in[0], b_in[0], gmlp_ln_g[0], gmlp_ln_b[0], w_spatial[0], b_spatial[0],
                             w_o[0], b_o[0], ln1_g[0], ln1_b[0], w_router[0], b_router[0])
    sinks = attn_sinks[0]
    h_p, hw_p, meta_p, gates_p, cnt_p, kt, vt = _mix_prompt(sinks, x_prompt, weights)
    h_s, hw_s, meta_s, gates_s, cnt, ks, vs, gvs = _mix_sample(
        sinks, x_sample.reshape(tok_s, D_MODEL),
        cache_k[0].reshape(dec_batch * WINDOW, KV_WIDTH), cache_v[0].reshape(dec_batch * WINDOW, KV_WIDTH),
        cnt_p, weights)

    n_assign = (tok_p + tok_s) * TOP_K
    n_tiles = (n_assign + N_EXPERTS * (ROW_TILE - 1)) // ROW_TILE
    counts = cnt[0, :N_EXPERTS].astype(I32)
    padded = (counts + ROW_TILE - 1) // ROW_TILE * ROW_TILE
    pend = jnp.cumsum(padded)
    pstart = pend - padded
    n_active = (pend[-1:] // ROW_TILE).astype(I32)
    tile_start = jnp.arange(n_tiles, dtype=I32) * ROW_TILE
    tile_e = jnp.minimum(jnp.sum(pend[None, :] <= tile_start[:, None], axis=1), N_EXPERTS - 1).astype(I32)

    rows_p, rows_s = _dest_rows(meta_p, pstart), _dest_rows(meta_s, pstart)
    lists = lambda d: [d[:, kk][None, :] for kk in range(TOP_K)]
    dests_p, dests_s = lists(rows_p), lists(rows_s)
    xs, gs = _sc_dispatch(dests_p, hw_p, gates_p, dests_s, hw_s, gates_s, n_tiles * ROW_TILE)

    ys = _experts(tile_e, n_active, xs, gs, w_gate_up[0].astype(BF16), b_gate_up[0][:, None, :],
                  w_down[0].astype(BF16), b_down[0][:, None, :])

    f_p, f_s = _sc_combine(ys, dests_p, dests_s)
    g2, b2 = ln2_g[0][None, :], ln2_b[0][None, :]
    y_p = _final_norm(h_p, f_p, g2, b2).reshape(batch, seq, D_MODEL)
    y_s = _final_norm(h_s, f_s, g2, b2).reshape(dec_batch, dec_seq, D_MODEL)

    kv5 = lambda a, nb, rows: a.reshape(DEPTH, nb, rows, N_KV_HEADS, HEAD_DIM)
    return (y_p, y_s, kv5(kt, batch, WINDOW), kv5(vt, batch, WINDOW),
            kv5(ks, dec_batch, dec_seq), kv5(vs, dec_batch, dec_seq),
            gvs.reshape(DEPTH, dec_batch, dec_seq, GMLP_GROUPS, GMLP_GROUP_DIM))
```

```python
import functools

import jax
import jax.numpy as jnp
from jax import lax
from jax.experimental import pallas as pl
from jax.experimental.pallas import tpu as pltpu
from jax.experimental.pallas import tpu_sc as plsc

F32 = jnp.float32
BF16 = jnp.bfloat16
I32 = jnp.int32

D_MODEL = 1024
CHUNK = 64
N_HEADS = 8
N_KV_HEADS = 2
HEAD_DIM = 64
Q_GROUP = N_HEADS // N_KV_HEADS
KV_WIDTH = N_KV_HEADS * HEAD_DIM
WINDOW = 128
KEYS = WINDOW + CHUNK
GMLP_GROUPS = 4
GMLP_GROUP_DIM = 128
GMLP_WIDTH = GMLP_GROUPS * GMLP_GROUP_DIM
GMLP_CHUNK = 128
ATTN_WIDTH = N_HEADS * HEAD_DIM
N_EXPERTS = 32
TOP_K = 4
D_FF = 1024
SWIGLU_LIMIT = 7.0
SWIGLU_ALPHA = 1.702
DEPTH = 1
DN_ALPHA = (2 * DEPTH) ** 0.25
LN_EPS = 1e-5
NEG_INF = -1e30

LANES = 128
QPAD_WIDTH = N_HEADS * LANES
K_OFF = QPAD_WIDTH
V_OFF = K_OFF + KV_WIDTH
U_OFF = V_OFF + KV_WIDTH
GV_OFF = U_OFF + GMLP_WIDTH
ZPAD_WIDTH = GV_OFF + GMLP_WIDTH

MIX_TILE = 512
ROW_TILE = 512
FINAL_TILE = 512
SC_WINDOW = 128
SC_SLABS = 4
SLAB_WIDTH = D_MODEL // SC_SLABS
PACK_WIDTH = D_MODEL // 2
PACK_SLABS = PACK_WIDTH // SLAB_WIDTH
VMEM_LIMIT = 56 * 1024 * 1024

META_E, META_R = 0, 4


def _gelu_tanh(x):
    return 0.5 * x * (1.0 + jnp.tanh(0.7978845608028654 * (x + 0.044715 * x * x * x)))


def _layer_norm(x, g, b):
    mu = jnp.mean(x, axis=-1, keepdims=True)
    xc = x - mu
    var = jnp.mean(xc * xc, axis=-1, keepdims=True)
    return xc * lax.rsqrt(var + LN_EPS) * g + b


def _dot(a, b):
    return jnp.dot(a, b, preferred_element_type=F32)


def _mixer_kernel(*refs, is_prompt, tt):
    n_chunks = tt // CHUNK
    it = iter(refs)
    sinks_ref = next(it)
    x_ref = next(it)
    if not is_prompt:
        ck_ref, cv_ref, base0_ref = next(it), next(it), next(it)
    (win_ref, bin_ref, lng_ref, lnb_ref, ws_ref, bs_ref, woa_ref, wog_ref, bo_ref, l1g_ref, l1b_ref,
     wrh_ref, wrl_ref, br_ref, tri_ref) = (next(it) for _ in range(15))
    h_ref, hw_ref, meta_ref, gate_ref, cnt_ref = (next(it) for _ in range(5))
    if is_prompt:
        kt_ref, vt_ref = next(it), next(it)
    else:
        ko_ref, vo_ref, gvo_ref = next(it), next(it), next(it)
    z_ref, kext_ref, vext_ref, att_ref, gm_ref, base_ref = (next(it) for _ in range(6))

    if is_prompt:
        first = (pl.program_id(0) == 0) & (pl.program_id(1) == 0)
        tile_in_seq = pl.program_id(1)

        @pl.when(first)
        def _():
            base_ref[...] = jnp.zeros_like(base_ref)

        @pl.when(tile_in_seq == 0)
        def _():
            kext_ref[0:WINDOW, :] = jnp.zeros((WINDOW, KV_WIDTH), BF16)
            vext_ref[0:WINDOW, :] = jnp.zeros((WINDOW, KV_WIDTH), BF16)
    else:
        @pl.when(pl.program_id(0) == 0)
        def _():
            base_ref[...] = base0_ref[...]

    x = x_ref[...]
    z_ref[...] = _dot(x.astype(BF16), win_ref[...]) + bin_ref[...]

    k = z_ref[:, K_OFF:K_OFF + KV_WIDTH]
    v = z_ref[:, V_OFF:V_OFF + KV_WIDTH]
    if is_prompt:
        kext_ref[WINDOW:WINDOW + tt, :] = k.astype(BF16)
        vext_ref[WINDOW:WINDOW + tt, :] = v.astype(BF16)
        kt_ref[...] = k[tt - WINDOW:, :]
        vt_ref[...] = v[tt - WINDOW:, :]
        key_stride = CHUNK
    else:
        ko_ref[...] = k
        vo_ref[...] = v
        for c in range(n_chunks):
            kext_ref[KEYS * c:KEYS * c + WINDOW, :] = ck_ref[WINDOW * c:WINDOW * (c + 1), :].astype(BF16)
            vext_ref[KEYS * c:KEYS * c + WINDOW, :] = cv_ref[WINDOW * c:WINDOW * (c + 1), :].astype(BF16)
            kext_ref[KEYS * c + WINDOW:KEYS * (c + 1), :] = k[CHUNK * c:CHUNK * (c + 1), :].astype(BF16)
            vext_ref[KEYS * c + WINDOW:KEYS * (c + 1), :] = v[CHUNK * c:CHUNK * (c + 1), :].astype(BF16)
        key_stride = KEYS

    rows = Q_GROUP * CHUNK
    row_i = lax.broadcasted_iota(I32, (rows, 1), 0)
    key_i = lax.broadcasted_iota(I32, (rows, KEYS), 1)
    sink_cols = []
    for hk in range(N_KV_HEADS):
        s = [sinks_ref[hk * Q_GROUP + g] for g in range(Q_GROUP)]
        sink_cols.append(jnp.where(row_i < CHUNK, s[0],
                                   jnp.where(row_i < 2 * CHUNK, s[1],
                                             jnp.where(row_i < 3 * CHUNK, s[2], s[3]))))
    for c in range(n_chunks):
        kc = kext_ref[key_stride * c:key_stride * c + KEYS, :]
        vc = vext_ref[key_stride * c:key_stride * c + KEYS, :]
        outs = []
        for hk in range(N_KV_HEADS):
            qs = jnp.concatenate(
                [z_ref[CHUNK * c:CHUNK * (c + 1), (hk * Q_GROUP + g) * LANES:(hk * Q_GROUP + g + 1) * LANES]
                 for g in range(Q_GROUP)], axis=0).astype(BF16)
            sc = lax.dot_general(qs, kc, (((1,), (1,)), ((), ())), preferred_element_type=F32)
            sc = sc * (HEAD_DIM ** -0.5)
            if is_prompt and c < WINDOW // CHUNK:
                valid = (key_i >= WINDOW - CHUNK * c) | (tile_in_seq > 0)
                sc = jnp.where(valid, sc, NEG_INF)
            sink = sink_cols[hk]
            m = jnp.maximum(jnp.max(sc, axis=-1, keepdims=True), sink)
            p = jnp.exp(sc - m)
            den = jnp.sum(p, axis=-1, keepdims=True) + jnp.exp(sink - m)
            o = _dot(p.astype(BF16), vc) / den
            outs.extend(o[CHUNK * g:CHUNK * (g + 1), :] for g in range(Q_GROUP))
        att_ref[CHUNK * c:CHUNK * (c + 1), :] = jnp.concatenate(outs, axis=1).astype(BF16)

    if is_prompt:
        kext_ref[0:WINDOW, :] = kext_ref[tt:tt + WINDOW, :]
        vext_ref[0:WINDOW, :] = vext_ref[tt:tt + WINDOW, :]

    gc = GMLP_CHUNK if is_prompt else CHUNK
    ri = lax.broadcasted_iota(I32, (gc, gc), 0)
    ci = lax.broadcasted_iota(I32, (gc, gc), 1)
    causal = (ci // CHUNK) <= (ri // CHUNK)
    for g in range(GMLP_GROUPS):
        lo, hi = g * GMLP_GROUP_DIM, (g + 1) * GMLP_GROUP_DIM
        u = _gelu_tanh(z_ref[:, U_OFF + lo:U_OFF + hi])
        gv = _layer_norm(_gelu_tanh(z_ref[:, GV_OFF + lo:GV_OFF + hi]), lng_ref[:, lo:hi], lnb_ref[:, lo:hi])
        if not is_prompt:
            gvo_ref[:, lo:hi] = gv
        gvb = gv.astype(BF16)
        wm = jnp.where(causal, ws_ref[g, 0:gc, 0:gc], 0.0).astype(BF16)
        bcol = bs_ref[0:gc, g:g + 1]
        for n in range(tt // gc):
            sp = _dot(wm, gvb[gc * n:gc * (n + 1), :]) + bcol
            gm_ref[gc * n:gc * (n + 1), lo:hi] = (u[gc * n:gc * (n + 1), :] * sp).astype(BF16)

    y = _dot(att_ref[...], woa_ref[...]) + _dot(gm_ref[...], wog_ref[...]) + bo_ref[...]
    h = _layer_norm(DN_ALPHA * x + y, l1g_ref[...], l1b_ref[...])
    h_ref[...] = h
    hw_ref[...] = pltpu.pack_elementwise([h[:, :PACK_WIDTH], h[:, PACK_WIDTH:]], packed_dtype=BF16)

    h_hi = h.astype(BF16)
    h_lo = (h - h_hi.astype(F32)).astype(BF16)
    logits = _dot(h_hi, wrh_ref[...]) + _dot(h_lo, wrh_ref[...]) + _dot(h_hi, wrl_ref[...]) + br_ref[...]
    lane = lax.broadcasted_iota(I32, (tt, LANES), 1)
    lane_f = lane.astype(F32)
    l = jnp.where(lane < N_EXPERTS, logits, -jnp.inf)
    tops, idxs, hots = [], [], []
    for _ in range(TOP_K):
        m = jnp.max(l, axis=-1, keepdims=True)
        idx = jnp.min(jnp.where(l == m, lane_f, float(LANES)), axis=-1, keepdims=True)
        hot = lane_f == idx
        l = jnp.where(hot, -jnp.inf, l)
        tops.append(m)
        idxs.append(idx)
        hots.append(hot)
    es = [jnp.exp(t - tops[0]) for t in tops]
    esum = es[0] + es[1] + es[2] + es[3]
    chosen = jnp.where(hots[0] | hots[1] | hots[2] | hots[3], 1.0, 0.0)
    before = _dot(tri_ref[...], chosen.astype(BF16)) + base_ref[...]
    meta = jnp.zeros((tt, LANES), F32)
    for kk in range(TOP_K):
        rank = jnp.sum(jnp.where(hots[kk], before, 0.0), axis=-1, keepdims=True)
        meta = jnp.where(lane == META_E + kk, idxs[kk], meta)
        meta = jnp.where(lane == META_R + kk, rank, meta)
        gate_ref[:, kk * LANES:(kk + 1) * LANES] = jnp.broadcast_to(es[kk] / esum, (tt, LANES))
    meta_ref[...] = meta
    base_ref[...] = base_ref[...] + jnp.sum(chosen, axis=0, keepdims=True)
    cnt_ref[...] = base_ref[...]


def _const_spec(shape):
    nd = len(shape)
    return pl.BlockSpec(shape, lambda *_: (0,) * nd)


def _mixer_weight_specs(tt):
    return [
        _const_spec((D_MODEL, ZPAD_WIDTH)), _const_spec((1, ZPAD_WIDTH)),
        _const_spec((1, GMLP_WIDTH)), _const_spec((1, GMLP_WIDTH)),
        _const_spec((GMLP_GROUPS, GMLP_CHUNK, GMLP_CHUNK)), _const_spec((GMLP_CHUNK, GMLP_GROUPS)),
        _const_spec((QPAD_WIDTH, D_MODEL)), _const_spec((GMLP_WIDTH, D_MODEL)), _const_spec((1, D_MODEL)),
        _const_spec((1, D_MODEL)), _const_spec((1, D_MODEL)),
        _const_spec((D_MODEL, LANES)), _const_spec((D_MODEL, LANES)), _const_spec((1, LANES)),
        _const_spec((tt, tt)),
    ]


def _mixer_scratch(tt, kext_rows):
    return [
        pltpu.VMEM((tt, ZPAD_WIDTH), F32),
        pltpu.VMEM((kext_rows, KV_WIDTH), BF16), pltpu.VMEM((kext_rows, KV_WIDTH), BF16),
        pltpu.VMEM((tt, QPAD_WIDTH), BF16), pltpu.VMEM((tt, GMLP_WIDTH), BF16),
        pltpu.VMEM((1, LANES), F32),
    ]


def _mix_prompt(sinks, x, weights, first_batch, batch):
    seq = x.shape[1]
    tt = MIX_TILE
    n_tiles = seq // tt
    tok = batch * seq
    smem = pl.BlockSpec(memory_space=pltpu.SMEM)
    return pl.pallas_call(
        functools.partial(_mixer_kernel, is_prompt=True, tt=tt),
        grid=(batch, n_tiles),
        in_specs=[smem, pl.BlockSpec((None, tt, D_MODEL), lambda b, i: (b + first_batch, i, 0))]
        + _mixer_weight_specs(tt),
        out_specs=[
            pl.BlockSpec((tt, D_MODEL), lambda b, i: (b * n_tiles + i, 0)),
            pl.BlockSpec((tt, PACK_WIDTH), lambda b, i: (b * n_tiles + i, 0)),
            pl.BlockSpec((tt, LANES), lambda b, i: (b * n_tiles + i, 0)),
            pl.BlockSpec((tt, TOP_K * LANES), lambda b, i: (b * n_tiles + i, 0)),
            pl.BlockSpec((1, LANES), lambda b, i: (0, 0)),
            pl.BlockSpec((WINDOW, KV_WIDTH), lambda b, i: (b, 0)),
            pl.BlockSpec((WINDOW, KV_WIDTH), lambda b, i: (b, 0)),
        ],
        out_shape=[
            jax.ShapeDtypeStruct((tok, D_MODEL), F32), jax.ShapeDtypeStruct((tok, PACK_WIDTH), jnp.uint32),
            jax.ShapeDtypeStruct((tok, LANES), F32), jax.ShapeDtypeStruct((tok, TOP_K * LANES), F32),
            jax.ShapeDtypeStruct((1, LANES), F32),
            jax.ShapeDtypeStruct((batch * WINDOW, KV_WIDTH), F32),
            jax.ShapeDtypeStruct((batch * WINDOW, KV_WIDTH), F32),
        ],
        scratch_shapes=_mixer_scratch(tt, WINDOW + tt),
        compiler_params=pltpu.CompilerParams(
            dimension_semantics=("arbitrary", "arbitrary"), vmem_limit_bytes=VMEM_LIMIT),
        name="mix_prompt",
    )(sinks, x, *weights)


def _mix_sample(sinks, x2, ck, cv, base0, weights):
    tok = x2.shape[0]
    tt = MIX_TILE
    n_chunks = tt // CHUNK
    cache_rows = n_chunks * WINDOW
    smem = pl.BlockSpec(memory_space=pltpu.SMEM)
    row = lambda w: pl.BlockSpec((tt, w), lambda i: (i, 0))
    return pl.pallas_call(
        functools.partial(_mixer_kernel, is_prompt=False, tt=tt),
        grid=(tok // tt,),
        in_specs=[smem, row(D_MODEL),
                  pl.BlockSpec((cache_rows, KV_WIDTH), lambda i: (i, 0)),
                  pl.BlockSpec((cache_rows, KV_WIDTH), lambda i: (i, 0)),
                  _const_spec((1, LANES))] + _mixer_weight_specs(tt),
        out_specs=[row(D_MODEL), row(PACK_WIDTH), row(LANES), row(TOP_K * LANES),
                   pl.BlockSpec((1, LANES), lambda i: (0, 0)),
                   row(KV_WIDTH), row(KV_WIDTH), row(GMLP_WIDTH)],
        out_shape=[
            jax.ShapeDtypeStruct((tok, D_MODEL), F32), jax.ShapeDtypeStruct((tok, PACK_WIDTH), jnp.uint32),
            jax.ShapeDtypeStruct((tok, LANES), F32), jax.ShapeDtypeStruct((tok, TOP_K * LANES), F32),
            jax.ShapeDtypeStruct((1, LANES), F32),
            jax.ShapeDtypeStruct((tok, KV_WIDTH), F32), jax.ShapeDtypeStruct((tok, KV_WIDTH), F32),
            jax.ShapeDtypeStruct((tok, GMLP_WIDTH), F32),
        ],
        scratch_shapes=_mixer_scratch(tt, n_chunks * KEYS),
        compiler_params=pltpu.CompilerParams(
            dimension_semantics=("arbitrary",), vmem_limit_bytes=VMEM_LIMIT),
        name="mix_sample",
    )(sinks, x2, ck, cv, base0, *weights)


def _sc_pipeline(body, n_tok, in_specs, out_specs):
    return pltpu.emit_pipeline(
        body, grid=(n_tok // SC_WINDOW,), in_specs=in_specs, out_specs=out_specs,
        core_axis_name=("core", "subcore"), dimension_semantics=(pltpu.PARALLEL,))


def _index_specs():
    return [pl.BlockSpec((1, SC_WINDOW), lambda i: (0, i))] * TOP_K


def _sc_dispatch(token_sets, n_rows):
    win, wid = SC_WINDOW, SLAB_WIDTH
    per_set = 2 + TOP_K
    n_in = per_set * len(token_sets)
    mesh = plsc.VectorSubcoreMesh(core_axis_name="core", subcore_axis_name="subcore")
    out_type = ([jax.ShapeDtypeStruct((n_rows, wid), jnp.uint32)] * PACK_SLABS
                + [jax.ShapeDtypeStruct((n_rows, LANES), F32)])

    @functools.partial(pl.kernel, out_type=out_type, mesh=mesh, scratch_types=[], name="sc_dispatch")
    def run(*refs):
        xs_hbm, gs_hbm = refs[n_in:n_in + PACK_SLABS], refs[n_in + PACK_SLABS]

        for s in range(len(token_sets)):
            h_hbm, g_hbm = refs[per_set * s], refs[per_set * s + 1]
            i_hbm = refs[per_set * s + 2:per_set * (s + 1)]
            n_tok = h_hbm.shape[0]
            for q in range(PACK_SLABS):
                def rows_body(x_vmem, *i_vmem, q=q):
                    for kk in range(TOP_K):
                        pltpu.sync_copy(x_vmem, xs_hbm[q].at[i_vmem[kk].at[0]])

                _sc_pipeline(rows_body, n_tok, [pl.BlockSpec((win, wid), lambda i, q=q: (i, q))] + _index_specs(),
                             [])(h_hbm, *i_hbm)
            for kk in range(TOP_K):
                def gate_body(g_vmem, i_vmem):
                    pltpu.sync_copy(g_vmem, gs_hbm.at[i_vmem.at[0]])

                _sc_pipeline(gate_body, n_tok,
                             [pl.BlockSpec((win, LANES), lambda i, kk=kk: (i, kk)), _index_specs()[0]],
                             [])(g_hbm, i_hbm[kk])

    outs = run(*[a for hw, gates, dests in token_sets for a in (hw, gates, *dests)])
    return outs[:PACK_SLABS], outs[PACK_SLABS]


def _expert_kernel(te_ref, na_ref, *refs):
    del te_ref
    x_refs, (gs_ref, wgu_ref, bgu_ref, wd_ref, bd_ref) = refs[:PACK_SLABS], refs[PACK_SLABS:PACK_SLABS + 5]
    y_refs = refs[PACK_SLABS + 5:]

    @pl.when(pl.program_id(0) < na_ref[0])
    def _():
        words = [r[...] for r in x_refs]
        x = jnp.concatenate(
            [pltpu.unpack_elementwise(w, index=index, packed_dtype=BF16, unpacked_dtype=F32)
             for index in range(2) for w in words], axis=1).astype(BF16)
        hmid = _dot(x, wgu_ref[0]) + bgu_ref[0]
        gate = jnp.minimum(hmid[:, :D_FF], SWIGLU_LIMIT)
        up = jnp.clip(hmid[:, D_FF:], -SWIGLU_LIMIT, SWIGLU_LIMIT)
        act = (up + 1.0) * gate * jax.nn.sigmoid(SWIGLU_ALPHA * gate)
        y = _dot(act.astype(BF16), wd_ref[0]) + bd_ref[0]
        y = y * jnp.concatenate([gs_ref[...]] * (D_MODEL // LANES), axis=1)
        packed = pltpu.pack_elementwise([y[:, :PACK_WIDTH], y[:, PACK_WIDTH:]], packed_dtype=BF16)
        for q in range(PACK_SLABS):
            y_refs[q][...] = packed[:, q * SLAB_WIDTH:(q + 1) * SLAB_WIDTH]


def _experts(tile_e, n_active, xs, gs, wgu, bgu, wd, bd):
    tm = ROW_TILE
    n_tiles = gs.shape[0] // tm
    row_map = lambda i, te, na: (jnp.minimum(i, na[0] - 1), 0)
    w_map = lambda i, te, na: (te[i], 0, 0)
    return pl.pallas_call(
        _expert_kernel,
        grid_spec=pltpu.PrefetchScalarGridSpec(
            num_scalar_prefetch=2,
            grid=(n_tiles,),
            in_specs=[pl.BlockSpec((tm, SLAB_WIDTH), row_map)] * PACK_SLABS + [
                pl.BlockSpec((tm, LANES), row_map),
                pl.BlockSpec((1, D_MODEL, 2 * D_FF), w_map),
                pl.BlockSpec((1, 1, 2 * D_FF), w_map),
                pl.BlockSpec((1, D_FF, D_MODEL), w_map),
                pl.BlockSpec((1, 1, D_MODEL), w_map),
            ],
            out_specs=[pl.BlockSpec((tm, SLAB_WIDTH), row_map)] * PACK_SLABS,
        ),
        out_shape=[jax.ShapeDtypeStruct((n_tiles * tm, SLAB_WIDTH), jnp.uint32)] * PACK_SLABS,
        compiler_params=pltpu.CompilerParams(
            dimension_semantics=("arbitrary",), vmem_limit_bytes=VMEM_LIMIT),
        name="experts",
    )(tile_e, n_active, *xs, gs, wgu, bgu, wd, bd)


def _sc_combine(ys, dest_sets):
    n_out = TOP_K * PACK_SLABS
    n_sets = len(dest_sets)
    toks = [dests[0].shape[1] for dests in dest_sets]
    mesh = plsc.VectorSubcoreMesh(core_axis_name="core", subcore_axis_name="subcore")
    out_type = [jax.ShapeDtypeStruct((t, SLAB_WIDTH), jnp.uint32) for t in toks for _ in range(n_out)]

    @functools.partial(pl.kernel, out_type=out_type, mesh=mesh, scratch_types=[], name="sc_combine")
    def run(*refs):
        ys_hbm = refs[:PACK_SLABS]
        out0 = PACK_SLABS + TOP_K * n_sets

        for s in range(n_sets):
            i_hbm = refs[PACK_SLABS + TOP_K * s:PACK_SLABS + TOP_K * (s + 1)]
            f_hbm = refs[out0 + n_out * s:out0 + n_out * (s + 1)]
            n_tok = toks[s]
            for kk in range(TOP_K):
                for q in range(PACK_SLABS):
                    def body(i_vmem, o_vmem, q=q):
                        pltpu.sync_copy(ys_hbm[q].at[i_vmem.at[0]], o_vmem)

                    _sc_pipeline(body, n_tok, [_index_specs()[0]],
                                 [pl.BlockSpec((SC_WINDOW, SLAB_WIDTH), lambda i: (i, 0))]
                                 )(i_hbm[kk], f_hbm[kk * PACK_SLABS + q])

    outs = run(*ys, *[d for dests in dest_sets for d in dests])
    return [outs[n_out * s:n_out * (s + 1)] for s in range(n_sets)]


def _final_kernel(*refs):
    n_in = TOP_K * PACK_SLABS
    h_ref, p_refs, g_ref, b_ref, out_ref = refs[0], refs[1:1 + n_in], refs[1 + n_in], refs[2 + n_in], refs[-1]
    halves = []
    for index in range(2):
        for q in range(PACK_SLABS):
            parts = [pltpu.unpack_elementwise(p_refs[kk * PACK_SLABS + q][...], index=index,
                                              packed_dtype=BF16, unpacked_dtype=F32) for kk in range(TOP_K)]
            halves.append((parts[0] + parts[1]) + (parts[2] + parts[3]))
    f = jnp.concatenate(halves, axis=1)
    out_ref[...] = _layer_norm(DN_ALPHA * h_ref[...] + f, g_ref[...], b_ref[...])


def _final_norm(h, picked, ln_g, ln_b, out_rows, first_row, earlier=None):
    tt = FINAL_TILE
    tok = h.shape[0]
    first_tile = first_row // tt
    row = lambda w: pl.BlockSpec((tt, w), lambda i: (i, 0))
    in_specs = ([row(D_MODEL)] + [row(SLAB_WIDTH)] * (TOP_K * PACK_SLABS)
                + [_const_spec((1, D_MODEL)), _const_spec((1, D_MODEL))])
    args = [h, *picked, ln_g, ln_b]
    aliases = {}
    if earlier is not None:
        in_specs.append(pl.BlockSpec(memory_space=pl.ANY))
        aliases = {len(args): 0}
        args.append(earlier)
    return pl.pallas_call(
        _final_kernel,
        grid=(tok // tt,),
        in_specs=in_specs,
        out_specs=pl.BlockSpec((tt, D_MODEL), lambda i: (i + first_tile, 0)),
        out_shape=jax.ShapeDtypeStruct((out_rows, D_MODEL), F32),
        input_output_aliases=aliases,
        compiler_params=pltpu.CompilerParams(dimension_semantics=("arbitrary",)),
        name="final_norm",
    )(*args)


def _pad_heads_last(w):
    lead = w.shape[:-1]
    w = w.reshape(lead + (N_KV_HEADS, Q_GROUP, HEAD_DIM))
    zero = jnp.zeros_like(w[..., 0, :, :])
    parts = []
    for hk in range(N_KV_HEADS):
        cols = [zero] * N_KV_HEADS
        cols[hk] = w[..., hk, :, :]
        parts.append(jnp.concatenate(cols, axis=-1))
    return jnp.stack(parts, axis=-3).reshape(lead + (QPAD_WIDTH,))


def _mixer_weights(w_in, b_in, ln_g, ln_b, w_s, b_s, w_o, b_o, ln1_g, ln1_b, w_router, b_router):
    win = jnp.concatenate([_pad_heads_last(w_in[:, :ATTN_WIDTH]), w_in[:, ATTN_WIDTH:]], axis=1).astype(BF16)
    bin_ = jnp.concatenate([_pad_heads_last(b_in[:ATTN_WIDTH]), b_in[ATTN_WIDTH:]])[None, :]
    woa = _pad_heads_last(w_o[:ATTN_WIDTH].T).T.astype(BF16)
    wog = w_o[ATTN_WIDTH:].astype(BF16)
    wr = jnp.pad(w_router, ((0, 0), (0, LANES - N_EXPERTS)))
    wrh = wr.astype(BF16)
    wrl = (wr - wrh.astype(F32)).astype(BF16)
    br = jnp.pad(b_router, (0, LANES - N_EXPERTS))[None, :]
    tri = (lax.broadcasted_iota(I32, (MIX_TILE, MIX_TILE), 1)
           < lax.broadcasted_iota(I32, (MIX_TILE, MIX_TILE), 0)).astype(BF16)
    return (win, bin_, ln_g.reshape(1, GMLP_WIDTH), ln_b.reshape(1, GMLP_WIDTH), w_s, b_s.T,
            woa, wog, b_o[None, :], ln1_g[None, :], ln1_b[None, :], wrh, wrl, br, tri)


def _dest_rows(meta, pstart):
    e = meta[:, META_E:META_E + TOP_K].astype(I32)
    r = meta[:, META_R:META_R + TOP_K].astype(I32)
    hit = e[..., None] == jnp.arange(N_EXPERTS, dtype=I32)
    return jnp.sum(jnp.where(hit, pstart, 0), axis=-1) + r


def _moe(token_sets, cnt, experts):
    n_assign = sum(hw.shape[0] for hw, _, _ in token_sets) * TOP_K
    n_tiles = (n_assign + N_EXPERTS * (ROW_TILE - 1)) // ROW_TILE
    counts = cnt[0, :N_EXPERTS].astype(I32)
    padded = (counts + ROW_TILE - 1) // ROW_TILE * ROW_TILE
    pend = jnp.cumsum(padded)
    pstart = pend - padded
    n_active = (pend[-1:] // ROW_TILE).astype(I32)
    tile_start = jnp.arange(n_tiles, dtype=I32) * ROW_TILE
    tile_e = jnp.minimum(jnp.sum(pend[None, :] <= tile_start[:, None], axis=1), N_EXPERTS - 1).astype(I32)

    lists = lambda d: [d[:, kk][None, :] for kk in range(TOP_K)]
    dest_sets = [lists(_dest_rows(meta, pstart)) for _, _, meta in token_sets]
    xs, gs = _sc_dispatch([(hw, gates, dests) for (hw, gates, _), dests in zip(token_sets, dest_sets)],
                          n_tiles * ROW_TILE)
    ys = _experts(tile_e, n_active, xs, gs, *experts)
    return _sc_combine(ys, dest_sets)


def kernel(x_prompt, x_sample, cache_k, cache_v, w_in, b_in, attn_sinks, gmlp_ln_g, gmlp_ln_b, w_spatial, b_spatial, w_o, b_o, ln1_g, ln1_b, w_router, b_router, w_gate_up, b_gate_up, w_down, b_down, ln2_g, ln2_b):
    assert w_in.shape[0] == DEPTH
    batch, seq, _ = x_prompt.shape
    dec_batch, dec_seq, _ = x_sample.shape
    tok_p, tok_s = batch * seq, dec_batch * dec_seq
    assert dec_seq == CHUNK and seq % MIX_TILE == 0 and batch >= 2
    assert all(t % MIX_TILE == 0 and t % FINAL_TILE == 0 and t % SC_WINDOW == 0 for t in (tok_p, tok_s))

    weights = _mixer_weights(w_in[0], b_in[0], gmlp_ln_g[0], gmlp_ln_b[0], w_spatial[0], b_spatial[0],
                             w_o[0], b_o[0], ln1_g[0], ln1_b[0], w_router[0], b_router[0])
    sinks = attn_sinks[0]
    experts = (w_gate_up[0].astype(BF16), b_gate_up[0][:, None, :], w_down[0].astype(BF16), b_down[0][:, None, :])

    batch_a = batch // 2
    h_a, hw_a, meta_a, gates_a, cnt_a, kt_a, vt_a = _mix_prompt(sinks, x_prompt, weights, 0, batch_a)
    picked_a, = _moe([(hw_a, gates_a, meta_a)], cnt_a, experts)
    h_b, hw_b, meta_b, gates_b, cnt_b, kt_b, vt_b = _mix_prompt(sinks, x_prompt, weights, batch_a, batch - batch_a)
    h_s, hw_s, meta_s, gates_s, cnt_bs, ks, vs, gvs = _mix_sample(
        sinks, x_sample.reshape(tok_s, D_MODEL),
        cache_k[0].reshape(dec_batch * WINDOW, KV_WIDTH), cache_v[0].reshape(dec_batch * WINDOW, KV_WIDTH),
        cnt_b, weights)
    picked_b, picked_s = _moe([(hw_b, gates_b, meta_b), (hw_s, gates_s, meta_s)], cnt_bs, experts)

    g2, b2 = ln2_g[0][None, :], ln2_b[0][None, :]
    y_p = _final_norm(h_a, picked_a, g2, b2, tok_p, 0)
    y_p = _final_norm(h_b, picked_b, g2, b2, tok_p, batch_a * seq, earlier=y_p).reshape(batch, seq, D_MODEL)
    y_s = _final_norm(h_s, picked_s, g2, b2, tok_s, 0).reshape(dec_batch, dec_seq, D_MODEL)
    kt, vt = jnp.concatenate([kt_a, kt_b]), jnp.concatenate([vt_a, vt_b])

    kv5 = lambda a, nb, rows: a.reshape(DEPTH, nb, rows, N_KV_HEADS, HEAD_DIM)
    return (y_p, y_s, kv5(kt, batch, WINDOW), kv5(vt, batch, WINDOW),
            kv5(ks, dec_batch, dec_seq), kv5(vs, dec_batch, dec_seq),
            gvs.reshape(DEPTH, dec_batch, dec_seq, GMLP_GROUPS, GMLP_GROUP_DIM))
```

```python
import functools

import jax
import jax.numpy as jnp
from jax import lax
from jax.experimental import pallas as pl
from jax.experimental.pallas import tpu as pltpu
from jax.experimental.pallas import tpu_sc as plsc

F32 = jnp.float32
BF16 = jnp.bfloat16
I32 = jnp.int32

D_MODEL = 1024
CHUNK = 64
N_HEADS = 8
N_KV_HEADS = 2
HEAD_DIM = 64
Q_GROUP = N_HEADS // N_KV_HEADS
KV_WIDTH = N_KV_HEADS * HEAD_DIM
WINDOW = 128
KEYS = WINDOW + CHUNK
GMLP_GROUPS = 4
GMLP_GROUP_DIM = 128
GMLP_WIDTH = GMLP_GROUPS * GMLP_GROUP_DIM
GMLP_CHUNK = 128
ATTN_WIDTH = N_HEADS * HEAD_DIM
N_EXPERTS = 32
TOP_K = 4
D_FF = 1024
SWIGLU_LIMIT = 7.0
SWIGLU_ALPHA = 1.702
DEPTH = 1
DN_ALPHA = (2 * DEPTH) ** 0.25
LN_EPS = 1e-5
NEG_INF = -1e30

LANES = 128
QPAD_WIDTH = N_HEADS * LANES
K_OFF = QPAD_WIDTH
V_OFF = K_OFF + KV_WIDTH
U_OFF = V_OFF + KV_WIDTH
GV_OFF = U_OFF + GMLP_WIDTH
ZPAD_WIDTH = GV_OFF + GMLP_WIDTH

MIX_TILE = 512
ROW_TILE = 512
FINAL_TILE = 512
SC_WINDOW = 128
SC_SLABS = 4
SLAB_WIDTH = D_MODEL // SC_SLABS
PACK_WIDTH = D_MODEL // 2
PACK_SLABS = PACK_WIDTH // SLAB_WIDTH
VMEM_LIMIT = 56 * 1024 * 1024

META_E, META_R = 0, 4


def _gelu_tanh(x):
    return 0.5 * x * (1.0 + jnp.tanh(0.7978845608028654 * (x + 0.044715 * x * x * x)))


def _layer_norm(x, g, b):
    mu = jnp.mean(x, axis=-1, keepdims=True)
    xc = x - mu
    var = jnp.mean(xc * xc, axis=-1, keepdims=True)
    return xc * lax.rsqrt(var + LN_EPS) * g + b


def _dot(a, b):
    return jnp.dot(a, b, preferred_element_type=F32)


def _mixer_kernel(*refs, is_prompt, tt):
    n_chunks = tt // CHUNK
    it = iter(refs)
    sinks_ref = next(it)
    x_ref = next(it)
    if not is_prompt:
        ck_ref, cv_ref, base0_ref = next(it), next(it), next(it)
    (win_ref, bin_ref, lng_ref, lnb_ref, ws_ref, bs_ref, woa_ref, wog_ref, bo_ref, l1g_ref, l1b_ref,
     wrh_ref, wrl_ref, br_ref, tri_ref) = (next(it) for _ in range(15))
    h_ref, hw_ref, meta_ref, gate_ref, cnt_ref = (next(it) for _ in range(5))
    if is_prompt:
        kt_ref, vt_ref = next(it), next(it)
    else:
        ko_ref, vo_ref, gvo_ref = next(it), next(it), next(it)
    z_ref, kext_ref, vext_ref, att_ref, gm_ref, base_ref = (next(it) for _ in range(6))

    if is_prompt:
        first = (pl.program_id(0) == 0) & (pl.program_id(1) == 0)
        tile_in_seq = pl.program_id(1)

        @pl.when(first)
        def _():
            base_ref[...] = jnp.zeros_like(base_ref)

        @pl.when(tile_in_seq == 0)
        def _():
            kext_ref[0:WINDOW, :] = jnp.zeros((WINDOW, KV_WIDTH), BF16)
            vext_ref[0:WINDOW, :] = jnp.zeros((WINDOW, KV_WIDTH), BF16)
    else:
        @pl.when(pl.program_id(0) == 0)
        def _():
            base_ref[...] = base0_ref[...]

    x = x_ref[...]
    z_ref[...] = _dot(x.astype(BF16), win_ref[...]) + bin_ref[...]

    k = z_ref[:, K_OFF:K_OFF + KV_WIDTH]
    v = z_ref[:, V_OFF:V_OFF + KV_WIDTH]
    if is_prompt:
        kext_ref[WINDOW:WINDOW + tt, :] = k.astype(BF16)
        vext_ref[WINDOW:WINDOW + tt, :] = v.astype(BF16)
        kt_ref[...] = k[tt - WINDOW:, :]
        vt_ref[...] = v[tt - WINDOW:, :]
        key_stride = CHUNK
    else:
        ko_ref[...] = k
        vo_ref[...] = v
        for c in range(n_chunks):
            kext_ref[KEYS * c:KEYS * c + WINDOW, :] = ck_ref[WINDOW * c:WINDOW * (c + 1), :].astype(BF16)
            vext_ref[KEYS * c:KEYS * c + WINDOW, :] = cv_ref[WINDOW * c:WINDOW * (c + 1), :].astype(BF16)
            kext_ref[KEYS * c + WINDOW:KEYS * (c + 1), :] = k[CHUNK * c:CHUNK * (c + 1), :].astype(BF16)
            vext_ref[KEYS * c + WINDOW:KEYS * (c + 1), :] = v[CHUNK * c:CHUNK * (c + 1), :].astype(BF16)
        key_stride = KEYS

    rows = Q_GROUP * CHUNK
    row_i = lax.broadcasted_iota(I32, (rows, 1), 0)
    key_i = lax.broadcasted_iota(I32, (rows, KEYS), 1)
    sink_cols = []
    for hk in range(N_KV_HEADS):
        s = [sinks_ref[hk * Q_GROUP + g] for g in range(Q_GROUP)]
        sink_cols.append(jnp.where(row_i < CHUNK, s[0],
                                   jnp.where(row_i < 2 * CHUNK, s[1],
                                             jnp.where(row_i < 3 * CHUNK, s[2], s[3]))))
    for c in range(n_chunks):
        kc = kext_ref[key_stride * c:key_stride * c + KEYS, :]
        vc = vext_ref[key_stride * c:key_stride * c + KEYS, :]
        outs = []
        for hk in range(N_KV_HEADS):
            qs = jnp.concatenate(
                [z_ref[CHUNK * c:CHUNK * (c + 1), (hk * Q_GROUP + g) * LANES:(hk * Q_GROUP + g + 1) * LANES]
                 for g in range(Q_GROUP)], axis=0).astype(BF16)
            sc = lax.dot_general(qs, kc, (((1,), (1,)), ((), ())), preferred_element_type=F32)
            sc = sc * (HEAD_DIM ** -0.5)
            if is_prompt and c < WINDOW // CHUNK:
                valid = (key_i >= WINDOW - CHUNK * c) | (tile_in_seq > 0)
                sc = jnp.where(valid, sc, NEG_INF)
            sink = sink_cols[hk]
            m = jnp.maximum(jnp.max(sc, axis=-1, keepdims=True), sink)
            p = jnp.exp(sc - m)
            den = jnp.sum(p, axis=-1, keepdims=True) + jnp.exp(sink - m)
            o = _dot(p.astype(BF16), vc) / den
            outs.extend(o[CHUNK * g:CHUNK * (g + 1), :] for g in range(Q_GROUP))
        att_ref[CHUNK * c:CHUNK * (c + 1), :] = jnp.concatenate(outs, axis=1).astype(BF16)

    if is_prompt:
        kext_ref[0:WINDOW, :] = kext_ref[tt:tt + WINDOW, :]
        vext_ref[0:WINDOW, :] = vext_ref[tt:tt + WINDOW, :]

    gc = GMLP_CHUNK if is_prompt else CHUNK
    ri = lax.broadcasted_iota(I32, (gc, gc), 0)
    ci = lax.broadcasted_iota(I32, (gc, gc), 1)
    causal = (ci // CHUNK) <= (ri // CHUNK)
    for g in range(GMLP_GROUPS):
        lo, hi = g * GMLP_GROUP_DIM, (g + 1) * GMLP_GROUP_DIM
        u = _gelu_tanh(z_ref[:, U_OFF + lo:U_OFF + hi])
        gv = _layer_norm(_gelu_tanh(z_ref[:, GV_OFF + lo:GV_OFF + hi]), lng_ref[:, lo:hi], lnb_ref[:, lo:hi])
        if not is_prompt:
            gvo_ref[:, lo:hi] = gv
        gvb = gv.astype(BF16)
        wm = jnp.where(causal, ws_ref[g, 0:gc, 0:gc], 0.0).astype(BF16)
        bcol = bs_ref[0:gc, g:g + 1]
        for n in range(tt // gc):
            sp = _dot(wm, gvb[gc * n:gc * (n + 1), :]) + bcol
            gm_ref[gc * n:gc * (n + 1), lo:hi] = (u[gc * n:gc * (n + 1), :] * sp).astype(BF16)

    y = _dot(att_ref[...], woa_ref[...]) + _dot(gm_ref[...], wog_ref[...]) + bo_ref[...]
    h = _layer_norm(DN_ALPHA * x + y, l1g_ref[...], l1b_ref[...])
    h_ref[...] = h
    hw_ref[...] = pltpu.pack_elementwise([h[:, :PACK_WIDTH], h[:, PACK_WIDTH:]], packed_dtype=BF16)

    h_hi = h.astype(BF16)
    h_lo = (h - h_hi.astype(F32)).astype(BF16)
    logits = _dot(h_hi, wrh_ref[...]) + _dot(h_lo, wrh_ref[...]) + _dot(h_hi, wrl_ref[...]) + br_ref[...]
    lane = lax.broadcasted_iota(I32, (tt, LANES), 1)
    lane_f = lane.astype(F32)
    l = jnp.where(lane < N_EXPERTS, logits, -jnp.inf)
    tops, idxs, hots = [], [], []
    for _ in range(TOP_K):
        m = jnp.max(l, axis=-1, keepdims=True)
        idx = jnp.min(jnp.where(l == m, lane_f, float(LANES)), axis=-1, keepdims=True)
        hot = lane_f == idx
        l = jnp.where(hot, -jnp.inf, l)
        tops.append(m)
        idxs.append(idx)
        hots.append(hot)
    es = [jnp.exp(t - tops[0]) for t in tops]
    esum = es[0] + es[1] + es[2] + es[3]
    chosen = jnp.where(hots[0] | hots[1] | hots[2] | hots[3], 1.0, 0.0)
    before = _dot(tri_ref[...], chosen.astype(BF16)) + base_ref[...]
    meta = jnp.zeros((tt, LANES), F32)
    for kk in range(TOP_K):
        rank = jnp.sum(jnp.where(hots[kk], before, 0.0), axis=-1, keepdims=True)
        meta = jnp.where(lane == META_E + kk, idxs[kk], meta)
        meta = jnp.where(lane == META_R + kk, rank, meta)
        gate_ref[:, kk * LANES:(kk + 1) * LANES] = jnp.broadcast_to(es[kk] / esum, (tt, LANES))
    meta_ref[...] = jnp.transpose(meta)[0:2 * TOP_K, :]
    base_ref[...] = base_ref[...] + jnp.sum(chosen, axis=0, keepdims=True)
    cnt_ref[...] = base_ref[...]


def _const_spec(shape):
    nd = len(shape)
    return pl.BlockSpec(shape, lambda *_: (0,) * nd)


def _mixer_weight_specs(tt):
    return [
        _const_spec((D_MODEL, ZPAD_WIDTH)), _const_spec((1, ZPAD_WIDTH)),
        _const_spec((1, GMLP_WIDTH)), _const_spec((1, GMLP_WIDTH)),
        _const_spec((GMLP_GROUPS, GMLP_CHUNK, GMLP_CHUNK)), _const_spec((GMLP_CHUNK, GMLP_GROUPS)),
        _const_spec((QPAD_WIDTH, D_MODEL)), _const_spec((GMLP_WIDTH, D_MODEL)), _const_spec((1, D_MODEL)),
        _const_spec((1, D_MODEL)), _const_spec((1, D_MODEL)),
        _const_spec((D_MODEL, LANES)), _const_spec((D_MODEL, LANES)), _const_spec((1, LANES)),
        _const_spec((tt, tt)),
    ]


def _mixer_scratch(tt, kext_rows):
    return [
        pltpu.VMEM((tt, ZPAD_WIDTH), F32),
        pltpu.VMEM((kext_rows, KV_WIDTH), BF16), pltpu.VMEM((kext_rows, KV_WIDTH), BF16),
        pltpu.VMEM((tt, QPAD_WIDTH), BF16), pltpu.VMEM((tt, GMLP_WIDTH), BF16),
        pltpu.VMEM((1, LANES), F32),
    ]


def _mix_prompt(sinks, x, weights, first_batch, batch):
    seq = x.shape[1]
    tt = MIX_TILE
    n_tiles = seq // tt
    tok = batch * seq
    smem = pl.BlockSpec(memory_space=pltpu.SMEM)
    return pl.pallas_call(
        functools.partial(_mixer_kernel, is_prompt=True, tt=tt),
        grid=(batch, n_tiles),
        in_specs=[smem, pl.BlockSpec((None, tt, D_MODEL), lambda b, i: (b + first_batch, i, 0))]
        + _mixer_weight_specs(tt),
        out_specs=[
            pl.BlockSpec((tt, D_MODEL), lambda b, i: (b * n_tiles + i, 0)),
            pl.BlockSpec((tt, PACK_WIDTH), lambda b, i: (b * n_tiles + i, 0)),
            pl.BlockSpec((2 * TOP_K, tt), lambda b, i: (0, b * n_tiles + i)),
            pl.BlockSpec((tt, TOP_K * LANES), lambda b, i: (b * n_tiles + i, 0)),
            pl.BlockSpec((1, LANES), lambda b, i: (0, 0)),
            pl.BlockSpec((WINDOW, KV_WIDTH), lambda b, i: (b, 0)),
            pl.BlockSpec((WINDOW, KV_WIDTH), lambda b, i: (b, 0)),
        ],
        out_shape=[
            jax.ShapeDtypeStruct((tok, D_MODEL), F32), jax.ShapeDtypeStruct((tok, PACK_WIDTH), jnp.uint32),
            jax.ShapeDtypeStruct((2 * TOP_K, tok), F32), jax.ShapeDtypeStruct((tok, TOP_K * LANES), F32),
            jax.ShapeDtypeStruct((1, LANES), F32),
            jax.ShapeDtypeStruct((batch * WINDOW, KV_WIDTH), F32),
            jax.ShapeDtypeStruct((batch * WINDOW, KV_WIDTH), F32),
        ],
        scratch_shapes=_mixer_scratch(tt, WINDOW + tt),
        compiler_params=pltpu.CompilerParams(
            dimension_semantics=("arbitrary", "arbitrary"), vmem_limit_bytes=VMEM_LIMIT),
        name="mix_prompt",
    )(sinks, x, *weights)


def _mix_sample(sinks, x2, ck, cv, base0, weights):
    tok = x2.shape[0]
    tt = MIX_TILE
    n_chunks = tt // CHUNK
    cache_rows = n_chunks * WINDOW
    smem = pl.BlockSpec(memory_space=pltpu.SMEM)
    row = lambda w: pl.BlockSpec((tt, w), lambda i: (i, 0))
    return pl.pallas_call(
        functools.partial(_mixer_kernel, is_prompt=False, tt=tt),
        grid=(tok // tt,),
        in_specs=[smem, row(D_MODEL),
                  pl.BlockSpec((cache_rows, KV_WIDTH), lambda i: (i, 0)),
                  pl.BlockSpec((cache_rows, KV_WIDTH), lambda i: (i, 0)),
                  _const_spec((1, LANES))] + _mixer_weight_specs(tt),
        out_specs=[row(D_MODEL), row(PACK_WIDTH), pl.BlockSpec((2 * TOP_K, tt), lambda i: (0, i)),
                   row(TOP_K * LANES),
                   pl.BlockSpec((1, LANES), lambda i: (0, 0)),
                   row(KV_WIDTH), row(KV_WIDTH), row(GMLP_WIDTH)],
        out_shape=[
            jax.ShapeDtypeStruct((tok, D_MODEL), F32), jax.ShapeDtypeStruct((tok, PACK_WIDTH), jnp.uint32),
            jax.ShapeDtypeStruct((2 * TOP_K, tok), F32), jax.ShapeDtypeStruct((tok, TOP_K * LANES), F32),
            jax.ShapeDtypeStruct((1, LANES), F32),
            jax.ShapeDtypeStruct((tok, KV_WIDTH), F32), jax.ShapeDtypeStruct((tok, KV_WIDTH), F32),
            jax.ShapeDtypeStruct((tok, GMLP_WIDTH), F32),
        ],
        scratch_shapes=_mixer_scratch(tt, n_chunks * KEYS),
        compiler_params=pltpu.CompilerParams(
            dimension_semantics=("arbitrary",), vmem_limit_bytes=VMEM_LIMIT),
        name="mix_sample",
    )(sinks, x2, ck, cv, base0, *weights)


def _sc_pipeline(body, n_tok, in_specs, out_specs):
    return pltpu.emit_pipeline(
        body, grid=(n_tok // SC_WINDOW,), in_specs=in_specs, out_specs=out_specs,
        core_axis_name=("core", "subcore"), dimension_semantics=(pltpu.PARALLEL,))


def _index_specs():
    return [pl.BlockSpec((1, SC_WINDOW), lambda i: (0, i))] * TOP_K


def _sc_dispatch(token_sets, n_rows):
    win, wid = SC_WINDOW, SLAB_WIDTH
    per_set = 2 + TOP_K
    n_in = per_set * len(token_sets)
    mesh = plsc.VectorSubcoreMesh(core_axis_name="core", subcore_axis_name="subcore")
    out_type = ([jax.ShapeDtypeStruct((n_rows, wid), jnp.uint32)] * PACK_SLABS
                + [jax.ShapeDtypeStruct((n_rows, LANES), F32)])

    @functools.partial(pl.kernel, out_type=out_type, mesh=mesh, scratch_types=[], name="sc_dispatch")
    def run(*refs):
        xs_hbm, gs_hbm = refs[n_in:n_in + PACK_SLABS], refs[n_in + PACK_SLABS]

        for s in range(len(token_sets)):
            h_hbm, g_hbm = refs[per_set * s], refs[per_set * s + 1]
            i_hbm = refs[per_set * s + 2:per_set * (s + 1)]
            n_tok = h_hbm.shape[0]
            for q in range(PACK_SLABS):
                def rows_body(x_vmem, *i_vmem, q=q):
                    for kk in range(TOP_K):
                        pltpu.sync_copy(x_vmem, xs_hbm[q].at[i_vmem[kk].at[0]])

                _sc_pipeline(rows_body, n_tok, [pl.BlockSpec((win, wid), lambda i, q=q: (i, q))] + _index_specs(),
                             [])(h_hbm, *i_hbm)
            for kk in range(TOP_K):
                def gate_body(g_vmem, i_vmem):
                    pltpu.sync_copy(g_vmem, gs_hbm.at[i_vmem.at[0]])

                _sc_pipeline(gate_body, n_tok,
                             [pl.BlockSpec((win, LANES), lambda i, kk=kk: (i, kk)), _index_specs()[0]],
                             [])(g_hbm, i_hbm[kk])

    outs = run(*[a for hw, gates, dests in token_sets for a in (hw, gates, *dests)])
    return outs[:PACK_SLABS], outs[PACK_SLABS]


def _expert_kernel(te_ref, na_ref, *refs):
    x_refs, (gs_ref, wgu_ref, bgu_ref, wd_ref, bd_ref) = refs[:PACK_SLABS], refs[PACK_SLABS:PACK_SLABS + 5]
    y_refs = refs[PACK_SLABS + 5:PACK_SLABS + 5 + PACK_SLABS]
    wgu_bf, wd_bf = refs[PACK_SLABS + 5 + PACK_SLABS:]
    i = pl.program_id(0)
    active = i < na_ref[0]

    @pl.when(active & ((i == 0) | (te_ref[i] != te_ref[jnp.maximum(i - 1, 0)])))
    def _():
        wgu_bf[...] = wgu_ref[0].astype(BF16)
        wd_bf[...] = wd_ref[0].astype(BF16)

    @pl.when(active)
    def _():
        words = [r[...] for r in x_refs]
        x = jnp.concatenate(
            [pltpu.unpack_elementwise(w, index=index, packed_dtype=BF16, unpacked_dtype=F32)
             for index in range(2) for w in words], axis=1).astype(BF16)
        hmid = _dot(x, wgu_bf[...]) + bgu_ref[0]
        gate = jnp.minimum(hmid[:, :D_FF], SWIGLU_LIMIT)
        up = jnp.clip(hmid[:, D_FF:], -SWIGLU_LIMIT, SWIGLU_LIMIT)
        act = (up + 1.0) * gate * jax.nn.sigmoid(SWIGLU_ALPHA * gate)
        y = _dot(act.astype(BF16), wd_bf[...]) + bd_ref[0]
        y = y * jnp.concatenate([gs_ref[...]] * (D_MODEL // LANES), axis=1)
        packed = pltpu.pack_elementwise([y[:, :PACK_WIDTH], y[:, PACK_WIDTH:]], packed_dtype=BF16)
        for q in range(PACK_SLABS):
            y_refs[q][...] = packed[:, q * SLAB_WIDTH:(q + 1) * SLAB_WIDTH]


def _experts(tile_e, n_active, xs, gs, wgu, bgu, wd, bd):
    tm = ROW_TILE
    n_tiles = gs.shape[0] // tm
    row_map = lambda i, te, na: (jnp.minimum(i, na[0] - 1), 0)
    w_map = lambda i, te, na: (te[i], 0, 0)
    w4_map = lambda i, te, na: (0, te[i], 0, 0)
    return pl.pallas_call(
        _expert_kernel,
        grid_spec=pltpu.PrefetchScalarGridSpec(
            num_scalar_prefetch=2,
            grid=(n_tiles,),
            in_specs=[pl.BlockSpec((tm, SLAB_WIDTH), row_map)] * PACK_SLABS + [
                pl.BlockSpec((tm, LANES), row_map),
                pl.BlockSpec((None, 1, D_MODEL, 2 * D_FF), w4_map),
                pl.BlockSpec((1, 1, 2 * D_FF), w_map),
                pl.BlockSpec((None, 1, D_FF, D_MODEL), w4_map),
                pl.BlockSpec((1, 1, D_MODEL), w_map),
            ],
            out_specs=[pl.BlockSpec((tm, SLAB_WIDTH), row_map)] * PACK_SLABS,
            scratch_shapes=[pltpu.VMEM((D_MODEL, 2 * D_FF), BF16), pltpu.VMEM((D_FF, D_MODEL), BF16)],
        ),
        out_shape=[jax.ShapeDtypeStruct((n_tiles * tm, SLAB_WIDTH), jnp.uint32)] * PACK_SLABS,
        compiler_params=pltpu.CompilerParams(
            dimension_semantics=("arbitrary",), vmem_limit_bytes=VMEM_LIMIT),
        name="experts",
    )(tile_e, n_active, *xs, gs, wgu, bgu, wd, bd)


def _sc_combine(ys, dest_sets):
    n_out = TOP_K * PACK_SLABS
    n_sets = len(dest_sets)
    toks = [dests[0].shape[1] for dests in dest_sets]
    mesh = plsc.VectorSubcoreMesh(core_axis_name="core", subcore_axis_name="subcore")
    out_type = [jax.ShapeDtypeStruct((t, SLAB_WIDTH), jnp.uint32) for t in toks for _ in range(n_out)]

    @functools.partial(pl.kernel, out_type=out_type, mesh=mesh, scratch_types=[], name="sc_combine")
    def run(*refs):
        ys_hbm = refs[:PACK_SLABS]
        out0 = PACK_SLABS + TOP_K * n_sets

        for s in range(n_sets):
            i_hbm = refs[PACK_SLABS + TOP_K * s:PACK_SLABS + TOP_K * (s + 1)]
            f_hbm = refs[out0 + n_out * s:out0 + n_out * (s + 1)]
            n_tok = toks[s]
            for kk in range(TOP_K):
                for q in range(PACK_SLABS):
                    def body(i_vmem, o_vmem, q=q):
                        pltpu.sync_copy(ys_hbm[q].at[i_vmem.at[0]], o_vmem)

                    _sc_pipeline(body, n_tok, [_index_specs()[0]],
                                 [pl.BlockSpec((SC_WINDOW, SLAB_WIDTH), lambda i: (i, 0))]
                                 )(i_hbm[kk], f_hbm[kk * PACK_SLABS + q])

    outs = run(*ys, *[d for dests in dest_sets for d in dests])
    return [outs[n_out * s:n_out * (s + 1)] for s in range(n_sets)]


def _final_kernel(*refs):
    n_in = TOP_K * PACK_SLABS
    h_ref, p_refs, g_ref, b_ref, out_ref = refs[0], refs[1:1 + n_in], refs[1 + n_in], refs[2 + n_in], refs[-1]
    halves = []
    for index in range(2):
        for q in range(PACK_SLABS):
            parts = [pltpu.unpack_elementwise(p_refs[kk * PACK_SLABS + q][...], index=index,
                                              packed_dtype=BF16, unpacked_dtype=F32) for kk in range(TOP_K)]
            halves.append((parts[0] + parts[1]) + (parts[2] + parts[3]))
    f = jnp.concatenate(halves, axis=1)
    out_ref[...] = _layer_norm(DN_ALPHA * h_ref[...] + f, g_ref[...], b_ref[...])


def _final_norm(h, picked, ln_g, ln_b, out_rows, first_row, earlier=None):
    tt = FINAL_TILE
    tok = h.shape[0]
    first_tile = first_row // tt
    row = lambda w: pl.BlockSpec((tt, w), lambda i: (i, 0))
    in_specs = ([row(D_MODEL)] + [row(SLAB_WIDTH)] * (TOP_K * PACK_SLABS)
                + [_const_spec((1, D_MODEL)), _const_spec((1, D_MODEL))])
    args = [h, *picked, ln_g, ln_b]
    aliases = {}
    if earlier is not None:
        in_specs.append(pl.BlockSpec(memory_space=pl.ANY))
        aliases = {len(args): 0}
        args.append(earlier)
    return pl.pallas_call(
        _final_kernel,
        grid=(tok // tt,),
        in_specs=in_specs,
        out_specs=pl.BlockSpec((tt, D_MODEL), lambda i: (i + first_tile, 0)),
        out_shape=jax.ShapeDtypeStruct((out_rows, D_MODEL), F32),
        input_output_aliases=aliases,
        compiler_params=pltpu.CompilerParams(dimension_semantics=("arbitrary",)),
        name="final_norm",
    )(*args)


def _pad_heads_last(w):
    lead = w.shape[:-1]
    w = w.reshape(lead + (N_KV_HEADS, Q_GROUP, HEAD_DIM))
    zero = jnp.zeros_like(w[..., 0, :, :])
    parts = []
    for hk in range(N_KV_HEADS):
        cols = [zero] * N_KV_HEADS
        cols[hk] = w[..., hk, :, :]
        parts.append(jnp.concatenate(cols, axis=-1))
    return jnp.stack(parts, axis=-3).reshape(lead + (QPAD_WIDTH,))


def _mixer_weights(w_in, b_in, ln_g, ln_b, w_s, b_s, w_o, b_o, ln1_g, ln1_b, w_router, b_router):
    win = jnp.concatenate([_pad_heads_last(w_in[:, :ATTN_WIDTH]), w_in[:, ATTN_WIDTH:]], axis=1).astype(BF16)
    bin_ = jnp.concatenate([_pad_heads_last(b_in[:ATTN_WIDTH]), b_in[ATTN_WIDTH:]])[None, :]
    woa = _pad_heads_last(w_o[:ATTN_WIDTH].T).T.astype(BF16)
    wog = w_o[ATTN_WIDTH:].astype(BF16)
    wr = jnp.pad(w_router, ((0, 0), (0, LANES - N_EXPERTS)))
    wrh = wr.astype(BF16)
    wrl = (wr - wrh.astype(F32)).astype(BF16)
    br = jnp.pad(b_router, (0, LANES - N_EXPERTS))[None, :]
    tri = (lax.broadcasted_iota(I32, (MIX_TILE, MIX_TILE), 1)
           < lax.broadcasted_iota(I32, (MIX_TILE, MIX_TILE), 0)).astype(BF16)
    return (win, bin_, ln_g.reshape(1, GMLP_WIDTH), ln_b.reshape(1, GMLP_WIDTH), w_s, b_s.T,
            woa, wog, b_o[None, :], ln1_g[None, :], ln1_b[None, :], wrh, wrl, br, tri)


def _dest_rows(meta, pstart):
    e = meta[META_E:META_E + TOP_K].astype(I32)
    r = meta[META_R:META_R + TOP_K].astype(I32)
    hit = e[None] == jnp.arange(N_EXPERTS, dtype=I32)[:, None, None]
    return jnp.sum(jnp.where(hit, pstart[:, None, None], 0), axis=0) + r


def _moe(token_sets, cnt, experts):
    n_assign = sum(hw.shape[0] for hw, _, _ in token_sets) * TOP_K
    n_tiles = (n_assign + N_EXPERTS * (ROW_TILE - 1)) // ROW_TILE
    counts = cnt[0, :N_EXPERTS].astype(I32)
    padded = (counts + ROW_TILE - 1) // ROW_TILE * ROW_TILE
    pend = jnp.cumsum(padded)
    pstart = pend - padded
    n_active = (pend[-1:] // ROW_TILE).astype(I32)
    tile_start = jnp.arange(n_tiles, dtype=I32) * ROW_TILE
    tile_e = jnp.minimum(jnp.sum(pend[None, :] <= tile_start[:, None], axis=1), N_EXPERTS - 1).astype(I32)

    lists = lambda d: [d[kk][None, :] for kk in range(TOP_K)]
    dest_sets = [lists(_dest_rows(meta, pstart)) for _, _, meta in token_sets]
    xs, gs = _sc_dispatch([(hw, gates, dests) for (hw, gates, _), dests in zip(token_sets, dest_sets)],
                          n_tiles * ROW_TILE)
    ys = _experts(tile_e, n_active, xs, gs, *experts)
    return _sc_combine(ys, dest_sets)


def kernel(x_prompt, x_sample, cache_k, cache_v, w_in, b_in, attn_sinks, gmlp_ln_g, gmlp_ln_b, w_spatial, b_spatial, w_o, b_o, ln1_g, ln1_b, w_router, b_router, w_gate_up, b_gate_up, w_down, b_down, ln2_g, ln2_b):
    assert w_in.shape[0] == DEPTH
    batch, seq, _ = x_prompt.shape
    dec_batch, dec_seq, _ = x_sample.shape
    tok_p, tok_s = batch * seq, dec_batch * dec_seq
    assert dec_seq == CHUNK and seq % MIX_TILE == 0 and batch >= 2
    assert all(t % MIX_TILE == 0 and t % FINAL_TILE == 0 and t % SC_WINDOW == 0 for t in (tok_p, tok_s))

    weights = _mixer_weights(w_in[0], b_in[0], gmlp_ln_g[0], gmlp_ln_b[0], w_spatial[0], b_spatial[0],
                             w_o[0], b_o[0], ln1_g[0], ln1_b[0], w_router[0], b_router[0])
    sinks = attn_sinks[0]
    experts = (w_gate_up, b_gate_up[0][:, None, :], w_down, b_down[0][:, None, :])

    batch_a = batch // 2
    h_a, hw_a, meta_a, gates_a, cnt_a, kt_a, vt_a = _mix_prompt(sinks, x_prompt, weights, 0, batch_a)
    picked_a, = _moe([(hw_a, gates_a, meta_a)], cnt_a, experts)
    h_b, hw_b, meta_b, gates_b, cnt_b, kt_b, vt_b = _mix_prompt(sinks, x_prompt, weights, batch_a, batch - batch_a)
    h_s, hw_s, meta_s, gates_s, cnt_bs, ks, vs, gvs = _mix_sample(
        sinks, x_sample.reshape(tok_s, D_MODEL),
        cache_k[0].reshape(dec_batch * WINDOW, KV_WIDTH), cache_v[0].reshape(dec_batch * WINDOW, KV_WIDTH),
        cnt_b, weights)
    picked_b, picked_s = _moe([(hw_b, gates_b, meta_b), (hw_s, gates_s, meta_s)], cnt_bs, experts)

    g2, b2 = ln2_g[0][None, :], ln2_b[0][None, :]
    y_p = _final_norm(h_a, picked_a, g2, b2, tok_p, 0)
    y_p = _final_norm(h_b, picked_b, g2, b2, tok_p, batch_a * seq, earlier=y_p).reshape(batch, seq, D_MODEL)
    y_s = _final_norm(h_s, picked_s, g2, b2, tok_s, 0).reshape(dec_batch, dec_seq, D_MODEL)
    kt, vt = jnp.concatenate([kt_a, kt_b]), jnp.concatenate([vt_a, vt_b])

    kv5 = lambda a, nb, rows: a.reshape(DEPTH, nb, rows, N_KV_HEADS, HEAD_DIM)
    return (y_p, y_s, kv5(kt, batch, WINDOW), kv5(vt, batch, WINDOW),
            kv5(ks, dec_batch, dec_seq), kv5(vs, dec_batch, dec_seq),
            gvs.reshape(DEPTH, dec_batch, dec_seq, GMLP_GROUPS, GMLP_GROUP_DIM))
```

```python
import functools

import jax
import jax.numpy as jnp
from jax import lax
from jax.experimental import pallas as pl
from jax.experimental.pallas import tpu as pltpu
from jax.experimental.pallas import tpu_sc as plsc

F32 = jnp.float32
BF16 = jnp.bfloat16
I32 = jnp.int32

D_MODEL = 1024
CHUNK = 64
N_HEADS = 8
N_KV_HEADS = 2
HEAD_DIM = 64
Q_GROUP = N_HEADS // N_KV_HEADS
KV_WIDTH = N_KV_HEADS * HEAD_DIM
WINDOW = 128
KEYS = WINDOW + CHUNK
GMLP_GROUPS = 4
GMLP_GROUP_DIM = 128
GMLP_WIDTH = GMLP_GROUPS * GMLP_GROUP_DIM
GMLP_CHUNK = 128
ATTN_WIDTH = N_HEADS * HEAD_DIM
N_EXPERTS = 32
TOP_K = 4
D_FF = 1024
SWIGLU_LIMIT = 7.0
SWIGLU_ALPHA = 1.702
DEPTH = 1
DN_ALPHA = (2 * DEPTH) ** 0.25
LN_EPS = 1e-5
NEG_INF = -1e30

LANES = 128
QPAD_WIDTH = N_HEADS * LANES
K_OFF = QPAD_WIDTH
V_OFF = K_OFF + KV_WIDTH
U_OFF = V_OFF + KV_WIDTH
GV_OFF = U_OFF + GMLP_WIDTH
ZPAD_WIDTH = GV_OFF + GMLP_WIDTH

MIX_TILE = 512
ROW_TILE = 512
FINAL_TILE = 512
SC_WINDOW = 128
SC_SLABS = 4
SLAB_WIDTH = D_MODEL // SC_SLABS
PACK_WIDTH = D_MODEL // 2
PACK_SLABS = PACK_WIDTH // SLAB_WIDTH
VMEM_LIMIT = 58 * 1024 * 1024

META_E, META_R = 0, 4


def _gelu_tanh(x):
    return 0.5 * x * (1.0 + jnp.tanh(0.7978845608028654 * (x + 0.044715 * x * x * x)))


def _layer_norm(x, g, b):
    mu = jnp.mean(x, axis=-1, keepdims=True)
    xc = x - mu
    var = jnp.mean(xc * xc, axis=-1, keepdims=True)
    return xc * lax.rsqrt(var + LN_EPS) * g + b


def _dot(a, b):
    return jnp.dot(a, b, preferred_element_type=F32)


def _mixer_kernel(*refs, is_prompt, tt):
    n_chunks = tt // CHUNK
    it = iter(refs)
    sinks_ref = next(it)
    x_ref = next(it)
    if not is_prompt:
        ck_ref, cv_ref, base0_ref = next(it), next(it), next(it)
    (win_ref, bin_ref, lng_ref, lnb_ref, ws_ref, bs_ref, woa_ref, wog_ref, bo_ref, l1g_ref, l1b_ref,
     wrh_ref, wrl_ref, br_ref, tri_ref) = (next(it) for _ in range(15))
    h_ref, hw_ref, meta_ref, gate_ref, cnt_ref = (next(it) for _ in range(5))
    if is_prompt:
        kt_ref, vt_ref = next(it), next(it)
    else:
        ko_ref, vo_ref, gvo_ref = next(it), next(it), next(it)
    z_ref, kext_ref, vext_ref, att_ref, gm_ref, base_ref = (next(it) for _ in range(6))

    if is_prompt:
        first = (pl.program_id(0) == 0) & (pl.program_id(1) == 0)
        tile_in_seq = pl.program_id(1)

        @pl.when(first)
        def _():
            base_ref[...] = jnp.zeros_like(base_ref)

        @pl.when(tile_in_seq == 0)
        def _():
            kext_ref[0:WINDOW, :] = jnp.zeros((WINDOW, KV_WIDTH), BF16)
            vext_ref[0:WINDOW, :] = jnp.zeros((WINDOW, KV_WIDTH), BF16)
    else:
        @pl.when(pl.program_id(0) == 0)
        def _():
            base_ref[...] = base0_ref[...]

    x = x_ref[...]
    z_ref[...] = _dot(x.astype(BF16), win_ref[...]) + bin_ref[...]

    k = z_ref[:, K_OFF:K_OFF + KV_WIDTH]
    v = z_ref[:, V_OFF:V_OFF + KV_WIDTH]
    if is_prompt:
        kext_ref[WINDOW:WINDOW + tt, :] = k.astype(BF16)
        vext_ref[WINDOW:WINDOW + tt, :] = v.astype(BF16)
        kt_ref[...] = k[tt - WINDOW:, :]
        vt_ref[...] = v[tt - WINDOW:, :]
        key_stride = CHUNK
    else:
        ko_ref[...] = k
        vo_ref[...] = v
        for c in range(n_chunks):
            kext_ref[KEYS * c:KEYS * c + WINDOW, :] = ck_ref[WINDOW * c:WINDOW * (c + 1), :].astype(BF16)
            vext_ref[KEYS * c:KEYS * c + WINDOW, :] = cv_ref[WINDOW * c:WINDOW * (c + 1), :].astype(BF16)
            kext_ref[KEYS * c + WINDOW:KEYS * (c + 1), :] = k[CHUNK * c:CHUNK * (c + 1), :].astype(BF16)
            vext_ref[KEYS * c + WINDOW:KEYS * (c + 1), :] = v[CHUNK * c:CHUNK * (c + 1), :].astype(BF16)
        key_stride = KEYS

    rows = Q_GROUP * CHUNK
    row_i = lax.broadcasted_iota(I32, (rows, 1), 0)
    key_i = lax.broadcasted_iota(I32, (rows, KEYS), 1)
    sink_cols = []
    for hk in range(N_KV_HEADS):
        s = [sinks_ref[hk * Q_GROUP + g] for g in range(Q_GROUP)]
        sink_cols.append(jnp.where(row_i < CHUNK, s[0],
                                   jnp.where(row_i < 2 * CHUNK, s[1],
                                             jnp.where(row_i < 3 * CHUNK, s[2], s[3]))))
    for c in range(n_chunks):
        kc = kext_ref[key_stride * c:key_stride * c + KEYS, :]
        vc = vext_ref[key_stride * c:key_stride * c + KEYS, :]
        outs = []
        for hk in range(N_KV_HEADS):
            qs = jnp.concatenate(
                [z_ref[CHUNK * c:CHUNK * (c + 1), (hk * Q_GROUP + g) * LANES:(hk * Q_GROUP + g + 1) * LANES]
                 for g in range(Q_GROUP)], axis=0).astype(BF16)
            sc = lax.dot_general(qs, kc, (((1,), (1,)), ((), ())), preferred_element_type=F32)
            sc = sc * (HEAD_DIM ** -0.5)
            if is_prompt and c < WINDOW // CHUNK:
                valid = (key_i >= WINDOW - CHUNK * c) | (tile_in_seq > 0)
                sc = jnp.where(valid, sc, NEG_INF)
            sink = sink_cols[hk]
            m = jnp.maximum(jnp.max(sc, axis=-1, keepdims=True), sink)
            p = jnp.exp(sc - m)
            den = jnp.sum(p, axis=-1, keepdims=True) + jnp.exp(sink - m)
            o = _dot(p.astype(BF16), vc) / den
            outs.extend(o[CHUNK * g:CHUNK * (g + 1), :] for g in range(Q_GROUP))
        att_ref[CHUNK * c:CHUNK * (c + 1), :] = jnp.concatenate(outs, axis=1).astype(BF16)

    if is_prompt:
        kext_ref[0:WINDOW, :] = kext_ref[tt:tt + WINDOW, :]
        vext_ref[0:WINDOW, :] = vext_ref[tt:tt + WINDOW, :]

    gc = GMLP_CHUNK if is_prompt else CHUNK
    ri = lax.broadcasted_iota(I32, (gc, gc), 0)
    ci = lax.broadcasted_iota(I32, (gc, gc), 1)
    causal = (ci // CHUNK) <= (ri // CHUNK)
    for g in range(GMLP_GROUPS):
        lo, hi = g * GMLP_GROUP_DIM, (g + 1) * GMLP_GROUP_DIM
        u = _gelu_tanh(z_ref[:, U_OFF + lo:U_OFF + hi])
        gv = _layer_norm(_gelu_tanh(z_ref[:, GV_OFF + lo:GV_OFF + hi]), lng_ref[:, lo:hi], lnb_ref[:, lo:hi])
        if not is_prompt:
            gvo_ref[:, lo:hi] = gv
        gvb = gv.astype(BF16)
        wm = jnp.where(causal, ws_ref[g, 0:gc, 0:gc], 0.0).astype(BF16)
        bcol = bs_ref[0:gc, g:g + 1]
        for n in range(tt // gc):
            sp = _dot(wm, gvb[gc * n:gc * (n + 1), :]) + bcol
            gm_ref[gc * n:gc * (n + 1), lo:hi] = (u[gc * n:gc * (n + 1), :] * sp).astype(BF16)

    y = _dot(att_ref[...], woa_ref[...]) + _dot(gm_ref[...], wog_ref[...]) + bo_ref[...]
    h = _layer_norm(DN_ALPHA * x + y, l1g_ref[...], l1b_ref[...])
    h_ref[...] = h
    hw_ref[...] = pltpu.pack_elementwise([h[:, :PACK_WIDTH], h[:, PACK_WIDTH:]], packed_dtype=BF16)

    h_hi = h.astype(BF16)
    h_lo = (h - h_hi.astype(F32)).astype(BF16)
    logits = _dot(h_hi, wrh_ref[...]) + _dot(h_lo, wrh_ref[...]) + _dot(h_hi, wrl_ref[...]) + br_ref[...]
    lane = lax.broadcasted_iota(I32, (tt, LANES), 1)
    lane_f = lane.astype(F32)
    l = jnp.where(lane < N_EXPERTS, logits, -jnp.inf)
    tops, idxs, hots = [], [], []
    for _ in range(TOP_K):
        m = jnp.max(l, axis=-1, keepdims=True)
        idx = jnp.min(jnp.where(l == m, lane_f, float(LANES)), axis=-1, keepdims=True)
        hot = lane_f == idx
        l = jnp.where(hot, -jnp.inf, l)
        tops.append(m)
        idxs.append(idx)
        hots.append(hot)
    es = [jnp.exp(t - tops[0]) for t in tops]
    esum = es[0] + es[1] + es[2] + es[3]
    chosen = jnp.where(hots[0] | hots[1] | hots[2] | hots[3], 1.0, 0.0)
    before = _dot(tri_ref[...], chosen.astype(BF16)) + base_ref[...]
    meta = jnp.zeros((tt, LANES), F32)
    for kk in range(TOP_K):
        rank = jnp.sum(jnp.where(hots[kk], before, 0.0), axis=-1, keepdims=True)
        meta = jnp.where(lane == META_E + kk, idxs[kk], meta)
        meta = jnp.where(lane == META_R + kk, rank, meta)
        gate_ref[:, kk * LANES:(kk + 1) * LANES] = jnp.broadcast_to(es[kk] / esum, (tt, LANES))
    meta_ref[...] = jnp.transpose(meta)[0:2 * TOP_K, :]
    base_ref[...] = base_ref[...] + jnp.sum(chosen, axis=0, keepdims=True)
    cnt_ref[...] = base_ref[...]


def _const_spec(shape):
    nd = len(shape)
    return pl.BlockSpec(shape, lambda *_: (0,) * nd, pipeline_mode=pl.Buffered(1))


def _mixer_weight_specs(tt):
    return [
        _const_spec((D_MODEL, ZPAD_WIDTH)), _const_spec((1, ZPAD_WIDTH)),
        _const_spec((1, GMLP_WIDTH)), _const_spec((1, GMLP_WIDTH)),
        _const_spec((GMLP_GROUPS, GMLP_CHUNK, GMLP_CHUNK)), _const_spec((GMLP_CHUNK, GMLP_GROUPS)),
        _const_spec((QPAD_WIDTH, D_MODEL)), _const_spec((GMLP_WIDTH, D_MODEL)), _const_spec((1, D_MODEL)),
        _const_spec((1, D_MODEL)), _const_spec((1, D_MODEL)),
        _const_spec((D_MODEL, LANES)), _const_spec((D_MODEL, LANES)), _const_spec((1, LANES)),
        _const_spec((tt, tt)),
    ]


def _mixer_scratch(tt, kext_rows):
    return [
        pltpu.VMEM((tt, ZPAD_WIDTH), F32),
        pltpu.VMEM((kext_rows, KV_WIDTH), BF16), pltpu.VMEM((kext_rows, KV_WIDTH), BF16),
        pltpu.VMEM((tt, QPAD_WIDTH), BF16), pltpu.VMEM((tt, GMLP_WIDTH), BF16),
        pltpu.VMEM((1, LANES), F32),
    ]


def _mix_prompt(sinks, x, weights, first_batch, batch):
    seq = x.shape[1]
    tt = MIX_TILE
    n_tiles = seq // tt
    tok = batch * seq
    smem = pl.BlockSpec(memory_space=pltpu.SMEM)
    return pl.pallas_call(
        functools.partial(_mixer_kernel, is_prompt=True, tt=tt),
        grid=(batch, n_tiles),
        in_specs=[smem, pl.BlockSpec((None, tt, D_MODEL), lambda b, i: (b + first_batch, i, 0))]
        + _mixer_weight_specs(tt),
        out_specs=[
            pl.BlockSpec((tt, D_MODEL), lambda b, i: (b * n_tiles + i, 0)),
            pl.BlockSpec((tt, PACK_WIDTH), lambda b, i: (b * n_tiles + i, 0)),
            pl.BlockSpec((2 * TOP_K, tt), lambda b, i: (0, b * n_tiles + i)),
            pl.BlockSpec((tt, TOP_K * LANES), lambda b, i: (b * n_tiles + i, 0)),
            pl.BlockSpec((1, LANES), lambda b, i: (0, 0)),
            pl.BlockSpec((WINDOW, KV_WIDTH), lambda b, i: (b, 0)),
            pl.BlockSpec((WINDOW, KV_WIDTH), lambda b, i: (b, 0)),
        ],
        out_shape=[
            jax.ShapeDtypeStruct((tok, D_MODEL), F32), jax.ShapeDtypeStruct((tok, PACK_WIDTH), jnp.uint32),
            jax.ShapeDtypeStruct((2 * TOP_K, tok), F32), jax.ShapeDtypeStruct((tok, TOP_K * LANES), F32),
            jax.ShapeDtypeStruct((1, LANES), F32),
            jax.ShapeDtypeStruct((batch * WINDOW, KV_WIDTH), F32),
            jax.ShapeDtypeStruct((batch * WINDOW, KV_WIDTH), F32),
        ],
        scratch_shapes=_mixer_scratch(tt, WINDOW + tt),
        compiler_params=pltpu.CompilerParams(
            dimension_semantics=("arbitrary", "arbitrary"), vmem_limit_bytes=VMEM_LIMIT),
        name="mix_prompt",
    )(sinks, x, *weights)


def _mix_sample(sinks, x2, ck, cv, base0, weights):
    tok = x2.shape[0]
    tt = MIX_TILE
    n_chunks = tt // CHUNK
    cache_rows = n_chunks * WINDOW
    smem = pl.BlockSpec(memory_space=pltpu.SMEM)
    row = lambda w: pl.BlockSpec((tt, w), lambda i: (i, 0))
    return pl.pallas_call(
        functools.partial(_mixer_kernel, is_prompt=False, tt=tt),
        grid=(tok // tt,),
        in_specs=[smem, row(D_MODEL),
                  pl.BlockSpec((cache_rows, KV_WIDTH), lambda i: (i, 0)),
                  pl.BlockSpec((cache_rows, KV_WIDTH), lambda i: (i, 0)),
                  _const_spec((1, LANES))] + _mixer_weight_specs(tt),
        out_specs=[row(D_MODEL), row(PACK_WIDTH), pl.BlockSpec((2 * TOP_K, tt), lambda i: (0, i)),
                   row(TOP_K * LANES),
                   pl.BlockSpec((1, LANES), lambda i: (0, 0)),
                   row(KV_WIDTH), row(KV_WIDTH), row(GMLP_WIDTH)],
        out_shape=[
            jax.ShapeDtypeStruct((tok, D_MODEL), F32), jax.ShapeDtypeStruct((tok, PACK_WIDTH), jnp.uint32),
            jax.ShapeDtypeStruct((2 * TOP_K, tok), F32), jax.ShapeDtypeStruct((tok, TOP_K * LANES), F32),
            jax.ShapeDtypeStruct((1, LANES), F32),
            jax.ShapeDtypeStruct((tok, KV_WIDTH), F32), jax.ShapeDtypeStruct((tok, KV_WIDTH), F32),
            jax.ShapeDtypeStruct((tok, GMLP_WIDTH), F32),
        ],
        scratch_shapes=_mixer_scratch(tt, n_chunks * KEYS),
        compiler_params=pltpu.CompilerParams(
            dimension_semantics=("arbitrary",), vmem_limit_bytes=VMEM_LIMIT),
        name="mix_sample",
    )(sinks, x2, ck, cv, base0, *weights)


def _sc_pipeline(body, n_tok, in_specs, out_specs):
    return pltpu.emit_pipeline(
        body, grid=(n_tok // SC_WINDOW,), in_specs=in_specs, out_specs=out_specs,
        core_axis_name=("core", "subcore"), dimension_semantics=(pltpu.PARALLEL,))


def _index_specs():
    return [pl.BlockSpec((1, SC_WINDOW), lambda i: (0, i))] * TOP_K


def _sc_dispatch(token_sets, n_rows):
    win, wid = SC_WINDOW, SLAB_WIDTH
    per_set = 2 + TOP_K
    n_in = per_set * len(token_sets)
    mesh = plsc.VectorSubcoreMesh(core_axis_name="core", subcore_axis_name="subcore")
    out_type = ([jax.ShapeDtypeStruct((n_rows, wid), jnp.uint32)] * PACK_SLABS
                + [jax.ShapeDtypeStruct((n_rows, LANES), F32)])

    @functools.partial(pl.kernel, out_type=out_type, mesh=mesh, scratch_types=[], name="sc_dispatch")
    def run(*refs):
        xs_hbm, gs_hbm = refs[n_in:n_in + PACK_SLABS], refs[n_in + PACK_SLABS]

        for s in range(len(token_sets)):
            h_hbm, g_hbm = refs[per_set * s], refs[per_set * s + 1]
            i_hbm = refs[per_set * s + 2:per_set * (s + 1)]
            n_tok = h_hbm.shape[0]
            for q in range(PACK_SLABS):
                def rows_body(x_vmem, *i_vmem, q=q):
                    for kk in range(TOP_K):
                        pltpu.sync_copy(x_vmem, xs_hbm[q].at[i_vmem[kk].at[0]])

                _sc_pipeline(rows_body, n_tok, [pl.BlockSpec((win, wid), lambda i, q=q: (i, q))] + _index_specs(),
                             [])(h_hbm, *i_hbm)
            for kk in range(TOP_K):
                def gate_body(g_vmem, i_vmem):
                    pltpu.sync_copy(g_vmem, gs_hbm.at[i_vmem.at[0]])

                _sc_pipeline(gate_body, n_tok,
                             [pl.BlockSpec((win, LANES), lambda i, kk=kk: (i, kk)), _index_specs()[0]],
                             [])(g_hbm, i_hbm[kk])

    outs = run(*[a for hw, gates, dests in token_sets for a in (hw, gates, *dests)])
    return outs[:PACK_SLABS], outs[PACK_SLABS]


def _expert_kernel(te_ref, na_ref, par_ref, *refs):
    x_refs, (gs_ref, wgu_ref, bgu_ref, wd_ref, bd_ref) = refs[:PACK_SLABS], refs[PACK_SLABS:PACK_SLABS + 5]
    y_refs = refs[PACK_SLABS + 5:PACK_SLABS + 5 + PACK_SLABS]
    wgu_bf, wd_bf = refs[PACK_SLABS + 5 + PACK_SLABS:]
    s = pl.program_id(0)
    t = jnp.maximum(s - 1, 0)
    do_tile = (s >= 1) & (t < na_ref[0])

    def starts_expert(i):
        i = jnp.minimum(i, te_ref.shape[0] - 1)
        return (i == 0) | (te_ref[i] != te_ref[jnp.maximum(i - 1, 0)])

    @pl.when(do_tile & starts_expert(t))
    def _():
        wd_bf[...] = wd_ref[0].astype(BF16)

    @pl.when(do_tile)
    def _():
        words = [r[...] for r in x_refs]
        x = jnp.concatenate(
            [pltpu.unpack_elementwise(w, index=index, packed_dtype=BF16, unpacked_dtype=F32)
             for index in range(2) for w in words], axis=1).astype(BF16)
        hmid = _dot(x, wgu_bf[par_ref[t]]) + bgu_ref[0]
        gate = jnp.minimum(hmid[:, :D_FF], SWIGLU_LIMIT)
        up = jnp.clip(hmid[:, D_FF:], -SWIGLU_LIMIT, SWIGLU_LIMIT)
        act = (up + 1.0) * gate * jax.nn.sigmoid(SWIGLU_ALPHA * gate)
        y = _dot(act.astype(BF16), wd_bf[...]) + bd_ref[0]
        y = y * jnp.concatenate([gs_ref[...]] * (D_MODEL // LANES), axis=1)
        packed = pltpu.pack_elementwise([y[:, :PACK_WIDTH], y[:, PACK_WIDTH:]], packed_dtype=BF16)
        for q in range(PACK_SLABS):
            y_refs[q][...] = packed[:, q * SLAB_WIDTH:(q + 1) * SLAB_WIDTH]

    @pl.when((s < na_ref[0]) & starts_expert(s))
    def _():
        wgu_bf[par_ref[s]] = wgu_ref[0].astype(BF16)


def _experts(tile_e, n_active, xs, gs, wgu, bgu, wd, bd):
    tm = ROW_TILE
    n_tiles = gs.shape[0] // tm
    parity = (jnp.cumsum(jnp.concatenate([jnp.zeros((1,), I32), (tile_e[1:] != tile_e[:-1]).astype(I32)])) % 2
              ).astype(I32)
    tile_of = lambda s, na: jnp.clip(s - 1, 0, na[0] - 1)
    row_map = lambda s, te, na, par: (tile_of(s, na), 0)
    w_map = lambda s, te, na, par: (te[tile_of(s, na)], 0, 0)
    wd_map = lambda s, te, na, par: (0, te[tile_of(s, na)], 0, 0)
    wgu_map = lambda s, te, na, par: (0, te[jnp.minimum(s, na[0] - 1)], 0, 0)
    return pl.pallas_call(
        _expert_kernel,
        grid_spec=pltpu.PrefetchScalarGridSpec(
            num_scalar_prefetch=3,
            grid=(n_tiles + 1,),
            in_specs=[pl.BlockSpec((tm, SLAB_WIDTH), row_map)] * PACK_SLABS + [
                pl.BlockSpec((tm, LANES), row_map),
                pl.BlockSpec((None, 1, D_MODEL, 2 * D_FF), wgu_map),
                pl.BlockSpec((1, 1, 2 * D_FF), w_map),
                pl.BlockSpec((None, 1, D_FF, D_MODEL), wd_map),
                pl.BlockSpec((1, 1, D_MODEL), w_map),
            ],
            out_specs=[pl.BlockSpec((tm, SLAB_WIDTH), row_map)] * PACK_SLABS,
            scratch_shapes=[pltpu.VMEM((2, D_MODEL, 2 * D_FF), BF16), pltpu.VMEM((D_FF, D_MODEL), BF16)],
        ),
        out_shape=[jax.ShapeDtypeStruct((n_tiles * tm, SLAB_WIDTH), jnp.uint32)] * PACK_SLABS,
        compiler_params=pltpu.CompilerParams(
            dimension_semantics=("arbitrary",), vmem_limit_bytes=VMEM_LIMIT),
        name="experts",
    )(tile_e, n_active, parity, *xs, gs, wgu, bgu, wd, bd)


def _sc_combine(ys, dest_sets):
    n_out = TOP_K * PACK_SLABS
    n_sets = len(dest_sets)
    toks = [dests[0].shape[1] for dests in dest_sets]
    mesh = plsc.VectorSubcoreMesh(core_axis_name="core", subcore_axis_name="subcore")
    out_type = [jax.ShapeDtypeStruct((t, SLAB_WIDTH), jnp.uint32) for t in toks for _ in range(n_out)]

    @functools.partial(pl.kernel, out_type=out_type, mesh=mesh, scratch_types=[], name="sc_combine")
    def run(*refs):
        ys_hbm = refs[:PACK_SLABS]
        out0 = PACK_SLABS + TOP_K * n_sets

        for s in range(n_sets):
            i_hbm = refs[PACK_SLABS + TOP_K * s:PACK_SLABS + TOP_K * (s + 1)]
            f_hbm = refs[out0 + n_out * s:out0 + n_out * (s + 1)]
            n_tok = toks[s]
            for kk in range(TOP_K):
                for q in range(PACK_SLABS):
                    def body(i_vmem, o_vmem, q=q):
                        pltpu.sync_copy(ys_hbm[q].at[i_vmem.at[0]], o_vmem)

                    _sc_pipeline(body, n_tok, [_index_specs()[0]],
                                 [pl.BlockSpec((SC_WINDOW, SLAB_WIDTH), lambda i: (i, 0))]
                                 )(i_hbm[kk], f_hbm[kk * PACK_SLABS + q])

    outs = run(*ys, *[d for dests in dest_sets for d in dests])
    return [outs[n_out * s:n_out * (s + 1)] for s in range(n_sets)]


def _final_kernel(*refs):
    n_in = TOP_K * PACK_SLABS
    h_ref, p_refs, g_ref, b_ref, out_ref = refs[0], refs[1:1 + n_in], refs[1 + n_in], refs[2 + n_in], refs[-1]
    halves = []
    for index in range(2):
        for q in range(PACK_SLABS):
            parts = [pltpu.unpack_elementwise(p_refs[kk * PACK_SLABS + q][...], index=index,
                                              packed_dtype=BF16, unpacked_dtype=F32) for kk in range(TOP_K)]
            halves.append((parts[0] + parts[1]) + (parts[2] + parts[3]))
    f = jnp.concatenate(halves, axis=1)
    out_ref[...] = _layer_norm(DN_ALPHA * h_ref[...] + f, g_ref[...], b_ref[...])


def _final_norm(h, picked, ln_g, ln_b, out_rows, first_row, earlier=None):
    tt = FINAL_TILE
    tok = h.shape[0]
    first_tile = first_row // tt
    row = lambda w: pl.BlockSpec((tt, w), lambda i: (i, 0))
    in_specs = ([row(D_MODEL)] + [row(SLAB_WIDTH)] * (TOP_K * PACK_SLABS)
                + [_const_spec((1, D_MODEL)), _const_spec((1, D_MODEL))])
    args = [h, *picked, ln_g, ln_b]
    aliases = {}
    if earlier is not None:
        in_specs.append(pl.BlockSpec(memory_space=pl.ANY))
        aliases = {len(args): 0}
        args.append(earlier)
    return pl.pallas_call(
        _final_kernel,
        grid=(tok // tt,),
        in_specs=in_specs,
        out_specs=pl.BlockSpec((tt, D_MODEL), lambda i: (i + first_tile, 0)),
        out_shape=jax.ShapeDtypeStruct((out_rows, D_MODEL), F32),
        input_output_aliases=aliases,
        compiler_params=pltpu.CompilerParams(dimension_semantics=("arbitrary",)),
        name="final_norm",
    )(*args)


def _pad_heads_last(w):
    lead = w.shape[:-1]
    w = w.reshape(lead + (N_KV_HEADS, Q_GROUP, HEAD_DIM))
    zero = jnp.zeros_like(w[..., 0, :, :])
    parts = []
    for hk in range(N_KV_HEADS):
        cols = [zero] * N_KV_HEADS
        cols[hk] = w[..., hk, :, :]
        parts.append(jnp.concatenate(cols, axis=-1))
    return jnp.stack(parts, axis=-3).reshape(lead + (QPAD_WIDTH,))


def _mixer_weights(w_in, b_in, ln_g, ln_b, w_s, b_s, w_o, b_o, ln1_g, ln1_b, w_router, b_router):
    win = jnp.concatenate([_pad_heads_last(w_in[:, :ATTN_WIDTH]), w_in[:, ATTN_WIDTH:]], axis=1).astype(BF16)
    bin_ = jnp.concatenate([_pad_heads_last(b_in[:ATTN_WIDTH]), b_in[ATTN_WIDTH:]])[None, :]
    woa = _pad_heads_last(w_o[:ATTN_WIDTH].T).T.astype(BF16)
    wog = w_o[ATTN_WIDTH:].astype(BF16)
    wr = jnp.pad(w_router, ((0, 0), (0, LANES - N_EXPERTS)))
    wrh = wr.astype(BF16)
    wrl = (wr - wrh.astype(F32)).astype(BF16)
    br = jnp.pad(b_router, (0, LANES - N_EXPERTS))[None, :]
    tri = (lax.broadcasted_iota(I32, (MIX_TILE, MIX_TILE), 1)
           < lax.broadcasted_iota(I32, (MIX_TILE, MIX_TILE), 0)).astype(BF16)
    return (win, bin_, ln_g.reshape(1, GMLP_WIDTH), ln_b.reshape(1, GMLP_WIDTH), w_s, b_s.T,
            woa, wog, b_o[None, :], ln1_g[None, :], ln1_b[None, :], wrh, wrl, br, tri)


def _dest_rows(meta, pstart):
    e = meta[META_E:META_E + TOP_K].astype(I32)
    r = meta[META_R:META_R + TOP_K].astype(I32)
    hit = e[None] == jnp.arange(N_EXPERTS, dtype=I32)[:, None, None]
    return jnp.sum(jnp.where(hit, pstart[:, None, None], 0), axis=0) + r


def _moe(token_sets, cnt, experts):
    n_assign = sum(hw.shape[0] for hw, _, _ in token_sets) * TOP_K
    n_tiles = (n_assign + N_EXPERTS * (ROW_TILE - 1)) // ROW_TILE
    counts = cnt[0, :N_EXPERTS].astype(I32)
    padded = (counts + ROW_TILE - 1) // ROW_TILE * ROW_TILE
    pend = jnp.cumsum(padded)
    pstart = pend - padded
    n_active = (pend[-1:] // ROW_TILE).astype(I32)
    tile_start = jnp.arange(n_tiles, dtype=I32) * ROW_TILE
    tile_e = jnp.minimum(jnp.sum(pend[None, :] <= tile_start[:, None], axis=1), N_EXPERTS - 1).astype(I32)

    lists = lambda d: [d[kk][None, :] for kk in range(TOP_K)]
    dest_sets = [lists(_dest_rows(meta, pstart)) for _, _, meta in token_sets]
    xs, gs = _sc_dispatch([(hw, gates, dests) for (hw, gates, _), dests in zip(token_sets, dest_sets)],
                          n_tiles * ROW_TILE)
    ys = _experts(tile_e, n_active, xs, gs, *experts)
    return _sc_combine(ys, dest_sets)


def kernel(x_prompt, x_sample, cache_k, cache_v, w_in, b_in, attn_sinks, gmlp_ln_g, gmlp_ln_b, w_spatial, b_spatial, w_o, b_o, ln1_g, ln1_b, w_router, b_router, w_gate_up, b_gate_up, w_down, b_down, ln2_g, ln2_b):
    assert w_in.shape[0] == DEPTH
    batch, seq, _ = x_prompt.shape
    dec_batch, dec_seq, _ = x_sample.shape
    tok_p, tok_s = batch * seq, dec_batch * dec_seq
    assert dec_seq == CHUNK and seq % MIX_TILE == 0 and batch >= 2
    assert all(t % MIX_TILE == 0 and t % FINAL_TILE == 0 and t % SC_WINDOW == 0 for t in (tok_p, tok_s))

    weights = _mixer_weights(w_in[0], b_in[0], gmlp_ln_g[0], gmlp_ln_b[0], w_spatial[0], b_spatial[0],
                             w_o[0], b_o[0], ln1_g[0], ln1_b[0], w_router[0], b_router[0])
    sinks = attn_sinks[0]
    experts = (w_gate_up, b_gate_up[0][:, None, :], w_down, b_down[0][:, None, :])

    batch_a = batch // 2
    h_a, hw_a, meta_a, gates_a, cnt_a, kt_a, vt_a = _mix_prompt(sinks, x_prompt, weights, 0, batch_a)
    picked_a, = _moe([(hw_a, gates_a, meta_a)], cnt_a, experts)
    h_b, hw_b, meta_b, gates_b, cnt_b, kt_b, vt_b = _mix_prompt(sinks, x_prompt, weights, batch_a, batch - batch_a)
    h_s, hw_s, meta_s, gates_s, cnt_bs, ks, vs, gvs = _mix_sample(
        sinks, x_sample.reshape(tok_s, D_MODEL),
        cache_k[0].reshape(dec_batch * WINDOW, KV_WIDTH), cache_v[0].reshape(dec_batch * WINDOW, KV_WIDTH),
        cnt_b, weights)
    picked_b, picked_s = _moe([(hw_b, gates_b, meta_b), (hw_s, gates_s, meta_s)], cnt_bs, experts)

    g2, b2 = ln2_g[0][None, :], ln2_b[0][None, :]
    y_p = _final_norm(h_a, picked_a, g2, b2, tok_p, 0)
    y_p = _final_norm(h_b, picked_b, g2, b2, tok_p, batch_a * seq, earlier=y_p).reshape(batch, seq, D_MODEL)
    y_s = _final_norm(h_s, picked_s, g2, b2, tok_s, 0).reshape(dec_batch, dec_seq, D_MODEL)
    kt, vt = jnp.concatenate([kt_a, kt_b]), jnp.concatenate([vt_a, vt_b])

    kv5 = lambda a, nb, rows: a.reshape(DEPTH, nb, rows, N_KV_HEADS, HEAD_DIM)
    return (y_p, y_s, kv5(kt, batch, WINDOW), kv5(vt, batch, WINDOW),
            kv5(ks, dec_batch, dec_seq), kv5(vs, dec_batch, dec_seq),
            gvs.reshape(DEPTH, dec_batch, dec_seq, GMLP_GROUPS, GMLP_GROUP_DIM))
```

```python
import functools

import jax
import jax.numpy as jnp
from jax import lax
from jax.experimental import pallas as pl
from jax.experimental.pallas import tpu as pltpu
from jax.experimental.pallas import tpu_sc as plsc

F32 = jnp.float32
BF16 = jnp.bfloat16
I32 = jnp.int32

D_MODEL = 1024
CHUNK = 64
N_HEADS = 8
N_KV_HEADS = 2
HEAD_DIM = 64
Q_GROUP = N_HEADS // N_KV_HEADS
KV_WIDTH = N_KV_HEADS * HEAD_DIM
WINDOW = 128
KEYS = WINDOW + CHUNK
GMLP_GROUPS = 4
GMLP_GROUP_DIM = 128
GMLP_WIDTH = GMLP_GROUPS * GMLP_GROUP_DIM
GMLP_CHUNK = 128
ATTN_WIDTH = N_HEADS * HEAD_DIM
N_EXPERTS = 32
TOP_K = 4
D_FF = 1024
SWIGLU_LIMIT = 7.0
SWIGLU_ALPHA = 1.702
DEPTH = 1
DN_ALPHA = (2 * DEPTH) ** 0.25
LN_EPS = 1e-5
NEG_INF = -1e30

LANES = 128
K_OFF = ATTN_WIDTH
V_OFF = K_OFF + KV_WIDTH
U_OFF = V_OFF + KV_WIDTH
GV_OFF = U_OFF + GMLP_WIDTH
ZPAD_WIDTH = GV_OFF + GMLP_WIDTH

MIX_TILE = 512
ROW_TILE = 512
FINAL_TILE = 512
SC_WINDOW = 128
SC_SLABS = 4
SLAB_WIDTH = D_MODEL // SC_SLABS
PACK_WIDTH = D_MODEL // 2
PACK_SLABS = PACK_WIDTH // SLAB_WIDTH
VMEM_LIMIT = 58 * 1024 * 1024

META_E, META_R = 0, 4


def _gelu_tanh(x):
    return 0.5 * x * (1.0 + jnp.tanh(0.7978845608028654 * (x + 0.044715 * x * x * x)))


def _layer_norm(x, g, b):
    mu = jnp.mean(x, axis=-1, keepdims=True)
    xc = x - mu
    var = jnp.mean(xc * xc, axis=-1, keepdims=True)
    return xc * lax.rsqrt(var + LN_EPS) * g + b


def _dot(a, b):
    return jnp.dot(a, b, preferred_element_type=F32)


def _mixer_kernel(*refs, is_prompt, tt):
    n_chunks = tt // CHUNK
    it = iter(refs)
    sinks_ref = next(it)
    x_ref = next(it)
    if not is_prompt:
        ck_ref, cv_ref, base0_ref = next(it), next(it), next(it)
    (win_ref, bin_ref, lng_ref, lnb_ref, ws_ref, bs_ref, woa_ref, wog_ref, bo_ref, l1g_ref, l1b_ref,
     wrh_ref, wrl_ref, br_ref, tri_ref) = (next(it) for _ in range(15))
    h_ref, hw_ref, meta_ref, gate_ref, cnt_ref = (next(it) for _ in range(5))
    if is_prompt:
        kt_ref, vt_ref = next(it), next(it)
    else:
        ko_ref, vo_ref, gvo_ref = next(it), next(it), next(it)
    z_ref, kext_ref, vext_ref, att_ref, gm_ref, base_ref = (next(it) for _ in range(6))

    if is_prompt:
        first = (pl.program_id(0) == 0) & (pl.program_id(1) == 0)
        tile_in_seq = pl.program_id(1)

        @pl.when(first)
        def _():
            base_ref[...] = jnp.zeros_like(base_ref)

        @pl.when(tile_in_seq == 0)
        def _():
            kext_ref[:, 0:WINDOW, :] = jnp.zeros((N_KV_HEADS, WINDOW, KV_WIDTH), BF16)
            vext_ref[:, 0:WINDOW, :] = jnp.zeros((N_KV_HEADS, WINDOW, KV_WIDTH), BF16)
    else:
        @pl.when(pl.program_id(0) == 0)
        def _():
            base_ref[...] = base0_ref[...]

    x = x_ref[...]
    z_ref[...] = _dot(x.astype(BF16), win_ref[...]) + bin_ref[...]

    def put_kv(ext_ref, row0, rows_f32):
        n = rows_f32.shape[0]
        swapped = pltpu.roll(rows_f32, HEAD_DIM, axis=1)
        low = lax.broadcasted_iota(I32, (n, KV_WIDTH), 1) < HEAD_DIM
        ext_ref[0, row0:row0 + n, :] = jnp.where(low, rows_f32, swapped).astype(BF16)
        ext_ref[1, row0:row0 + n, :] = jnp.where(low, swapped, rows_f32).astype(BF16)

    k = z_ref[:, K_OFF:K_OFF + KV_WIDTH]
    v = z_ref[:, V_OFF:V_OFF + KV_WIDTH]
    if is_prompt:
        put_kv(kext_ref, WINDOW, k)
        put_kv(vext_ref, WINDOW, v)
        kt_ref[...] = k[tt - WINDOW:, :]
        vt_ref[...] = v[tt - WINDOW:, :]
        key_stride = CHUNK
    else:
        ko_ref[...] = k
        vo_ref[...] = v
        for c in range(n_chunks):
            put_kv(kext_ref, KEYS * c, ck_ref[WINDOW * c:WINDOW * (c + 1), :])
            put_kv(vext_ref, KEYS * c, cv_ref[WINDOW * c:WINDOW * (c + 1), :])
            put_kv(kext_ref, KEYS * c + WINDOW, k[CHUNK * c:CHUNK * (c + 1), :])
            put_kv(vext_ref, KEYS * c + WINDOW, v[CHUNK * c:CHUNK * (c + 1), :])
        key_stride = KEYS

    rows = Q_GROUP * CHUNK
    row_i = lax.broadcasted_iota(I32, (rows, 1), 0)
    key_i = lax.broadcasted_iota(I32, (rows, KEYS), 1)
    sink_cols = []
    for hk in range(N_KV_HEADS):
        s = [sinks_ref[hk * Q_GROUP + g] for g in range(Q_GROUP)]
        sink_cols.append(jnp.where(row_i < CHUNK, s[0],
                                   jnp.where(row_i < 2 * CHUNK, s[1],
                                             jnp.where(row_i < 3 * CHUNK, s[2], s[3]))))
    low_half = lax.broadcasted_iota(I32, (CHUNK, LANES), 1) < HEAD_DIM
    for c in range(n_chunks):
        outs = []
        for hk in range(N_KV_HEADS):
            kc = kext_ref[hk, key_stride * c:key_stride * c + KEYS, :]
            vc = vext_ref[hk, key_stride * c:key_stride * c + KEYS, :]
            heads = []
            for pair in range(hk * Q_GROUP // 2, (hk + 1) * Q_GROUP // 2):
                both = z_ref[CHUNK * c:CHUNK * (c + 1), pair * LANES:(pair + 1) * LANES]
                heads += [jnp.where(low_half, both, 0.0), jnp.where(low_half, 0.0, both)]
            qs = jnp.concatenate(heads, axis=0).astype(BF16)
            sc = lax.dot_general(qs, kc, (((1,), (1,)), ((), ())), preferred_element_type=F32)
            sc = sc * (HEAD_DIM ** -0.5)
            if is_prompt and c < WINDOW // CHUNK:
                valid = (key_i >= WINDOW - CHUNK * c) | (tile_in_seq > 0)
                sc = jnp.where(valid, sc, NEG_INF)
            sink = sink_cols[hk]
            m = jnp.maximum(jnp.max(sc, axis=-1, keepdims=True), sink)
            p = jnp.exp(sc - m)
            den = jnp.sum(p, axis=-1, keepdims=True) + jnp.exp(sink - m)
            o = _dot(p.astype(BF16), vc) / den
            for j in range(Q_GROUP // 2):
                even, odd = o[CHUNK * 2 * j:CHUNK * (2 * j + 1), :], o[CHUNK * (2 * j + 1):CHUNK * (2 * j + 2), :]
                outs.append(jnp.where(low_half, even, odd))
        att_ref[CHUNK * c:CHUNK * (c + 1), :] = jnp.concatenate(outs, axis=1).astype(BF16)

    if is_prompt:
        kext_ref[:, 0:WINDOW, :] = kext_ref[:, tt:tt + WINDOW, :]
        vext_ref[:, 0:WINDOW, :] = vext_ref[:, tt:tt + WINDOW, :]

    gc = GMLP_CHUNK if is_prompt else CHUNK
    ri = lax.broadcasted_iota(I32, (gc, gc), 0)
    ci = lax.broadcasted_iota(I32, (gc, gc), 1)
    causal = (ci // CHUNK) <= (ri // CHUNK)
    for g in range(GMLP_GROUPS):
        lo, hi = g * GMLP_GROUP_DIM, (g + 1) * GMLP_GROUP_DIM
        u = _gelu_tanh(z_ref[:, U_OFF + lo:U_OFF + hi])
        gv = _layer_norm(_gelu_tanh(z_ref[:, GV_OFF + lo:GV_OFF + hi]), lng_ref[:, lo:hi], lnb_ref[:, lo:hi])
        if not is_prompt:
            gvo_ref[:, lo:hi] = gv
        gvb = gv.astype(BF16)
        wm = jnp.where(causal, ws_ref[g, 0:gc, 0:gc], 0.0).astype(BF16)
        bcol = bs_ref[0:gc, g:g + 1]
        for n in range(tt // gc):
            sp = _dot(wm, gvb[gc * n:gc * (n + 1), :]) + bcol
            gm_ref[gc * n:gc * (n + 1), lo:hi] = (u[gc * n:gc * (n + 1), :] * sp).astype(BF16)

    y = _dot(att_ref[...], woa_ref[...]) + _dot(gm_ref[...], wog_ref[...]) + bo_ref[...]
    h = _layer_norm(DN_ALPHA * x + y, l1g_ref[...], l1b_ref[...])
    h_ref[...] = h
    hw_ref[...] = pltpu.pack_elementwise([h[:, :PACK_WIDTH], h[:, PACK_WIDTH:]], packed_dtype=BF16)

    h_hi = h.astype(BF16)
    h_lo = (h - h_hi.astype(F32)).astype(BF16)
    logits = _dot(h_hi, wrh_ref[...]) + _dot(h_lo, wrh_ref[...]) + _dot(h_hi, wrl_ref[...]) + br_ref[...]
    lane = lax.broadcasted_iota(I32, (tt, LANES), 1)
    lane_f = lane.astype(F32)
    l = jnp.where(lane < N_EXPERTS, logits, -jnp.inf)
    tops, idxs, hots = [], [], []
    for _ in range(TOP_K):
        m = jnp.max(l, axis=-1, keepdims=True)
        idx = jnp.min(jnp.where(l == m, lane_f, float(LANES)), axis=-1, keepdims=True)
        hot = lane_f == idx
        l = jnp.where(hot, -jnp.inf, l)
        tops.append(m)
        idxs.append(idx)
        hots.append(hot)
    es = [jnp.exp(t - tops[0]) for t in tops]
    esum = es[0] + es[1] + es[2] + es[3]
    chosen = jnp.where(hots[0] | hots[1] | hots[2] | hots[3], 1.0, 0.0)
    before = _dot(tri_ref[...], chosen.astype(BF16)) + base_ref[...]
    meta = jnp.zeros((tt, LANES), F32)
    for kk in range(TOP_K):
        rank = jnp.sum(jnp.where(hots[kk], before, 0.0), axis=-1, keepdims=True)
        meta = jnp.where(lane == META_E + kk, idxs[kk], meta)
        meta = jnp.where(lane == META_R + kk, rank, meta)
        gate_ref[:, kk * LANES:(kk + 1) * LANES] = jnp.broadcast_to(es[kk] / esum, (tt, LANES))
    meta_ref[...] = jnp.transpose(meta)[0:2 * TOP_K, :]
    base_ref[...] = base_ref[...] + jnp.sum(chosen, axis=0, keepdims=True)
    cnt_ref[...] = base_ref[...]


def _const_spec(shape):
    nd = len(shape)
    return pl.BlockSpec(shape, lambda *_: (0,) * nd, pipeline_mode=pl.Buffered(1))


def _mixer_weight_specs(tt):
    return [
        _const_spec((D_MODEL, ZPAD_WIDTH)), _const_spec((1, ZPAD_WIDTH)),
        _const_spec((1, GMLP_WIDTH)), _const_spec((1, GMLP_WIDTH)),
        _const_spec((GMLP_GROUPS, GMLP_CHUNK, GMLP_CHUNK)), _const_spec((GMLP_CHUNK, GMLP_GROUPS)),
        _const_spec((ATTN_WIDTH, D_MODEL)), _const_spec((GMLP_WIDTH, D_MODEL)), _const_spec((1, D_MODEL)),
        _const_spec((1, D_MODEL)), _const_spec((1, D_MODEL)),
        _const_spec((D_MODEL, LANES)), _const_spec((D_MODEL, LANES)), _const_spec((1, LANES)),
        _const_spec((tt, tt)),
    ]


def _mixer_scratch(tt, kext_rows):
    return [
        pltpu.VMEM((tt, ZPAD_WIDTH), F32),
        pltpu.VMEM((N_KV_HEADS, kext_rows, KV_WIDTH), BF16), pltpu.VMEM((N_KV_HEADS, kext_rows, KV_WIDTH), BF16),
        pltpu.VMEM((tt, ATTN_WIDTH), BF16), pltpu.VMEM((tt, GMLP_WIDTH), BF16),
        pltpu.VMEM((1, LANES), F32),
    ]


def _mix_prompt(sinks, x, weights, first_batch, batch):
    seq = x.shape[1]
    tt = MIX_TILE
    n_tiles = seq // tt
    tok = batch * seq
    smem = pl.BlockSpec(memory_space=pltpu.SMEM)
    return pl.pallas_call(
        functools.partial(_mixer_kernel, is_prompt=True, tt=tt),
        grid=(batch, n_tiles),
        in_specs=[smem, pl.BlockSpec((None, tt, D_MODEL), lambda b, i: (b + first_batch, i, 0))]
        + _mixer_weight_specs(tt),
        out_specs=[
            pl.BlockSpec((tt, D_MODEL), lambda b, i: (b * n_tiles + i, 0)),
            pl.BlockSpec((tt, PACK_WIDTH), lambda b, i: (b * n_tiles + i, 0)),
            pl.BlockSpec((2 * TOP_K, tt), lambda b, i: (0, b * n_tiles + i)),
            pl.BlockSpec((tt, TOP_K * LANES), lambda b, i: (b * n_tiles + i, 0)),
            pl.BlockSpec((1, LANES), lambda b, i: (0, 0)),
            pl.BlockSpec((WINDOW, KV_WIDTH), lambda b, i: (b, 0)),
            pl.BlockSpec((WINDOW, KV_WIDTH), lambda b, i: (b, 0)),
        ],
        out_shape=[
            jax.ShapeDtypeStruct((tok, D_MODEL), F32), jax.ShapeDtypeStruct((tok, PACK_WIDTH), jnp.uint32),
            jax.ShapeDtypeStruct((2 * TOP_K, tok), F32), jax.ShapeDtypeStruct((tok, TOP_K * LANES), F32),
            jax.ShapeDtypeStruct((1, LANES), F32),
            jax.ShapeDtypeStruct((batch * WINDOW, KV_WIDTH), F32),
            jax.ShapeDtypeStruct((batch * WINDOW, KV_WIDTH), F32),
        ],
        scratch_shapes=_mixer_scratch(tt, WINDOW + tt),
        compiler_params=pltpu.CompilerParams(
            dimension_semantics=("arbitrary", "arbitrary"), vmem_limit_bytes=VMEM_LIMIT),
        name="mix_prompt",
    )(sinks, x, *weights)


def _mix_sample(sinks, x2, ck, cv, base0, weights):
    tok = x2.shape[0]
    tt = MIX_TILE
    n_chunks = tt // CHUNK
    cache_rows = n_chunks * WINDOW
    smem = pl.BlockSpec(memory_space=pltpu.SMEM)
    row = lambda w: pl.BlockSpec((tt, w), lambda i: (i, 0))
    return pl.pallas_call(
        functools.partial(_mixer_kernel, is_prompt=False, tt=tt),
        grid=(tok // tt,),
        in_specs=[smem, row(D_MODEL),
                  pl.BlockSpec((cache_rows, KV_WIDTH), lambda i: (i, 0)),
                  pl.BlockSpec((cache_rows, KV_WIDTH), lambda i: (i, 0)),
                  _const_spec((1, LANES))] + _mixer_weight_specs(tt),
        out_specs=[row(D_MODEL), row(PACK_WIDTH), pl.BlockSpec((2 * TOP_K, tt), lambda i: (0, i)),
                   row(TOP_K * LANES),
                   pl.BlockSpec((1, LANES), lambda i: (0, 0)),
                   row(KV_WIDTH), row(KV_WIDTH), row(GMLP_WIDTH)],
        out_shape=[
            jax.ShapeDtypeStruct((tok, D_MODEL), F32), jax.ShapeDtypeStruct((tok, PACK_WIDTH), jnp.uint32),
            jax.ShapeDtypeStruct((2 * TOP_K, tok), F32), jax.ShapeDtypeStruct((tok, TOP_K * LANES), F32),
            jax.ShapeDtypeStruct((1, LANES), F32),
            jax.ShapeDtypeStruct((tok, KV_WIDTH), F32), jax.ShapeDtypeStruct((tok, KV_WIDTH), F32),
            jax.ShapeDtypeStruct((tok, GMLP_WIDTH), F32),
        ],
        scratch_shapes=_mixer_scratch(tt, n_chunks * KEYS),
        compiler_params=pltpu.CompilerParams(
            dimension_semantics=("arbitrary",), vmem_limit_bytes=VMEM_LIMIT),
        name="mix_sample",
    )(sinks, x2, ck, cv, base0, *weights)


def _sc_pipeline(body, n_tok, in_specs, out_specs):
    return pltpu.emit_pipeline(
        body, grid=(n_tok // SC_WINDOW,), in_specs=in_specs, out_specs=out_specs,
        core_axis_name=("core", "subcore"), dimension_semantics=(pltpu.PARALLEL,))


def _index_specs():
    return [pl.BlockSpec((1, SC_WINDOW), lambda i: (0, i))] * TOP_K


def _sc_dispatch(token_sets, n_rows):
    win, wid = SC_WINDOW, SLAB_WIDTH
    per_set = 2 + TOP_K
    n_in = per_set * len(token_sets)
    mesh = plsc.VectorSubcoreMesh(core_axis_name="core", subcore_axis_name="subcore")
    out_type = ([jax.ShapeDtypeStruct((n_rows, wid), jnp.uint32)] * PACK_SLABS
                + [jax.ShapeDtypeStruct((n_rows, LANES), F32)])

    @functools.partial(pl.kernel, out_type=out_type, mesh=mesh, scratch_types=[], name="sc_dispatch")
    def run(*refs):
        xs_hbm, gs_hbm = refs[n_in:n_in + PACK_SLABS], refs[n_in + PACK_SLABS]

        for s in range(len(token_sets)):
            h_hbm, g_hbm = refs[per_set * s], refs[per_set * s + 1]
            i_hbm = refs[per_set * s + 2:per_set * (s + 1)]
            n_tok = h_hbm.shape[0]
            for q in range(PACK_SLABS):
                def rows_body(x_vmem, *i_vmem, q=q):
                    for kk in range(TOP_K):
                        pltpu.sync_copy(x_vmem, xs_hbm[q].at[i_vmem[kk].at[0]])

                _sc_pipeline(rows_body, n_tok, [pl.BlockSpec((win, wid), lambda i, q=q: (i, q))] + _index_specs(),
                             [])(h_hbm, *i_hbm)
            for kk in range(TOP_K):
                def gate_body(g_vmem, i_vmem):
                    pltpu.sync_copy(g_vmem, gs_hbm.at[i_vmem.at[0]])

                _sc_pipeline(gate_body, n_tok,
                             [pl.BlockSpec((win, LANES), lambda i, kk=kk: (i, kk)), _index_specs()[0]],
                             [])(g_hbm, i_hbm[kk])

    outs = run(*[a for hw, gates, dests in token_sets for a in (hw, gates, *dests)])
    return outs[:PACK_SLABS], outs[PACK_SLABS]


def _expert_kernel(te_ref, na_ref, par_ref, *refs):
    x_refs, (gs_ref, wgu_ref, bgu_ref, wd_ref, bd_ref) = refs[:PACK_SLABS], refs[PACK_SLABS:PACK_SLABS + 5]
    y_refs = refs[PACK_SLABS + 5:PACK_SLABS + 5 + PACK_SLABS]
    wgu_bf, wd_bf = refs[PACK_SLABS + 5 + PACK_SLABS:]
    s = pl.program_id(0)
    t = jnp.maximum(s - 1, 0)
    do_tile = (s >= 1) & (t < na_ref[0])

    def starts_expert(i):
        i = jnp.minimum(i, te_ref.shape[0] - 1)
        return (i == 0) | (te_ref[i] != te_ref[jnp.maximum(i - 1, 0)])

    @pl.when(do_tile & starts_expert(t))
    def _():
        wd_bf[...] = wd_ref[0].astype(BF16)

    @pl.when(do_tile)
    def _():
        words = [r[...] for r in x_refs]
        x = jnp.concatenate(
            [pltpu.unpack_elementwise(w, index=index, packed_dtype=BF16, unpacked_dtype=F32)
             for index in range(2) for w in words], axis=1).astype(BF16)
        hmid = _dot(x, wgu_bf[par_ref[t]]) + bgu_ref[0]
        gate = jnp.minimum(hmid[:, :D_FF], SWIGLU_LIMIT)
        up = jnp.clip(hmid[:, D_FF:], -SWIGLU_LIMIT, SWIGLU_LIMIT)
        act = (up + 1.0) * gate * jax.nn.sigmoid(SWIGLU_ALPHA * gate)
        y = _dot(act.astype(BF16), wd_bf[...]) + bd_ref[0]
        y = y * jnp.concatenate([gs_ref[...]] * (D_MODEL // LANES), axis=1)
        packed = pltpu.pack_elementwise([y[:, :PACK_WIDTH], y[:, PACK_WIDTH:]], packed_dtype=BF16)
        for q in range(PACK_SLABS):
            y_refs[q][...] = packed[:, q * SLAB_WIDTH:(q + 1) * SLAB_WIDTH]

    @pl.when((s < na_ref[0]) & starts_expert(s))
    def _():
        wgu_bf[par_ref[s]] = wgu_ref[0].astype(BF16)


def _experts(tile_e, n_active, xs, gs, wgu, bgu, wd, bd):
    tm = ROW_TILE
    n_tiles = gs.shape[0] // tm
    parity = (jnp.cumsum(jnp.concatenate([jnp.zeros((1,), I32), (tile_e[1:] != tile_e[:-1]).astype(I32)])) % 2
              ).astype(I32)
    tile_of = lambda s, na: jnp.clip(s - 1, 0, na[0] - 1)
    row_map = lambda s, te, na, par: (tile_of(s, na), 0)
    w_map = lambda s, te, na, par: (te[tile_of(s, na)], 0, 0)
    wd_map = lambda s, te, na, par: (0, te[tile_of(s, na)], 0, 0)
    wgu_map = lambda s, te, na, par: (0, te[jnp.minimum(s, na[0] - 1)], 0, 0)
    return pl.pallas_call(
        _expert_kernel,
        grid_spec=pltpu.PrefetchScalarGridSpec(
            num_scalar_prefetch=3,
            grid=(n_tiles + 1,),
            in_specs=[pl.BlockSpec((tm, SLAB_WIDTH), row_map)] * PACK_SLABS + [
                pl.BlockSpec((tm, LANES), row_map),
                pl.BlockSpec((None, 1, D_MODEL, 2 * D_FF), wgu_map),
                pl.BlockSpec((1, 1, 2 * D_FF), w_map),
                pl.BlockSpec((None, 1, D_FF, D_MODEL), wd_map),
                pl.BlockSpec((1, 1, D_MODEL), w_map),
            ],
            out_specs=[pl.BlockSpec((tm, SLAB_WIDTH), row_map)] * PACK_SLABS,
            scratch_shapes=[pltpu.VMEM((2, D_MODEL, 2 * D_FF), BF16), pltpu.VMEM((D_FF, D_MODEL), BF16)],
        ),
        out_shape=[jax.ShapeDtypeStruct((n_tiles * tm, SLAB_WIDTH), jnp.uint32)] * PACK_SLABS,
        compiler_params=pltpu.CompilerParams(
            dimension_semantics=("arbitrary",), vmem_limit_bytes=VMEM_LIMIT),
        name="experts",
    )(tile_e, n_active, parity, *xs, gs, wgu, bgu, wd, bd)


def _sc_combine(ys, dest_sets):
    n_out = TOP_K * PACK_SLABS
    n_sets = len(dest_sets)
    toks = [dests[0].shape[1] for dests in dest_sets]
    mesh = plsc.VectorSubcoreMesh(core_axis_name="core", subcore_axis_name="subcore")
    out_type = [jax.ShapeDtypeStruct((t, SLAB_WIDTH), jnp.uint32) for t in toks for _ in range(n_out)]

    @functools.partial(pl.kernel, out_type=out_type, mesh=mesh, scratch_types=[], name="sc_combine")
    def run(*refs):
        ys_hbm = refs[:PACK_SLABS]
        out0 = PACK_SLABS + TOP_K * n_sets

        for s in range(n_sets):
            i_hbm = refs[PACK_SLABS + TOP_K * s:PACK_SLABS + TOP_K * (s + 1)]
            f_hbm = refs[out0 + n_out * s:out0 + n_out * (s + 1)]
            n_tok = toks[s]
            for kk in range(TOP_K):
                for q in range(PACK_SLABS):
                    def body(i_vmem, o_vmem, q=q):
                        pltpu.sync_copy(ys_hbm[q].at[i_vmem.at[0]], o_vmem)

                    _sc_pipeline(body, n_tok, [_index_specs()[0]],
                                 [pl.BlockSpec((SC_WINDOW, SLAB_WIDTH), lambda i: (i, 0))]
                                 )(i_hbm[kk], f_hbm[kk * PACK_SLABS + q])

    outs = run(*ys, *[d for dests in dest_sets for d in dests])
    return [outs[n_out * s:n_out * (s + 1)] for s in range(n_sets)]


def _final_kernel(*refs):
    n_in = TOP_K * PACK_SLABS
    h_ref, p_refs, g_ref, b_ref, out_ref = refs[0], refs[1:1 + n_in], refs[1 + n_in], refs[2 + n_in], refs[-1]
    halves = []
    for index in range(2):
        for q in range(PACK_SLABS):
            parts = [pltpu.unpack_elementwise(p_refs[kk * PACK_SLABS + q][...], index=index,
                                              packed_dtype=BF16, unpacked_dtype=F32) for kk in range(TOP_K)]
            halves.append((parts[0] + parts[1]) + (parts[2] + parts[3]))
    f = jnp.concatenate(halves, axis=1)
    out_ref[...] = _layer_norm(DN_ALPHA * h_ref[...] + f, g_ref[...], b_ref[...])


def _final_norm(h, picked, ln_g, ln_b, out_rows, first_row, earlier=None):
    tt = FINAL_TILE
    tok = h.shape[0]
    first_tile = first_row // tt
    row = lambda w: pl.BlockSpec((tt, w), lambda i: (i, 0))
    in_specs = ([row(D_MODEL)] + [row(SLAB_WIDTH)] * (TOP_K * PACK_SLABS)
                + [_const_spec((1, D_MODEL)), _const_spec((1, D_MODEL))])
    args = [h, *picked, ln_g, ln_b]
    aliases = {}
    if earlier is not None:
        in_specs.append(pl.BlockSpec(memory_space=pl.ANY))
        aliases = {len(args): 0}
        args.append(earlier)
    return pl.pallas_call(
        _final_kernel,
        grid=(tok // tt,),
        in_specs=in_specs,
        out_specs=pl.BlockSpec((tt, D_MODEL), lambda i: (i + first_tile, 0)),
        out_shape=jax.ShapeDtypeStruct((out_rows, D_MODEL), F32),
        input_output_aliases=aliases,
        compiler_params=pltpu.CompilerParams(dimension_semantics=("arbitrary",)),
        name="final_norm",
    )(*args)


def _mixer_weights(w_in, b_in, ln_g, ln_b, w_s, b_s, w_o, b_o, ln1_g, ln1_b, w_router, b_router):
    win = w_in.astype(BF16)
    bin_ = b_in[None, :]
    woa = w_o[:ATTN_WIDTH].astype(BF16)
    wog = w_o[ATTN_WIDTH:].astype(BF16)
    wr = jnp.pad(w_router, ((0, 0), (0, LANES - N_EXPERTS)))
    wrh = wr.astype(BF16)
    wrl = (wr - wrh.astype(F32)).astype(BF16)
    br = jnp.pad(b_router, (0, LANES - N_EXPERTS))[None, :]
    tri = (lax.broadcasted_iota(I32, (MIX_TILE, MIX_TILE), 1)
           < lax.broadcasted_iota(I32, (MIX_TILE, MIX_TILE), 0)).astype(BF16)
    return (win, bin_, ln_g.reshape(1, GMLP_WIDTH), ln_b.reshape(1, GMLP_WIDTH), w_s, b_s.T,
            woa, wog, b_o[None, :], ln1_g[None, :], ln1_b[None, :], wrh, wrl, br, tri)


def _dest_rows(meta, pstart):
    e = meta[META_E:META_E + TOP_K].astype(I32)
    r = meta[META_R:META_R + TOP_K].astype(I32)
    hit = e[None] == jnp.arange(N_EXPERTS, dtype=I32)[:, None, None]
    return jnp.sum(jnp.where(hit, pstart[:, None, None], 0), axis=0) + r


def _moe(token_sets, cnt, experts):
    n_assign = sum(hw.shape[0] for hw, _, _ in token_sets) * TOP_K
    n_tiles = (n_assign + N_EXPERTS * (ROW_TILE - 1)) // ROW_TILE
    counts = cnt[0, :N_EXPERTS].astype(I32)
    padded = (counts + ROW_TILE - 1) // ROW_TILE * ROW_TILE
    pend = jnp.cumsum(padded)
    pstart = pend - padded
    n_active = (pend[-1:] // ROW_TILE).astype(I32)
    tile_start = jnp.arange(n_tiles, dtype=I32) * ROW_TILE
    tile_e = jnp.minimum(jnp.sum(pend[None, :] <= tile_start[:, None], axis=1), N_EXPERTS - 1).astype(I32)

    lists = lambda d: [d[kk][None, :] for kk in range(TOP_K)]
    dest_sets = [lists(_dest_rows(meta, pstart)) for _, _, meta in token_sets]
    xs, gs = _sc_dispatch([(hw, gates, dests) for (hw, gates, _), dests in zip(token_sets, dest_sets)],
                          n_tiles * ROW_TILE)
    ys = _experts(tile_e, n_active, xs, gs, *experts)
    return _sc_combine(ys, dest_sets)


def kernel(x_prompt, x_sample, cache_k, cache_v, w_in, b_in, attn_sinks, gmlp_ln_g, gmlp_ln_b, w_spatial, b_spatial, w_o, b_o, ln1_g, ln1_b, w_router, b_router, w_gate_up, b_gate_up, w_down, b_down, ln2_g, ln2_b):
    assert w_in.shape[0] == DEPTH
    batch, seq, _ = x_prompt.shape
    dec_batch, dec_seq, _ = x_sample.shape
    tok_p, tok_s = batch * seq, dec_batch * dec_seq
    assert dec_seq == CHUNK and seq % MIX_TILE == 0 and batch >= 2
    assert all(t % MIX_TILE == 0 and t % FINAL_TILE == 0 and t % SC_WINDOW == 0 for t in (tok_p, tok_s))

    weights = _mixer_weights(w_in[0], b_in[0], gmlp_ln_g[0], gmlp_ln_b[0], w_spatial[0], b_spatial[0],
                             w_o[0], b_o[0], ln1_g[0], ln1_b[0], w_router[0], b_router[0])
    sinks = attn_sinks[0]
    experts = (w_gate_up, b_gate_up[0][:, None, :], w_down, b_down[0][:, None, :])

    batch_a = batch // 2
    h_a, hw_a, meta_a, gates_a, cnt_a, kt_a, vt_a = _mix_prompt(sinks, x_prompt, weights, 0, batch_a)
    picked_a, = _moe([(hw_a, gates_a, meta_a)], cnt_a, experts)
    h_b, hw_b, meta_b, gates_b, cnt_b, kt_b, vt_b = _mix_prompt(sinks, x_prompt, weights, batch_a, batch - batch_a)
    h_s, hw_s, meta_s, gates_s, cnt_bs, ks, vs, gvs = _mix_sample(
        sinks, x_sample.reshape(tok_s, D_MODEL),
        cache_k[0].reshape(dec_batch * WINDOW, KV_WIDTH), cache_v[0].reshape(dec_batch * WINDOW, KV_WIDTH),
        cnt_b, weights)
    picked_b, picked_s = _moe([(hw_b, gates_b, meta_b), (hw_s, gates_s, meta_s)], cnt_bs, experts)

    g2, b2 = ln2_g[0][None, :], ln2_b[0][None, :]
    y_p = _final_norm(h_a, picked_a, g2, b2, tok_p, 0)
    y_p = _final_norm(h_b, picked_b, g2, b2, tok_p, batch_a * seq, earlier=y_p).reshape(batch, seq, D_MODEL)
    y_s = _final_norm(h_s, picked_s, g2, b2, tok_s, 0).reshape(dec_batch, dec_seq, D_MODEL)
    kt, vt = jnp.concatenate([kt_a, kt_b]), jnp.concatenate([vt_a, vt_b])

    kv5 = lambda a, nb, rows: a.reshape(DEPTH, nb, rows, N_KV_HEADS, HEAD_DIM)
    return (y_p, y_s, kv5(kt, batch, WINDOW), kv5(vt, batch, WINDOW),
            kv5(ks, dec_batch, dec_seq), kv5(vs, dec_batch, dec_seq),
            gvs.reshape(DEPTH, dec_batch, dec_seq, GMLP_GROUPS, GMLP_GROUP_DIM))
```

```python
import functools

import jax
import jax.numpy as jnp
from jax import lax
from jax.experimental import pallas as pl
from jax.experimental.pallas import tpu as pltpu
from jax.experimental.pallas import tpu_sc as plsc

F32 = jnp.float32
BF16 = jnp.bfloat16
I32 = jnp.int32

D_MODEL = 1024
CHUNK = 64
N_HEADS = 8
N_KV_HEADS = 2
HEAD_DIM = 64
Q_GROUP = N_HEADS // N_KV_HEADS
KV_WIDTH = N_KV_HEADS * HEAD_DIM
WINDOW = 128
KEYS = WINDOW + CHUNK
GMLP_GROUPS = 4
GMLP_GROUP_DIM = 128
GMLP_WIDTH = GMLP_GROUPS * GMLP_GROUP_DIM
GMLP_CHUNK = 128
ATTN_WIDTH = N_HEADS * HEAD_DIM
N_EXPERTS = 32
TOP_K = 4
D_FF = 1024
SWIGLU_LIMIT = 7.0
SWIGLU_ALPHA = 1.702
DEPTH = 1
DN_ALPHA = (2 * DEPTH) ** 0.25
LN_EPS = 1e-5
NEG_INF = -1e30

LANES = 128
K_OFF = ATTN_WIDTH
V_OFF = K_OFF + KV_WIDTH
U_OFF = V_OFF + KV_WIDTH
GV_OFF = U_OFF + GMLP_WIDTH
ZPAD_WIDTH = GV_OFF + GMLP_WIDTH

MIX_TILE = 512
ROW_TILE = 512
FINAL_TILE = 512
SC_WINDOW = 128
SC_SLABS = 4
SLAB_WIDTH = D_MODEL // SC_SLABS
PACK_WIDTH = D_MODEL // 2
PACK_SLABS = PACK_WIDTH // SLAB_WIDTH
COMBINE_WIDTH = LANES
COMBINE_SLABS = PACK_WIDTH // COMBINE_WIDTH
VMEM_LIMIT = 58 * 1024 * 1024

META_E, META_R = 0, 4


def _gelu_tanh(x):
    return 0.5 * x * (1.0 + jnp.tanh(0.7978845608028654 * (x + 0.044715 * x * x * x)))


def _layer_norm(x, g, b):
    mu = jnp.mean(x, axis=-1, keepdims=True)
    xc = x - mu
    var = jnp.mean(xc * xc, axis=-1, keepdims=True)
    return xc * lax.rsqrt(var + LN_EPS) * g + b


def _dot(a, b):
    return jnp.dot(a, b, preferred_element_type=F32)


def _mixer_kernel(*refs, is_prompt, tt):
    n_chunks = tt // CHUNK
    it = iter(refs)
    sinks_ref = next(it)
    x_ref = next(it)
    if not is_prompt:
        ck_ref, cv_ref, base0_ref = next(it), next(it), next(it)
    (win_ref, bin_ref, lng_ref, lnb_ref, ws_ref, bs_ref, woa_ref, wog_ref, bo_ref, l1g_ref, l1b_ref,
     wrh_ref, wrl_ref, br_ref, tri_ref) = (next(it) for _ in range(15))
    h_ref, hw_ref, meta_ref, gate_ref, cnt_ref = (next(it) for _ in range(5))
    if is_prompt:
        kt_ref, vt_ref = next(it), next(it)
    else:
        ko_ref, vo_ref, gvo_ref = next(it), next(it), next(it)
    z_ref, kext_ref, vext_ref, att_ref, gm_ref, base_ref = (next(it) for _ in range(6))

    if is_prompt:
        first = (pl.program_id(0) == 0) & (pl.program_id(1) == 0)
        tile_in_seq = pl.program_id(1)

        @pl.when(first)
        def _():
            base_ref[...] = jnp.zeros_like(base_ref)

        @pl.when(tile_in_seq == 0)
        def _():
            kext_ref[:, 0:WINDOW, :] = jnp.zeros((N_KV_HEADS, WINDOW, KV_WIDTH), BF16)
            vext_ref[:, 0:WINDOW, :] = jnp.zeros((N_KV_HEADS, WINDOW, KV_WIDTH), BF16)
    else:
        @pl.when(pl.program_id(0) == 0)
        def _():
            base_ref[...] = base0_ref[...]

    x = x_ref[...]
    z_ref[...] = _dot(x.astype(BF16), win_ref[...]) + bin_ref[...]

    def put_kv(ext_ref, row0, rows_f32):
        n = rows_f32.shape[0]
        swapped = pltpu.roll(rows_f32, HEAD_DIM, axis=1)
        low = lax.broadcasted_iota(I32, (n, KV_WIDTH), 1) < HEAD_DIM
        ext_ref[0, row0:row0 + n, :] = jnp.where(low, rows_f32, swapped).astype(BF16)
        ext_ref[1, row0:row0 + n, :] = jnp.where(low, swapped, rows_f32).astype(BF16)

    k = z_ref[:, K_OFF:K_OFF + KV_WIDTH]
    v = z_ref[:, V_OFF:V_OFF + KV_WIDTH]
    if is_prompt:
        put_kv(kext_ref, WINDOW, k)
        put_kv(vext_ref, WINDOW, v)
        kt_ref[...] = k[tt - WINDOW:, :]
        vt_ref[...] = v[tt - WINDOW:, :]
        key_stride = CHUNK
    else:
        ko_ref[...] = k
        vo_ref[...] = v
        for c in range(n_chunks):
            put_kv(kext_ref, KEYS * c, ck_ref[WINDOW * c:WINDOW * (c + 1), :])
            put_kv(vext_ref, KEYS * c, cv_ref[WINDOW * c:WINDOW * (c + 1), :])
            put_kv(kext_ref, KEYS * c + WINDOW, k[CHUNK * c:CHUNK * (c + 1), :])
            put_kv(vext_ref, KEYS * c + WINDOW, v[CHUNK * c:CHUNK * (c + 1), :])
        key_stride = KEYS

    rows = Q_GROUP * CHUNK
    row_i = lax.broadcasted_iota(I32, (rows, 1), 0)
    key_i = lax.broadcasted_iota(I32, (rows, KEYS), 1)
    sink_cols = []
    for hk in range(N_KV_HEADS):
        s = [sinks_ref[hk * Q_GROUP + g] for g in range(Q_GROUP)]
        sink_cols.append(jnp.where(row_i < CHUNK, s[0],
                                   jnp.where(row_i < 2 * CHUNK, s[1],
                                             jnp.where(row_i < 3 * CHUNK, s[2], s[3]))))
    low_half = lax.broadcasted_iota(I32, (CHUNK, LANES), 1) < HEAD_DIM
    for c in range(n_chunks):
        outs = []
        for hk in range(N_KV_HEADS):
            kc = kext_ref[hk, key_stride * c:key_stride * c + KEYS, :]
            vc = vext_ref[hk, key_stride * c:key_stride * c + KEYS, :]
            heads = []
            for pair in range(hk * Q_GROUP // 2, (hk + 1) * Q_GROUP // 2):
                both = z_ref[CHUNK * c:CHUNK * (c + 1), pair * LANES:(pair + 1) * LANES]
                heads += [jnp.where(low_half, both, 0.0), jnp.where(low_half, 0.0, both)]
            qs = jnp.concatenate(heads, axis=0).astype(BF16)
            sc = lax.dot_general(qs, kc, (((1,), (1,)), ((), ())), preferred_element_type=F32)
            sc = sc * (HEAD_DIM ** -0.5)
            if is_prompt and c < WINDOW // CHUNK:
                valid = (key_i >= WINDOW - CHUNK * c) | (tile_in_seq > 0)
                sc = jnp.where(valid, sc, NEG_INF)
            sink = sink_cols[hk]
            m = jnp.maximum(jnp.max(sc, axis=-1, keepdims=True), sink)
            p = jnp.exp(sc - m)
            den = jnp.sum(p, axis=-1, keepdims=True) + jnp.exp(sink - m)
            o = _dot(p.astype(BF16), vc) / den
            for j in range(Q_GROUP // 2):
                even, odd = o[CHUNK * 2 * j:CHUNK * (2 * j + 1), :], o[CHUNK * (2 * j + 1):CHUNK * (2 * j + 2), :]
                outs.append(jnp.where(low_half, even, odd))
        att_ref[CHUNK * c:CHUNK * (c + 1), :] = jnp.concatenate(outs, axis=1).astype(BF16)

    if is_prompt:
        kext_ref[:, 0:WINDOW, :] = kext_ref[:, tt:tt + WINDOW, :]
        vext_ref[:, 0:WINDOW, :] = vext_ref[:, tt:tt + WINDOW, :]

    gc = GMLP_CHUNK if is_prompt else CHUNK
    ri = lax.broadcasted_iota(I32, (gc, gc), 0)
    ci = lax.broadcasted_iota(I32, (gc, gc), 1)
    causal = (ci // CHUNK) <= (ri // CHUNK)
    for g in range(GMLP_GROUPS):
        lo, hi = g * GMLP_GROUP_DIM, (g + 1) * GMLP_GROUP_DIM
        u = _gelu_tanh(z_ref[:, U_OFF + lo:U_OFF + hi])
        gv = _layer_norm(_gelu_tanh(z_ref[:, GV_OFF + lo:GV_OFF + hi]), lng_ref[:, lo:hi], lnb_ref[:, lo:hi])
        if not is_prompt:
            gvo_ref[:, lo:hi] = gv
        gvb = gv.astype(BF16)
        wm = jnp.where(causal, ws_ref[g, 0:gc, 0:gc], 0.0).astype(BF16)
        bcol = bs_ref[0:gc, g:g + 1]
        for n in range(tt // gc):
            sp = _dot(wm, gvb[gc * n:gc * (n + 1), :]) + bcol
            gm_ref[gc * n:gc * (n + 1), lo:hi] = (u[gc * n:gc * (n + 1), :] * sp).astype(BF16)

    y = _dot(att_ref[...], woa_ref[...]) + _dot(gm_ref[...], wog_ref[...]) + bo_ref[...]
    h = _layer_norm(DN_ALPHA * x + y, l1g_ref[...], l1b_ref[...])
    h_ref[...] = h
    hw_ref[...] = pltpu.pack_elementwise([h[:, :PACK_WIDTH], h[:, PACK_WIDTH:]], packed_dtype=BF16)

    h_hi = h.astype(BF16)
    h_lo = (h - h_hi.astype(F32)).astype(BF16)
    logits = _dot(h_hi, wrh_ref[...]) + _dot(h_lo, wrh_ref[...]) + _dot(h_hi, wrl_ref[...]) + br_ref[...]
    lane = lax.broadcasted_iota(I32, (tt, LANES), 1)
    lane_f = lane.astype(F32)
    l = jnp.where(lane < N_EXPERTS, logits, -jnp.inf)
    tops, idxs, hots = [], [], []
    for _ in range(TOP_K):
        m = jnp.max(l, axis=-1, keepdims=True)
        idx = jnp.min(jnp.where(l == m, lane_f, float(LANES)), axis=-1, keepdims=True)
        hot = lane_f == idx
        l = jnp.where(hot, -jnp.inf, l)
        tops.append(m)
        idxs.append(idx)
        hots.append(hot)
    es = [jnp.exp(t - tops[0]) for t in tops]
    esum = es[0] + es[1] + es[2] + es[3]
    chosen = jnp.where(hots[0] | hots[1] | hots[2] | hots[3], 1.0, 0.0)
    before = _dot(tri_ref[...], chosen.astype(BF16)) + base_ref[...]
    meta = jnp.zeros((tt, LANES), F32)
    for kk in range(TOP_K):
        rank = jnp.sum(jnp.where(hots[kk], before, 0.0), axis=-1, keepdims=True)
        meta = jnp.where(lane == META_E + kk, idxs[kk], meta)
        meta = jnp.where(lane == META_R + kk, rank, meta)
        gate_ref[:, kk * LANES:(kk + 1) * LANES] = jnp.broadcast_to(es[kk] / esum, (tt, LANES))
    meta_ref[...] = jnp.transpose(meta)[0:2 * TOP_K, :]
    base_ref[...] = base_ref[...] + jnp.sum(chosen, axis=0, keepdims=True)
    cnt_ref[...] = base_ref[...]


def _const_spec(shape):
    nd = len(shape)
    return pl.BlockSpec(shape, lambda *_: (0,) * nd, pipeline_mode=pl.Buffered(1))


def _mixer_weight_specs(tt):
    return [
        _const_spec((D_MODEL, ZPAD_WIDTH)), _const_spec((1, ZPAD_WIDTH)),
        _const_spec((1, GMLP_WIDTH)), _const_spec((1, GMLP_WIDTH)),
        _const_spec((GMLP_GROUPS, GMLP_CHUNK, GMLP_CHUNK)), _const_spec((GMLP_CHUNK, GMLP_GROUPS)),
        _const_spec((ATTN_WIDTH, D_MODEL)), _const_spec((GMLP_WIDTH, D_MODEL)), _const_spec((1, D_MODEL)),
        _const_spec((1, D_MODEL)), _const_spec((1, D_MODEL)),
        _const_spec((D_MODEL, LANES)), _const_spec((D_MODEL, LANES)), _const_spec((1, LANES)),
        _const_spec((tt, tt)),
    ]


def _mixer_scratch(tt, kext_rows):
    return [
        pltpu.VMEM((tt, ZPAD_WIDTH), F32),
        pltpu.VMEM((N_KV_HEADS, kext_rows, KV_WIDTH), BF16), pltpu.VMEM((N_KV_HEADS, kext_rows, KV_WIDTH), BF16),
        pltpu.VMEM((tt, ATTN_WIDTH), BF16), pltpu.VMEM((tt, GMLP_WIDTH), BF16),
        pltpu.VMEM((1, LANES), F32),
    ]


def _mix_prompt(sinks, x, weights, first_batch, batch):
    seq = x.shape[1]
    tt = MIX_TILE
    n_tiles = seq // tt
    tok = batch * seq
    smem = pl.BlockSpec(memory_space=pltpu.SMEM)
    return pl.pallas_call(
        functools.partial(_mixer_kernel, is_prompt=True, tt=tt),
        grid=(batch, n_tiles),
        in_specs=[smem, pl.BlockSpec((None, tt, D_MODEL), lambda b, i: (b + first_batch, i, 0))]
        + _mixer_weight_specs(tt),
        out_specs=[
            pl.BlockSpec((tt, D_MODEL), lambda b, i: (b * n_tiles + i, 0)),
            pl.BlockSpec((tt, PACK_WIDTH), lambda b, i: (b * n_tiles + i, 0)),
            pl.BlockSpec((2 * TOP_K, tt), lambda b, i: (0, b * n_tiles + i)),
            pl.BlockSpec((tt, TOP_K * LANES), lambda b, i: (b * n_tiles + i, 0)),
            pl.BlockSpec((1, LANES), lambda b, i: (0, 0)),
            pl.BlockSpec((WINDOW, KV_WIDTH), lambda b, i: (b, 0)),
            pl.BlockSpec((WINDOW, KV_WIDTH), lambda b, i: (b, 0)),
        ],
        out_shape=[
            jax.ShapeDtypeStruct((tok, D_MODEL), F32), jax.ShapeDtypeStruct((tok, PACK_WIDTH), jnp.uint32),
            jax.ShapeDtypeStruct((2 * TOP_K, tok), F32), jax.ShapeDtypeStruct((tok, TOP_K * LANES), F32),
            jax.ShapeDtypeStruct((1, LANES), F32),
            jax.ShapeDtypeStruct((batch * WINDOW, KV_WIDTH), F32),
            jax.ShapeDtypeStruct((batch * WINDOW, KV_WIDTH), F32),
        ],
        scratch_shapes=_mixer_scratch(tt, WINDOW + tt),
        compiler_params=pltpu.CompilerParams(
            dimension_semantics=("arbitrary", "arbitrary"), vmem_limit_bytes=VMEM_LIMIT),
        name="mix_prompt",
    )(sinks, x, *weights)


def _mix_sample(sinks, x2, ck, cv, base0, weights):
    tok = x2.shape[0]
    tt = MIX_TILE
    n_chunks = tt // CHUNK
    cache_rows = n_chunks * WINDOW
    smem = pl.BlockSpec(memory_space=pltpu.SMEM)
    row = lambda w: pl.BlockSpec((tt, w), lambda i: (i, 0))
    return pl.pallas_call(
        functools.partial(_mixer_kernel, is_prompt=False, tt=tt),
        grid=(tok // tt,),
        in_specs=[smem, row(D_MODEL),
                  pl.BlockSpec((cache_rows, KV_WIDTH), lambda i: (i, 0)),
                  pl.BlockSpec((cache_rows, KV_WIDTH), lambda i: (i, 0)),
                  _const_spec((1, LANES))] + _mixer_weight_specs(tt),
        out_specs=[row(D_MODEL), row(PACK_WIDTH), pl.BlockSpec((2 * TOP_K, tt), lambda i: (0, i)),
                   row(TOP_K * LANES),
                   pl.BlockSpec((1, LANES), lambda i: (0, 0)),
                   row(KV_WIDTH), row(KV_WIDTH), row(GMLP_WIDTH)],
        out_shape=[
            jax.ShapeDtypeStruct((tok, D_MODEL), F32), jax.ShapeDtypeStruct((tok, PACK_WIDTH), jnp.uint32),
            jax.ShapeDtypeStruct((2 * TOP_K, tok), F32), jax.ShapeDtypeStruct((tok, TOP_K * LANES), F32),
            jax.ShapeDtypeStruct((1, LANES), F32),
            jax.ShapeDtypeStruct((tok, KV_WIDTH), F32), jax.ShapeDtypeStruct((tok, KV_WIDTH), F32),
            jax.ShapeDtypeStruct((tok, GMLP_WIDTH), F32),
        ],
        scratch_shapes=_mixer_scratch(tt, n_chunks * KEYS),
        compiler_params=pltpu.CompilerParams(
            dimension_semantics=("arbitrary",), vmem_limit_bytes=VMEM_LIMIT),
        name="mix_sample",
    )(sinks, x2, ck, cv, base0, *weights)


def _sc_pipeline(body, n_tok, in_specs, out_specs):
    return pltpu.emit_pipeline(
        body, grid=(n_tok // SC_WINDOW,), in_specs=in_specs, out_specs=out_specs,
        core_axis_name=("core", "subcore"), dimension_semantics=(pltpu.PARALLEL,))


def _index_specs():
    return [pl.BlockSpec((1, SC_WINDOW), lambda i: (0, i))] * TOP_K


def _sc_dispatch(token_sets, n_rows):
    win, wid = SC_WINDOW, SLAB_WIDTH
    per_set = 2 + TOP_K
    n_in = per_set * len(token_sets)
    mesh = plsc.VectorSubcoreMesh(core_axis_name="core", subcore_axis_name="subcore")
    out_type = ([jax.ShapeDtypeStruct((n_rows, wid), jnp.uint32)] * PACK_SLABS
                + [jax.ShapeDtypeStruct((n_rows, LANES), F32)])

    @functools.partial(pl.kernel, out_type=out_type, mesh=mesh, scratch_types=[], name="sc_dispatch")
    def run(*refs):
        xs_hbm, gs_hbm = refs[n_in:n_in + PACK_SLABS], refs[n_in + PACK_SLABS]

        for s in range(len(token_sets)):
            h_hbm, g_hbm = refs[per_set * s], refs[per_set * s + 1]
            i_hbm = refs[per_set * s + 2:per_set * (s + 1)]
            n_tok = h_hbm.shape[0]
            for q in range(PACK_SLABS):
                def rows_body(x_vmem, *i_vmem, q=q):
                    for kk in range(TOP_K):
                        pltpu.sync_copy(x_vmem, xs_hbm[q].at[i_vmem[kk].at[0]])

                _sc_pipeline(rows_body, n_tok, [pl.BlockSpec((win, wid), lambda i, q=q: (i, q))] + _index_specs(),
                             [])(h_hbm, *i_hbm)
            for kk in range(TOP_K):
                def gate_body(g_vmem, i_vmem):
                    pltpu.sync_copy(g_vmem, gs_hbm.at[i_vmem.at[0]])

                _sc_pipeline(gate_body, n_tok,
                             [pl.BlockSpec((win, LANES), lambda i, kk=kk: (i, kk)), _index_specs()[0]],
                             [])(g_hbm, i_hbm[kk])

    outs = run(*[a for hw, gates, dests in token_sets for a in (hw, gates, *dests)])
    return outs[:PACK_SLABS], outs[PACK_SLABS]


def _expert_kernel(te_ref, na_ref, par_ref, *refs):
    x_refs, (gs_ref, wgu_ref, bgu_ref, wd_ref, bd_ref) = refs[:PACK_SLABS], refs[PACK_SLABS:PACK_SLABS + 5]
    y_refs = refs[PACK_SLABS + 5:PACK_SLABS + 5 + COMBINE_SLABS]
    wgu_bf, wd_bf = refs[PACK_SLABS + 5 + COMBINE_SLABS:]
    s = pl.program_id(0)
    t = jnp.maximum(s - 1, 0)
    do_tile = (s >= 1) & (t < na_ref[0])

    def starts_expert(i):
        i = jnp.minimum(i, te_ref.shape[0] - 1)
        return (i == 0) | (te_ref[i] != te_ref[jnp.maximum(i - 1, 0)])

    @pl.when(do_tile & starts_expert(t))
    def _():
        wd_bf[...] = wd_ref[0].astype(BF16)

    @pl.when(do_tile)
    def _():
        words = [r[...] for r in x_refs]
        x = jnp.concatenate(
            [pltpu.unpack_elementwise(w, index=index, packed_dtype=BF16, unpacked_dtype=F32)
             for index in range(2) for w in words], axis=1).astype(BF16)
        hmid = _dot(x, wgu_bf[par_ref[t]]) + bgu_ref[0]
        gate = jnp.minimum(hmid[:, :D_FF], SWIGLU_LIMIT)
        up = jnp.clip(hmid[:, D_FF:], -SWIGLU_LIMIT, SWIGLU_LIMIT)
        act = (up + 1.0) * gate * jax.nn.sigmoid(SWIGLU_ALPHA * gate)
        y = _dot(act.astype(BF16), wd_bf[...]) + bd_ref[0]
        y = y * jnp.concatenate([gs_ref[...]] * (D_MODEL // LANES), axis=1)
        packed = pltpu.pack_elementwise([y[:, :PACK_WIDTH], y[:, PACK_WIDTH:]], packed_dtype=BF16)
        for q in range(COMBINE_SLABS):
            y_refs[q][...] = packed[:, q * COMBINE_WIDTH:(q + 1) * COMBINE_WIDTH]

    @pl.when((s < na_ref[0]) & starts_expert(s))
    def _():
        wgu_bf[par_ref[s]] = wgu_ref[0].astype(BF16)


def _experts(tile_e, n_active, xs, gs, wgu, bgu, wd, bd):
    tm = ROW_TILE
    n_tiles = gs.shape[0] // tm
    parity = (jnp.cumsum(jnp.concatenate([jnp.zeros((1,), I32), (tile_e[1:] != tile_e[:-1]).astype(I32)])) % 2
              ).astype(I32)
    tile_of = lambda s, na: jnp.clip(s - 1, 0, na[0] - 1)
    row_map = lambda s, te, na, par: (tile_of(s, na), 0)
    w_map = lambda s, te, na, par: (te[tile_of(s, na)], 0, 0)
    wd_map = lambda s, te, na, par: (0, te[tile_of(s, na)], 0, 0)
    wgu_map = lambda s, te, na, par: (0, te[jnp.minimum(s, na[0] - 1)], 0, 0)
    return pl.pallas_call(
        _expert_kernel,
        grid_spec=pltpu.PrefetchScalarGridSpec(
            num_scalar_prefetch=3,
            grid=(n_tiles + 1,),
            in_specs=[pl.BlockSpec((tm, SLAB_WIDTH), row_map)] * PACK_SLABS + [
                pl.BlockSpec((tm, LANES), row_map),
                pl.BlockSpec((None, 1, D_MODEL, 2 * D_FF), wgu_map),
                pl.BlockSpec((1, 1, 2 * D_FF), w_map),
                pl.BlockSpec((None, 1, D_FF, D_MODEL), wd_map),
                pl.BlockSpec((1, 1, D_MODEL), w_map),
            ],
            out_specs=[pl.BlockSpec((tm, COMBINE_WIDTH), row_map)] * COMBINE_SLABS,
            scratch_shapes=[pltpu.VMEM((2, D_MODEL, 2 * D_FF), BF16), pltpu.VMEM((D_FF, D_MODEL), BF16)],
        ),
        out_shape=[jax.ShapeDtypeStruct((n_tiles * tm, COMBINE_WIDTH), jnp.uint32)] * COMBINE_SLABS,
        compiler_params=pltpu.CompilerParams(
            dimension_semantics=("arbitrary",), vmem_limit_bytes=VMEM_LIMIT),
        name="experts",
    )(tile_e, n_active, parity, *xs, gs, wgu, bgu, wd, bd)


def _sc_combine(ys, dest_sets):
    n_sets = len(dest_sets)
    toks = [dests[0].shape[1] for dests in dest_sets]
    mesh = plsc.VectorSubcoreMesh(core_axis_name="core", subcore_axis_name="subcore")
    out_type = [jax.ShapeDtypeStruct((t, 2 * COMBINE_WIDTH), F32) for t in toks for _ in range(COMBINE_SLABS)]
    lanes = plsc.get_sparse_core_info().num_lanes
    high = jnp.uint32(0xFFFF0000)

    @functools.partial(pl.kernel, out_type=out_type, mesh=mesh, name="sc_combine",
                       scratch_types=[pltpu.VMEM((SC_WINDOW, COMBINE_WIDTH), jnp.uint32)] * 2,
                       compiler_params=pltpu.CompilerParams(needs_layout_passes=False))
    def run(*refs):
        ys_hbm = refs[:COMBINE_SLABS]
        out0 = COMBINE_SLABS + TOP_K * n_sets
        buf_a, buf_b = refs[-2:]

        def add_pair(ys_q, i_a, i_b, o_vmem, first):
            pltpu.sync_copy(ys_q.at[i_a.at[0]], buf_a)
            pltpu.sync_copy(ys_q.at[i_b.at[0]], buf_b)

            @pl.loop(0, SC_WINDOW)
            def _(r):
                for c in range(COMBINE_WIDTH // lanes):
                    cols = pl.ds(c * lanes, lanes)
                    wa, wb = buf_a[r, cols], buf_b[r, cols]
                    lo = plsc.bitcast(wa << 16, F32) + plsc.bitcast(wb << 16, F32)
                    hi = plsc.bitcast(wa & high, F32) + plsc.bitcast(wb & high, F32)
                    lo_cols, hi_cols = cols, pl.ds(COMBINE_WIDTH + c * lanes, lanes)
                    if first:
                        o_vmem[r, lo_cols] = lo
                        o_vmem[r, hi_cols] = hi
                    else:
                        o_vmem[r, lo_cols] = o_vmem[r, lo_cols] + lo
                        o_vmem[r, hi_cols] = o_vmem[r, hi_cols] + hi

        for s in range(n_sets):
            i_hbm = refs[COMBINE_SLABS + TOP_K * s:COMBINE_SLABS + TOP_K * (s + 1)]
            f_hbm = refs[out0 + COMBINE_SLABS * s:out0 + COMBINE_SLABS * (s + 1)]
            for q in range(COMBINE_SLABS):
                def body(i0, i1, i2, i3, o_vmem, q=q):
                    add_pair(ys_hbm[q], i0, i1, o_vmem, True)
                    add_pair(ys_hbm[q], i2, i3, o_vmem, False)

                _sc_pipeline(body, toks[s], _index_specs(),
                             [pl.BlockSpec((SC_WINDOW, 2 * COMBINE_WIDTH), lambda i: (i, 0))])(*i_hbm, f_hbm[q])

    outs = run(*ys, *[d for dests in dest_sets for d in dests])
    return [outs[COMBINE_SLABS * s:COMBINE_SLABS * (s + 1)] for s in range(n_sets)]


def _final_kernel(*refs):
    h_ref, f_refs = refs[0], refs[1:1 + COMBINE_SLABS]
    g_ref, b_ref, out_ref = refs[1 + COMBINE_SLABS], refs[2 + COMBINE_SLABS], refs[-1]
    f = jnp.concatenate([r[:, :COMBINE_WIDTH] for r in f_refs] + [r[:, COMBINE_WIDTH:] for r in f_refs], axis=1)
    out_ref[...] = _layer_norm(DN_ALPHA * h_ref[...] + f, g_ref[...], b_ref[...])


def _final_norm(h, picked, ln_g, ln_b, out_rows, first_row, earlier=None):
    tt = FINAL_TILE
    tok = h.shape[0]
    first_tile = first_row // tt
    row = lambda w: pl.BlockSpec((tt, w), lambda i: (i, 0))
    in_specs = ([row(D_MODEL)] + [row(2 * COMBINE_WIDTH)] * COMBINE_SLABS
                + [_const_spec((1, D_MODEL)), _const_spec((1, D_MODEL))])
    args = [h, *picked, ln_g, ln_b]
    aliases = {}
    if earlier is not None:
        in_specs.append(pl.BlockSpec(memory_space=pl.ANY))
        aliases = {len(args): 0}
        args.append(earlier)
    return pl.pallas_call(
        _final_kernel,
        grid=(tok // tt,),
        in_specs=in_specs,
        out_specs=pl.BlockSpec((tt, D_MODEL), lambda i: (i + first_tile, 0)),
        out_shape=jax.ShapeDtypeStruct((out_rows, D_MODEL), F32),
        input_output_aliases=aliases,
        compiler_params=pltpu.CompilerParams(dimension_semantics=("arbitrary",)),
        name="final_norm",
    )(*args)


def _mixer_weights(w_in, b_in, ln_g, ln_b, w_s, b_s, w_o, b_o, ln1_g, ln1_b, w_router, b_router):
    win = w_in.astype(BF16)
    bin_ = b_in[None, :]
    woa = w_o[:ATTN_WIDTH].astype(BF16)
    wog = w_o[ATTN_WIDTH:].astype(BF16)
    wr = jnp.pad(w_router, ((0, 0), (0, LANES - N_EXPERTS)))
    wrh = wr.astype(BF16)
    wrl = (wr - wrh.astype(F32)).astype(BF16)
    br = jnp.pad(b_router, (0, LANES - N_EXPERTS))[None, :]
    tri = (lax.broadcasted_iota(I32, (MIX_TILE, MIX_TILE), 1)
           < lax.broadcasted_iota(I32, (MIX_TILE, MIX_TILE), 0)).astype(BF16)
    return (win, bin_, ln_g.reshape(1, GMLP_WIDTH), ln_b.reshape(1, GMLP_WIDTH), w_s, b_s.T,
            woa, wog, b_o[None, :], ln1_g[None, :], ln1_b[None, :], wrh, wrl, br, tri)


def _dest_rows(meta, pstart):
    e = meta[META_E:META_E + TOP_K].astype(I32)
    r = meta[META_R:META_R + TOP_K].astype(I32)
    hit = e[None] == jnp.arange(N_EXPERTS, dtype=I32)[:, None, None]
    return jnp.sum(jnp.where(hit, pstart[:, None, None], 0), axis=0) + r


def _moe(token_sets, cnt, experts):
    n_assign = sum(hw.shape[0] for hw, _, _ in token_sets) * TOP_K
    n_tiles = (n_assign + N_EXPERTS * (ROW_TILE - 1)) // ROW_TILE
    counts = cnt[0, :N_EXPERTS].astype(I32)
    padded = (counts + ROW_TILE - 1) // ROW_TILE * ROW_TILE
    pend = jnp.cumsum(padded)
    pstart = pend - padded
    n_active = (pend[-1:] // ROW_TILE).astype(I32)
    tile_start = jnp.arange(n_tiles, dtype=I32) * ROW_TILE
    tile_e = jnp.minimum(jnp.sum(pend[None, :] <= tile_start[:, None], axis=1), N_EXPERTS - 1).astype(I32)

    lists = lambda d: [d[kk][None, :] for kk in range(TOP_K)]
    dest_sets = [lists(_dest_rows(meta, pstart)) for _, _, meta in token_sets]
    xs, gs = _sc_dispatch([(hw, gates, dests) for (hw, gates, _), dests in zip(token_sets, dest_sets)],
                          n_tiles * ROW_TILE)
    ys = _experts(tile_e, n_active, xs, gs, *experts)
    return _sc_combine(ys, dest_sets)


def kernel(x_prompt, x_sample, cache_k, cache_v, w_in, b_in, attn_sinks, gmlp_ln_g, gmlp_ln_b, w_spatial, b_spatial, w_o, b_o, ln1_g, ln1_b, w_router, b_router, w_gate_up, b_gate_up, w_down, b_down, ln2_g, ln2_b):
    assert w_in.shape[0] == DEPTH
    batch, seq, _ = x_prompt.shape
    dec_batch, dec_seq, _ = x_sample.shape
    tok_p, tok_s = batch * seq, dec_batch * dec_seq
    assert dec_seq == CHUNK and seq % MIX_TILE == 0 and batch >= 2
    assert all(t % MIX_TILE == 0 and t % FINAL_TILE == 0 and t % SC_WINDOW == 0 for t in (tok_p, tok_s))

    weights = _mixer_weights(w_in[0], b_in[0], gmlp_ln_g[0], gmlp_ln_b[0], w_spatial[0], b_spatial[0],
                             w_o[0], b_o[0], ln1_g[0], ln1_b[0], w_router[0], b_router[0])
    sinks = attn_sinks[0]
    experts = (w_gate_up, b_gate_up[0][:, None, :], w_down, b_down[0][:, None, :])

    batch_a = batch // 2
    h_a, hw_a, meta_a, gates_a, cnt_a, kt_a, vt_a = _mix_prompt(sinks, x_prompt, weights, 0, batch_a)
    picked_a, = _moe([(hw_a, gates_a, meta_a)], cnt_a, experts)
    h_b, hw_b, meta_b, gates_b, cnt_b, kt_b, vt_b = _mix_prompt(sinks, x_prompt, weights, batch_a, batch - batch_a)
    h_s, hw_s, meta_s, gates_s, cnt_bs, ks, vs, gvs = _mix_sample(
        sinks, x_sample.reshape(tok_s, D_MODEL),
        cache_k[0].reshape(dec_batch * WINDOW, KV_WIDTH), cache_v[0].reshape(dec_batch * WINDOW, KV_WIDTH),
        cnt_b, weights)
    picked_b, picked_s = _moe([(hw_b, gates_b, meta_b), (hw_s, gates_s, meta_s)], cnt_bs, experts)

    g2, b2 = ln2_g[0][None, :], ln2_b[0][None, :]
    y_p = _final_norm(h_a, picked_a, g2, b2, tok_p, 0)
    y_p = _final_norm(h_b, picked_b, g2, b2, tok_p, batch_a * seq, earlier=y_p).reshape(batch, seq, D_MODEL)
    y_s = _final_norm(h_s, picked_s, g2, b2, tok_s, 0).reshape(dec_batch, dec_seq, D_MODEL)
    kt, vt = jnp.concatenate([kt_a, kt_b]), jnp.concatenate([vt_a, vt_b])

    kv5 = lambda a, nb, rows: a.reshape(DEPTH, nb, rows, N_KV_HEADS, HEAD_DIM)
    return (y_p, y_s, kv5(kt, batch, WINDOW), kv5(vt, batch, WINDOW),
            kv5(ks, dec_batch, dec_seq), kv5(vs, dec_batch, dec_seq),
            gvs.reshape(DEPTH, dec_batch, dec_seq, GMLP_GROUPS, GMLP_GROUP_DIM))
```

```python
import functools

import jax
import jax.numpy as jnp
from jax import lax
from jax.experimental import pallas as pl
from jax.experimental.pallas import tpu as pltpu
from jax.experimental.pallas import tpu_sc as plsc

F32 = jnp.float32
BF16 = jnp.bfloat16
I32 = jnp.int32

D_MODEL = 1024
CHUNK = 64
N_HEADS = 8
N_KV_HEADS = 2
HEAD_DIM = 64
Q_GROUP = N_HEADS // N_KV_HEADS
KV_WIDTH = N_KV_HEADS * HEAD_DIM
WINDOW = 128
KEYS = WINDOW + CHUNK
GMLP_GROUPS = 4
GMLP_GROUP_DIM = 128
GMLP_WIDTH = GMLP_GROUPS * GMLP_GROUP_DIM
GMLP_CHUNK = 128
ATTN_WIDTH = N_HEADS * HEAD_DIM
N_EXPERTS = 32
TOP_K = 4
D_FF = 1024
SWIGLU_LIMIT = 7.0
SWIGLU_ALPHA = 1.702
DEPTH = 1
DN_ALPHA = (2 * DEPTH) ** 0.25
LN_EPS = 1e-5
NEG_INF = -1e30

LANES = 128
K_OFF = ATTN_WIDTH
V_OFF = K_OFF + KV_WIDTH
U_OFF = V_OFF + KV_WIDTH
GV_OFF = U_OFF + GMLP_WIDTH
ZPAD_WIDTH = GV_OFF + GMLP_WIDTH

MIX_TILE = 512
ROW_TILE = 512
FINAL_TILE = 512
SC_WINDOW = 128
SC_SLABS = 4
SLAB_WIDTH = D_MODEL // SC_SLABS
PACK_WIDTH = D_MODEL // 2
PACK_SLABS = PACK_WIDTH // SLAB_WIDTH
COMBINE_WIDTH = LANES
COMBINE_SLABS = PACK_WIDTH // COMBINE_WIDTH
VMEM_LIMIT = 58 * 1024 * 1024

META_E, META_R = 0, 4


def _gelu_tanh(x):
    return 0.5 * x * (1.0 + jnp.tanh(0.7978845608028654 * (x + 0.044715 * x * x * x)))


def _layer_norm(x, g, b):
    mu = jnp.mean(x, axis=-1, keepdims=True)
    xc = x - mu
    var = jnp.mean(xc * xc, axis=-1, keepdims=True)
    return xc * lax.rsqrt(var + LN_EPS) * g + b


def _dot(a, b):
    return jnp.dot(a, b, preferred_element_type=F32)


def _mixer_kernel(*refs, is_prompt, tt):
    n_chunks = tt // CHUNK
    it = iter(refs)
    sinks_ref = next(it)
    x_ref = next(it)
    if not is_prompt:
        ck_ref, cv_ref, base0_ref = next(it), next(it), next(it)
    (win_ref, bin_ref, lng_ref, lnb_ref, ws_ref, bs_ref, woa_ref, wog_ref, bo_ref, l1g_ref, l1b_ref,
     wrh_ref, wrl_ref, br_ref, tri_ref) = (next(it) for _ in range(15))
    h_ref, hw_ref, meta_ref, gate_ref, cnt_ref = (next(it) for _ in range(5))
    if is_prompt:
        kt_ref, vt_ref = next(it), next(it)
    else:
        ko_ref, vo_ref, gvo_ref = next(it), next(it), next(it)
    z_ref, kext_ref, vext_ref, att_ref, gm_ref, base_ref = (next(it) for _ in range(6))

    if is_prompt:
        first = (pl.program_id(0) == 0) & (pl.program_id(1) == 0)
        tile_in_seq = pl.program_id(1)

        @pl.when(first)
        def _():
            base_ref[...] = jnp.zeros_like(base_ref)

        @pl.when(tile_in_seq == 0)
        def _():
            kext_ref[:, 0:WINDOW, :] = jnp.zeros((N_KV_HEADS, WINDOW, KV_WIDTH), BF16)
            vext_ref[:, 0:WINDOW, :] = jnp.zeros((N_KV_HEADS, WINDOW, KV_WIDTH), BF16)
    else:
        @pl.when(pl.program_id(0) == 0)
        def _():
            base_ref[...] = base0_ref[...]

    x = x_ref[...]
    z_ref[...] = _dot(x.astype(BF16), win_ref[...]) + bin_ref[...]

    def put_kv(ext_ref, row0, rows_f32):
        n = rows_f32.shape[0]
        swapped = pltpu.roll(rows_f32, HEAD_DIM, axis=1)
        low = lax.broadcasted_iota(I32, (n, KV_WIDTH), 1) < HEAD_DIM
        ext_ref[0, row0:row0 + n, :] = jnp.where(low, rows_f32, swapped).astype(BF16)
        ext_ref[1, row0:row0 + n, :] = jnp.where(low, swapped, rows_f32).astype(BF16)

    k = z_ref[:, K_OFF:K_OFF + KV_WIDTH]
    v = z_ref[:, V_OFF:V_OFF + KV_WIDTH]
    if is_prompt:
        put_kv(kext_ref, WINDOW, k)
        put_kv(vext_ref, WINDOW, v)
        kt_ref[...] = k[tt - WINDOW:, :]
        vt_ref[...] = v[tt - WINDOW:, :]
        key_stride = CHUNK
    else:
        ko_ref[...] = k
        vo_ref[...] = v
        for c in range(n_chunks):
            put_kv(kext_ref, KEYS * c, ck_ref[WINDOW * c:WINDOW * (c + 1), :])
            put_kv(vext_ref, KEYS * c, cv_ref[WINDOW * c:WINDOW * (c + 1), :])
            put_kv(kext_ref, KEYS * c + WINDOW, k[CHUNK * c:CHUNK * (c + 1), :])
            put_kv(vext_ref, KEYS * c + WINDOW, v[CHUNK * c:CHUNK * (c + 1), :])
        key_stride = KEYS

    rows = Q_GROUP * CHUNK
    row_i = lax.broadcasted_iota(I32, (rows, 1), 0)
    key_i = lax.broadcasted_iota(I32, (rows, KEYS), 1)
    sink_cols = []
    for hk in range(N_KV_HEADS):
        s = [sinks_ref[hk * Q_GROUP + g] for g in range(Q_GROUP)]
        sink_cols.append(jnp.where(row_i < CHUNK, s[0],
                                   jnp.where(row_i < 2 * CHUNK, s[1],
                                             jnp.where(row_i < 3 * CHUNK, s[2], s[3]))))
    low_half = lax.broadcasted_iota(I32, (CHUNK, LANES), 1) < HEAD_DIM
    for c in range(n_chunks):
        outs = []
        for hk in range(N_KV_HEADS):
            kc = kext_ref[hk, key_stride * c:key_stride * c + KEYS, :]
            vc = vext_ref[hk, key_stride * c:key_stride * c + KEYS, :]
            heads = []
            for pair in range(hk * Q_GROUP // 2, (hk + 1) * Q_GROUP // 2):
                both = z_ref[CHUNK * c:CHUNK * (c + 1), pair * LANES:(pair + 1) * LANES]
                heads += [jnp.where(low_half, both, 0.0), jnp.where(low_half, 0.0, both)]
            qs = jnp.concatenate(heads, axis=0).astype(BF16)
            sc = lax.dot_general(qs, kc, (((1,), (1,)), ((), ())), preferred_element_type=F32)
            sc = sc * (HEAD_DIM ** -0.5)
            if is_prompt and c < WINDOW // CHUNK:
                valid = (key_i >= WINDOW - CHUNK * c) | (tile_in_seq > 0)
                sc = jnp.where(valid, sc, NEG_INF)
            sink = sink_cols[hk]
            m = jnp.maximum(jnp.max(sc, axis=-1, keepdims=True), sink)
            p = jnp.exp(sc - m)
            den = jnp.sum(p, axis=-1, keepdims=True) + jnp.exp(sink - m)
            o = _dot(p.astype(BF16), vc) / den
            for j in range(Q_GROUP // 2):
                even, odd = o[CHUNK * 2 * j:CHUNK * (2 * j + 1), :], o[CHUNK * (2 * j + 1):CHUNK * (2 * j + 2), :]
                outs.append(jnp.where(low_half, even, odd))
        att_ref[CHUNK * c:CHUNK * (c + 1), :] = jnp.concatenate(outs, axis=1).astype(BF16)

    if is_prompt:
        kext_ref[:, 0:WINDOW, :] = kext_ref[:, tt:tt + WINDOW, :]
        vext_ref[:, 0:WINDOW, :] = vext_ref[:, tt:tt + WINDOW, :]

    gc = GMLP_CHUNK if is_prompt else CHUNK
    ri = lax.broadcasted_iota(I32, (gc, gc), 0)
    ci = lax.broadcasted_iota(I32, (gc, gc), 1)
    causal = (ci // CHUNK) <= (ri // CHUNK)
    for g in range(GMLP_GROUPS):
        lo, hi = g * GMLP_GROUP_DIM, (g + 1) * GMLP_GROUP_DIM
        u = _gelu_tanh(z_ref[:, U_OFF + lo:U_OFF + hi])
        gv = _layer_norm(_gelu_tanh(z_ref[:, GV_OFF + lo:GV_OFF + hi]), lng_ref[:, lo:hi], lnb_ref[:, lo:hi])
        if not is_prompt:
            gvo_ref[:, lo:hi] = gv
        gvb = gv.astype(BF16)
        wm = jnp.where(causal, ws_ref[g, 0:gc, 0:gc], 0.0).astype(BF16)
        bcol = bs_ref[0:gc, g:g + 1]
        for n in range(tt // gc):
            sp = _dot(wm, gvb[gc * n:gc * (n + 1), :]) + bcol
            gm_ref[gc * n:gc * (n + 1), lo:hi] = (u[gc * n:gc * (n + 1), :] * sp).astype(BF16)

    y = _dot(att_ref[...], woa_ref[...]) + _dot(gm_ref[...], wog_ref[...]) + bo_ref[...]
    h = _layer_norm(DN_ALPHA * x + y, l1g_ref[...], l1b_ref[...])
    h_ref[...] = h
    hw_ref[...] = pltpu.pack_elementwise([h[:, :PACK_WIDTH], h[:, PACK_WIDTH:]], packed_dtype=BF16)

    h_hi = h.astype(BF16)
    h_lo = (h - h_hi.astype(F32)).astype(BF16)
    logits = _dot(h_hi, wrh_ref[...]) + _dot(h_lo, wrh_ref[...]) + _dot(h_hi, wrl_ref[...]) + br_ref[...]
    lane = lax.broadcasted_iota(I32, (tt, LANES), 1)
    lane_f = lane.astype(F32)
    l = jnp.where(lane < N_EXPERTS, logits, -jnp.inf)
    tops, idxs, hots = [], [], []
    for _ in range(TOP_K):
        m = jnp.max(l, axis=-1, keepdims=True)
        idx = jnp.min(jnp.where(l == m, lane_f, float(LANES)), axis=-1, keepdims=True)
        hot = lane_f == idx
        l = jnp.where(hot, -jnp.inf, l)
        tops.append(m)
        idxs.append(idx)
        hots.append(hot)
    es = [jnp.exp(t - tops[0]) for t in tops]
    esum = es[0] + es[1] + es[2] + es[3]
    chosen = jnp.where(hots[0] | hots[1] | hots[2] | hots[3], 1.0, 0.0)
    before = _dot(tri_ref[...], chosen.astype(BF16)) + base_ref[...]
    meta = jnp.zeros((tt, LANES), F32)
    for kk in range(TOP_K):
        rank = jnp.sum(jnp.where(hots[kk], before, 0.0), axis=-1, keepdims=True)
        meta = jnp.where(lane == META_E + kk, idxs[kk], meta)
        meta = jnp.where(lane == META_R + kk, rank, meta)
        gate_ref[:, kk * LANES:(kk + 1) * LANES] = jnp.broadcast_to(es[kk] / esum, (tt, LANES))
    meta_ref[...] = jnp.transpose(meta)[0:2 * TOP_K, :]
    base_ref[...] = base_ref[...] + jnp.sum(chosen, axis=0, keepdims=True)
    cnt_ref[...] = base_ref[...]


def _const_spec(shape):
    nd = len(shape)
    return pl.BlockSpec(shape, lambda *_: (0,) * nd, pipeline_mode=pl.Buffered(1))


def _mixer_weight_specs(tt):
    return [
        _const_spec((D_MODEL, ZPAD_WIDTH)), _const_spec((1, ZPAD_WIDTH)),
        _const_spec((1, GMLP_WIDTH)), _const_spec((1, GMLP_WIDTH)),
        _const_spec((GMLP_GROUPS, GMLP_CHUNK, GMLP_CHUNK)), _const_spec((GMLP_CHUNK, GMLP_GROUPS)),
        _const_spec((ATTN_WIDTH, D_MODEL)), _const_spec((GMLP_WIDTH, D_MODEL)), _const_spec((1, D_MODEL)),
        _const_spec((1, D_MODEL)), _const_spec((1, D_MODEL)),
        _const_spec((D_MODEL, LANES)), _const_spec((D_MODEL, LANES)), _const_spec((1, LANES)),
        _const_spec((tt, tt)),
    ]


def _mixer_scratch(tt, kext_rows):
    return [
        pltpu.VMEM((tt, ZPAD_WIDTH), F32),
        pltpu.VMEM((N_KV_HEADS, kext_rows, KV_WIDTH), BF16), pltpu.VMEM((N_KV_HEADS, kext_rows, KV_WIDTH), BF16),
        pltpu.VMEM((tt, ATTN_WIDTH), BF16), pltpu.VMEM((tt, GMLP_WIDTH), BF16),
        pltpu.VMEM((1, LANES), F32),
    ]


def _mix_prompt(sinks, x, weights, first_batch, batch):
    seq = x.shape[1]
    tt = MIX_TILE
    n_tiles = seq // tt
    tok = batch * seq
    smem = pl.BlockSpec(memory_space=pltpu.SMEM)
    return pl.pallas_call(
        functools.partial(_mixer_kernel, is_prompt=True, tt=tt),
        grid=(batch, n_tiles),
        in_specs=[smem, pl.BlockSpec((None, tt, D_MODEL), lambda b, i: (b + first_batch, i, 0))]
        + _mixer_weight_specs(tt),
        out_specs=[
            pl.BlockSpec((tt, D_MODEL), lambda b, i: (b * n_tiles + i, 0)),
            pl.BlockSpec((tt, PACK_WIDTH), lambda b, i: (b * n_tiles + i, 0)),
            pl.BlockSpec((2 * TOP_K, tt), lambda b, i: (0, b * n_tiles + i)),
            pl.BlockSpec((tt, TOP_K * LANES), lambda b, i: (b * n_tiles + i, 0)),
            pl.BlockSpec((1, LANES), lambda b, i: (0, 0)),
            pl.BlockSpec((WINDOW, KV_WIDTH), lambda b, i: (b, 0)),
            pl.BlockSpec((WINDOW, KV_WIDTH), lambda b, i: (b, 0)),
        ],
        out_shape=[
            jax.ShapeDtypeStruct((tok, D_MODEL), F32), jax.ShapeDtypeStruct((tok, PACK_WIDTH), jnp.uint32),
            jax.ShapeDtypeStruct((2 * TOP_K, tok), F32), jax.ShapeDtypeStruct((tok, TOP_K * LANES), F32),
            jax.ShapeDtypeStruct((1, LANES), F32),
            jax.ShapeDtypeStruct((batch * WINDOW, KV_WIDTH), F32),
            jax.ShapeDtypeStruct((batch * WINDOW, KV_WIDTH), F32),
        ],
        scratch_shapes=_mixer_scratch(tt, WINDOW + tt),
        compiler_params=pltpu.CompilerParams(
            dimension_semantics=("arbitrary", "arbitrary"), vmem_limit_bytes=VMEM_LIMIT),
        name="mix_prompt",
    )(sinks, x, *weights)


def _mix_sample(sinks, x2, ck, cv, base0, weights):
    tok = x2.shape[0]
    tt = MIX_TILE
    n_chunks = tt // CHUNK
    cache_rows = n_chunks * WINDOW
    smem = pl.BlockSpec(memory_space=pltpu.SMEM)
    row = lambda w: pl.BlockSpec((tt, w), lambda i: (i, 0))
    return pl.pallas_call(
        functools.partial(_mixer_kernel, is_prompt=False, tt=tt),
        grid=(tok // tt,),
        in_specs=[smem, row(D_MODEL),
                  pl.BlockSpec((cache_rows, KV_WIDTH), lambda i: (i, 0)),
                  pl.BlockSpec((cache_rows, KV_WIDTH), lambda i: (i, 0)),
                  _const_spec((1, LANES))] + _mixer_weight_specs(tt),
        out_specs=[row(D_MODEL), row(PACK_WIDTH), pl.BlockSpec((2 * TOP_K, tt), lambda i: (0, i)),
                   row(TOP_K * LANES),
                   pl.BlockSpec((1, LANES), lambda i: (0, 0)),
                   row(KV_WIDTH), row(KV_WIDTH), row(GMLP_WIDTH)],
        out_shape=[
            jax.ShapeDtypeStruct((tok, D_MODEL), F32), jax.ShapeDtypeStruct((tok, PACK_WIDTH), jnp.uint32),
            jax.ShapeDtypeStruct((2 * TOP_K, tok), F32), jax.ShapeDtypeStruct((tok, TOP_K * LANES), F32),
            jax.ShapeDtypeStruct((1, LANES), F32),
            jax.ShapeDtypeStruct((tok, KV_WIDTH), F32), jax.ShapeDtypeStruct((tok, KV_WIDTH), F32),
            jax.ShapeDtypeStruct((tok, GMLP_WIDTH), F32),
        ],
        scratch_shapes=_mixer_scratch(tt, n_chunks * KEYS),
        compiler_params=pltpu.CompilerParams(
            dimension_semantics=("arbitrary",), vmem_limit_bytes=VMEM_LIMIT),
        name="mix_sample",
    )(sinks, x2, ck, cv, base0, *weights)


def _sc_pipeline(body, n_tok, in_specs, out_specs):
    return pltpu.emit_pipeline(
        body, grid=(n_tok // SC_WINDOW,), in_specs=in_specs, out_specs=out_specs,
        core_axis_name=("core", "subcore"), dimension_semantics=(pltpu.PARALLEL,))


def _index_specs():
    return [pl.BlockSpec((1, SC_WINDOW), lambda i: (0, i))] * TOP_K


def _sc_dispatch(token_sets, n_rows):
    win, wid = SC_WINDOW, SLAB_WIDTH
    per_set = 2 + TOP_K
    n_in = per_set * len(token_sets)
    mesh = plsc.VectorSubcoreMesh(core_axis_name="core", subcore_axis_name="subcore")
    out_type = ([jax.ShapeDtypeStruct((n_rows, wid), jnp.uint32)] * PACK_SLABS
                + [jax.ShapeDtypeStruct((n_rows, LANES), F32)])

    @functools.partial(pl.kernel, out_type=out_type, mesh=mesh, scratch_types=[], name="sc_dispatch")
    def run(*refs):
        xs_hbm, gs_hbm = refs[n_in:n_in + PACK_SLABS], refs[n_in + PACK_SLABS]

        for s in range(len(token_sets)):
            h_hbm, g_hbm = refs[per_set * s], refs[per_set * s + 1]
            i_hbm = refs[per_set * s + 2:per_set * (s + 1)]
            n_tok = h_hbm.shape[0]
            for q in range(PACK_SLABS):
                def rows_body(x_vmem, *i_vmem, q=q):
                    for kk in range(TOP_K):
                        pltpu.sync_copy(x_vmem, xs_hbm[q].at[i_vmem[kk].at[0]])

                _sc_pipeline(rows_body, n_tok, [pl.BlockSpec((win, wid), lambda i, q=q: (i, q))] + _index_specs(),
                             [])(h_hbm, *i_hbm)
            for kk in range(TOP_K):
                def gate_body(g_vmem, i_vmem):
                    pltpu.sync_copy(g_vmem, gs_hbm.at[i_vmem.at[0]])

                _sc_pipeline(gate_body, n_tok,
                             [pl.BlockSpec((win, LANES), lambda i, kk=kk: (i, kk)), _index_specs()[0]],
                             [])(g_hbm, i_hbm[kk])

    outs = run(*[a for hw, gates, dests in token_sets for a in (hw, gates, *dests)])
    return outs[:PACK_SLABS], outs[PACK_SLABS]


def _expert_kernel(te_ref, na_ref, par_ref, *refs, out_slabs):
    x_refs, (gs_ref, wgu_ref, bgu_ref, wd_ref, bd_ref) = refs[:PACK_SLABS], refs[PACK_SLABS:PACK_SLABS + 5]
    y_refs = refs[PACK_SLABS + 5:PACK_SLABS + 5 + out_slabs]
    wgu_bf, wd_bf = refs[PACK_SLABS + 5 + out_slabs:]
    out_width = PACK_WIDTH // out_slabs
    s = pl.program_id(0)
    t = jnp.maximum(s - 1, 0)
    do_tile = (s >= 1) & (t < na_ref[0])

    def starts_expert(i):
        i = jnp.minimum(i, te_ref.shape[0] - 1)
        return (i == 0) | (te_ref[i] != te_ref[jnp.maximum(i - 1, 0)])

    @pl.when(do_tile & starts_expert(t))
    def _():
        wd_bf[...] = wd_ref[0].astype(BF16)

    @pl.when(do_tile)
    def _():
        words = [r[...] for r in x_refs]
        x = jnp.concatenate(
            [pltpu.unpack_elementwise(w, index=index, packed_dtype=BF16, unpacked_dtype=F32)
             for index in range(2) for w in words], axis=1).astype(BF16)
        hmid = _dot(x, wgu_bf[par_ref[t]]) + bgu_ref[0]
        gate = jnp.minimum(hmid[:, :D_FF], SWIGLU_LIMIT)
        up = jnp.clip(hmid[:, D_FF:], -SWIGLU_LIMIT, SWIGLU_LIMIT)
        act = (up + 1.0) * gate * jax.nn.sigmoid(SWIGLU_ALPHA * gate)
        y = _dot(act.astype(BF16), wd_bf[...]) + bd_ref[0]
        y = y * jnp.concatenate([gs_ref[...]] * (D_MODEL // LANES), axis=1)
        packed = pltpu.pack_elementwise([y[:, :PACK_WIDTH], y[:, PACK_WIDTH:]], packed_dtype=BF16)
        for q in range(out_slabs):
            y_refs[q][...] = packed[:, q * out_width:(q + 1) * out_width]

    @pl.when((s < na_ref[0]) & starts_expert(s))
    def _():
        wgu_bf[par_ref[s]] = wgu_ref[0].astype(BF16)


def _experts(tile_e, n_active, xs, gs, wgu, bgu, wd, bd, out_width):
    tm = ROW_TILE
    out_slabs = PACK_WIDTH // out_width
    n_tiles = gs.shape[0] // tm
    parity = (jnp.cumsum(jnp.concatenate([jnp.zeros((1,), I32), (tile_e[1:] != tile_e[:-1]).astype(I32)])) % 2
              ).astype(I32)
    tile_of = lambda s, na: jnp.clip(s - 1, 0, na[0] - 1)
    row_map = lambda s, te, na, par: (tile_of(s, na), 0)
    w_map = lambda s, te, na, par: (te[tile_of(s, na)], 0, 0)
    wd_map = lambda s, te, na, par: (0, te[tile_of(s, na)], 0, 0)
    wgu_map = lambda s, te, na, par: (0, te[jnp.minimum(s, na[0] - 1)], 0, 0)
    return pl.pallas_call(
        functools.partial(_expert_kernel, out_slabs=out_slabs),
        grid_spec=pltpu.PrefetchScalarGridSpec(
            num_scalar_prefetch=3,
            grid=(n_tiles + 1,),
            in_specs=[pl.BlockSpec((tm, SLAB_WIDTH), row_map)] * PACK_SLABS + [
                pl.BlockSpec((tm, LANES), row_map),
                pl.BlockSpec((None, 1, D_MODEL, 2 * D_FF), wgu_map),
                pl.BlockSpec((1, 1, 2 * D_FF), w_map),
                pl.BlockSpec((None, 1, D_FF, D_MODEL), wd_map),
                pl.BlockSpec((1, 1, D_MODEL), w_map),
            ],
            out_specs=[pl.BlockSpec((tm, out_width), row_map)] * out_slabs,
            scratch_shapes=[pltpu.VMEM((2, D_MODEL, 2 * D_FF), BF16), pltpu.VMEM((D_FF, D_MODEL), BF16)],
        ),
        out_shape=[jax.ShapeDtypeStruct((n_tiles * tm, out_width), jnp.uint32)] * out_slabs,
        compiler_params=pltpu.CompilerParams(
            dimension_semantics=("arbitrary",), vmem_limit_bytes=VMEM_LIMIT),
        name="experts",
    )(tile_e, n_active, parity, *xs, gs, wgu, bgu, wd, bd)


def _sc_combine(ys, dest_sets):
    n_sets = len(dest_sets)
    toks = [dests[0].shape[1] for dests in dest_sets]
    mesh = plsc.VectorSubcoreMesh(core_axis_name="core", subcore_axis_name="subcore")
    out_type = [jax.ShapeDtypeStruct((t, 2 * COMBINE_WIDTH), F32) for t in toks for _ in range(COMBINE_SLABS)]
    lanes = plsc.get_sparse_core_info().num_lanes
    high = jnp.uint32(0xFFFF0000)

    @functools.partial(pl.kernel, out_type=out_type, mesh=mesh, name="sc_combine",
                       scratch_types=[pltpu.VMEM((SC_WINDOW, COMBINE_WIDTH), jnp.uint32)] * 2,
                       compiler_params=pltpu.CompilerParams(needs_layout_passes=False))
    def run(*refs):
        ys_hbm = refs[:COMBINE_SLABS]
        out0 = COMBINE_SLABS + TOP_K * n_sets
        buf_a, buf_b = refs[-2:]

        def add_pair(ys_q, i_a, i_b, o_vmem, first):
            pltpu.sync_copy(ys_q.at[i_a.at[0]], buf_a)
            pltpu.sync_copy(ys_q.at[i_b.at[0]], buf_b)

            @pl.loop(0, SC_WINDOW)
            def _(r):
                for c in range(COMBINE_WIDTH // lanes):
                    cols = pl.ds(c * lanes, lanes)
                    wa, wb = buf_a[r, cols], buf_b[r, cols]
                    lo = plsc.bitcast(wa << 16, F32) + plsc.bitcast(wb << 16, F32)
                    hi = plsc.bitcast(wa & high, F32) + plsc.bitcast(wb & high, F32)
                    lo_cols, hi_cols = cols, pl.ds(COMBINE_WIDTH + c * lanes, lanes)
                    if first:
                        o_vmem[r, lo_cols] = lo
                        o_vmem[r, hi_cols] = hi
                    else:
                        o_vmem[r, lo_cols] = o_vmem[r, lo_cols] + lo
                        o_vmem[r, hi_cols] = o_vmem[r, hi_cols] + hi

        for s in range(n_sets):
            i_hbm = refs[COMBINE_SLABS + TOP_K * s:COMBINE_SLABS + TOP_K * (s + 1)]
            f_hbm = refs[out0 + COMBINE_SLABS * s:out0 + COMBINE_SLABS * (s + 1)]
            for q in range(COMBINE_SLABS):
                def body(i0, i1, i2, i3, o_vmem, q=q):
                    add_pair(ys_hbm[q], i0, i1, o_vmem, True)
                    add_pair(ys_hbm[q], i2, i3, o_vmem, False)

                _sc_pipeline(body, toks[s], _index_specs(),
                             [pl.BlockSpec((SC_WINDOW, 2 * COMBINE_WIDTH), lambda i: (i, 0))])(*i_hbm, f_hbm[q])

    outs = run(*ys, *[d for dests in dest_sets for d in dests])
    return [outs[COMBINE_SLABS * s:COMBINE_SLABS * (s + 1)] for s in range(n_sets)]


def _sc_gather(ys, dest_sets):
    n_out = TOP_K * PACK_SLABS
    n_sets = len(dest_sets)
    toks = [dests[0].shape[1] for dests in dest_sets]
    mesh = plsc.VectorSubcoreMesh(core_axis_name="core", subcore_axis_name="subcore")
    out_type = [jax.ShapeDtypeStruct((t, SLAB_WIDTH), jnp.uint32) for t in toks for _ in range(n_out)]

    @functools.partial(pl.kernel, out_type=out_type, mesh=mesh, scratch_types=[], name="sc_gather")
    def run(*refs):
        ys_hbm = refs[:PACK_SLABS]
        out0 = PACK_SLABS + TOP_K * n_sets

        for s in range(n_sets):
            i_hbm = refs[PACK_SLABS + TOP_K * s:PACK_SLABS + TOP_K * (s + 1)]
            f_hbm = refs[out0 + n_out * s:out0 + n_out * (s + 1)]
            for kk in range(TOP_K):
                for q in range(PACK_SLABS):
                    def body(i_vmem, o_vmem, q=q):
                        pltpu.sync_copy(ys_hbm[q].at[i_vmem.at[0]], o_vmem)

                    _sc_pipeline(body, toks[s], [_index_specs()[0]],
                                 [pl.BlockSpec((SC_WINDOW, SLAB_WIDTH), lambda i: (i, 0))]
                                 )(i_hbm[kk], f_hbm[kk * PACK_SLABS + q])

    outs = run(*ys, *[d for dests in dest_sets for d in dests])
    return [outs[n_out * s:n_out * (s + 1)] for s in range(n_sets)]


def _final_kernel(*refs, summed):
    n_in = COMBINE_SLABS if summed else TOP_K * PACK_SLABS
    h_ref, f_refs, g_ref, b_ref, out_ref = refs[0], refs[1:1 + n_in], refs[1 + n_in], refs[2 + n_in], refs[-1]
    if summed:
        f = jnp.concatenate([r[:, :COMBINE_WIDTH] for r in f_refs] + [r[:, COMBINE_WIDTH:] for r in f_refs], axis=1)
    else:
        halves = []
        for index in range(2):
            for q in range(PACK_SLABS):
                parts = [pltpu.unpack_elementwise(f_refs[kk * PACK_SLABS + q][...], index=index,
                                                  packed_dtype=BF16, unpacked_dtype=F32) for kk in range(TOP_K)]
                halves.append((parts[0] + parts[1]) + (parts[2] + parts[3]))
        f = jnp.concatenate(halves, axis=1)
    out_ref[...] = _layer_norm(DN_ALPHA * h_ref[...] + f, g_ref[...], b_ref[...])


def _final_norm(h, combined, summed, ln_g, ln_b, out_rows, first_row, earlier=None):
    tt = FINAL_TILE
    tok = h.shape[0]
    first_tile = first_row // tt
    row = lambda w: pl.BlockSpec((tt, w), lambda i: (i, 0))
    in_specs = ([row(D_MODEL)] + [row(c.shape[1]) for c in combined]
                + [_const_spec((1, D_MODEL)), _const_spec((1, D_MODEL))])
    args = [h, *combined, ln_g, ln_b]
    aliases = {}
    if earlier is not None:
        in_specs.append(pl.BlockSpec(memory_space=pl.ANY))
        aliases = {len(args): 0}
        args.append(earlier)
    return pl.pallas_call(
        functools.partial(_final_kernel, summed=summed),
        grid=(tok // tt,),
        in_specs=in_specs,
        out_specs=pl.BlockSpec((tt, D_MODEL), lambda i: (i + first_tile, 0)),
        out_shape=jax.ShapeDtypeStruct((out_rows, D_MODEL), F32),
        input_output_aliases=aliases,
        compiler_params=pltpu.CompilerParams(dimension_semantics=("arbitrary",)),
        name="final_norm",
    )(*args)


def _mixer_weights(w_in, b_in, ln_g, ln_b, w_s, b_s, w_o, b_o, ln1_g, ln1_b, w_router, b_router):
    win = w_in.astype(BF16)
    bin_ = b_in[None, :]
    woa = w_o[:ATTN_WIDTH].astype(BF16)
    wog = w_o[ATTN_WIDTH:].astype(BF16)
    wr = jnp.pad(w_router, ((0, 0), (0, LANES - N_EXPERTS)))
    wrh = wr.astype(BF16)
    wrl = (wr - wrh.astype(F32)).astype(BF16)
    br = jnp.pad(b_router, (0, LANES - N_EXPERTS))[None, :]
    tri = (lax.broadcasted_iota(I32, (MIX_TILE, MIX_TILE), 1)
           < lax.broadcasted_iota(I32, (MIX_TILE, MIX_TILE), 0)).astype(BF16)
    return (win, bin_, ln_g.reshape(1, GMLP_WIDTH), ln_b.reshape(1, GMLP_WIDTH), w_s, b_s.T,
            woa, wog, b_o[None, :], ln1_g[None, :], ln1_b[None, :], wrh, wrl, br, tri)


def _dest_rows(meta, pstart):
    e = meta[META_E:META_E + TOP_K].astype(I32)
    r = meta[META_R:META_R + TOP_K].astype(I32)
    hit = e[None] == jnp.arange(N_EXPERTS, dtype=I32)[:, None, None]
    return jnp.sum(jnp.where(hit, pstart[:, None, None], 0), axis=0) + r


def _moe(token_sets, cnt, experts, sum_on_sc):
    n_assign = sum(hw.shape[0] for hw, _, _ in token_sets) * TOP_K
    n_tiles = (n_assign + N_EXPERTS * (ROW_TILE - 1)) // ROW_TILE
    counts = cnt[0, :N_EXPERTS].astype(I32)
    padded = (counts + ROW_TILE - 1) // ROW_TILE * ROW_TILE
    pend = jnp.cumsum(padded)
    pstart = pend - padded
    n_active = (pend[-1:] // ROW_TILE).astype(I32)
    tile_start = jnp.arange(n_tiles, dtype=I32) * ROW_TILE
    tile_e = jnp.minimum(jnp.sum(pend[None, :] <= tile_start[:, None], axis=1), N_EXPERTS - 1).astype(I32)

    lists = lambda d: [d[kk][None, :] for kk in range(TOP_K)]
    dest_sets = [lists(_dest_rows(meta, pstart)) for _, _, meta in token_sets]
    xs, gs = _sc_dispatch([(hw, gates, dests) for (hw, gates, _), dests in zip(token_sets, dest_sets)],
                          n_tiles * ROW_TILE)
    if sum_on_sc:
        return _sc_combine(_experts(tile_e, n_active, xs, gs, *experts, COMBINE_WIDTH), dest_sets)
    return _sc_gather(_experts(tile_e, n_active, xs, gs, *experts, SLAB_WIDTH), dest_sets)


def kernel(x_prompt, x_sample, cache_k, cache_v, w_in, b_in, attn_sinks, gmlp_ln_g, gmlp_ln_b, w_spatial, b_spatial, w_o, b_o, ln1_g, ln1_b, w_router, b_router, w_gate_up, b_gate_up, w_down, b_down, ln2_g, ln2_b):
    assert w_in.shape[0] == DEPTH
    batch, seq, _ = x_prompt.shape
    dec_batch, dec_seq, _ = x_sample.shape
    tok_p, tok_s = batch * seq, dec_batch * dec_seq
    assert dec_seq == CHUNK and seq % MIX_TILE == 0 and batch >= 2
    assert all(t % MIX_TILE == 0 and t % FINAL_TILE == 0 and t % SC_WINDOW == 0 for t in (tok_p, tok_s))

    weights = _mixer_weights(w_in[0], b_in[0], gmlp_ln_g[0], gmlp_ln_b[0], w_spatial[0], b_spatial[0],
                             w_o[0], b_o[0], ln1_g[0], ln1_b[0], w_router[0], b_router[0])
    sinks = attn_sinks[0]
    experts = (w_gate_up, b_gate_up[0][:, None, :], w_down, b_down[0][:, None, :])

    batch_a = batch // 2
    h_a, hw_a, meta_a, gates_a, cnt_a, kt_a, vt_a = _mix_prompt(sinks, x_prompt, weights, 0, batch_a)
    summed_a, = _moe([(hw_a, gates_a, meta_a)], cnt_a, experts, sum_on_sc=True)
    h_b, hw_b, meta_b, gates_b, cnt_b, kt_b, vt_b = _mix_prompt(sinks, x_prompt, weights, batch_a, batch - batch_a)
    h_s, hw_s, meta_s, gates_s, cnt_bs, ks, vs, gvs = _mix_sample(
        sinks, x_sample.reshape(tok_s, D_MODEL),
        cache_k[0].reshape(dec_batch * WINDOW, KV_WIDTH), cache_v[0].reshape(dec_batch * WINDOW, KV_WIDTH),
        cnt_b, weights)
    picked_b, picked_s = _moe([(hw_b, gates_b, meta_b), (hw_s, gates_s, meta_s)], cnt_bs, experts, sum_on_sc=False)

    g2, b2 = ln2_g[0][None, :], ln2_b[0][None, :]
    y_p = _final_norm(h_a, summed_a, True, g2, b2, tok_p, 0)
    y_p = _final_norm(h_b, picked_b, False, g2, b2, tok_p, batch_a * seq, earlier=y_p).reshape(batch, seq, D_MODEL)
    y_s = _final_norm(h_s, picked_s, False, g2, b2, tok_s, 0).reshape(dec_batch, dec_seq, D_MODEL)
    kt, vt = jnp.concatenate([kt_a, kt_b]), jnp.concatenate([vt_a, vt_b])

    kv5 = lambda a, nb, rows: a.reshape(DEPTH, nb, rows, N_KV_HEADS, HEAD_DIM)
    return (y_p, y_s, kv5(kt, batch, WINDOW), kv5(vt, batch, WINDOW),
            kv5(ks, dec_batch, dec_seq), kv5(vs, dec_batch, dec_seq),
            gvs.reshape(DEPTH, dec_batch, dec_seq, GMLP_GROUPS, GMLP_GROUP_DIM))
```

```python
import functools

import jax
import jax.numpy as jnp
from jax import lax
from jax.experimental import pallas as pl
from jax.experimental.pallas import tpu as pltpu
from jax.experimental.pallas import tpu_sc as plsc

F32 = jnp.float32
BF16 = jnp.bfloat16
I32 = jnp.int32

D_MODEL = 1024
CHUNK = 64
N_HEADS = 8
N_KV_HEADS = 2
HEAD_DIM = 64
Q_GROUP = N_HEADS // N_KV_HEADS
KV_WIDTH = N_KV_HEADS * HEAD_DIM
WINDOW = 128
KEYS = WINDOW + CHUNK
GMLP_GROUPS = 4
GMLP_GROUP_DIM = 128
GMLP_WIDTH = GMLP_GROUPS * GMLP_GROUP_DIM
GMLP_CHUNK = 128
ATTN_WIDTH = N_HEADS * HEAD_DIM
N_EXPERTS = 32
TOP_K = 4
D_FF = 1024
SWIGLU_LIMIT = 7.0
SWIGLU_ALPHA = 1.702
DEPTH = 1
DN_ALPHA = (2 * DEPTH) ** 0.25
LN_EPS = 1e-5
NEG_INF = -1e30

LANES = 128
K_OFF = ATTN_WIDTH
V_OFF = K_OFF + KV_WIDTH
U_OFF = V_OFF + KV_WIDTH
GV_OFF = U_OFF + GMLP_WIDTH
Z_WIDTH = GV_OFF + GMLP_WIDTH

MIX_TILE = 512
ROW_TILE = 512
FINAL_TILE = 512
SC_WINDOW = 128
SC_SLABS = 4
SLAB_WIDTH = D_MODEL // SC_SLABS
PACK_WIDTH = D_MODEL // 2
PACK_SLABS = PACK_WIDTH // SLAB_WIDTH
COMBINE_WIDTH = LANES
COMBINE_SLABS = PACK_WIDTH // COMBINE_WIDTH
VMEM_LIMIT = 58 * 1024 * 1024

META_E, META_R = 0, 4


def _gelu_tanh(x):
    return 0.5 * x * (1.0 + jnp.tanh(0.7978845608028654 * (x + 0.044715 * x * x * x)))


def _layer_norm(x, g, b):
    mu = jnp.mean(x, axis=-1, keepdims=True)
    xc = x - mu
    var = jnp.mean(xc * xc, axis=-1, keepdims=True)
    return xc * lax.rsqrt(var + LN_EPS) * g + b


def _dot(a, b):
    return jnp.dot(a, b, preferred_element_type=F32)


def _mixer_kernel(*refs, is_prompt, tt):
    n_chunks = tt // CHUNK
    it = iter(refs)
    sinks_ref = next(it)
    x_ref = next(it)
    if not is_prompt:
        ck_ref, cv_ref, base0_ref = next(it), next(it), next(it)
    (win_ref, bin_ref, lng_ref, lnb_ref, ws_ref, bs_ref, woa_ref, wog_ref, bo_ref, l1g_ref, l1b_ref,
     wrh_ref, wrl_ref, br_ref, tri_ref) = (next(it) for _ in range(15))
    h_ref, hw_ref, meta_ref, gate_ref, cnt_ref = (next(it) for _ in range(5))
    if is_prompt:
        kt_ref, vt_ref = next(it), next(it)
    else:
        ko_ref, vo_ref, gvo_ref = next(it), next(it), next(it)
    z_ref, kext_ref, vext_ref, att_ref, gm_ref, base_ref = (next(it) for _ in range(6))

    if is_prompt:
        first = (pl.program_id(0) == 0) & (pl.program_id(1) == 0)
        tile_in_seq = pl.program_id(1)

        @pl.when(first)
        def _():
            base_ref[...] = jnp.zeros_like(base_ref)

        @pl.when(tile_in_seq == 0)
        def _():
            kext_ref[:, 0:WINDOW, :] = jnp.zeros((N_KV_HEADS, WINDOW, KV_WIDTH), BF16)
            vext_ref[:, 0:WINDOW, :] = jnp.zeros((N_KV_HEADS, WINDOW, KV_WIDTH), BF16)
    else:
        @pl.when(pl.program_id(0) == 0)
        def _():
            base_ref[...] = base0_ref[...]

    x = x_ref[...]
    z_ref[...] = _dot(x.astype(BF16), win_ref[...]) + bin_ref[...]

    def put_kv(ext_ref, row0, rows_f32):
        n = rows_f32.shape[0]
        swapped = pltpu.roll(rows_f32, HEAD_DIM, axis=1)
        low = lax.broadcasted_iota(I32, (n, KV_WIDTH), 1) < HEAD_DIM
        ext_ref[0, row0:row0 + n, :] = jnp.where(low, rows_f32, swapped).astype(BF16)
        ext_ref[1, row0:row0 + n, :] = jnp.where(low, swapped, rows_f32).astype(BF16)

    k = z_ref[:, K_OFF:K_OFF + KV_WIDTH]
    v = z_ref[:, V_OFF:V_OFF + KV_WIDTH]
    if is_prompt:
        put_kv(kext_ref, WINDOW, k)
        put_kv(vext_ref, WINDOW, v)
        kt_ref[...] = k[tt - WINDOW:, :]
        vt_ref[...] = v[tt - WINDOW:, :]
        key_stride = CHUNK
    else:
        ko_ref[...] = k
        vo_ref[...] = v
        for c in range(n_chunks):
            put_kv(kext_ref, KEYS * c, ck_ref[WINDOW * c:WINDOW * (c + 1), :])
            put_kv(vext_ref, KEYS * c, cv_ref[WINDOW * c:WINDOW * (c + 1), :])
            put_kv(kext_ref, KEYS * c + WINDOW, k[CHUNK * c:CHUNK * (c + 1), :])
            put_kv(vext_ref, KEYS * c + WINDOW, v[CHUNK * c:CHUNK * (c + 1), :])
        key_stride = KEYS

    rows = Q_GROUP * CHUNK
    row_i = lax.broadcasted_iota(I32, (rows, 1), 0)
    key_i = lax.broadcasted_iota(I32, (rows, KEYS), 1)
    sink_cols = []
    for hk in range(N_KV_HEADS):
        s = [sinks_ref[hk * Q_GROUP + g] for g in range(Q_GROUP)]
        sink_cols.append(jnp.where(row_i < CHUNK, s[0],
                                   jnp.where(row_i < 2 * CHUNK, s[1],
                                             jnp.where(row_i < 3 * CHUNK, s[2], s[3]))))
    low_half = lax.broadcasted_iota(I32, (CHUNK, LANES), 1) < HEAD_DIM
    for c in range(n_chunks):
        outs = []
        for hk in range(N_KV_HEADS):
            kc = kext_ref[hk, key_stride * c:key_stride * c + KEYS, :]
            vc = vext_ref[hk, key_stride * c:key_stride * c + KEYS, :]
            heads = []
            for pair in range(hk * Q_GROUP // 2, (hk + 1) * Q_GROUP // 2):
                both = z_ref[CHUNK * c:CHUNK * (c + 1), pair * LANES:(pair + 1) * LANES]
                heads += [jnp.where(low_half, both, 0.0), jnp.where(low_half, 0.0, both)]
            qs = jnp.concatenate(heads, axis=0).astype(BF16)
            sc = lax.dot_general(qs, kc, (((1,), (1,)), ((), ())), preferred_element_type=F32)
            sc = sc * (HEAD_DIM ** -0.5)
            if is_prompt and c < WINDOW // CHUNK:
                valid = (key_i >= WINDOW - CHUNK * c) | (tile_in_seq > 0)
                sc = jnp.where(valid, sc, NEG_INF)
            sink = sink_cols[hk]
            m = jnp.maximum(jnp.max(sc, axis=-1, keepdims=True), sink)
            p = jnp.exp(sc - m)
            den = jnp.sum(p, axis=-1, keepdims=True) + jnp.exp(sink - m)
            o = _dot(p.astype(BF16), vc) / den
            for j in range(Q_GROUP // 2):
                even, odd = o[CHUNK * 2 * j:CHUNK * (2 * j + 1), :], o[CHUNK * (2 * j + 1):CHUNK * (2 * j + 2), :]
                outs.append(jnp.where(low_half, even, odd))
        att_ref[CHUNK * c:CHUNK * (c + 1), :] = jnp.concatenate(outs, axis=1).astype(BF16)

    if is_prompt:
        kext_ref[:, 0:WINDOW, :] = kext_ref[:, tt:tt + WINDOW, :]
        vext_ref[:, 0:WINDOW, :] = vext_ref[:, tt:tt + WINDOW, :]

    gc = GMLP_CHUNK if is_prompt else CHUNK
    ri = lax.broadcasted_iota(I32, (gc, gc), 0)
    ci = lax.broadcasted_iota(I32, (gc, gc), 1)
    causal = (ci // CHUNK) <= (ri // CHUNK)
    for g in range(GMLP_GROUPS):
        lo, hi = g * GMLP_GROUP_DIM, (g + 1) * GMLP_GROUP_DIM
        u = _gelu_tanh(z_ref[:, U_OFF + lo:U_OFF + hi])
        gv = _layer_norm(_gelu_tanh(z_ref[:, GV_OFF + lo:GV_OFF + hi]), lng_ref[:, lo:hi], lnb_ref[:, lo:hi])
        if not is_prompt:
            gvo_ref[:, lo:hi] = gv
        gvb = gv.astype(BF16)
        wm = jnp.where(causal, ws_ref[g, 0:gc, 0:gc], 0.0).astype(BF16)
        bcol = bs_ref[0:gc, g:g + 1]
        for n in range(tt // gc):
            sp = _dot(wm, gvb[gc * n:gc * (n + 1), :]) + bcol
            gm_ref[gc * n:gc * (n + 1), lo:hi] = (u[gc * n:gc * (n + 1), :] * sp).astype(BF16)

    y = _dot(att_ref[...], woa_ref[...]) + _dot(gm_ref[...], wog_ref[...]) + bo_ref[...]
    h = _layer_norm(DN_ALPHA * x + y, l1g_ref[...], l1b_ref[...])
    h_ref[...] = h
    hw_ref[...] = pltpu.pack_elementwise([h[:, :PACK_WIDTH], h[:, PACK_WIDTH:]], packed_dtype=BF16)

    h_hi = h.astype(BF16)
    h_lo = (h - h_hi.astype(F32)).astype(BF16)
    logits = _dot(h_hi, wrh_ref[...]) + _dot(h_lo, wrh_ref[...]) + _dot(h_hi, wrl_ref[...]) + br_ref[...]
    lane = lax.broadcasted_iota(I32, (tt, LANES), 1)
    lane_f = lane.astype(F32)
    l = jnp.where(lane < N_EXPERTS, logits, -jnp.inf)
    tops, idxs, hots = [], [], []
    for _ in range(TOP_K):
        m = jnp.max(l, axis=-1, keepdims=True)
        idx = jnp.min(jnp.where(l == m, lane_f, float(LANES)), axis=-1, keepdims=True)
        hot = lane_f == idx
        l = jnp.where(hot, -jnp.inf, l)
        tops.append(m)
        idxs.append(idx)
        hots.append(hot)
    es = [jnp.exp(t - tops[0]) for t in tops]
    esum = es[0] + es[1] + es[2] + es[3]
    chosen = jnp.where(hots[0] | hots[1] | hots[2] | hots[3], 1.0, 0.0)
    before = _dot(tri_ref[...], chosen.astype(BF16)) + base_ref[...]
    meta = jnp.zeros((tt, LANES), F32)
    for kk in range(TOP_K):
        rank = jnp.sum(jnp.where(hots[kk], before, 0.0), axis=-1, keepdims=True)
        meta = jnp.where(lane == META_E + kk, idxs[kk], meta)
        meta = jnp.where(lane == META_R + kk, rank, meta)
        gate_ref[:, kk * LANES:(kk + 1) * LANES] = jnp.broadcast_to(es[kk] / esum, (tt, LANES))
    meta_ref[...] = jnp.transpose(meta)[0:2 * TOP_K, :]
    base_ref[...] = base_ref[...] + jnp.sum(chosen, axis=0, keepdims=True)
    cnt_ref[...] = base_ref[...]


def _const_spec(shape):
    nd = len(shape)
    return pl.BlockSpec(shape, lambda *_: (0,) * nd, pipeline_mode=pl.Buffered(1))


def _mixer_weight_specs(tt):
    return [
        _const_spec((D_MODEL, Z_WIDTH)), _const_spec((1, Z_WIDTH)),
        _const_spec((1, GMLP_WIDTH)), _const_spec((1, GMLP_WIDTH)),
        _const_spec((GMLP_GROUPS, GMLP_CHUNK, GMLP_CHUNK)), _const_spec((GMLP_CHUNK, GMLP_GROUPS)),
        _const_spec((ATTN_WIDTH, D_MODEL)), _const_spec((GMLP_WIDTH, D_MODEL)), _const_spec((1, D_MODEL)),
        _const_spec((1, D_MODEL)), _const_spec((1, D_MODEL)),
        _const_spec((D_MODEL, LANES)), _const_spec((D_MODEL, LANES)), _const_spec((1, LANES)),
        _const_spec((tt, tt)),
    ]


def _mixer_scratch(tt, kext_rows):
    return [
        pltpu.VMEM((tt, Z_WIDTH), F32),
        pltpu.VMEM((N_KV_HEADS, kext_rows, KV_WIDTH), BF16), pltpu.VMEM((N_KV_HEADS, kext_rows, KV_WIDTH), BF16),
        pltpu.VMEM((tt, ATTN_WIDTH), BF16), pltpu.VMEM((tt, GMLP_WIDTH), BF16),
        pltpu.VMEM((1, LANES), F32),
    ]


def _mix_prompt(sinks, x, weights, first_batch, batch):
    seq = x.shape[1]
    tt = MIX_TILE
    n_tiles = seq // tt
    tok = batch * seq
    smem = pl.BlockSpec(memory_space=pltpu.SMEM)
    return pl.pallas_call(
        functools.partial(_mixer_kernel, is_prompt=True, tt=tt),
        grid=(batch, n_tiles),
        in_specs=[smem, pl.BlockSpec((None, tt, D_MODEL), lambda b, i: (b + first_batch, i, 0))]
        + _mixer_weight_specs(tt),
        out_specs=[
            pl.BlockSpec((tt, D_MODEL), lambda b, i: (b * n_tiles + i, 0)),
            pl.BlockSpec((tt, PACK_WIDTH), lambda b, i: (b * n_tiles + i, 0)),
            pl.BlockSpec((2 * TOP_K, tt), lambda b, i: (0, b * n_tiles + i)),
            pl.BlockSpec((tt, TOP_K * LANES), lambda b, i: (b * n_tiles + i, 0)),
            pl.BlockSpec((1, LANES), lambda b, i: (0, 0)),
            pl.BlockSpec((WINDOW, KV_WIDTH), lambda b, i: (b, 0)),
            pl.BlockSpec((WINDOW, KV_WIDTH), lambda b, i: (b, 0)),
        ],
        out_shape=[
            jax.ShapeDtypeStruct((tok, D_MODEL), F32), jax.ShapeDtypeStruct((tok, PACK_WIDTH), jnp.uint32),
            jax.ShapeDtypeStruct((2 * TOP_K, tok), F32), jax.ShapeDtypeStruct((tok, TOP_K * LANES), F32),
            jax.ShapeDtypeStruct((1, LANES), F32),
            jax.ShapeDtypeStruct((batch * WINDOW, KV_WIDTH), F32),
            jax.ShapeDtypeStruct((batch * WINDOW, KV_WIDTH), F32),
        ],
        scratch_shapes=_mixer_scratch(tt, WINDOW + tt),
        compiler_params=pltpu.CompilerParams(
            dimension_semantics=("arbitrary", "arbitrary"), vmem_limit_bytes=VMEM_LIMIT),
        name="mix_prompt",
    )(sinks, x, *weights)


def _mix_sample(sinks, x2, ck, cv, base0, weights):
    tok = x2.shape[0]
    tt = MIX_TILE
    n_chunks = tt // CHUNK
    cache_rows = n_chunks * WINDOW
    smem = pl.BlockSpec(memory_space=pltpu.SMEM)
    row = lambda w: pl.BlockSpec((tt, w), lambda i: (i, 0))
    return pl.pallas_call(
        functools.partial(_mixer_kernel, is_prompt=False, tt=tt),
        grid=(tok // tt,),
        in_specs=[smem, row(D_MODEL),
                  pl.BlockSpec((cache_rows, KV_WIDTH), lambda i: (i, 0)),
                  pl.BlockSpec((cache_rows, KV_WIDTH), lambda i: (i, 0)),
                  _const_spec((1, LANES))] + _mixer_weight_specs(tt),
        out_specs=[row(D_MODEL), row(PACK_WIDTH), pl.BlockSpec((2 * TOP_K, tt), lambda i: (0, i)),
                   row(TOP_K * LANES),
                   pl.BlockSpec((1, LANES), lambda i: (0, 0)),
                   row(KV_WIDTH), row(KV_WIDTH), row(GMLP_WIDTH)],
        out_shape=[
            jax.ShapeDtypeStruct((tok, D_MODEL), F32), jax.ShapeDtypeStruct((tok, PACK_WIDTH), jnp.uint32),
            jax.ShapeDtypeStruct((2 * TOP_K, tok), F32), jax.ShapeDtypeStruct((tok, TOP_K * LANES), F32),
            jax.ShapeDtypeStruct((1, LANES), F32),
            jax.ShapeDtypeStruct((tok, KV_WIDTH), F32), jax.ShapeDtypeStruct((tok, KV_WIDTH), F32),
            jax.ShapeDtypeStruct((tok, GMLP_WIDTH), F32),
        ],
        scratch_shapes=_mixer_scratch(tt, n_chunks * KEYS),
        compiler_params=pltpu.CompilerParams(
            dimension_semantics=("arbitrary",), vmem_limit_bytes=VMEM_LIMIT),
        name="mix_sample",
    )(sinks, x2, ck, cv, base0, *weights)


def _sc_pipeline(body, n_tok, in_specs, out_specs):
    return pltpu.emit_pipeline(
        body, grid=(n_tok // SC_WINDOW,), in_specs=in_specs, out_specs=out_specs,
        core_axis_name=("core", "subcore"), dimension_semantics=(pltpu.PARALLEL,))


def _index_specs():
    return [pl.BlockSpec((1, SC_WINDOW), lambda i: (0, i))] * TOP_K


def _sc_dispatch(token_sets, n_rows):
    win, wid = SC_WINDOW, SLAB_WIDTH
    per_set = 2 + TOP_K
    n_in = per_set * len(token_sets)
    mesh = plsc.VectorSubcoreMesh(core_axis_name="core", subcore_axis_name="subcore")
    out_type = ([jax.ShapeDtypeStruct((n_rows, wid), jnp.uint32)] * PACK_SLABS
                + [jax.ShapeDtypeStruct((n_rows, LANES), F32)])

    @functools.partial(pl.kernel, out_type=out_type, mesh=mesh, scratch_types=[], name="sc_dispatch")
    def run(*refs):
        xs_hbm, gs_hbm = refs[n_in:n_in + PACK_SLABS], refs[n_in + PACK_SLABS]

        for s in range(len(token_sets)):
            h_hbm, g_hbm = refs[per_set * s], refs[per_set * s + 1]
            i_hbm = refs[per_set * s + 2:per_set * (s + 1)]
            n_tok = h_hbm.shape[0]
            for q in range(PACK_SLABS):
                def rows_body(x_vmem, *i_vmem, q=q):
                    for kk in range(TOP_K):
                        pltpu.sync_copy(x_vmem, xs_hbm[q].at[i_vmem[kk].at[0]])

                _sc_pipeline(rows_body, n_tok, [pl.BlockSpec((win, wid), lambda i, q=q: (i, q))] + _index_specs(),
                             [])(h_hbm, *i_hbm)
            for kk in range(TOP_K):
                def gate_body(g_vmem, i_vmem):
                    pltpu.sync_copy(g_vmem, gs_hbm.at[i_vmem.at[0]])

                _sc_pipeline(gate_body, n_tok,
                             [pl.BlockSpec((win, LANES), lambda i, kk=kk: (i, kk)), _index_specs()[0]],
                             [])(g_hbm, i_hbm[kk])

    outs = run(*[a for hw, gates, dests in token_sets for a in (hw, gates, *dests)])
    return outs[:PACK_SLABS], outs[PACK_SLABS]


def _expert_kernel(te_ref, na_ref, par_ref, half_ref, *refs, out_slabs):
    x_refs, (gs_ref, wgu_ref, bgu_ref, wd_ref, bd_ref) = refs[:PACK_SLABS], refs[PACK_SLABS:PACK_SLABS + 5]
    y_refs = refs[PACK_SLABS + 5:PACK_SLABS + 5 + out_slabs]
    wgu_bf, wd_bf = refs[PACK_SLABS + 5 + out_slabs:]
    out_width = PACK_WIDTH // out_slabs
    s = pl.program_id(0)
    t = jnp.maximum(s - 1, 0)
    do_tile = (s >= 1) & (t < na_ref[0])

    def starts_expert(i):
        i = jnp.minimum(i, te_ref.shape[0] - 1)
        return (i == 0) | (te_ref[i] != te_ref[jnp.maximum(i - 1, 0)])

    @pl.when(do_tile & starts_expert(t))
    def _():
        wd_bf[...] = wd_ref[0].astype(BF16)

    def compute(n_rows):
        words = [r[0:n_rows, :] for r in x_refs]
        x = jnp.concatenate(
            [pltpu.unpack_elementwise(w, index=index, packed_dtype=BF16, unpacked_dtype=F32)
             for index in range(2) for w in words], axis=1).astype(BF16)
        hmid = _dot(x, wgu_bf[par_ref[t]]) + bgu_ref[0]
        gate = jnp.minimum(hmid[:, :D_FF], SWIGLU_LIMIT)
        up = jnp.clip(hmid[:, D_FF:], -SWIGLU_LIMIT, SWIGLU_LIMIT)
        act = (up + 1.0) * gate * jax.nn.sigmoid(SWIGLU_ALPHA * gate)
        y = _dot(act.astype(BF16), wd_bf[...]) + bd_ref[0]
        y = y * jnp.concatenate([gs_ref[0:n_rows, :]] * (D_MODEL // LANES), axis=1)
        packed = pltpu.pack_elementwise([y[:, :PACK_WIDTH], y[:, PACK_WIDTH:]], packed_dtype=BF16)
        for q in range(out_slabs):
            y_refs[q][0:n_rows, :] = packed[:, q * out_width:(q + 1) * out_width]

    @pl.when(do_tile & (half_ref[t] == 0))
    def _():
        compute(ROW_TILE)

    @pl.when(do_tile & (half_ref[t] != 0))
    def _():
        compute(ROW_TILE // 2)

    @pl.when((s < na_ref[0]) & starts_expert(s))
    def _():
        wgu_bf[par_ref[s]] = wgu_ref[0].astype(BF16)


def _experts(tile_e, n_active, half, xs, gs, wgu, bgu, wd, bd, out_width):
    tm = ROW_TILE
    out_slabs = PACK_WIDTH // out_width
    n_tiles = gs.shape[0] // tm
    parity = (jnp.cumsum(jnp.concatenate([jnp.zeros((1,), I32), (tile_e[1:] != tile_e[:-1]).astype(I32)])) % 2
              ).astype(I32)
    tile_of = lambda s, na: jnp.clip(s - 1, 0, na[0] - 1)
    row_map = lambda s, te, na, *_: (tile_of(s, na), 0)
    w_map = lambda s, te, na, *_: (te[tile_of(s, na)], 0, 0)
    wd_map = lambda s, te, na, *_: (0, te[tile_of(s, na)], 0, 0)
    wgu_map = lambda s, te, na, *_: (0, te[jnp.minimum(s, na[0] - 1)], 0, 0)
    return pl.pallas_call(
        functools.partial(_expert_kernel, out_slabs=out_slabs),
        grid_spec=pltpu.PrefetchScalarGridSpec(
            num_scalar_prefetch=4,
            grid=(n_tiles + 1,),
            in_specs=[pl.BlockSpec((tm, SLAB_WIDTH), row_map)] * PACK_SLABS + [
                pl.BlockSpec((tm, LANES), row_map),
                pl.BlockSpec((None, 1, D_MODEL, 2 * D_FF), wgu_map),
                pl.BlockSpec((1, 1, 2 * D_FF), w_map),
                pl.BlockSpec((None, 1, D_FF, D_MODEL), wd_map),
                pl.BlockSpec((1, 1, D_MODEL), w_map),
            ],
            out_specs=[pl.BlockSpec((tm, out_width), row_map)] * out_slabs,
            scratch_shapes=[pltpu.VMEM((2, D_MODEL, 2 * D_FF), BF16), pltpu.VMEM((D_FF, D_MODEL), BF16)],
        ),
        out_shape=[jax.ShapeDtypeStruct((n_tiles * tm, out_width), jnp.uint32)] * out_slabs,
        compiler_params=pltpu.CompilerParams(
            dimension_semantics=("arbitrary",), vmem_limit_bytes=VMEM_LIMIT),
        name="experts",
    )(tile_e, n_active, parity, half, *xs, gs, wgu, bgu, wd, bd)


def _sc_combine(ys, dest_sets):
    n_sets = len(dest_sets)
    toks = [dests[0].shape[1] for dests in dest_sets]
    mesh = plsc.VectorSubcoreMesh(core_axis_name="core", subcore_axis_name="subcore")
    out_type = [jax.ShapeDtypeStruct((t, 2 * COMBINE_WIDTH), F32) for t in toks for _ in range(COMBINE_SLABS)]
    lanes = plsc.get_sparse_core_info().num_lanes
    high = jnp.uint32(0xFFFF0000)

    @functools.partial(pl.kernel, out_type=out_type, mesh=mesh, name="sc_combine",
                       scratch_types=[pltpu.VMEM((SC_WINDOW, COMBINE_WIDTH), jnp.uint32)] * 2,
                       compiler_params=pltpu.CompilerParams(needs_layout_passes=False))
    def run(*refs):
        ys_hbm = refs[:COMBINE_SLABS]
        out0 = COMBINE_SLABS + TOP_K * n_sets
        buf_a, buf_b = refs[-2:]

        def add_pair(ys_q, i_a, i_b, o_vmem, first):
            pltpu.sync_copy(ys_q.at[i_a.at[0]], buf_a)
            pltpu.sync_copy(ys_q.at[i_b.at[0]], buf_b)

            @pl.loop(0, SC_WINDOW)
            def _(r):
                for c in range(COMBINE_WIDTH // lanes):
                    cols = pl.ds(c * lanes, lanes)
                    wa, wb = buf_a[r, cols], buf_b[r, cols]
                    lo = plsc.bitcast(wa << 16, F32) + plsc.bitcast(wb << 16, F32)
                    hi = plsc.bitcast(wa & high, F32) + plsc.bitcast(wb & high, F32)
                    lo_cols, hi_cols = cols, pl.ds(COMBINE_WIDTH + c * lanes, lanes)
                    if first:
                        o_vmem[r, lo_cols] = lo
                        o_vmem[r, hi_cols] = hi
                    else:
                        o_vmem[r, lo_cols] = o_vmem[r, lo_cols] + lo
                        o_vmem[r, hi_cols] = o_vmem[r, hi_cols] + hi

        for s in range(n_sets):
            i_hbm = refs[COMBINE_SLABS + TOP_K * s:COMBINE_SLABS + TOP_K * (s + 1)]
            f_hbm = refs[out0 + COMBINE_SLABS * s:out0 + COMBINE_SLABS * (s + 1)]
            for q in range(COMBINE_SLABS):
                def body(i0, i1, i2, i3, o_vmem, q=q):
                    add_pair(ys_hbm[q], i0, i1, o_vmem, True)
                    add_pair(ys_hbm[q], i2, i3, o_vmem, False)

                _sc_pipeline(body, toks[s], _index_specs(),
                             [pl.BlockSpec((SC_WINDOW, 2 * COMBINE_WIDTH), lambda i: (i, 0))])(*i_hbm, f_hbm[q])

    outs = run(*ys, *[d for dests in dest_sets for d in dests])
    return [outs[COMBINE_SLABS * s:COMBINE_SLABS * (s + 1)] for s in range(n_sets)]


def _sc_gather(ys, dest_sets):
    n_out = TOP_K * PACK_SLABS
    n_sets = len(dest_sets)
    toks = [dests[0].shape[1] for dests in dest_sets]
    mesh = plsc.VectorSubcoreMesh(core_axis_name="core", subcore_axis_name="subcore")
    out_type = [jax.ShapeDtypeStruct((t, SLAB_WIDTH), jnp.uint32) for t in toks for _ in range(n_out)]

    @functools.partial(pl.kernel, out_type=out_type, mesh=mesh, scratch_types=[], name="sc_gather")
    def run(*refs):
        ys_hbm = refs[:PACK_SLABS]
        out0 = PACK_SLABS + TOP_K * n_sets

        for s in range(n_sets):
            i_hbm = refs[PACK_SLABS + TOP_K * s:PACK_SLABS + TOP_K * (s + 1)]
            f_hbm = refs[out0 + n_out * s:out0 + n_out * (s + 1)]
            for kk in range(TOP_K):
                for q in range(PACK_SLABS):
                    def body(i_vmem, o_vmem, q=q):
                        pltpu.sync_copy(ys_hbm[q].at[i_vmem.at[0]], o_vmem)

                    _sc_pipeline(body, toks[s], [_index_specs()[0]],
                                 [pl.BlockSpec((SC_WINDOW, SLAB_WIDTH), lambda i: (i, 0))]
                                 )(i_hbm[kk], f_hbm[kk * PACK_SLABS + q])

    outs = run(*ys, *[d for dests in dest_sets for d in dests])
    return [outs[n_out * s:n_out * (s + 1)] for s in range(n_sets)]


def _final_kernel(*refs, summed):
    n_in = COMBINE_SLABS if summed else TOP_K * PACK_SLABS
    h_ref, f_refs, g_ref, b_ref, out_ref = refs[0], refs[1:1 + n_in], refs[1 + n_in], refs[2 + n_in], refs[-1]
    if summed:
        f = jnp.concatenate([r[:, :COMBINE_WIDTH] for r in f_refs] + [r[:, COMBINE_WIDTH:] for r in f_refs], axis=1)
    else:
        halves = []
        for index in range(2):
            for q in range(PACK_SLABS):
                parts = [pltpu.unpack_elementwise(f_refs[kk * PACK_SLABS + q][...], index=index,
                                                  packed_dtype=BF16, unpacked_dtype=F32) for kk in range(TOP_K)]
                halves.append((parts[0] + parts[1]) + (parts[2] + parts[3]))
        f = jnp.concatenate(halves, axis=1)
    out_ref[...] = _layer_norm(DN_ALPHA * h_ref[...] + f, g_ref[...], b_ref[...])


def _final_norm(h, combined, summed, ln_g, ln_b, out_rows, first_row, earlier=None):
    tt = FINAL_TILE
    tok = h.shape[0]
    first_tile = first_row // tt
    row = lambda w: pl.BlockSpec((tt, w), lambda i: (i, 0))
    in_specs = ([row(D_MODEL)] + [row(c.shape[1]) for c in combined]
                + [_const_spec((1, D_MODEL)), _const_spec((1, D_MODEL))])
    args = [h, *combined, ln_g, ln_b]
    aliases = {}
    if earlier is not None:
        in_specs.append(pl.BlockSpec(memory_space=pl.ANY))
        aliases = {len(args): 0}
        args.append(earlier)
    return pl.pallas_call(
        functools.partial(_final_kernel, summed=summed),
        grid=(tok // tt,),
        in_specs=in_specs,
        out_specs=pl.BlockSpec((tt, D_MODEL), lambda i: (i + first_tile, 0)),
        out_shape=jax.ShapeDtypeStruct((out_rows, D_MODEL), F32),
        input_output_aliases=aliases,
        compiler_params=pltpu.CompilerParams(dimension_semantics=("arbitrary",)),
        name="final_norm",
    )(*args)


def _mixer_weights(w_in, b_in, ln_g, ln_b, w_s, b_s, w_o, b_o, ln1_g, ln1_b, w_router, b_router):
    win = w_in.astype(BF16)
    bin_ = b_in[None, :]
    woa = w_o[:ATTN_WIDTH].astype(BF16)
    wog = w_o[ATTN_WIDTH:].astype(BF16)
    wr = jnp.pad(w_router, ((0, 0), (0, LANES - N_EXPERTS)))
    wrh = wr.astype(BF16)
    wrl = (wr - wrh.astype(F32)).astype(BF16)
    br = jnp.pad(b_router, (0, LANES - N_EXPERTS))[None, :]
    tri = (lax.broadcasted_iota(I32, (MIX_TILE, MIX_TILE), 1)
           < lax.broadcasted_iota(I32, (MIX_TILE, MIX_TILE), 0)).astype(BF16)
    return (win, bin_, ln_g.reshape(1, GMLP_WIDTH), ln_b.reshape(1, GMLP_WIDTH), w_s, b_s.T,
            woa, wog, b_o[None, :], ln1_g[None, :], ln1_b[None, :], wrh, wrl, br, tri)


def _dest_rows(meta, pstart):
    e = meta[META_E:META_E + TOP_K].astype(I32)
    r = meta[META_R:META_R + TOP_K].astype(I32)
    hit = e[None] == jnp.arange(N_EXPERTS, dtype=I32)[:, None, None]
    return jnp.sum(jnp.where(hit, pstart[:, None, None], 0), axis=0) + r


def _moe(token_sets, cnt, experts, sum_on_sc):
    n_assign = sum(hw.shape[0] for hw, _, _ in token_sets) * TOP_K
    n_tiles = (n_assign + N_EXPERTS * (ROW_TILE - 1)) // ROW_TILE
    counts = cnt[0, :N_EXPERTS].astype(I32)
    padded = (counts + ROW_TILE - 1) // ROW_TILE * ROW_TILE
    pend = jnp.cumsum(padded)
    pstart = pend - padded
    n_active = (pend[-1:] // ROW_TILE).astype(I32)
    tile_start = jnp.arange(n_tiles, dtype=I32) * ROW_TILE
    tile_e = jnp.minimum(jnp.sum(pend[None, :] <= tile_start[:, None], axis=1), N_EXPERTS - 1).astype(I32)
    hit = tile_e[:, None] == jnp.arange(N_EXPERTS, dtype=I32)[None, :]
    valid_end = jnp.sum(jnp.where(hit, pstart + counts, 0), axis=1)
    half = (valid_end - tile_start <= ROW_TILE // 2).astype(I32)

    lists = lambda d: [d[kk][None, :] for kk in range(TOP_K)]
    dest_sets = [lists(_dest_rows(meta, pstart)) for _, _, meta in token_sets]
    xs, gs = _sc_dispatch([(hw, gates, dests) for (hw, gates, _), dests in zip(token_sets, dest_sets)],
                          n_tiles * ROW_TILE)
    if sum_on_sc:
        return _sc_combine(_experts(tile_e, n_active, half, xs, gs, *experts, COMBINE_WIDTH), dest_sets)
    return _sc_gather(_experts(tile_e, n_active, half, xs, gs, *experts, SLAB_WIDTH), dest_sets)


def kernel(x_prompt, x_sample, cache_k, cache_v, w_in, b_in, attn_sinks, gmlp_ln_g, gmlp_ln_b, w_spatial, b_spatial, w_o, b_o, ln1_g, ln1_b, w_router, b_router, w_gate_up, b_gate_up, w_down, b_down, ln2_g, ln2_b):
    assert w_in.shape[0] == DEPTH
    batch, seq, _ = x_prompt.shape
    dec_batch, dec_seq, _ = x_sample.shape
    tok_p, tok_s = batch * seq, dec_batch * dec_seq
    assert dec_seq == CHUNK and seq % MIX_TILE == 0 and batch >= 2
    assert all(t % MIX_TILE == 0 and t % FINAL_TILE == 0 and t % SC_WINDOW == 0 for t in (tok_p, tok_s))

    weights = _mixer_weights(w_in[0], b_in[0], gmlp_ln_g[0], gmlp_ln_b[0], w_spatial[0], b_spatial[0],
                             w_o[0], b_o[0], ln1_g[0], ln1_b[0], w_router[0], b_router[0])
    sinks = attn_sinks[0]
    experts = (w_gate_up, b_gate_up[0][:, None, :], w_down, b_down[0][:, None, :])

    batch_a = batch // 2
    h_a, hw_a, meta_a, gates_a, cnt_a, kt_a, vt_a = _mix_prompt(sinks, x_prompt, weights, 0, batch_a)
    summed_a, = _moe([(hw_a, gates_a, meta_a)], cnt_a, experts, sum_on_sc=True)
    h_b, hw_b, meta_b, gates_b, cnt_b, kt_b, vt_b = _mix_prompt(sinks, x_prompt, weights, batch_a, batch - batch_a)
    h_s, hw_s, meta_s, gates_s, cnt_bs, ks, vs, gvs = _mix_sample(
        sinks, x_sample.reshape(tok_s, D_MODEL),
        cache_k[0].reshape(dec_batch * WINDOW, KV_WIDTH), cache_v[0].reshape(dec_batch * WINDOW, KV_WIDTH),
        cnt_b, weights)
    picked_b, picked_s = _moe([(hw_b, gates_b, meta_b), (hw_s, gates_s, meta_s)], cnt_bs, experts, sum_on_sc=False)

    g2, b2 = ln2_g[0][None, :], ln2_b[0][None, :]
    y_p = _final_norm(h_a, summed_a, True, g2, b2, tok_p, 0)
    y_p = _final_norm(h_b, picked_b, False, g2, b2, tok_p, batch_a * seq, earlier=y_p).reshape(batch, seq, D_MODEL)
    y_s = _final_norm(h_s, picked_s, False, g2, b2, tok_s, 0).reshape(dec_batch, dec_seq, D_MODEL)
    kt, vt = jnp.concatenate([kt_a, kt_b]), jnp.concatenate([vt_a, vt_b])

    kv5 = lambda a, nb, rows: a.reshape(DEPTH, nb, rows, N_KV_HEADS, HEAD_DIM)
    return (y_p, y_s, kv5(kt, batch, WINDOW), kv5(vt, batch, WINDOW),
            kv5(ks, dec_batch, dec_seq), kv5(vs, dec_batch, dec_seq),
            gvs.reshape(DEPTH, dec_batch, dec_seq, GMLP_GROUPS, GMLP_GROUP_DIM))
```

```python
import functools

import jax
import jax.numpy as jnp
from jax import lax
from jax.experimental import pallas as pl
from jax.experimental.pallas import tpu as pltpu
from jax.experimental.pallas import tpu_sc as plsc

F32 = jnp.float32
BF16 = jnp.bfloat16
I32 = jnp.int32

D_MODEL = 1024
CHUNK = 64
N_HEADS = 8
N_KV_HEADS = 2
HEAD_DIM = 64
Q_GROUP = N_HEADS // N_KV_HEADS
KV_WIDTH = N_KV_HEADS * HEAD_DIM
WINDOW = 128
KEYS = WINDOW + CHUNK
GMLP_GROUPS = 4
GMLP_GROUP_DIM = 128
GMLP_WIDTH = GMLP_GROUPS * GMLP_GROUP_DIM
GMLP_CHUNK = 128
ATTN_WIDTH = N_HEADS * HEAD_DIM
N_EXPERTS = 32
TOP_K = 4
D_FF = 1024
SWIGLU_LIMIT = 7.0
SWIGLU_ALPHA = 1.702
DEPTH = 1
DN_ALPHA = (2 * DEPTH) ** 0.25
LN_EPS = 1e-5
NEG_INF = -1e30

LANES = 128
K_OFF = ATTN_WIDTH
V_OFF = K_OFF + KV_WIDTH
U_OFF = V_OFF + KV_WIDTH
GV_OFF = U_OFF + GMLP_WIDTH
Z_WIDTH = GV_OFF + GMLP_WIDTH

MIX_TILE = 512
ROW_TILE = 768
ROW_UNIT = 256
FINAL_TILE = 512
SC_WINDOW = 128
SC_SLABS = 4
SLAB_WIDTH = D_MODEL // SC_SLABS
PACK_WIDTH = D_MODEL // 2
PACK_SLABS = PACK_WIDTH // SLAB_WIDTH
COMBINE_WIDTH = LANES
COMBINE_SLABS = PACK_WIDTH // COMBINE_WIDTH
VMEM_LIMIT = 58 * 1024 * 1024

META_E, META_R = 0, 4


def _gelu_tanh(x):
    return 0.5 * x * (1.0 + jnp.tanh(0.7978845608028654 * (x + 0.044715 * x * x * x)))


def _layer_norm(x, g, b):
    mu = jnp.mean(x, axis=-1, keepdims=True)
    xc = x - mu
    var = jnp.mean(xc * xc, axis=-1, keepdims=True)
    return xc * lax.rsqrt(var + LN_EPS) * g + b


def _dot(a, b):
    return jnp.dot(a, b, preferred_element_type=F32)


def _mixer_kernel(*refs, is_prompt, tt):
    n_chunks = tt // CHUNK
    it = iter(refs)
    sinks_ref = next(it)
    x_ref = next(it)
    if not is_prompt:
        ck_ref, cv_ref, base0_ref = next(it), next(it), next(it)
    (win_ref, bin_ref, lng_ref, lnb_ref, ws_ref, bs_ref, woa_ref, wog_ref, bo_ref, l1g_ref, l1b_ref,
     wrh_ref, wrl_ref, br_ref, tri_ref) = (next(it) for _ in range(15))
    h_ref, hw_ref, meta_ref, gate_ref, cnt_ref = (next(it) for _ in range(5))
    if is_prompt:
        kt_ref, vt_ref = next(it), next(it)
    else:
        ko_ref, vo_ref, gvo_ref = next(it), next(it), next(it)
    z_ref, kext_ref, vext_ref, att_ref, gm_ref, base_ref = (next(it) for _ in range(6))

    if is_prompt:
        first = (pl.program_id(0) == 0) & (pl.program_id(1) == 0)
        tile_in_seq = pl.program_id(1)

        @pl.when(first)
        def _():
            base_ref[...] = jnp.zeros_like(base_ref)

        @pl.when(tile_in_seq == 0)
        def _():
            kext_ref[:, 0:WINDOW, :] = jnp.zeros((N_KV_HEADS, WINDOW, KV_WIDTH), BF16)
            vext_ref[:, 0:WINDOW, :] = jnp.zeros((N_KV_HEADS, WINDOW, KV_WIDTH), BF16)
    else:
        @pl.when(pl.program_id(0) == 0)
        def _():
            base_ref[...] = base0_ref[...]

    x = x_ref[...]
    z_ref[...] = _dot(x.astype(BF16), win_ref[...]) + bin_ref[...]

    def put_kv(ext_ref, row0, rows_f32):
        n = rows_f32.shape[0]
        swapped = pltpu.roll(rows_f32, HEAD_DIM, axis=1)
        low = lax.broadcasted_iota(I32, (n, KV_WIDTH), 1) < HEAD_DIM
        ext_ref[0, row0:row0 + n, :] = jnp.where(low, rows_f32, swapped).astype(BF16)
        ext_ref[1, row0:row0 + n, :] = jnp.where(low, swapped, rows_f32).astype(BF16)

    k = z_ref[:, K_OFF:K_OFF + KV_WIDTH]
    v = z_ref[:, V_OFF:V_OFF + KV_WIDTH]
    if is_prompt:
        put_kv(kext_ref, WINDOW, k)
        put_kv(vext_ref, WINDOW, v)
        kt_ref[...] = k[tt - WINDOW:, :]
        vt_ref[...] = v[tt - WINDOW:, :]
        key_stride = CHUNK
    else:
        ko_ref[...] = k
        vo_ref[...] = v
        for c in range(n_chunks):
            put_kv(kext_ref, KEYS * c, ck_ref[WINDOW * c:WINDOW * (c + 1), :])
            put_kv(vext_ref, KEYS * c, cv_ref[WINDOW * c:WINDOW * (c + 1), :])
            put_kv(kext_ref, KEYS * c + WINDOW, k[CHUNK * c:CHUNK * (c + 1), :])
            put_kv(vext_ref, KEYS * c + WINDOW, v[CHUNK * c:CHUNK * (c + 1), :])
        key_stride = KEYS

    rows = Q_GROUP * CHUNK
    row_i = lax.broadcasted_iota(I32, (rows, 1), 0)
    key_i = lax.broadcasted_iota(I32, (rows, KEYS), 1)
    sink_cols = []
    for hk in range(N_KV_HEADS):
        s = [sinks_ref[hk * Q_GROUP + g] for g in range(Q_GROUP)]
        sink_cols.append(jnp.where(row_i < CHUNK, s[0],
                                   jnp.where(row_i < 2 * CHUNK, s[1],
                                             jnp.where(row_i < 3 * CHUNK, s[2], s[3]))))
    low_half = lax.broadcasted_iota(I32, (CHUNK, LANES), 1) < HEAD_DIM
    for c in range(n_chunks):
        outs = []
        for hk in range(N_KV_HEADS):
            kc = kext_ref[hk, key_stride * c:key_stride * c + KEYS, :]
            vc = vext_ref[hk, key_stride * c:key_stride * c + KEYS, :]
            heads = []
            for pair in range(hk * Q_GROUP // 2, (hk + 1) * Q_GROUP // 2):
                both = z_ref[CHUNK * c:CHUNK * (c + 1), pair * LANES:(pair + 1) * LANES]
                heads += [jnp.where(low_half, both, 0.0), jnp.where(low_half, 0.0, both)]
            qs = jnp.concatenate(heads, axis=0).astype(BF16)
            sc = lax.dot_general(qs, kc, (((1,), (1,)), ((), ())), preferred_element_type=F32)
            sc = sc * (HEAD_DIM ** -0.5)
            if is_prompt and c < WINDOW // CHUNK:
                valid = (key_i >= WINDOW - CHUNK * c) | (tile_in_seq > 0)
                sc = jnp.where(valid, sc, NEG_INF)
            sink = sink_cols[hk]
            m = jnp.maximum(jnp.max(sc, axis=-1, keepdims=True), sink)
            p = jnp.exp(sc - m)
            den = jnp.sum(p, axis=-1, keepdims=True) + jnp.exp(sink - m)
            o = _dot(p.astype(BF16), vc) / den
            for j in range(Q_GROUP // 2):
                even, odd = o[CHUNK * 2 * j:CHUNK * (2 * j + 1), :], o[CHUNK * (2 * j + 1):CHUNK * (2 * j + 2), :]
                outs.append(jnp.where(low_half, even, odd))
        att_ref[CHUNK * c:CHUNK * (c + 1), :] = jnp.concatenate(outs, axis=1).astype(BF16)

    if is_prompt:
        kext_ref[:, 0:WINDOW, :] = kext_ref[:, tt:tt + WINDOW, :]
        vext_ref[:, 0:WINDOW, :] = vext_ref[:, tt:tt + WINDOW, :]

    gc = GMLP_CHUNK if is_prompt else CHUNK
    ri = lax.broadcasted_iota(I32, (gc, gc), 0)
    ci = lax.broadcasted_iota(I32, (gc, gc), 1)
    causal = (ci // CHUNK) <= (ri // CHUNK)
    for g in range(GMLP_GROUPS):
        lo, hi = g * GMLP_GROUP_DIM, (g + 1) * GMLP_GROUP_DIM
        u = _gelu_tanh(z_ref[:, U_OFF + lo:U_OFF + hi])
        gv = _layer_norm(_gelu_tanh(z_ref[:, GV_OFF + lo:GV_OFF + hi]), lng_ref[:, lo:hi], lnb_ref[:, lo:hi])
        if not is_prompt:
            gvo_ref[:, lo:hi] = gv
        gvb = gv.astype(BF16)
        wm = jnp.where(causal, ws_ref[g, 0:gc, 0:gc], 0.0).astype(BF16)
        bcol = bs_ref[0:gc, g:g + 1]
        for n in range(tt // gc):
            sp = _dot(wm, gvb[gc * n:gc * (n + 1), :]) + bcol
            gm_ref[gc * n:gc * (n + 1), lo:hi] = (u[gc * n:gc * (n + 1), :] * sp).astype(BF16)

    y = _dot(att_ref[...], woa_ref[...]) + _dot(gm_ref[...], wog_ref[...]) + bo_ref[...]
    h = _layer_norm(DN_ALPHA * x + y, l1g_ref[...], l1b_ref[...])
    h_ref[...] = h
    hw_ref[...] = pltpu.pack_elementwise([h[:, :PACK_WIDTH], h[:, PACK_WIDTH:]], packed_dtype=BF16)

    h_hi = h.astype(BF16)
    h_lo = (h - h_hi.astype(F32)).astype(BF16)
    logits = _dot(h_hi, wrh_ref[...]) + _dot(h_lo, wrh_ref[...]) + _dot(h_hi, wrl_ref[...]) + br_ref[...]
    lane = lax.broadcasted_iota(I32, (tt, LANES), 1)
    lane_f = lane.astype(F32)
    l = jnp.where(lane < N_EXPERTS, logits, -jnp.inf)
    tops, idxs, hots = [], [], []
    for _ in range(TOP_K):
        m = jnp.max(l, axis=-1, keepdims=True)
        idx = jnp.min(jnp.where(l == m, lane_f, float(LANES)), axis=-1, keepdims=True)
        hot = lane_f == idx
        l = jnp.where(hot, -jnp.inf, l)
        tops.append(m)
        idxs.append(idx)
        hots.append(hot)
    es = [jnp.exp(t - tops[0]) for t in tops]
    esum = es[0] + es[1] + es[2] + es[3]
    chosen = jnp.where(hots[0] | hots[1] | hots[2] | hots[3], 1.0, 0.0)
    before = _dot(tri_ref[...], chosen.astype(BF16)) + base_ref[...]
    meta = jnp.zeros((tt, LANES), F32)
    for kk in range(TOP_K):
        rank = jnp.sum(jnp.where(hots[kk], before, 0.0), axis=-1, keepdims=True)
        meta = jnp.where(lane == META_E + kk, idxs[kk], meta)
        meta = jnp.where(lane == META_R + kk, rank, meta)
        gate_ref[:, kk * LANES:(kk + 1) * LANES] = jnp.broadcast_to(es[kk] / esum, (tt, LANES))
    meta_ref[...] = jnp.transpose(meta)[0:2 * TOP_K, :]
    base_ref[...] = base_ref[...] + jnp.sum(chosen, axis=0, keepdims=True)
    cnt_ref[...] = base_ref[...]


def _const_spec(shape):
    nd = len(shape)
    return pl.BlockSpec(shape, lambda *_: (0,) * nd, pipeline_mode=pl.Buffered(1))


def _mixer_weight_specs(tt):
    return [
        _const_spec((D_MODEL, Z_WIDTH)), _const_spec((1, Z_WIDTH)),
        _const_spec((1, GMLP_WIDTH)), _const_spec((1, GMLP_WIDTH)),
        _const_spec((GMLP_GROUPS, GMLP_CHUNK, GMLP_CHUNK)), _const_spec((GMLP_CHUNK, GMLP_GROUPS)),
        _const_spec((ATTN_WIDTH, D_MODEL)), _const_spec((GMLP_WIDTH, D_MODEL)), _const_spec((1, D_MODEL)),
        _const_spec((1, D_MODEL)), _const_spec((1, D_MODEL)),
        _const_spec((D_MODEL, LANES)), _const_spec((D_MODEL, LANES)), _const_spec((1, LANES)),
        _const_spec((tt, tt)),
    ]


def _mixer_scratch(tt, kext_rows):
    return [
        pltpu.VMEM((tt, Z_WIDTH), F32),
        pltpu.VMEM((N_KV_HEADS, kext_rows, KV_WIDTH), BF16), pltpu.VMEM((N_KV_HEADS, kext_rows, KV_WIDTH), BF16),
        pltpu.VMEM((tt, ATTN_WIDTH), BF16), pltpu.VMEM((tt, GMLP_WIDTH), BF16),
        pltpu.VMEM((1, LANES), F32),
    ]


def _mix_prompt(sinks, x, weights, first_batch, batch):
    seq = x.shape[1]
    tt = MIX_TILE
    n_tiles = seq // tt
    tok = batch * seq
    smem = pl.BlockSpec(memory_space=pltpu.SMEM)
    return pl.pallas_call(
        functools.partial(_mixer_kernel, is_prompt=True, tt=tt),
        grid=(batch, n_tiles),
        in_specs=[smem, pl.BlockSpec((None, tt, D_MODEL), lambda b, i: (b + first_batch, i, 0))]
        + _mixer_weight_specs(tt),
        out_specs=[
            pl.BlockSpec((tt, D_MODEL), lambda b, i: (b * n_tiles + i, 0)),
            pl.BlockSpec((tt, PACK_WIDTH), lambda b, i: (b * n_tiles + i, 0)),
            pl.BlockSpec((2 * TOP_K, tt), lambda b, i: (0, b * n_tiles + i)),
            pl.BlockSpec((tt, TOP_K * LANES), lambda b, i: (b * n_tiles + i, 0)),
            pl.BlockSpec((1, LANES), lambda b, i: (0, 0)),
            pl.BlockSpec((WINDOW, KV_WIDTH), lambda b, i: (b, 0)),
            pl.BlockSpec((WINDOW, KV_WIDTH), lambda b, i: (b, 0)),
        ],
        out_shape=[
            jax.ShapeDtypeStruct((tok, D_MODEL), F32), jax.ShapeDtypeStruct((tok, PACK_WIDTH), jnp.uint32),
            jax.ShapeDtypeStruct((2 * TOP_K, tok), F32), jax.ShapeDtypeStruct((tok, TOP_K * LANES), F32),
            jax.ShapeDtypeStruct((1, LANES), F32),
            jax.ShapeDtypeStruct((batch * WINDOW, KV_WIDTH), F32),
            jax.ShapeDtypeStruct((batch * WINDOW, KV_WIDTH), F32),
        ],
        scratch_shapes=_mixer_scratch(tt, WINDOW + tt),
        compiler_params=pltpu.CompilerParams(
            dimension_semantics=("arbitrary", "arbitrary"), vmem_limit_bytes=VMEM_LIMIT),
        name="mix_prompt",
    )(sinks, x, *weights)


def _mix_sample(sinks, x2, ck, cv, base0, weights):
    tok = x2.shape[0]
    tt = MIX_TILE
    n_chunks = tt // CHUNK
    cache_rows = n_chunks * WINDOW
    smem = pl.BlockSpec(memory_space=pltpu.SMEM)
    row = lambda w: pl.BlockSpec((tt, w), lambda i: (i, 0))
    return pl.pallas_call(
        functools.partial(_mixer_kernel, is_prompt=False, tt=tt),
        grid=(tok // tt,),
        in_specs=[smem, row(D_MODEL),
                  pl.BlockSpec((cache_rows, KV_WIDTH), lambda i: (i, 0)),
                  pl.BlockSpec((cache_rows, KV_WIDTH), lambda i: (i, 0)),
                  _const_spec((1, LANES))] + _mixer_weight_specs(tt),
        out_specs=[row(D_MODEL), row(PACK_WIDTH), pl.BlockSpec((2 * TOP_K, tt), lambda i: (0, i)),
                   row(TOP_K * LANES),
                   pl.BlockSpec((1, LANES), lambda i: (0, 0)),
                   row(KV_WIDTH), row(KV_WIDTH), row(GMLP_WIDTH)],
        out_shape=[
            jax.ShapeDtypeStruct((tok, D_MODEL), F32), jax.ShapeDtypeStruct((tok, PACK_WIDTH), jnp.uint32),
            jax.ShapeDtypeStruct((2 * TOP_K, tok), F32), jax.ShapeDtypeStruct((tok, TOP_K * LANES), F32),
            jax.ShapeDtypeStruct((1, LANES), F32),
            jax.ShapeDtypeStruct((tok, KV_WIDTH), F32), jax.ShapeDtypeStruct((tok, KV_WIDTH), F32),
            jax.ShapeDtypeStruct((tok, GMLP_WIDTH), F32),
        ],
        scratch_shapes=_mixer_scratch(tt, n_chunks * KEYS),
        compiler_params=pltpu.CompilerParams(
            dimension_semantics=("arbitrary",), vmem_limit_bytes=VMEM_LIMIT),
        name="mix_sample",
    )(sinks, x2, ck, cv, base0, *weights)


def _sc_pipeline(body, n_tok, in_specs, out_specs):
    return pltpu.emit_pipeline(
        body, grid=(n_tok // SC_WINDOW,), in_specs=in_specs, out_specs=out_specs,
        core_axis_name=("core", "subcore"), dimension_semantics=(pltpu.PARALLEL,))


def _index_specs():
    return [pl.BlockSpec((1, SC_WINDOW), lambda i: (0, i))] * TOP_K


def _sc_dispatch(token_sets, n_rows):
    win, wid = SC_WINDOW, SLAB_WIDTH
    per_set = 2 + TOP_K
    n_in = per_set * len(token_sets)
    mesh = plsc.VectorSubcoreMesh(core_axis_name="core", subcore_axis_name="subcore")
    out_type = ([jax.ShapeDtypeStruct((n_rows, wid), jnp.uint32)] * PACK_SLABS
                + [jax.ShapeDtypeStruct((n_rows, LANES), F32)])

    @functools.partial(pl.kernel, out_type=out_type, mesh=mesh, scratch_types=[], name="sc_dispatch")
    def run(*refs):
        xs_hbm, gs_hbm = refs[n_in:n_in + PACK_SLABS], refs[n_in + PACK_SLABS]

        for s in range(len(token_sets)):
            h_hbm, g_hbm = refs[per_set * s], refs[per_set * s + 1]
            i_hbm = refs[per_set * s + 2:per_set * (s + 1)]
            n_tok = h_hbm.shape[0]
            for q in range(PACK_SLABS):
                def rows_body(x_vmem, *i_vmem, q=q):
                    for kk in range(TOP_K):
                        pltpu.sync_copy(x_vmem, xs_hbm[q].at[i_vmem[kk].at[0]])

                _sc_pipeline(rows_body, n_tok, [pl.BlockSpec((win, wid), lambda i, q=q: (i, q))] + _index_specs(),
                             [])(h_hbm, *i_hbm)
            for kk in range(TOP_K):
                def gate_body(g_vmem, i_vmem):
                    pltpu.sync_copy(g_vmem, gs_hbm.at[i_vmem.at[0]])

                _sc_pipeline(gate_body, n_tok,
                             [pl.BlockSpec((win, LANES), lambda i, kk=kk: (i, kk)), _index_specs()[0]],
                             [])(g_hbm, i_hbm[kk])

    outs = run(*[a for hw, gates, dests in token_sets for a in (hw, gates, *dests)])
    return outs[:PACK_SLABS], outs[PACK_SLABS]


def _expert_kernel(te_ref, na_ref, par_ref, half_ref, *refs, out_slabs):
    x_refs, (gs_ref, wgu_ref, bgu_ref, wd_ref, bd_ref) = refs[:PACK_SLABS], refs[PACK_SLABS:PACK_SLABS + 5]
    y_refs = refs[PACK_SLABS + 5:PACK_SLABS + 5 + out_slabs]
    wgu_bf, wd_bf = refs[PACK_SLABS + 5 + out_slabs:]
    out_width = PACK_WIDTH // out_slabs
    s = pl.program_id(0)
    t = jnp.maximum(s - 1, 0)
    do_tile = (s >= 1) & (t < na_ref[0])

    def starts_expert(i):
        i = jnp.minimum(i, te_ref.shape[0] - 1)
        return (i == 0) | (te_ref[i] != te_ref[jnp.maximum(i - 1, 0)])

    @pl.when(do_tile & starts_expert(t))
    def _():
        wd_bf[...] = wd_ref[0].astype(BF16)

    def compute(n_rows):
        words = [r[0:n_rows, :] for r in x_refs]
        x = jnp.concatenate(
            [pltpu.unpack_elementwise(w, index=index, packed_dtype=BF16, unpacked_dtype=F32)
             for index in range(2) for w in words], axis=1).astype(BF16)
        hmid = _dot(x, wgu_bf[par_ref[t]]) + bgu_ref[0]
        gate = jnp.minimum(hmid[:, :D_FF], SWIGLU_LIMIT)
        up = jnp.clip(hmid[:, D_FF:], -SWIGLU_LIMIT, SWIGLU_LIMIT)
        act = (up + 1.0) * gate * jax.nn.sigmoid(SWIGLU_ALPHA * gate)
        y = _dot(act.astype(BF16), wd_bf[...]) + bd_ref[0]
        y = y * jnp.concatenate([gs_ref[0:n_rows, :]] * (D_MODEL // LANES), axis=1)
        packed = pltpu.pack_elementwise([y[:, :PACK_WIDTH], y[:, PACK_WIDTH:]], packed_dtype=BF16)
        for q in range(out_slabs):
            y_refs[q][0:n_rows, :] = packed[:, q * out_width:(q + 1) * out_width]

    for units in range(1, ROW_TILE // ROW_UNIT + 1):
        @pl.when(do_tile & (half_ref[t] == units))
        def _(units=units):
            compute(units * ROW_UNIT)

    @pl.when((s < na_ref[0]) & starts_expert(s))
    def _():
        wgu_bf[par_ref[s]] = wgu_ref[0].astype(BF16)


def _experts(tile_e, n_active, half, xs, gs, wgu, bgu, wd, bd, out_width):
    tm = ROW_TILE
    out_slabs = PACK_WIDTH // out_width
    n_tiles = gs.shape[0] // tm
    parity = (jnp.cumsum(jnp.concatenate([jnp.zeros((1,), I32), (tile_e[1:] != tile_e[:-1]).astype(I32)])) % 2
              ).astype(I32)
    tile_of = lambda s, na: jnp.clip(s - 1, 0, na[0] - 1)
    row_map = lambda s, te, na, *_: (tile_of(s, na), 0)
    w_map = lambda s, te, na, *_: (te[tile_of(s, na)], 0, 0)
    wd_map = lambda s, te, na, *_: (0, te[tile_of(s, na)], 0, 0)
    wgu_map = lambda s, te, na, *_: (0, te[jnp.minimum(s, na[0] - 1)], 0, 0)
    return pl.pallas_call(
        functools.partial(_expert_kernel, out_slabs=out_slabs),
        grid_spec=pltpu.PrefetchScalarGridSpec(
            num_scalar_prefetch=4,
            grid=(n_tiles + 1,),
            in_specs=[pl.BlockSpec((tm, SLAB_WIDTH), row_map)] * PACK_SLABS + [
                pl.BlockSpec((tm, LANES), row_map),
                pl.BlockSpec((None, 1, D_MODEL, 2 * D_FF), wgu_map),
                pl.BlockSpec((1, 1, 2 * D_FF), w_map),
                pl.BlockSpec((None, 1, D_FF, D_MODEL), wd_map),
                pl.BlockSpec((1, 1, D_MODEL), w_map),
            ],
            out_specs=[pl.BlockSpec((tm, out_width), row_map)] * out_slabs,
            scratch_shapes=[pltpu.VMEM((2, D_MODEL, 2 * D_FF), BF16), pltpu.VMEM((D_FF, D_MODEL), BF16)],
        ),
        out_shape=[jax.ShapeDtypeStruct((n_tiles * tm, out_width), jnp.uint32)] * out_slabs,
        compiler_params=pltpu.CompilerParams(
            dimension_semantics=("arbitrary",), vmem_limit_bytes=VMEM_LIMIT),
        name="experts",
    )(tile_e, n_active, parity, half, *xs, gs, wgu, bgu, wd, bd)


def _sc_combine(ys, dest_sets):
    n_sets = len(dest_sets)
    toks = [dests[0].shape[1] for dests in dest_sets]
    mesh = plsc.VectorSubcoreMesh(core_axis_name="core", subcore_axis_name="subcore")
    out_type = [jax.ShapeDtypeStruct((t, 2 * COMBINE_WIDTH), F32) for t in toks for _ in range(COMBINE_SLABS)]
    lanes = plsc.get_sparse_core_info().num_lanes
    high = jnp.uint32(0xFFFF0000)

    @functools.partial(pl.kernel, out_type=out_type, mesh=mesh, name="sc_combine",
                       scratch_types=[pltpu.VMEM((SC_WINDOW, COMBINE_WIDTH), jnp.uint32)] * 2,
                       compiler_params=pltpu.CompilerParams(needs_layout_passes=False))
    def run(*refs):
        ys_hbm = refs[:COMBINE_SLABS]
        out0 = COMBINE_SLABS + TOP_K * n_sets
        buf_a, buf_b = refs[-2:]

        def add_pair(ys_q, i_a, i_b, o_vmem, first):
            pltpu.sync_copy(ys_q.at[i_a.at[0]], buf_a)
            pltpu.sync_copy(ys_q.at[i_b.at[0]], buf_b)

            @pl.loop(0, SC_WINDOW)
            def _(r):
                for c in range(COMBINE_WIDTH // lanes):
                    cols = pl.ds(c * lanes, lanes)
                    wa, wb = buf_a[r, cols], buf_b[r, cols]
                    lo = plsc.bitcast(wa << 16, F32) + plsc.bitcast(wb << 16, F32)
                    hi = plsc.bitcast(wa & high, F32) + plsc.bitcast(wb & high, F32)
                    lo_cols, hi_cols = cols, pl.ds(COMBINE_WIDTH + c * lanes, lanes)
                    if first:
                        o_vmem[r, lo_cols] = lo
                        o_vmem[r, hi_cols] = hi
                    else:
                        o_vmem[r, lo_cols] = o_vmem[r, lo_cols] + lo
                        o_vmem[r, hi_cols] = o_vmem[r, hi_cols] + hi

        for s in range(n_sets):
            i_hbm = refs[COMBINE_SLABS + TOP_K * s:COMBINE_SLABS + TOP_K * (s + 1)]
            f_hbm = refs[out0 + COMBINE_SLABS * s:out0 + COMBINE_SLABS * (s + 1)]
            for q in range(COMBINE_SLABS):
                def body(i0, i1, i2, i3, o_vmem, q=q):
                    add_pair(ys_hbm[q], i0, i1, o_vmem, True)
                    add_pair(ys_hbm[q], i2, i3, o_vmem, False)

                _sc_pipeline(body, toks[s], _index_specs(),
                             [pl.BlockSpec((SC_WINDOW, 2 * COMBINE_WIDTH), lambda i: (i, 0))])(*i_hbm, f_hbm[q])

    outs = run(*ys, *[d for dests in dest_sets for d in dests])
    return [outs[COMBINE_SLABS * s:COMBINE_SLABS * (s + 1)] for s in range(n_sets)]


def _sc_gather(ys, dest_sets):
    n_out = TOP_K * PACK_SLABS
    n_sets = len(dest_sets)
    toks = [dests[0].shape[1] for dests in dest_sets]
    mesh = plsc.VectorSubcoreMesh(core_axis_name="core", subcore_axis_name="subcore")
    out_type = [jax.ShapeDtypeStruct((t, SLAB_WIDTH), jnp.uint32) for t in toks for _ in range(n_out)]

    @functools.partial(pl.kernel, out_type=out_type, mesh=mesh, scratch_types=[], name="sc_gather")
    def run(*refs):
        ys_hbm = refs[:PACK_SLABS]
        out0 = PACK_SLABS + TOP_K * n_sets

        for s in range(n_sets):
            i_hbm = refs[PACK_SLABS + TOP_K * s:PACK_SLABS + TOP_K * (s + 1)]
            f_hbm = refs[out0 + n_out * s:out0 + n_out * (s + 1)]
            for kk in range(TOP_K):
                for q in range(PACK_SLABS):
                    def body(i_vmem, o_vmem, q=q):
                        pltpu.sync_copy(ys_hbm[q].at[i_vmem.at[0]], o_vmem)

                    _sc_pipeline(body, toks[s], [_index_specs()[0]],
                                 [pl.BlockSpec((SC_WINDOW, SLAB_WIDTH), lambda i: (i, 0))]
                                 )(i_hbm[kk], f_hbm[kk * PACK_SLABS + q])

    outs = run(*ys, *[d for dests in dest_sets for d in dests])
    return [outs[n_out * s:n_out * (s + 1)] for s in range(n_sets)]


def _final_kernel(*refs, summed):
    n_in = COMBINE_SLABS if summed else TOP_K * PACK_SLABS
    h_ref, f_refs, g_ref, b_ref, out_ref = refs[0], refs[1:1 + n_in], refs[1 + n_in], refs[2 + n_in], refs[-1]
    if summed:
        f = jnp.concatenate([r[:, :COMBINE_WIDTH] for r in f_refs] + [r[:, COMBINE_WIDTH:] for r in f_refs], axis=1)
    else:
        halves = []
        for index in range(2):
            for q in range(PACK_SLABS):
                parts = [pltpu.unpack_elementwise(f_refs[kk * PACK_SLABS + q][...], index=index,
                                                  packed_dtype=BF16, unpacked_dtype=F32) for kk in range(TOP_K)]
                halves.append((parts[0] + parts[1]) + (parts[2] + parts[3]))
        f = jnp.concatenate(halves, axis=1)
    out_ref[...] = _layer_norm(DN_ALPHA * h_ref[...] + f, g_ref[...], b_ref[...])


def _final_norm(h, combined, summed, ln_g, ln_b, out_rows, first_row, earlier=None):
    tt = FINAL_TILE
    tok = h.shape[0]
    first_tile = first_row // tt
    row = lambda w: pl.BlockSpec((tt, w), lambda i: (i, 0))
    in_specs = ([row(D_MODEL)] + [row(c.shape[1]) for c in combined]
                + [_const_spec((1, D_MODEL)), _const_spec((1, D_MODEL))])
    args = [h, *combined, ln_g, ln_b]
    aliases = {}
    if earlier is not None:
        in_specs.append(pl.BlockSpec(memory_space=pl.ANY))
        aliases = {len(args): 0}
        args.append(earlier)
    return pl.pallas_call(
        functools.partial(_final_kernel, summed=summed),
        grid=(tok // tt,),
        in_specs=in_specs,
        out_specs=pl.BlockSpec((tt, D_MODEL), lambda i: (i + first_tile, 0)),
        out_shape=jax.ShapeDtypeStruct((out_rows, D_MODEL), F32),
        input_output_aliases=aliases,
        compiler_params=pltpu.CompilerParams(dimension_semantics=("arbitrary",)),
        name="final_norm",
    )(*args)


def _mixer_weights(w_in, b_in, ln_g, ln_b, w_s, b_s, w_o, b_o, ln1_g, ln1_b, w_router, b_router):
    win = w_in.astype(BF16)
    bin_ = b_in[None, :]
    woa = w_o[:ATTN_WIDTH].astype(BF16)
    wog = w_o[ATTN_WIDTH:].astype(BF16)
    wr = jnp.pad(w_router, ((0, 0), (0, LANES - N_EXPERTS)))
    wrh = wr.astype(BF16)
    wrl = (wr - wrh.astype(F32)).astype(BF16)
    br = jnp.pad(b_router, (0, LANES - N_EXPERTS))[None, :]
    tri = (lax.broadcasted_iota(I32, (MIX_TILE, MIX_TILE), 1)
           < lax.broadcasted_iota(I32, (MIX_TILE, MIX_TILE), 0)).astype(BF16)
    return (win, bin_, ln_g.reshape(1, GMLP_WIDTH), ln_b.reshape(1, GMLP_WIDTH), w_s, b_s.T,
            woa, wog, b_o[None, :], ln1_g[None, :], ln1_b[None, :], wrh, wrl, br, tri)


def _dest_rows(meta, pstart):
    e = meta[META_E:META_E + TOP_K].astype(I32)
    r = meta[META_R:META_R + TOP_K].astype(I32)
    hit = e[None] == jnp.arange(N_EXPERTS, dtype=I32)[:, None, None]
    return jnp.sum(jnp.where(hit, pstart[:, None, None], 0), axis=0) + r


def _moe(token_sets, cnt, experts, sum_on_sc):
    n_assign = sum(hw.shape[0] for hw, _, _ in token_sets) * TOP_K
    n_tiles = (n_assign + N_EXPERTS * (ROW_TILE - 1)) // ROW_TILE
    counts = cnt[0, :N_EXPERTS].astype(I32)
    padded = (counts + ROW_TILE - 1) // ROW_TILE * ROW_TILE
    pend = jnp.cumsum(padded)
    pstart = pend - padded
    n_active = (pend[-1:] // ROW_TILE).astype(I32)
    tile_start = jnp.arange(n_tiles, dtype=I32) * ROW_TILE
    tile_e = jnp.minimum(jnp.sum(pend[None, :] <= tile_start[:, None], axis=1), N_EXPERTS - 1).astype(I32)
    hit = tile_e[:, None] == jnp.arange(N_EXPERTS, dtype=I32)[None, :]
    valid_end = jnp.sum(jnp.where(hit, pstart + counts, 0), axis=1)
    half = jnp.clip((valid_end - tile_start + ROW_UNIT - 1) // ROW_UNIT, 1, ROW_TILE // ROW_UNIT).astype(I32)

    lists = lambda d: [d[kk][None, :] for kk in range(TOP_K)]
    dest_sets = [lists(_dest_rows(meta, pstart)) for _, _, meta in token_sets]
    xs, gs = _sc_dispatch([(hw, gates, dests) for (hw, gates, _), dests in zip(token_sets, dest_sets)],
                          n_tiles * ROW_TILE)
    if sum_on_sc:
        return _sc_combine(_experts(tile_e, n_active, half, xs, gs, *experts, COMBINE_WIDTH), dest_sets)
    return _sc_gather(_experts(tile_e, n_active, half, xs, gs, *experts, SLAB_WIDTH), dest_sets)


def kernel(x_prompt, x_sample, cache_k, cache_v, w_in, b_in, attn_sinks, gmlp_ln_g, gmlp_ln_b, w_spatial, b_spatial, w_o, b_o, ln1_g, ln1_b, w_router, b_router, w_gate_up, b_gate_up, w_down, b_down, ln2_g, ln2_b):
    assert w_in.shape[0] == DEPTH
    batch, seq, _ = x_prompt.shape
    dec_batch, dec_seq, _ = x_sample.shape
    tok_p, tok_s = batch * seq, dec_batch * dec_seq
    assert dec_seq == CHUNK and seq % MIX_TILE == 0 and batch >= 2
    assert all(t % MIX_TILE == 0 and t % FINAL_TILE == 0 and t % SC_WINDOW == 0 for t in (tok_p, tok_s))

    weights = _mixer_weights(w_in[0], b_in[0], gmlp_ln_g[0], gmlp_ln_b[0], w_spatial[0], b_spatial[0],
                             w_o[0], b_o[0], ln1_g[0], ln1_b[0], w_router[0], b_router[0])
    sinks = attn_sinks[0]
    experts = (w_gate_up, b_gate_up[0][:, None, :], w_down, b_down[0][:, None, :])

    batch_a = batch // 2
    h_a, hw_a, meta_a, gates_a, cnt_a, kt_a, vt_a = _mix_prompt(sinks, x_prompt, weights, 0, batch_a)
    summed_a, = _moe([(hw_a, gates_a, meta_a)], cnt_a, experts, sum_on_sc=True)
    h_b, hw_b, meta_b, gates_b, cnt_b, kt_b, vt_b = _mix_prompt(sinks, x_prompt, weights, batch_a, batch - batch_a)
    h_s, hw_s, meta_s, gates_s, cnt_bs, ks, vs, gvs = _mix_sample(
        sinks, x_sample.reshape(tok_s, D_MODEL),
        cache_k[0].reshape(dec_batch * WINDOW, KV_WIDTH), cache_v[0].reshape(dec_batch * WINDOW, KV_WIDTH),
        cnt_b, weights)
    picked_b, picked_s = _moe([(hw_b, gates_b, meta_b), (hw_s, gates_s, meta_s)], cnt_bs, experts, sum_on_sc=False)

    g2, b2 = ln2_g[0][None, :], ln2_b[0][None, :]
    y_p = _final_norm(h_a, summed_a, True, g2, b2, tok_p, 0)
    y_p = _final_norm(h_b, picked_b, False, g2, b2, tok_p, batch_a * seq, earlier=y_p).reshape(batch, seq, D_MODEL)
    y_s = _final_norm(h_s, picked_s, False, g2, b2, tok_s, 0).reshape(dec_batch, dec_seq, D_MODEL)
    kt, vt = jnp.concatenate([kt_a, kt_b]), jnp.concatenate([vt_a, vt_b])

    kv5 = lambda a, nb, rows: a.reshape(DEPTH, nb, rows, N_KV_HEADS, HEAD_DIM)
    return (y_p, y_s, kv5(kt, batch, WINDOW), kv5(vt, batch, WINDOW),
            kv5(ks, dec_batch, dec_seq), kv5(vs, dec_batch, dec_seq),
            gvs.reshape(DEPTH, dec_batch, dec_seq, GMLP_GROUPS, GMLP_GROUP_DIM))
```

```python
import functools

import jax
import jax.numpy as jnp
from jax import lax
from jax.experimental import pallas as pl
from jax.experimental.pallas import tpu as pltpu
from jax.experimental.pallas import tpu_sc as plsc

F32 = jnp.float32
BF16 = jnp.bfloat16
I32 = jnp.int32

D_MODEL = 1024
CHUNK = 64
N_HEADS = 8
N_KV_HEADS = 2
HEAD_DIM = 64
Q_GROUP = N_HEADS // N_KV_HEADS
KV_WIDTH = N_KV_HEADS * HEAD_DIM
WINDOW = 128
KEYS = WINDOW + CHUNK
GMLP_GROUPS = 4
GMLP_GROUP_DIM = 128
GMLP_WIDTH = GMLP_GROUPS * GMLP_GROUP_DIM
GMLP_CHUNK = 128
ATTN_WIDTH = N_HEADS * HEAD_DIM
N_EXPERTS = 32
TOP_K = 4
D_FF = 1024
SWIGLU_LIMIT = 7.0
SWIGLU_ALPHA = 1.702
DEPTH = 1
DN_ALPHA = (2 * DEPTH) ** 0.25
LN_EPS = 1e-5
NEG_INF = -1e30

LANES = 128
K_OFF = ATTN_WIDTH
V_OFF = K_OFF + KV_WIDTH
U_OFF = V_OFF + KV_WIDTH
GV_OFF = U_OFF + GMLP_WIDTH
Z_WIDTH = GV_OFF + GMLP_WIDTH

MIX_TILE = 512
ROW_TILE = 768
ROW_UNIT = 256
FINAL_TILE = 1024
SC_WINDOW = 128
SC_SLABS = 4
SLAB_WIDTH = D_MODEL // SC_SLABS
PACK_WIDTH = D_MODEL // 2
PACK_SLABS = PACK_WIDTH // SLAB_WIDTH
COMBINE_WIDTH = LANES
COMBINE_SLABS = PACK_WIDTH // COMBINE_WIDTH
VMEM_LIMIT = 58 * 1024 * 1024

META_E, META_R = 0, 4


def _gelu_tanh(x):
    return 0.5 * x * (1.0 + jnp.tanh(0.7978845608028654 * (x + 0.044715 * x * x * x)))


def _layer_norm(x, g, b):
    mu = jnp.mean(x, axis=-1, keepdims=True)
    xc = x - mu
    var = jnp.mean(xc * xc, axis=-1, keepdims=True)
    return xc * lax.rsqrt(var + LN_EPS) * g + b


def _dot(a, b):
    return jnp.dot(a, b, preferred_element_type=F32)


def _mixer_kernel(*refs, is_prompt, tt):
    n_chunks = tt // CHUNK
    it = iter(refs)
    sinks_ref = next(it)
    x_ref = next(it)
    if not is_prompt:
        ck_ref, cv_ref, base0_ref = next(it), next(it), next(it)
    (win_ref, bin_ref, lng_ref, lnb_ref, ws_ref, bs_ref, woa_ref, wog_ref, bo_ref, l1g_ref, l1b_ref,
     wrh_ref, wrl_ref, br_ref, tri_ref) = (next(it) for _ in range(15))
    h_ref, hw_ref, meta_ref, gate_ref, cnt_ref = (next(it) for _ in range(5))
    if is_prompt:
        kt_ref, vt_ref = next(it), next(it)
    else:
        ko_ref, vo_ref, gvo_ref = next(it), next(it), next(it)
    z_ref, kext_ref, vext_ref, att_ref, gm_ref, base_ref = (next(it) for _ in range(6))

    if is_prompt:
        first = (pl.program_id(0) == 0) & (pl.program_id(1) == 0)
        tile_in_seq = pl.program_id(1)

        @pl.when(first)
        def _():
            base_ref[...] = jnp.zeros_like(base_ref)

        @pl.when(tile_in_seq == 0)
        def _():
            kext_ref[:, 0:WINDOW, :] = jnp.zeros((N_KV_HEADS, WINDOW, KV_WIDTH), BF16)
            vext_ref[:, 0:WINDOW, :] = jnp.zeros((N_KV_HEADS, WINDOW, KV_WIDTH), BF16)
    else:
        @pl.when(pl.program_id(0) == 0)
        def _():
            base_ref[...] = base0_ref[...]

    x = x_ref[...]
    z_ref[...] = _dot(x.astype(BF16), win_ref[...]) + bin_ref[...]

    def put_kv(ext_ref, row0, rows_f32):
        n = rows_f32.shape[0]
        swapped = pltpu.roll(rows_f32, HEAD_DIM, axis=1)
        low = lax.broadcasted_iota(I32, (n, KV_WIDTH), 1) < HEAD_DIM
        ext_ref[0, row0:row0 + n, :] = jnp.where(low, rows_f32, swapped).astype(BF16)
        ext_ref[1, row0:row0 + n, :] = jnp.where(low, swapped, rows_f32).astype(BF16)

    k = z_ref[:, K_OFF:K_OFF + KV_WIDTH]
    v = z_ref[:, V_OFF:V_OFF + KV_WIDTH]
    if is_prompt:
        put_kv(kext_ref, WINDOW, k)
        put_kv(vext_ref, WINDOW, v)
        kt_ref[...] = k[tt - WINDOW:, :]
        vt_ref[...] = v[tt - WINDOW:, :]
        key_stride = CHUNK
    else:
        ko_ref[...] = k
        vo_ref[...] = v
        for c in range(n_chunks):
            put_kv(kext_ref, KEYS * c, ck_ref[WINDOW * c:WINDOW * (c + 1), :])
            put_kv(vext_ref, KEYS * c, cv_ref[WINDOW * c:WINDOW * (c + 1), :])
            put_kv(kext_ref, KEYS * c + WINDOW, k[CHUNK * c:CHUNK * (c + 1), :])
            put_kv(vext_ref, KEYS * c + WINDOW, v[CHUNK * c:CHUNK * (c + 1), :])
        key_stride = KEYS

    rows = Q_GROUP * CHUNK
    row_i = lax.broadcasted_iota(I32, (rows, 1), 0)
    key_i = lax.broadcasted_iota(I32, (rows, KEYS), 1)
    sink_cols = []
    for hk in range(N_KV_HEADS):
        s = [sinks_ref[hk * Q_GROUP + g] for g in range(Q_GROUP)]
        sink_cols.append(jnp.where(row_i < CHUNK, s[0],
                                   jnp.where(row_i < 2 * CHUNK, s[1],
                                             jnp.where(row_i < 3 * CHUNK, s[2], s[3]))))
    low_half = lax.broadcasted_iota(I32, (CHUNK, LANES), 1) < HEAD_DIM
    for c in range(n_chunks):
        outs = []
        for hk in range(N_KV_HEADS):
            kc = kext_ref[hk, key_stride * c:key_stride * c + KEYS, :]
            vc = vext_ref[hk, key_stride * c:key_stride * c + KEYS, :]
            heads = []
            for pair in range(hk * Q_GROUP // 2, (hk + 1) * Q_GROUP // 2):
                both = z_ref[CHUNK * c:CHUNK * (c + 1), pair * LANES:(pair + 1) * LANES] * (HEAD_DIM ** -0.5)
                heads += [jnp.where(low_half, both, 0.0), jnp.where(low_half, 0.0, both)]
            qs = jnp.concatenate(heads, axis=0).astype(BF16)
            sc = lax.dot_general(qs, kc, (((1,), (1,)), ((), ())), preferred_element_type=F32)
            if is_prompt and c < WINDOW // CHUNK:
                valid = (key_i >= WINDOW - CHUNK * c) | (tile_in_seq > 0)
                sc = jnp.where(valid, sc, NEG_INF)
            sink = sink_cols[hk]
            m = jnp.maximum(jnp.max(sc, axis=-1, keepdims=True), sink)
            p = jnp.exp(sc - m)
            den = jnp.sum(p, axis=-1, keepdims=True) + jnp.exp(sink - m)
            o = _dot(p.astype(BF16), vc) / den
            for j in range(Q_GROUP // 2):
                even, odd = o[CHUNK * 2 * j:CHUNK * (2 * j + 1), :], o[CHUNK * (2 * j + 1):CHUNK * (2 * j + 2), :]
                outs.append(jnp.where(low_half, even, odd))
        att_ref[CHUNK * c:CHUNK * (c + 1), :] = jnp.concatenate(outs, axis=1).astype(BF16)

    if is_prompt:
        kext_ref[:, 0:WINDOW, :] = kext_ref[:, tt:tt + WINDOW, :]
        vext_ref[:, 0:WINDOW, :] = vext_ref[:, tt:tt + WINDOW, :]

    gc = GMLP_CHUNK if is_prompt else CHUNK
    ri = lax.broadcasted_iota(I32, (gc, gc), 0)
    ci = lax.broadcasted_iota(I32, (gc, gc), 1)
    causal = (ci // CHUNK) <= (ri // CHUNK)
    for g in range(GMLP_GROUPS):
        lo, hi = g * GMLP_GROUP_DIM, (g + 1) * GMLP_GROUP_DIM
        u = _gelu_tanh(z_ref[:, U_OFF + lo:U_OFF + hi])
        gv = _layer_norm(_gelu_tanh(z_ref[:, GV_OFF + lo:GV_OFF + hi]), lng_ref[:, lo:hi], lnb_ref[:, lo:hi])
        if not is_prompt:
            gvo_ref[:, lo:hi] = gv
        gvb = gv.astype(BF16)
        wm = jnp.where(causal, ws_ref[g, 0:gc, 0:gc], 0.0).astype(BF16)
        bcol = bs_ref[0:gc, g:g + 1]
        for n in range(tt // gc):
            sp = _dot(wm, gvb[gc * n:gc * (n + 1), :]) + bcol
            gm_ref[gc * n:gc * (n + 1), lo:hi] = (u[gc * n:gc * (n + 1), :] * sp).astype(BF16)

    y = _dot(att_ref[...], woa_ref[...]) + _dot(gm_ref[...], wog_ref[...]) + bo_ref[...]
    h = _layer_norm(DN_ALPHA * x + y, l1g_ref[...], l1b_ref[...])
    h_ref[...] = h
    hw_ref[...] = pltpu.pack_elementwise([h[:, :PACK_WIDTH], h[:, PACK_WIDTH:]], packed_dtype=BF16)

    h_hi = h.astype(BF16)
    h_lo = (h - h_hi.astype(F32)).astype(BF16)
    logits = _dot(h_hi, wrh_ref[...]) + _dot(h_lo, wrh_ref[...]) + _dot(h_hi, wrl_ref[...]) + br_ref[...]
    lane = lax.broadcasted_iota(I32, (tt, LANES), 1)
    lane_f = lane.astype(F32)
    l = jnp.where(lane < N_EXPERTS, logits, -jnp.inf)
    tops, idxs, hots = [], [], []
    for _ in range(TOP_K):
        m = jnp.max(l, axis=-1, keepdims=True)
        idx = jnp.min(jnp.where(l == m, lane_f, float(LANES)), axis=-1, keepdims=True)
        hot = lane_f == idx
        l = jnp.where(hot, -jnp.inf, l)
        tops.append(m)
        idxs.append(idx)
        hots.append(hot)
    es = [jnp.exp(t - tops[0]) for t in tops]
    esum = es[0] + es[1] + es[2] + es[3]
    chosen = jnp.where(hots[0] | hots[1] | hots[2] | hots[3], 1.0, 0.0)
    before = _dot(tri_ref[...], chosen.astype(BF16)) + base_ref[...]
    meta = jnp.zeros((tt, LANES), F32)
    for kk in range(TOP_K):
        rank = jnp.sum(jnp.where(hots[kk], before, 0.0), axis=-1, keepdims=True)
        meta = jnp.where(lane == META_E + kk, idxs[kk], meta)
        meta = jnp.where(lane == META_R + kk, rank, meta)
        gate_ref[:, kk * LANES:(kk + 1) * LANES] = jnp.broadcast_to(es[kk] / esum, (tt, LANES))
    meta_ref[...] = jnp.transpose(meta)[0:2 * TOP_K, :]
    base_ref[...] = base_ref[...] + jnp.sum(chosen, axis=0, keepdims=True)
    cnt_ref[...] = base_ref[...]


def _const_spec(shape):
    nd = len(shape)
    return pl.BlockSpec(shape, lambda *_: (0,) * nd, pipeline_mode=pl.Buffered(1))


def _mixer_weight_specs(tt):
    return [
        _const_spec((D_MODEL, Z_WIDTH)), _const_spec((1, Z_WIDTH)),
        _const_spec((1, GMLP_WIDTH)), _const_spec((1, GMLP_WIDTH)),
        _const_spec((GMLP_GROUPS, GMLP_CHUNK, GMLP_CHUNK)), _const_spec((GMLP_CHUNK, GMLP_GROUPS)),
        _const_spec((ATTN_WIDTH, D_MODEL)), _const_spec((GMLP_WIDTH, D_MODEL)), _const_spec((1, D_MODEL)),
        _const_spec((1, D_MODEL)), _const_spec((1, D_MODEL)),
        _const_spec((D_MODEL, LANES)), _const_spec((D_MODEL, LANES)), _const_spec((1, LANES)),
        _const_spec((tt, tt)),
    ]


def _mixer_scratch(tt, kext_rows):
    return [
        pltpu.VMEM((tt, Z_WIDTH), F32),
        pltpu.VMEM((N_KV_HEADS, kext_rows, KV_WIDTH), BF16), pltpu.VMEM((N_KV_HEADS, kext_rows, KV_WIDTH), BF16),
        pltpu.VMEM((tt, ATTN_WIDTH), BF16), pltpu.VMEM((tt, GMLP_WIDTH), BF16),
        pltpu.VMEM((1, LANES), F32),
    ]


def _mix_prompt(sinks, x, weights, first_batch, batch):
    seq = x.shape[1]
    tt = MIX_TILE
    n_tiles = seq // tt
    tok = batch * seq
    smem = pl.BlockSpec(memory_space=pltpu.SMEM)
    return pl.pallas_call(
        functools.partial(_mixer_kernel, is_prompt=True, tt=tt),
        grid=(batch, n_tiles),
        in_specs=[smem, pl.BlockSpec((None, tt, D_MODEL), lambda b, i: (b + first_batch, i, 0))]
        + _mixer_weight_specs(tt),
        out_specs=[
            pl.BlockSpec((tt, D_MODEL), lambda b, i: (b * n_tiles + i, 0)),
            pl.BlockSpec((tt, PACK_WIDTH), lambda b, i: (b * n_tiles + i, 0)),
            pl.BlockSpec((2 * TOP_K, tt), lambda b, i: (0, b * n_tiles + i)),
            pl.BlockSpec((tt, TOP_K * LANES), lambda b, i: (b * n_tiles + i, 0)),
            pl.BlockSpec((1, LANES), lambda b, i: (0, 0)),
            pl.BlockSpec((WINDOW, KV_WIDTH), lambda b, i: (b, 0)),
            pl.BlockSpec((WINDOW, KV_WIDTH), lambda b, i: (b, 0)),
        ],
        out_shape=[
            jax.ShapeDtypeStruct((tok, D_MODEL), F32), jax.ShapeDtypeStruct((tok, PACK_WIDTH), jnp.uint32),
            jax.ShapeDtypeStruct((2 * TOP_K, tok), F32), jax.ShapeDtypeStruct((tok, TOP_K * LANES), F32),
            jax.ShapeDtypeStruct((1, LANES), F32),
            jax.ShapeDtypeStruct((batch * WINDOW, KV_WIDTH), F32),
            jax.ShapeDtypeStruct((batch * WINDOW, KV_WIDTH), F32),
        ],
        scratch_shapes=_mixer_scratch(tt, WINDOW + tt),
        compiler_params=pltpu.CompilerParams(
            dimension_semantics=("arbitrary", "arbitrary"), vmem_limit_bytes=VMEM_LIMIT),
        name="mix_prompt",
    )(sinks, x, *weights)


def _mix_sample(sinks, x2, ck, cv, base0, weights):
    tok = x2.shape[0]
    tt = MIX_TILE
    n_chunks = tt // CHUNK
    cache_rows = n_chunks * WINDOW
    smem = pl.BlockSpec(memory_space=pltpu.SMEM)
    row = lambda w: pl.BlockSpec((tt, w), lambda i: (i, 0))
    return pl.pallas_call(
        functools.partial(_mixer_kernel, is_prompt=False, tt=tt),
        grid=(tok // tt,),
        in_specs=[smem, row(D_MODEL),
                  pl.BlockSpec((cache_rows, KV_WIDTH), lambda i: (i, 0)),
                  pl.BlockSpec((cache_rows, KV_WIDTH), lambda i: (i, 0)),
                  _const_spec((1, LANES))] + _mixer_weight_specs(tt),
        out_specs=[row(D_MODEL), row(PACK_WIDTH), pl.BlockSpec((2 * TOP_K, tt), lambda i: (0, i)),
                   row(TOP_K * LANES),
                   pl.BlockSpec((1, LANES), lambda i: (0, 0)),
                   row(KV_WIDTH), row(KV_WIDTH), row(GMLP_WIDTH)],
        out_shape=[
            jax.ShapeDtypeStruct((tok, D_MODEL), F32), jax.ShapeDtypeStruct((tok, PACK_WIDTH), jnp.uint32),
            jax.ShapeDtypeStruct((2 * TOP_K, tok), F32), jax.ShapeDtypeStruct((tok, TOP_K * LANES), F32),
            jax.ShapeDtypeStruct((1, LANES), F32),
            jax.ShapeDtypeStruct((tok, KV_WIDTH), F32), jax.ShapeDtypeStruct((tok, KV_WIDTH), F32),
            jax.ShapeDtypeStruct((tok, GMLP_WIDTH), F32),
        ],
        scratch_shapes=_mixer_scratch(tt, n_chunks * KEYS),
        compiler_params=pltpu.CompilerParams(
            dimension_semantics=("arbitrary",), vmem_limit_bytes=VMEM_LIMIT),
        name="mix_sample",
    )(sinks, x2, ck, cv, base0, *weights)


def _sc_pipeline(body, n_tok, in_specs, out_specs):
    return pltpu.emit_pipeline(
        body, grid=(n_tok // SC_WINDOW,), in_specs=in_specs, out_specs=out_specs,
        core_axis_name=("core", "subcore"), dimension_semantics=(pltpu.PARALLEL,))


def _index_specs():
    return [pl.BlockSpec((1, SC_WINDOW), lambda i: (0, i))] * TOP_K


def _sc_dispatch(token_sets, n_rows):
    win, wid = SC_WINDOW, SLAB_WIDTH
    per_set = 2 + TOP_K
    n_in = per_set * len(token_sets)
    mesh = plsc.VectorSubcoreMesh(core_axis_name="core", subcore_axis_name="subcore")
    out_type = ([jax.ShapeDtypeStruct((n_rows, wid), jnp.uint32)] * PACK_SLABS
                + [jax.ShapeDtypeStruct((n_rows, LANES), F32)])

    @functools.partial(pl.kernel, out_type=out_type, mesh=mesh, scratch_types=[], name="sc_dispatch")
    def run(*refs):
        xs_hbm, gs_hbm = refs[n_in:n_in + PACK_SLABS], refs[n_in + PACK_SLABS]

        for s in range(len(token_sets)):
            h_hbm, g_hbm = refs[per_set * s], refs[per_set * s + 1]
            i_hbm = refs[per_set * s + 2:per_set * (s + 1)]
            n_tok = h_hbm.shape[0]
            for q in range(PACK_SLABS):
                def rows_body(x_vmem, *i_vmem, q=q):
                    for kk in range(TOP_K):
                        pltpu.sync_copy(x_vmem, xs_hbm[q].at[i_vmem[kk].at[0]])

                _sc_pipeline(rows_body, n_tok, [pl.BlockSpec((win, wid), lambda i, q=q: (i, q))] + _index_specs(),
                             [])(h_hbm, *i_hbm)
            for kk in range(TOP_K):
                def gate_body(g_vmem, i_vmem):
                    pltpu.sync_copy(g_vmem, gs_hbm.at[i_vmem.at[0]])

                _sc_pipeline(gate_body, n_tok,
                             [pl.BlockSpec((win, LANES), lambda i, kk=kk: (i, kk)), _index_specs()[0]],
                             [])(g_hbm, i_hbm[kk])

    outs = run(*[a for hw, gates, dests in token_sets for a in (hw, gates, *dests)])
    return outs[:PACK_SLABS], outs[PACK_SLABS]


def _expert_kernel(te_ref, na_ref, par_ref, units_ref, *refs, out_slabs):
    x_refs, (gs_ref, wgu_ref, bgu_ref, wd_ref, bd_ref) = refs[:PACK_SLABS], refs[PACK_SLABS:PACK_SLABS + 5]
    y_refs = refs[PACK_SLABS + 5:PACK_SLABS + 5 + out_slabs]
    wgu_bf, wd_bf = refs[PACK_SLABS + 5 + out_slabs:]
    out_width = PACK_WIDTH // out_slabs
    s = pl.program_id(0)
    t = jnp.maximum(s - 1, 0)
    do_tile = (s >= 1) & (t < na_ref[0])

    def starts_expert(i):
        i = jnp.minimum(i, te_ref.shape[0] - 1)
        return (i == 0) | (te_ref[i] != te_ref[jnp.maximum(i - 1, 0)])

    @pl.when(do_tile & starts_expert(t))
    def _():
        wd_bf[...] = wd_ref[0].astype(BF16)

    def compute(n_rows):
        words = [r[0:n_rows, :] for r in x_refs]
        x = jnp.concatenate(
            [pltpu.unpack_elementwise(w, index=index, packed_dtype=BF16, unpacked_dtype=F32)
             for index in range(2) for w in words], axis=1).astype(BF16)
        hmid = _dot(x, wgu_bf[par_ref[t]]) + bgu_ref[0]
        gate = jnp.minimum(hmid[:, :D_FF], SWIGLU_LIMIT)
        up = jnp.clip(hmid[:, D_FF:], -SWIGLU_LIMIT, SWIGLU_LIMIT)
        act = (up + 1.0) * gate * jax.nn.sigmoid(SWIGLU_ALPHA * gate)
        y = _dot(act.astype(BF16), wd_bf[...]) + bd_ref[0]
        y = y * jnp.concatenate([gs_ref[0:n_rows, :]] * (D_MODEL // LANES), axis=1)
        packed = pltpu.pack_elementwise([y[:, :PACK_WIDTH], y[:, PACK_WIDTH:]], packed_dtype=BF16)
        for q in range(out_slabs):
            y_refs[q][0:n_rows, :] = packed[:, q * out_width:(q + 1) * out_width]

    for units in range(1, ROW_TILE // ROW_UNIT + 1):
        @pl.when(do_tile & (units_ref[t] == units))
        def _(units=units):
            compute(units * ROW_UNIT)

    @pl.when((s < na_ref[0]) & starts_expert(s))
    def _():
        wgu_bf[par_ref[s]] = wgu_ref[0].astype(BF16)


def _experts(tile_e, n_active, units, xs, gs, wgu, bgu, wd, bd, out_width):
    tm = ROW_TILE
    out_slabs = PACK_WIDTH // out_width
    n_tiles = gs.shape[0] // tm
    parity = (jnp.cumsum(jnp.concatenate([jnp.zeros((1,), I32), (tile_e[1:] != tile_e[:-1]).astype(I32)])) % 2
              ).astype(I32)
    tile_of = lambda s, na: jnp.clip(s - 1, 0, na[0] - 1)
    row_map = lambda s, te, na, *_: (tile_of(s, na), 0)
    w_map = lambda s, te, na, *_: (te[tile_of(s, na)], 0, 0)
    wd_map = lambda s, te, na, *_: (0, te[tile_of(s, na)], 0, 0)
    wgu_map = lambda s, te, na, *_: (0, te[jnp.minimum(s, na[0] - 1)], 0, 0)
    return pl.pallas_call(
        functools.partial(_expert_kernel, out_slabs=out_slabs),
        grid_spec=pltpu.PrefetchScalarGridSpec(
            num_scalar_prefetch=4,
            grid=(n_tiles + 1,),
            in_specs=[pl.BlockSpec((tm, SLAB_WIDTH), row_map)] * PACK_SLABS + [
                pl.BlockSpec((tm, LANES), row_map),
                pl.BlockSpec((None, 1, D_MODEL, 2 * D_FF), wgu_map),
                pl.BlockSpec((1, 1, 2 * D_FF), w_map),
                pl.BlockSpec((None, 1, D_FF, D_MODEL), wd_map),
                pl.BlockSpec((1, 1, D_MODEL), w_map),
            ],
            out_specs=[pl.BlockSpec((tm, out_width), row_map)] * out_slabs,
            scratch_shapes=[pltpu.VMEM((2, D_MODEL, 2 * D_FF), BF16), pltpu.VMEM((D_FF, D_MODEL), BF16)],
        ),
        out_shape=[jax.ShapeDtypeStruct((n_tiles * tm, out_width), jnp.uint32)] * out_slabs,
        compiler_params=pltpu.CompilerParams(
            dimension_semantics=("arbitrary",), vmem_limit_bytes=VMEM_LIMIT),
        name="experts",
    )(tile_e, n_active, parity, units, *xs, gs, wgu, bgu, wd, bd)


def _sc_combine(ys, dest_sets):
    n_sets = len(dest_sets)
    toks = [dests[0].shape[1] for dests in dest_sets]
    mesh = plsc.VectorSubcoreMesh(core_axis_name="core", subcore_axis_name="subcore")
    out_type = [jax.ShapeDtypeStruct((t, 2 * COMBINE_WIDTH), F32) for t in toks for _ in range(COMBINE_SLABS)]
    lanes = plsc.get_sparse_core_info().num_lanes
    high = jnp.uint32(0xFFFF0000)

    @functools.partial(pl.kernel, out_type=out_type, mesh=mesh, name="sc_combine",
                       scratch_types=[pltpu.VMEM((SC_WINDOW, COMBINE_WIDTH), jnp.uint32)] * 2,
                       compiler_params=pltpu.CompilerParams(needs_layout_passes=False))
    def run(*refs):
        ys_hbm = refs[:COMBINE_SLABS]
        out0 = COMBINE_SLABS + TOP_K * n_sets
        buf_a, buf_b = refs[-2:]

        def add_pair(ys_q, i_a, i_b, o_vmem, first):
            pltpu.sync_copy(ys_q.at[i_a.at[0]], buf_a)
            pltpu.sync_copy(ys_q.at[i_b.at[0]], buf_b)

            @pl.loop(0, SC_WINDOW)
            def _(r):
                for c in range(COMBINE_WIDTH // lanes):
                    cols = pl.ds(c * lanes, lanes)
                    wa, wb = buf_a[r, cols], buf_b[r, cols]
                    lo = plsc.bitcast(wa << 16, F32) + plsc.bitcast(wb << 16, F32)
                    hi = plsc.bitcast(wa & high, F32) + plsc.bitcast(wb & high, F32)
                    lo_cols, hi_cols = cols, pl.ds(COMBINE_WIDTH + c * lanes, lanes)
                    if first:
                        o_vmem[r, lo_cols] = lo
                        o_vmem[r, hi_cols] = hi
                    else:
                        o_vmem[r, lo_cols] = o_vmem[r, lo_cols] + lo
                        o_vmem[r, hi_cols] = o_vmem[r, hi_cols] + hi

        for s in range(n_sets):
            i_hbm = refs[COMBINE_SLABS + TOP_K * s:COMBINE_SLABS + TOP_K * (s + 1)]
            f_hbm = refs[out0 + COMBINE_SLABS * s:out0 + COMBINE_SLABS * (s + 1)]
            for q in range(COMBINE_SLABS):
                def body(i0, i1, i2, i3, o_vmem, q=q):
                    add_pair(ys_hbm[q], i0, i1, o_vmem, True)
                    add_pair(ys_hbm[q], i2, i3, o_vmem, False)

                _sc_pipeline(body, toks[s], _index_specs(),
                             [pl.BlockSpec((SC_WINDOW, 2 * COMBINE_WIDTH), lambda i: (i, 0))])(*i_hbm, f_hbm[q])

    outs = run(*ys, *[d for dests in dest_sets for d in dests])
    return [outs[COMBINE_SLABS * s:COMBINE_SLABS * (s + 1)] for s in range(n_sets)]


def _sc_gather(ys, dest_sets):
    n_out = TOP_K * PACK_SLABS
    n_sets = len(dest_sets)
    toks = [dests[0].shape[1] for dests in dest_sets]
    mesh = plsc.VectorSubcoreMesh(core_axis_name="core", subcore_axis_name="subcore")
    out_type = [jax.ShapeDtypeStruct((t, SLAB_WIDTH), jnp.uint32) for t in toks for _ in range(n_out)]

    @functools.partial(pl.kernel, out_type=out_type, mesh=mesh, scratch_types=[], name="sc_gather")
    def run(*refs):
        ys_hbm = refs[:PACK_SLABS]
        out0 = PACK_SLABS + TOP_K * n_sets

        for s in range(n_sets):
            i_hbm = refs[PACK_SLABS + TOP_K * s:PACK_SLABS + TOP_K * (s + 1)]
            f_hbm = refs[out0 + n_out * s:out0 + n_out * (s + 1)]
            for kk in range(TOP_K):
                for q in range(PACK_SLABS):
                    def body(i_vmem, o_vmem, q=q):
                        pltpu.sync_copy(ys_hbm[q].at[i_vmem.at[0]], o_vmem)

                    _sc_pipeline(body, toks[s], [_index_specs()[0]],
                                 [pl.BlockSpec((SC_WINDOW, SLAB_WIDTH), lambda i: (i, 0))]
                                 )(i_hbm[kk], f_hbm[kk * PACK_SLABS + q])

    outs = run(*ys, *[d for dests in dest_sets for d in dests])
    return [outs[n_out * s:n_out * (s + 1)] for s in range(n_sets)]


def _final_kernel(*refs, summed):
    n_in = COMBINE_SLABS if summed else TOP_K * PACK_SLABS
    h_ref, f_refs, g_ref, b_ref, out_ref = refs[0], refs[1:1 + n_in], refs[1 + n_in], refs[2 + n_in], refs[-1]
    if summed:
        f = jnp.concatenate([r[:, :COMBINE_WIDTH] for r in f_refs] + [r[:, COMBINE_WIDTH:] for r in f_refs], axis=1)
    else:
        halves = []
        for index in range(2):
            for q in range(PACK_SLABS):
                parts = [pltpu.unpack_elementwise(f_refs[kk * PACK_SLABS + q][...], index=index,
                                                  packed_dtype=BF16, unpacked_dtype=F32) for kk in range(TOP_K)]
                halves.append((parts[0] + parts[1]) + (parts[2] + parts[3]))
        f = jnp.concatenate(halves, axis=1)
    out_ref[...] = _layer_norm(DN_ALPHA * h_ref[...] + f, g_ref[...], b_ref[...])


def _final_norm(h, combined, summed, ln_g, ln_b, out_rows, first_row, earlier=None):
    tt = FINAL_TILE
    tok = h.shape[0]
    first_tile = first_row // tt
    row = lambda w: pl.BlockSpec((tt, w), lambda i: (i, 0))
    in_specs = ([row(D_MODEL)] + [row(c.shape[1]) for c in combined]
                + [_const_spec((1, D_MODEL)), _const_spec((1, D_MODEL))])
    args = [h, *combined, ln_g, ln_b]
    aliases = {}
    if earlier is not None:
        in_specs.append(pl.BlockSpec(memory_space=pl.ANY))
        aliases = {len(args): 0}
        args.append(earlier)
    return pl.pallas_call(
        functools.partial(_final_kernel, summed=summed),
        grid=(tok // tt,),
        in_specs=in_specs,
        out_specs=pl.BlockSpec((tt, D_MODEL), lambda i: (i + first_tile, 0)),
        out_shape=jax.ShapeDtypeStruct((out_rows, D_MODEL), F32),
        input_output_aliases=aliases,
        compiler_params=pltpu.CompilerParams(dimension_semantics=("arbitrary",), vmem_limit_bytes=VMEM_LIMIT),
        name="final_norm",
    )(*args)


def _mixer_weights(w_in, b_in, ln_g, ln_b, w_s, b_s, w_o, b_o, ln1_g, ln1_b, w_router, b_router):
    win = w_in.astype(BF16)
    bin_ = b_in[None, :]
    woa = w_o[:ATTN_WIDTH].astype(BF16)
    wog = w_o[ATTN_WIDTH:].astype(BF16)
    wr = jnp.pad(w_router, ((0, 0), (0, LANES - N_EXPERTS)))
    wrh = wr.astype(BF16)
    wrl = (wr - wrh.astype(F32)).astype(BF16)
    br = jnp.pad(b_router, (0, LANES - N_EXPERTS))[None, :]
    tri = (lax.broadcasted_iota(I32, (MIX_TILE, MIX_TILE), 1)
           < lax.broadcasted_iota(I32, (MIX_TILE, MIX_TILE), 0)).astype(BF16)
    return (win, bin_, ln_g.reshape(1, GMLP_WIDTH), ln_b.reshape(1, GMLP_WIDTH), w_s, b_s.T,
            woa, wog, b_o[None, :], ln1_g[None, :], ln1_b[None, :], wrh, wrl, br, tri)


def _dest_rows(meta, pstart):
    e = meta[META_E:META_E + TOP_K].astype(I32)
    r = meta[META_R:META_R + TOP_K].astype(I32)
    hit = e[None] == jnp.arange(N_EXPERTS, dtype=I32)[:, None, None]
    return jnp.sum(jnp.where(hit, pstart[:, None, None], 0), axis=0) + r


def _moe(token_sets, cnt, experts, sum_on_sc):
    n_assign = sum(hw.shape[0] for hw, _, _ in token_sets) * TOP_K
    n_tiles = (n_assign + N_EXPERTS * (ROW_TILE - 1)) // ROW_TILE
    counts = cnt[0, :N_EXPERTS].astype(I32)
    padded = (counts + ROW_TILE - 1) // ROW_TILE * ROW_TILE
    pend = jnp.cumsum(padded)
    pstart = pend - padded
    n_active = (pend[-1:] // ROW_TILE).astype(I32)
    tile_start = jnp.arange(n_tiles, dtype=I32) * ROW_TILE
    tile_e = jnp.minimum(jnp.sum(pend[None, :] <= tile_start[:, None], axis=1), N_EXPERTS - 1).astype(I32)
    hit = tile_e[:, None] == jnp.arange(N_EXPERTS, dtype=I32)[None, :]
    valid_end = jnp.sum(jnp.where(hit, pstart + counts, 0), axis=1)
    units = jnp.clip((valid_end - tile_start + ROW_UNIT - 1) // ROW_UNIT, 1, ROW_TILE // ROW_UNIT).astype(I32)

    lists = lambda d: [d[kk][None, :] for kk in range(TOP_K)]
    dest_sets = [lists(_dest_rows(meta, pstart)) for _, _, meta in token_sets]
    xs, gs = _sc_dispatch([(hw, gates, dests) for (hw, gates, _), dests in zip(token_sets, dest_sets)],
                          n_tiles * ROW_TILE)
    if sum_on_sc:
        return _sc_combine(_experts(tile_e, n_active, units, xs, gs, *experts, COMBINE_WIDTH), dest_sets)
    return _sc_gather(_experts(tile_e, n_active, units, xs, gs, *experts, SLAB_WIDTH), dest_sets)


def kernel(x_prompt, x_sample, cache_k, cache_v, w_in, b_in, attn_sinks, gmlp_ln_g, gmlp_ln_b, w_spatial, b_spatial, w_o, b_o, ln1_g, ln1_b, w_router, b_router, w_gate_up, b_gate_up, w_down, b_down, ln2_g, ln2_b):
    assert w_in.shape[0] == DEPTH
    batch, seq, _ = x_prompt.shape
    dec_batch, dec_seq, _ = x_sample.shape
    tok_p, tok_s = batch * seq, dec_batch * dec_seq
    assert dec_seq == CHUNK and seq % MIX_TILE == 0 and batch >= 2
    assert all(t % MIX_TILE == 0 and t % FINAL_TILE == 0 and t % SC_WINDOW == 0 for t in (tok_p, tok_s))

    weights = _mixer_weights(w_in[0], b_in[0], gmlp_ln_g[0], gmlp_ln_b[0], w_spatial[0], b_spatial[0],
                             w_o[0], b_o[0], ln1_g[0], ln1_b[0], w_router[0], b_router[0])
    sinks = attn_sinks[0]
    experts = (w_gate_up, b_gate_up[0][:, None, :], w_down, b_down[0][:, None, :])

    batch_a = batch // 2
    h_a, hw_a, meta_a, gates_a, cnt_a, kt_a, vt_a = _mix_prompt(sinks, x_prompt, weights, 0, batch_a)
    summed_a, = _moe([(hw_a, gates_a, meta_a)], cnt_a, experts, sum_on_sc=True)
    h_b, hw_b, meta_b, gates_b, cnt_b, kt_b, vt_b = _mix_prompt(sinks, x_prompt, weights, batch_a, batch - batch_a)
    h_s, hw_s, meta_s, gates_s, cnt_bs, ks, vs, gvs = _mix_sample(
        sinks, x_sample.reshape(tok_s, D_MODEL),
        cache_k[0].reshape(dec_batch * WINDOW, KV_WIDTH), cache_v[0].reshape(dec_batch * WINDOW, KV_WIDTH),
        cnt_b, weights)
    picked_b, picked_s = _moe([(hw_b, gates_b, meta_b), (hw_s, gates_s, meta_s)], cnt_bs, experts, sum_on_sc=False)

    g2, b2 = ln2_g[0][None, :], ln2_b[0][None, :]
    y_p = _final_norm(h_a, summed_a, True, g2, b2, tok_p, 0)
    y_p = _final_norm(h_b, picked_b, False, g2, b2, tok_p, batch_a * seq, earlier=y_p).reshape(batch, seq, D_MODEL)
    y_s = _final_norm(h_s, picked_s, False, g2, b2, tok_s, 0).reshape(dec_batch, dec_seq, D_MODEL)
    kt, vt = jnp.concatenate([kt_a, kt_b]), jnp.concatenate([vt_a, vt_b])

    kv5 = lambda a, nb, rows: a.reshape(DEPTH, nb, rows, N_KV_HEADS, HEAD_DIM)
    return (y_p, y_s, kv5(kt, batch, WINDOW), kv5(vt, batch, WINDOW),
            kv5(ks, dec_batch, dec_seq), kv5(vs, dec_batch, dec_seq),
            gvs.reshape(DEPTH, dec_batch, dec_seq, GMLP_GROUPS, GMLP_GROUP_DIM))
```

```python
import functools

import jax
import jax.numpy as jnp
from jax import lax
from jax.experimental import pallas as pl
from jax.experimental.pallas import tpu as pltpu
from jax.experimental.pallas import tpu_sc as plsc

F32 = jnp.float32
BF16 = jnp.bfloat16
I32 = jnp.int32

D_MODEL = 1024
CHUNK = 64
N_HEADS = 8
N_KV_HEADS = 2
HEAD_DIM = 64
Q_GROUP = N_HEADS // N_KV_HEADS
KV_WIDTH = N_KV_HEADS * HEAD_DIM
WINDOW = 128
KEYS = WINDOW + CHUNK
GMLP_GROUPS = 4
GMLP_GROUP_DIM = 128
GMLP_WIDTH = GMLP_GROUPS * GMLP_GROUP_DIM
GMLP_CHUNK = 128
ATTN_WIDTH = N_HEADS * HEAD_DIM
N_EXPERTS = 32
TOP_K = 4
D_FF = 1024
SWIGLU_LIMIT = 7.0
SWIGLU_ALPHA = 1.702
DEPTH = 1
DN_ALPHA = (2 * DEPTH) ** 0.25
LN_EPS = 1e-5
NEG_INF = -1e30

LANES = 128
K_OFF = ATTN_WIDTH
V_OFF = K_OFF + KV_WIDTH
U_OFF = V_OFF + KV_WIDTH
GV_OFF = U_OFF + GMLP_WIDTH
Z_WIDTH = GV_OFF + GMLP_WIDTH

MIX_TILE = 512
ROW_TILE = 768
ROW_UNIT = 128
FINAL_TILE = 1024
SC_WINDOW = 128
SC_SLABS = 4
SLAB_WIDTH = D_MODEL // SC_SLABS
PACK_WIDTH = D_MODEL // 2
PACK_SLABS = PACK_WIDTH // SLAB_WIDTH
COMBINE_WIDTH = LANES
COMBINE_SLABS = PACK_WIDTH // COMBINE_WIDTH
VMEM_LIMIT = 58 * 1024 * 1024

META_E, META_R = 0, 4


def _gelu_tanh(x):
    return 0.5 * x * (1.0 + jnp.tanh(0.7978845608028654 * (x + 0.044715 * x * x * x)))


def _layer_norm(x, g, b):
    mu = jnp.mean(x, axis=-1, keepdims=True)
    xc = x - mu
    var = jnp.mean(xc * xc, axis=-1, keepdims=True)
    return xc * lax.rsqrt(var + LN_EPS) * g + b


def _dot(a, b):
    return jnp.dot(a, b, preferred_element_type=F32)


def _mixer_kernel(*refs, is_prompt, tt):
    n_chunks = tt // CHUNK
    it = iter(refs)
    sinks_ref = next(it)
    x_ref = next(it)
    if not is_prompt:
        ck_ref, cv_ref, base0_ref = next(it), next(it), next(it)
    (win_ref, bin_ref, lng_ref, lnb_ref, ws_ref, bs_ref, woa_ref, wog_ref, bo_ref, l1g_ref, l1b_ref,
     wrh_ref, wrl_ref, br_ref, tri_ref) = (next(it) for _ in range(15))
    h_ref, hw_ref, meta_ref, gate_ref, cnt_ref = (next(it) for _ in range(5))
    if is_prompt:
        kt_ref, vt_ref = next(it), next(it)
    else:
        ko_ref, vo_ref, gvo_ref = next(it), next(it), next(it)
    z_ref, kext_ref, vext_ref, att_ref, gm_ref, base_ref = (next(it) for _ in range(6))

    if is_prompt:
        first = (pl.program_id(0) == 0) & (pl.program_id(1) == 0)
        tile_in_seq = pl.program_id(1)

        @pl.when(first)
        def _():
            base_ref[...] = jnp.zeros_like(base_ref)

        @pl.when(tile_in_seq == 0)
        def _():
            kext_ref[:, 0:WINDOW, :] = jnp.zeros((N_KV_HEADS, WINDOW, KV_WIDTH), BF16)
            vext_ref[:, 0:WINDOW, :] = jnp.zeros((N_KV_HEADS, WINDOW, KV_WIDTH), BF16)
    else:
        @pl.when(pl.program_id(0) == 0)
        def _():
            base_ref[...] = base0_ref[...]

    x = x_ref[...]
    z_ref[...] = _dot(x.astype(BF16), win_ref[...]) + bin_ref[...]

    def put_kv(ext_ref, row0, rows_f32):
        n = rows_f32.shape[0]
        swapped = pltpu.roll(rows_f32, HEAD_DIM, axis=1)
        low = lax.broadcasted_iota(I32, (n, KV_WIDTH), 1) < HEAD_DIM
        ext_ref[0, row0:row0 + n, :] = jnp.where(low, rows_f32, swapped).astype(BF16)
        ext_ref[1, row0:row0 + n, :] = jnp.where(low, swapped, rows_f32).astype(BF16)

    k = z_ref[:, K_OFF:K_OFF + KV_WIDTH]
    v = z_ref[:, V_OFF:V_OFF + KV_WIDTH]
    if is_prompt:
        put_kv(kext_ref, WINDOW, k)
        put_kv(vext_ref, WINDOW, v)
        kt_ref[...] = k[tt - WINDOW:, :]
        vt_ref[...] = v[tt - WINDOW:, :]
        key_stride = CHUNK
    else:
        ko_ref[...] = k
        vo_ref[...] = v
        for c in range(n_chunks):
            put_kv(kext_ref, KEYS * c, ck_ref[WINDOW * c:WINDOW * (c + 1), :])
            put_kv(vext_ref, KEYS * c, cv_ref[WINDOW * c:WINDOW * (c + 1), :])
            put_kv(kext_ref, KEYS * c + WINDOW, k[CHUNK * c:CHUNK * (c + 1), :])
            put_kv(vext_ref, KEYS * c + WINDOW, v[CHUNK * c:CHUNK * (c + 1), :])
        key_stride = KEYS

    rows = Q_GROUP * CHUNK
    row_i = lax.broadcasted_iota(I32, (rows, 1), 0)
    key_i = lax.broadcasted_iota(I32, (rows, KEYS), 1)
    sink_cols = []
    for hk in range(N_KV_HEADS):
        s = [sinks_ref[hk * Q_GROUP + g] for g in range(Q_GROUP)]
        sink_cols.append(jnp.where(row_i < CHUNK, s[0],
                                   jnp.where(row_i < 2 * CHUNK, s[1],
                                             jnp.where(row_i < 3 * CHUNK, s[2], s[3]))))
    low_half = lax.broadcasted_iota(I32, (CHUNK, LANES), 1) < HEAD_DIM
    for c in range(n_chunks):
        outs = []
        for hk in range(N_KV_HEADS):
            kc = kext_ref[hk, key_stride * c:key_stride * c + KEYS, :]
            vc = vext_ref[hk, key_stride * c:key_stride * c + KEYS, :]
            heads = []
            for pair in range(hk * Q_GROUP // 2, (hk + 1) * Q_GROUP // 2):
                both = z_ref[CHUNK * c:CHUNK * (c + 1), pair * LANES:(pair + 1) * LANES] * (HEAD_DIM ** -0.5)
                heads += [jnp.where(low_half, both, 0.0), jnp.where(low_half, 0.0, both)]
            qs = jnp.concatenate(heads, axis=0).astype(BF16)
            sc = lax.dot_general(qs, kc, (((1,), (1,)), ((), ())), preferred_element_type=F32)
            if is_prompt and c < WINDOW // CHUNK:
                valid = (key_i >= WINDOW - CHUNK * c) | (tile_in_seq > 0)
                sc = jnp.where(valid, sc, NEG_INF)
            sink = sink_cols[hk]
            m = jnp.maximum(jnp.max(sc, axis=-1, keepdims=True), sink)
            p = jnp.exp(sc - m)
            den = jnp.sum(p, axis=-1, keepdims=True) + jnp.exp(sink - m)
            o = _dot(p.astype(BF16), vc) / den
            for j in range(Q_GROUP // 2):
                even, odd = o[CHUNK * 2 * j:CHUNK * (2 * j + 1), :], o[CHUNK * (2 * j + 1):CHUNK * (2 * j + 2), :]
                outs.append(jnp.where(low_half, even, odd))
        att_ref[CHUNK * c:CHUNK * (c + 1), :] = jnp.concatenate(outs, axis=1).astype(BF16)

    if is_prompt:
        kext_ref[:, 0:WINDOW, :] = kext_ref[:, tt:tt + WINDOW, :]
        vext_ref[:, 0:WINDOW, :] = vext_ref[:, tt:tt + WINDOW, :]

    gc = GMLP_CHUNK if is_prompt else CHUNK
    ri = lax.broadcasted_iota(I32, (gc, gc), 0)
    ci = lax.broadcasted_iota(I32, (gc, gc), 1)
    causal = (ci // CHUNK) <= (ri // CHUNK)
    for g in range(GMLP_GROUPS):
        lo, hi = g * GMLP_GROUP_DIM, (g + 1) * GMLP_GROUP_DIM
        u = _gelu_tanh(z_ref[:, U_OFF + lo:U_OFF + hi])
        gv = _layer_norm(_gelu_tanh(z_ref[:, GV_OFF + lo:GV_OFF + hi]), lng_ref[:, lo:hi], lnb_ref[:, lo:hi])
        if not is_prompt:
            gvo_ref[:, lo:hi] = gv
        gvb = gv.astype(BF16)
        wm = jnp.where(causal, ws_ref[g, 0:gc, 0:gc], 0.0).astype(BF16)
        bcol = bs_ref[0:gc, g:g + 1]
        for n in range(tt // gc):
            sp = _dot(wm, gvb[gc * n:gc * (n + 1), :]) + bcol
            gm_ref[gc * n:gc * (n + 1), lo:hi] = (u[gc * n:gc * (n + 1), :] * sp).astype(BF16)

    y = _dot(att_ref[...], woa_ref[...]) + _dot(gm_ref[...], wog_ref[...]) + bo_ref[...]
    h = _layer_norm(DN_ALPHA * x + y, l1g_ref[...], l1b_ref[...])
    h_ref[...] = h
    hw_ref[...] = pltpu.pack_elementwise([h[:, :PACK_WIDTH], h[:, PACK_WIDTH:]], packed_dtype=BF16)

    h_hi = h.astype(BF16)
    h_lo = (h - h_hi.astype(F32)).astype(BF16)
    logits = _dot(h_hi, wrh_ref[...]) + _dot(h_lo, wrh_ref[...]) + _dot(h_hi, wrl_ref[...]) + br_ref[...]
    lane = lax.broadcasted_iota(I32, (tt, LANES), 1)
    lane_f = lane.astype(F32)
    l = jnp.where(lane < N_EXPERTS, logits, -jnp.inf)
    tops, idxs, hots = [], [], []
    for _ in range(TOP_K):
        m = jnp.max(l, axis=-1, keepdims=True)
        idx = jnp.min(jnp.where(l == m, lane_f, float(LANES)), axis=-1, keepdims=True)
        hot = lane_f == idx
        l = jnp.where(hot, -jnp.inf, l)
        tops.append(m)
        idxs.append(idx)
        hots.append(hot)
    es = [jnp.exp(t - tops[0]) for t in tops]
    esum = es[0] + es[1] + es[2] + es[3]
    chosen = jnp.where(hots[0] | hots[1] | hots[2] | hots[3], 1.0, 0.0)
    before = _dot(tri_ref[...], chosen.astype(BF16)) + base_ref[...]
    meta = jnp.zeros((tt, LANES), F32)
    for kk in range(TOP_K):
        rank = jnp.sum(jnp.where(hots[kk], before, 0.0), axis=-1, keepdims=True)
        meta = jnp.where(lane == META_E + kk, idxs[kk], meta)
        meta = jnp.where(lane == META_R + kk, rank, meta)
        gate_ref[:, kk * LANES:(kk + 1) * LANES] = jnp.broadcast_to(es[kk] / esum, (tt, LANES))
    meta_ref[...] = jnp.transpose(meta)[0:2 * TOP_K, :]
    base_ref[...] = base_ref[...] + jnp.sum(chosen, axis=0, keepdims=True)
    cnt_ref[...] = base_ref[...]


def _const_spec(shape):
    nd = len(shape)
    return pl.BlockSpec(shape, lambda *_: (0,) * nd, pipeline_mode=pl.Buffered(1))


def _mixer_weight_specs(tt):
    return [
        _const_spec((D_MODEL, Z_WIDTH)), _const_spec((1, Z_WIDTH)),
        _const_spec((1, GMLP_WIDTH)), _const_spec((1, GMLP_WIDTH)),
        _const_spec((GMLP_GROUPS, GMLP_CHUNK, GMLP_CHUNK)), _const_spec((GMLP_CHUNK, GMLP_GROUPS)),
        _const_spec((ATTN_WIDTH, D_MODEL)), _const_spec((GMLP_WIDTH, D_MODEL)), _const_spec((1, D_MODEL)),
        _const_spec((1, D_MODEL)), _const_spec((1, D_MODEL)),
        _const_spec((D_MODEL, LANES)), _const_spec((D_MODEL, LANES)), _const_spec((1, LANES)),
        _const_spec((tt, tt)),
    ]


def _mixer_scratch(tt, kext_rows):
    return [
        pltpu.VMEM((tt, Z_WIDTH), F32),
        pltpu.VMEM((N_KV_HEADS, kext_rows, KV_WIDTH), BF16), pltpu.VMEM((N_KV_HEADS, kext_rows, KV_WIDTH), BF16),
        pltpu.VMEM((tt, ATTN_WIDTH), BF16), pltpu.VMEM((tt, GMLP_WIDTH), BF16),
        pltpu.VMEM((1, LANES), F32),
    ]


def _mix_prompt(sinks, x, weights, first_batch, batch):
    seq = x.shape[1]
    tt = MIX_TILE
    n_tiles = seq // tt
    tok = batch * seq
    smem = pl.BlockSpec(memory_space=pltpu.SMEM)
    return pl.pallas_call(
        functools.partial(_mixer_kernel, is_prompt=True, tt=tt),
        grid=(batch, n_tiles),
        in_specs=[smem, pl.BlockSpec((None, tt, D_MODEL), lambda b, i: (b + first_batch, i, 0))]
        + _mixer_weight_specs(tt),
        out_specs=[
            pl.BlockSpec((tt, D_MODEL), lambda b, i: (b * n_tiles + i, 0)),
            pl.BlockSpec((tt, PACK_WIDTH), lambda b, i: (b * n_tiles + i, 0)),
            pl.BlockSpec((2 * TOP_K, tt), lambda b, i: (0, b * n_tiles + i)),
            pl.BlockSpec((tt, TOP_K * LANES), lambda b, i: (b * n_tiles + i, 0)),
            pl.BlockSpec((1, LANES), lambda b, i: (0, 0)),
            pl.BlockSpec((WINDOW, KV_WIDTH), lambda b, i: (b, 0)),
            pl.BlockSpec((WINDOW, KV_WIDTH), lambda b, i: (b, 0)),
        ],
        out_shape=[
            jax.ShapeDtypeStruct((tok, D_MODEL), F32), jax.ShapeDtypeStruct((tok, PACK_WIDTH), jnp.uint32),
            jax.ShapeDtypeStruct((2 * TOP_K, tok), F32), jax.ShapeDtypeStruct((tok, TOP_K * LANES), F32),
            jax.ShapeDtypeStruct((1, LANES), F32),
            jax.ShapeDtypeStruct((batch * WINDOW, KV_WIDTH), F32),
            jax.ShapeDtypeStruct((batch * WINDOW, KV_WIDTH), F32),
        ],
        scratch_shapes=_mixer_scratch(tt, WINDOW + tt),
        compiler_params=pltpu.CompilerParams(
            dimension_semantics=("arbitrary", "arbitrary"), vmem_limit_bytes=VMEM_LIMIT),
        name="mix_prompt",
    )(sinks, x, *weights)


def _mix_sample(sinks, x2, ck, cv, base0, weights):
    tok = x2.shape[0]
    tt = MIX_TILE
    n_chunks = tt // CHUNK
    cache_rows = n_chunks * WINDOW
    smem = pl.BlockSpec(memory_space=pltpu.SMEM)
    row = lambda w: pl.BlockSpec((tt, w), lambda i: (i, 0))
    return pl.pallas_call(
        functools.partial(_mixer_kernel, is_prompt=False, tt=tt),
        grid=(tok // tt,),
        in_specs=[smem, row(D_MODEL),
                  pl.BlockSpec((cache_rows, KV_WIDTH), lambda i: (i, 0)),
                  pl.BlockSpec((cache_rows, KV_WIDTH), lambda i: (i, 0)),
                  _const_spec((1, LANES))] + _mixer_weight_specs(tt),
        out_specs=[row(D_MODEL), row(PACK_WIDTH), pl.BlockSpec((2 * TOP_K, tt), lambda i: (0, i)),
                   row(TOP_K * LANES),
                   pl.BlockSpec((1, LANES), lambda i: (0, 0)),
                   row(KV_WIDTH), row(KV_WIDTH), row(GMLP_WIDTH)],
        out_shape=[
            jax.ShapeDtypeStruct((tok, D_MODEL), F32), jax.ShapeDtypeStruct((tok, PACK_WIDTH), jnp.uint32),
            jax.ShapeDtypeStruct((2 * TOP_K, tok), F32), jax.ShapeDtypeStruct((tok, TOP_K * LANES), F32),
            jax.ShapeDtypeStruct((1, LANES), F32),
            jax.ShapeDtypeStruct((tok, KV_WIDTH), F32), jax.ShapeDtypeStruct((tok, KV_WIDTH), F32),
            jax.ShapeDtypeStruct((tok, GMLP_WIDTH), F32),
        ],
        scratch_shapes=_mixer_scratch(tt, n_chunks * KEYS),
        compiler_params=pltpu.CompilerParams(
            dimension_semantics=("arbitrary",), vmem_limit_bytes=VMEM_LIMIT),
        name="mix_sample",
    )(sinks, x2, ck, cv, base0, *weights)


def _sc_pipeline(body, n_tok, in_specs, out_specs):
    return pltpu.emit_pipeline(
        body, grid=(n_tok // SC_WINDOW,), in_specs=in_specs, out_specs=out_specs,
        core_axis_name=("core", "subcore"), dimension_semantics=(pltpu.PARALLEL,))


def _index_specs():
    return [pl.BlockSpec((1, SC_WINDOW), lambda i: (0, i))] * TOP_K


def _sc_dispatch(token_sets, n_rows):
    win, wid = SC_WINDOW, SLAB_WIDTH
    per_set = 2 + TOP_K
    n_in = per_set * len(token_sets)
    mesh = plsc.VectorSubcoreMesh(core_axis_name="core", subcore_axis_name="subcore")
    out_type = ([jax.ShapeDtypeStruct((n_rows, wid), jnp.uint32)] * PACK_SLABS
                + [jax.ShapeDtypeStruct((n_rows, LANES), F32)])

    @functools.partial(pl.kernel, out_type=out_type, mesh=mesh, scratch_types=[], name="sc_dispatch")
    def run(*refs):
        xs_hbm, gs_hbm = refs[n_in:n_in + PACK_SLABS], refs[n_in + PACK_SLABS]

        for s in range(len(token_sets)):
            h_hbm, g_hbm = refs[per_set * s], refs[per_set * s + 1]
            i_hbm = refs[per_set * s + 2:per_set * (s + 1)]
            n_tok = h_hbm.shape[0]
            for q in range(PACK_SLABS):
                def rows_body(x_vmem, *i_vmem, q=q):
                    for kk in range(TOP_K):
                        pltpu.sync_copy(x_vmem, xs_hbm[q].at[i_vmem[kk].at[0]])

                _sc_pipeline(rows_body, n_tok, [pl.BlockSpec((win, wid), lambda i, q=q: (i, q))] + _index_specs(),
                             [])(h_hbm, *i_hbm)
            for kk in range(TOP_K):
                def gate_body(g_vmem, i_vmem):
                    pltpu.sync_copy(g_vmem, gs_hbm.at[i_vmem.at[0]])

                _sc_pipeline(gate_body, n_tok,
                             [pl.BlockSpec((win, LANES), lambda i, kk=kk: (i, kk)), _index_specs()[0]],
                             [])(g_hbm, i_hbm[kk])

    outs = run(*[a for hw, gates, dests in token_sets for a in (hw, gates, *dests)])
    return outs[:PACK_SLABS], outs[PACK_SLABS]


def _expert_kernel(te_ref, na_ref, par_ref, units_ref, *refs, out_slabs):
    x_refs, (gs_ref, wgu_ref, bgu_ref, wd_ref, bd_ref) = refs[:PACK_SLABS], refs[PACK_SLABS:PACK_SLABS + 5]
    y_refs = refs[PACK_SLABS + 5:PACK_SLABS + 5 + out_slabs]
    wgu_bf, wd_bf = refs[PACK_SLABS + 5 + out_slabs:]
    out_width = PACK_WIDTH // out_slabs
    s = pl.program_id(0)
    t = jnp.maximum(s - 1, 0)
    do_tile = (s >= 1) & (t < na_ref[0])

    def starts_expert(i):
        i = jnp.minimum(i, te_ref.shape[0] - 1)
        return (i == 0) | (te_ref[i] != te_ref[jnp.maximum(i - 1, 0)])

    @pl.when(do_tile & starts_expert(t))
    def _():
        wd_bf[...] = wd_ref[0].astype(BF16)

    def compute(n_rows):
        words = [r[0:n_rows, :] for r in x_refs]
        x = jnp.concatenate(
            [pltpu.unpack_elementwise(w, index=index, packed_dtype=BF16, unpacked_dtype=F32)
             for index in range(2) for w in words], axis=1).astype(BF16)
        hmid = _dot(x, wgu_bf[par_ref[t]]) + bgu_ref[0]
        gate = jnp.minimum(hmid[:, :D_FF], SWIGLU_LIMIT)
        up = jnp.clip(hmid[:, D_FF:], -SWIGLU_LIMIT, SWIGLU_LIMIT)
        act = (up + 1.0) * gate * jax.nn.sigmoid(SWIGLU_ALPHA * gate)
        y = _dot(act.astype(BF16), wd_bf[...]) + bd_ref[0]
        y = y * jnp.concatenate([gs_ref[0:n_rows, :]] * (D_MODEL // LANES), axis=1)
        packed = pltpu.pack_elementwise([y[:, :PACK_WIDTH], y[:, PACK_WIDTH:]], packed_dtype=BF16)
        for q in range(out_slabs):
            y_refs[q][0:n_rows, :] = packed[:, q * out_width:(q + 1) * out_width]

    for units in range(1, ROW_TILE // ROW_UNIT + 1):
        @pl.when(do_tile & (units_ref[t] == units))
        def _(units=units):
            compute(units * ROW_UNIT)

    @pl.when((s < na_ref[0]) & starts_expert(s))
    def _():
        wgu_bf[par_ref[s]] = wgu_ref[0].astype(BF16)


def _experts(tile_e, n_active, units, xs, gs, wgu, bgu, wd, bd, out_width):
    tm = ROW_TILE
    out_slabs = PACK_WIDTH // out_width
    n_tiles = gs.shape[0] // tm
    parity = (jnp.cumsum(jnp.concatenate([jnp.zeros((1,), I32), (tile_e[1:] != tile_e[:-1]).astype(I32)])) % 2
              ).astype(I32)
    tile_of = lambda s, na: jnp.clip(s - 1, 0, na[0] - 1)
    row_map = lambda s, te, na, *_: (tile_of(s, na), 0)
    w_map = lambda s, te, na, *_: (te[tile_of(s, na)], 0, 0)
    wd_map = lambda s, te, na, *_: (0, te[tile_of(s, na)], 0, 0)
    wgu_map = lambda s, te, na, *_: (0, te[jnp.minimum(s, na[0] - 1)], 0, 0)
    return pl.pallas_call(
        functools.partial(_expert_kernel, out_slabs=out_slabs),
        grid_spec=pltpu.PrefetchScalarGridSpec(
            num_scalar_prefetch=4,
            grid=(n_tiles + 1,),
            in_specs=[pl.BlockSpec((tm, SLAB_WIDTH), row_map)] * PACK_SLABS + [
                pl.BlockSpec((tm, LANES), row_map),
                pl.BlockSpec((None, 1, D_MODEL, 2 * D_FF), wgu_map),
                pl.BlockSpec((1, 1, 2 * D_FF), w_map),
                pl.BlockSpec((None, 1, D_FF, D_MODEL), wd_map),
                pl.BlockSpec((1, 1, D_MODEL), w_map),
            ],
            out_specs=[pl.BlockSpec((tm, out_width), row_map)] * out_slabs,
            scratch_shapes=[pltpu.VMEM((2, D_MODEL, 2 * D_FF), BF16), pltpu.VMEM((D_FF, D_MODEL), BF16)],
        ),
        out_shape=[jax.ShapeDtypeStruct((n_tiles * tm, out_width), jnp.uint32)] * out_slabs,
        compiler_params=pltpu.CompilerParams(
            dimension_semantics=("arbitrary",), vmem_limit_bytes=VMEM_LIMIT),
        name="experts",
    )(tile_e, n_active, parity, units, *xs, gs, wgu, bgu, wd, bd)


def _sc_combine(ys, dest_sets):
    n_sets = len(dest_sets)
    toks = [dests[0].shape[1] for dests in dest_sets]
    mesh = plsc.VectorSubcoreMesh(core_axis_name="core", subcore_axis_name="subcore")
    out_type = [jax.ShapeDtypeStruct((t, 2 * COMBINE_WIDTH), F32) for t in toks for _ in range(COMBINE_SLABS)]
    lanes = plsc.get_sparse_core_info().num_lanes
    high = jnp.uint32(0xFFFF0000)

    @functools.partial(pl.kernel, out_type=out_type, mesh=mesh, name="sc_combine",
                       scratch_types=[pltpu.VMEM((SC_WINDOW, COMBINE_WIDTH), jnp.uint32)] * 2,
                       compiler_params=pltpu.CompilerParams(needs_layout_passes=False))
    def run(*refs):
        ys_hbm = refs[:COMBINE_SLABS]
        out0 = COMBINE_SLABS + TOP_K * n_sets
        buf_a, buf_b = refs[-2:]

        def add_pair(ys_q, i_a, i_b, o_vmem, first):
            pltpu.sync_copy(ys_q.at[i_a.at[0]], buf_a)
            pltpu.sync_copy(ys_q.at[i_b.at[0]], buf_b)

            @pl.loop(0, SC_WINDOW)
            def _(r):
                for c in range(COMBINE_WIDTH // lanes):
                    cols = pl.ds(c * lanes, lanes)
                    wa, wb = buf_a[r, cols], buf_b[r, cols]
                    lo = plsc.bitcast(wa << 16, F32) + plsc.bitcast(wb << 16, F32)
                    hi = plsc.bitcast(wa & high, F32) + plsc.bitcast(wb & high, F32)
                    lo_cols, hi_cols = cols, pl.ds(COMBINE_WIDTH + c * lanes, lanes)
                    if first:
                        o_vmem[r, lo_cols] = lo
                        o_vmem[r, hi_cols] = hi
                    else:
                        o_vmem[r, lo_cols] = o_vmem[r, lo_cols] + lo
                        o_vmem[r, hi_cols] = o_vmem[r, hi_cols] + hi

        for s in range(n_sets):
            i_hbm = refs[COMBINE_SLABS + TOP_K * s:COMBINE_SLABS + TOP_K * (s + 1)]
            f_hbm = refs[out0 + COMBINE_SLABS * s:out0 + COMBINE_SLABS * (s + 1)]
            for q in range(COMBINE_SLABS):
                def body(i0, i1, i2, i3, o_vmem, q=q):
                    add_pair(ys_hbm[q], i0, i1, o_vmem, True)
                    add_pair(ys_hbm[q], i2, i3, o_vmem, False)

                _sc_pipeline(body, toks[s], _index_specs(),
                             [pl.BlockSpec((SC_WINDOW, 2 * COMBINE_WIDTH), lambda i: (i, 0))])(*i_hbm, f_hbm[q])

    outs = run(*ys, *[d for dests in dest_sets for d in dests])
    return [outs[COMBINE_SLABS * s:COMBINE_SLABS * (s + 1)] for s in range(n_sets)]


def _sc_gather(ys, dest_sets):
    n_out = TOP_K * PACK_SLABS
    n_sets = len(dest_sets)
    toks = [dests[0].shape[1] for dests in dest_sets]
    mesh = plsc.VectorSubcoreMesh(core_axis_name="core", subcore_axis_name="subcore")
    out_type = [jax.ShapeDtypeStruct((t, SLAB_WIDTH), jnp.uint32) for t in toks for _ in range(n_out)]

    @functools.partial(pl.kernel, out_type=out_type, mesh=mesh, scratch_types=[], name="sc_gather")
    def run(*refs):
        ys_hbm = refs[:PACK_SLABS]
        out0 = PACK_SLABS + TOP_K * n_sets

        for s in range(n_sets):
            i_hbm = refs[PACK_SLABS + TOP_K * s:PACK_SLABS + TOP_K * (s + 1)]
            f_hbm = refs[out0 + n_out * s:out0 + n_out * (s + 1)]
            for kk in range(TOP_K):
                for q in range(PACK_SLABS):
                    def body(i_vmem, o_vmem, q=q):
                        pltpu.sync_copy(ys_hbm[q].at[i_vmem.at[0]], o_vmem)

                    _sc_pipeline(body, toks[s], [_index_specs()[0]],
                                 [pl.BlockSpec((SC_WINDOW, SLAB_WIDTH), lambda i: (i, 0))]
                                 )(i_hbm[kk], f_hbm[kk * PACK_SLABS + q])

    outs = run(*ys, *[d for dests in dest_sets for d in dests])
    return [outs[n_out * s:n_out * (s + 1)] for s in range(n_sets)]


def _final_kernel(*refs, summed):
    n_in = COMBINE_SLABS if summed else TOP_K * PACK_SLABS
    h_ref, f_refs, g_ref, b_ref, out_ref = refs[0], refs[1:1 + n_in], refs[1 + n_in], refs[2 + n_in], refs[-1]
    if summed:
        f = jnp.concatenate([r[:, :COMBINE_WIDTH] for r in f_refs] + [r[:, COMBINE_WIDTH:] for r in f_refs], axis=1)
    else:
        halves = []
        for index in range(2):
            for q in range(PACK_SLABS):
                parts = [pltpu.unpack_elementwise(f_refs[kk * PACK_SLABS + q][...], index=index,
                                                  packed_dtype=BF16, unpacked_dtype=F32) for kk in range(TOP_K)]
                halves.append((parts[0] + parts[1]) + (parts[2] + parts[3]))
        f = jnp.concatenate(halves, axis=1)
    out_ref[...] = _layer_norm(DN_ALPHA * h_ref[...] + f, g_ref[...], b_ref[...])


def _final_norm(h, combined, summed, ln_g, ln_b, out_rows, first_row, earlier=None):
    tt = FINAL_TILE
    tok = h.shape[0]
    first_tile = first_row // tt
    row = lambda w: pl.BlockSpec((tt, w), lambda i: (i, 0))
    in_specs = ([row(D_MODEL)] + [row(c.shape[1]) for c in combined]
                + [_const_spec((1, D_MODEL)), _const_spec((1, D_MODEL))])
    args = [h, *combined, ln_g, ln_b]
    aliases = {}
    if earlier is not None:
        in_specs.append(pl.BlockSpec(memory_space=pl.ANY))
        aliases = {len(args): 0}
        args.append(earlier)
    return pl.pallas_call(
        functools.partial(_final_kernel, summed=summed),
        grid=(tok // tt,),
        in_specs=in_specs,
        out_specs=pl.BlockSpec((tt, D_MODEL), lambda i: (i + first_tile, 0)),
        out_shape=jax.ShapeDtypeStruct((out_rows, D_MODEL), F32),
        input_output_aliases=aliases,
        compiler_params=pltpu.CompilerParams(dimension_semantics=("arbitrary",), vmem_limit_bytes=VMEM_LIMIT),
        name="final_norm",
    )(*args)


def _mixer_weights(w_in, b_in, ln_g, ln_b, w_s, b_s, w_o, b_o, ln1_g, ln1_b, w_router, b_router):
    win = w_in.astype(BF16)
    bin_ = b_in[None, :]
    woa = w_o[:ATTN_WIDTH].astype(BF16)
    wog = w_o[ATTN_WIDTH:].astype(BF16)
    wr = jnp.pad(w_router, ((0, 0), (0, LANES - N_EXPERTS)))
    wrh = wr.astype(BF16)
    wrl = (wr - wrh.astype(F32)).astype(BF16)
    br = jnp.pad(b_router, (0, LANES - N_EXPERTS))[None, :]
    tri = (lax.broadcasted_iota(I32, (MIX_TILE, MIX_TILE), 1)
           < lax.broadcasted_iota(I32, (MIX_TILE, MIX_TILE), 0)).astype(BF16)
    return (win, bin_, ln_g.reshape(1, GMLP_WIDTH), ln_b.reshape(1, GMLP_WIDTH), w_s, b_s.T,
            woa, wog, b_o[None, :], ln1_g[None, :], ln1_b[None, :], wrh, wrl, br, tri)


def _dest_rows(meta, pstart):
    e = meta[META_E:META_E + TOP_K].astype(I32)
    r = meta[META_R:META_R + TOP_K].astype(I32)
    hit = e[None] == jnp.arange(N_EXPERTS, dtype=I32)[:, None, None]
    return jnp.sum(jnp.where(hit, pstart[:, None, None], 0), axis=0) + r


def _moe(token_sets, cnt, experts, sum_on_sc):
    n_assign = sum(hw.shape[0] for hw, _, _ in token_sets) * TOP_K
    n_tiles = (n_assign + N_EXPERTS * (ROW_TILE - 1)) // ROW_TILE
    counts = cnt[0, :N_EXPERTS].astype(I32)
    padded = (counts + ROW_TILE - 1) // ROW_TILE * ROW_TILE
    pend = jnp.cumsum(padded)
    pstart = pend - padded
    n_active = (pend[-1:] // ROW_TILE).astype(I32)
    tile_start = jnp.arange(n_tiles, dtype=I32) * ROW_TILE
    tile_e = jnp.minimum(jnp.sum(pend[None, :] <= tile_start[:, None], axis=1), N_EXPERTS - 1).astype(I32)
    hit = tile_e[:, None] == jnp.arange(N_EXPERTS, dtype=I32)[None, :]
    valid_end = jnp.sum(jnp.where(hit, pstart + counts, 0), axis=1)
    units = jnp.clip((valid_end - tile_start + ROW_UNIT - 1) // ROW_UNIT, 1, ROW_TILE // ROW_UNIT).astype(I32)

    lists = lambda d: [d[kk][None, :] for kk in range(TOP_K)]
    dest_sets = [lists(_dest_rows(meta, pstart)) for _, _, meta in token_sets]
    xs, gs = _sc_dispatch([(hw, gates, dests) for (hw, gates, _), dests in zip(token_sets, dest_sets)],
                          n_tiles * ROW_TILE)
    if sum_on_sc:
        return _sc_combine(_experts(tile_e, n_active, units, xs, gs, *experts, COMBINE_WIDTH), dest_sets)
    return _sc_gather(_experts(tile_e, n_active, units, xs, gs, *experts, SLAB_WIDTH), dest_sets)


def kernel(x_prompt, x_sample, cache_k, cache_v, w_in, b_in, attn_sinks, gmlp_ln_g, gmlp_ln_b, w_spatial, b_spatial, w_o, b_o, ln1_g, ln1_b, w_router, b_router, w_gate_up, b_gate_up, w_down, b_down, ln2_g, ln2_b):
    assert w_in.shape[0] == DEPTH
    batch, seq, _ = x_prompt.shape
    dec_batch, dec_seq, _ = x_sample.shape
    tok_p, tok_s = batch * seq, dec_batch * dec_seq
    assert dec_seq == CHUNK and seq % MIX_TILE == 0 and batch >= 2
    assert all(t % MIX_TILE == 0 and t % FINAL_TILE == 0 and t % SC_WINDOW == 0 for t in (tok_p, tok_s))

    weights = _mixer_weights(w_in[0], b_in[0], gmlp_ln_g[0], gmlp_ln_b[0], w_spatial[0], b_spatial[0],
                             w_o[0], b_o[0], ln1_g[0], ln1_b[0], w_router[0], b_router[0])
    sinks = attn_sinks[0]
    experts = (w_gate_up, b_gate_up[0][:, None, :], w_down, b_down[0][:, None, :])

    batch_a = batch // 2
    h_a, hw_a, meta_a, gates_a, cnt_a, kt_a, vt_a = _mix_prompt(sinks, x_prompt, weights, 0, batch_a)
    h_s, hw_s, meta_s, gates_s, cnt_as, ks, vs, gvs = _mix_sample(
        sinks, x_sample.reshape(tok_s, D_MODEL),
        cache_k[0].reshape(dec_batch * WINDOW, KV_WIDTH), cache_v[0].reshape(dec_batch * WINDOW, KV_WIDTH),
        cnt_a, weights)
    summed_a, summed_s = _moe([(hw_a, gates_a, meta_a), (hw_s, gates_s, meta_s)], cnt_as, experts, sum_on_sc=True)
    h_b, hw_b, meta_b, gates_b, cnt_b, kt_b, vt_b = _mix_prompt(sinks, x_prompt, weights, batch_a, batch - batch_a)
    picked_b, = _moe([(hw_b, gates_b, meta_b)], cnt_b, experts, sum_on_sc=False)

    g2, b2 = ln2_g[0][None, :], ln2_b[0][None, :]
    y_p = _final_norm(h_a, summed_a, True, g2, b2, tok_p, 0)
    y_p = _final_norm(h_b, picked_b, False, g2, b2, tok_p, batch_a * seq, earlier=y_p).reshape(batch, seq, D_MODEL)
    y_s = _final_norm(h_s, summed_s, True, g2, b2, tok_s, 0).reshape(dec_batch, dec_seq, D_MODEL)
    kt, vt = jnp.concatenate([kt_a, kt_b]), jnp.concatenate([vt_a, vt_b])

    kv5 = lambda a, nb, rows: a.reshape(DEPTH, nb, rows, N_KV_HEADS, HEAD_DIM)
    return (y_p, y_s, kv5(kt, batch, WINDOW), kv5(vt, batch, WINDOW),
            kv5(ks, dec_batch, dec_seq), kv5(vs, dec_batch, dec_seq),
            gvs.reshape(DEPTH, dec_batch, dec_seq, GMLP_GROUPS, GMLP_GROUP_DIM))
```

```python
import functools

import jax
import jax.numpy as jnp
from jax import lax
from jax.experimental import pallas as pl
from jax.experimental.pallas import tpu as pltpu
from jax.experimental.pallas import tpu_sc as plsc

F32 = jnp.float32
BF16 = jnp.bfloat16
I32 = jnp.int32

D_MODEL = 1024
CHUNK = 64
N_HEADS = 8
N_KV_HEADS = 2
HEAD_DIM = 64
Q_GROUP = N_HEADS // N_KV_HEADS
KV_WIDTH = N_KV_HEADS * HEAD_DIM
WINDOW = 128
KEYS = WINDOW + CHUNK
GMLP_GROUPS = 4
GMLP_GROUP_DIM = 128
GMLP_WIDTH = GMLP_GROUPS * GMLP_GROUP_DIM
GMLP_CHUNK = 128
ATTN_WIDTH = N_HEADS * HEAD_DIM
N_EXPERTS = 32
TOP_K = 4
D_FF = 1024
SWIGLU_LIMIT = 7.0
SWIGLU_ALPHA = 1.702
DEPTH = 1
DN_ALPHA = (2 * DEPTH) ** 0.25
LN_EPS = 1e-5
NEG_INF = -1e30

LANES = 128
K_OFF = ATTN_WIDTH
V_OFF = K_OFF + KV_WIDTH
U_OFF = V_OFF + KV_WIDTH
GV_OFF = U_OFF + GMLP_WIDTH
Z_WIDTH = GV_OFF + GMLP_WIDTH

MIX_TILE = 512
ROW_TILE = 768
ROW_UNIT = 128
FINAL_TILE = 1024
SC_WINDOW = 128
SC_SLABS = 4
SLAB_WIDTH = D_MODEL // SC_SLABS
PACK_WIDTH = D_MODEL // 2
PACK_SLABS = PACK_WIDTH // SLAB_WIDTH
COMBINE_WIDTH = LANES
COMBINE_SLABS = PACK_WIDTH // COMBINE_WIDTH
VMEM_LIMIT = 58 * 1024 * 1024

META_E, META_R = 0, 4


def _gelu_tanh(x):
    return 0.5 * x * (1.0 + jnp.tanh(0.7978845608028654 * (x + 0.044715 * x * x * x)))


def _layer_norm(x, g, b):
    mu = jnp.mean(x, axis=-1, keepdims=True)
    xc = x - mu
    var = jnp.mean(xc * xc, axis=-1, keepdims=True)
    return xc * lax.rsqrt(var + LN_EPS) * g + b


def _dot(a, b):
    return jnp.dot(a, b, preferred_element_type=F32)


def _mixer_kernel(*refs, is_prompt, tt):
    n_chunks = tt // CHUNK
    it = iter(refs)
    sinks_ref = next(it)
    x_ref = next(it)
    if not is_prompt:
        ck_ref, cv_ref, base0_ref = next(it), next(it), next(it)
    (win_ref, bin_ref, lng_ref, lnb_ref, ws_ref, bs_ref, woa_ref, wog_ref, bo_ref, l1g_ref, l1b_ref,
     wrh_ref, wrl_ref, br_ref, tri_ref) = (next(it) for _ in range(15))
    h_ref, hw_ref, meta_ref, gate_ref, cnt_ref = (next(it) for _ in range(5))
    if is_prompt:
        kt_ref, vt_ref = next(it), next(it)
    else:
        ko_ref, vo_ref, gvo_ref = next(it), next(it), next(it)
    z_ref, kext_ref, vext_ref, att_ref, gm_ref, base_ref = (next(it) for _ in range(6))

    if is_prompt:
        first = (pl.program_id(0) == 0) & (pl.program_id(1) == 0)
        tile_in_seq = pl.program_id(1)

        @pl.when(first)
        def _():
            base_ref[...] = jnp.zeros_like(base_ref)

        @pl.when(tile_in_seq == 0)
        def _():
            kext_ref[:, 0:WINDOW, :] = jnp.zeros((N_KV_HEADS, WINDOW, KV_WIDTH), BF16)
            vext_ref[:, 0:WINDOW, :] = jnp.zeros((N_KV_HEADS, WINDOW, KV_WIDTH), BF16)
    else:
        @pl.when(pl.program_id(0) == 0)
        def _():
            base_ref[...] = base0_ref[...]

    x = x_ref[...]
    z_ref[...] = _dot(x.astype(BF16), win_ref[...]) + bin_ref[...]

    def put_kv(ext_ref, row0, rows_f32):
        n = rows_f32.shape[0]
        swapped = pltpu.roll(rows_f32, HEAD_DIM, axis=1)
        low = lax.broadcasted_iota(I32, (n, KV_WIDTH), 1) < HEAD_DIM
        ext_ref[0, row0:row0 + n, :] = jnp.where(low, rows_f32, swapped).astype(BF16)
        ext_ref[1, row0:row0 + n, :] = jnp.where(low, swapped, rows_f32).astype(BF16)

    k = z_ref[:, K_OFF:K_OFF + KV_WIDTH]
    v = z_ref[:, V_OFF:V_OFF + KV_WIDTH]
    if is_prompt:
        put_kv(kext_ref, WINDOW, k)
        put_kv(vext_ref, WINDOW, v)
        kt_ref[...] = k[tt - WINDOW:, :]
        vt_ref[...] = v[tt - WINDOW:, :]
        key_stride = CHUNK
    else:
        ko_ref[...] = k
        vo_ref[...] = v
        for c in range(n_chunks):
            put_kv(kext_ref, KEYS * c, ck_ref[WINDOW * c:WINDOW * (c + 1), :])
            put_kv(vext_ref, KEYS * c, cv_ref[WINDOW * c:WINDOW * (c + 1), :])
            put_kv(kext_ref, KEYS * c + WINDOW, k[CHUNK * c:CHUNK * (c + 1), :])
            put_kv(vext_ref, KEYS * c + WINDOW, v[CHUNK * c:CHUNK * (c + 1), :])
        key_stride = KEYS

    rows = Q_GROUP * CHUNK
    row_i = lax.broadcasted_iota(I32, (rows, 1), 0)
    key_i = lax.broadcasted_iota(I32, (rows, KEYS), 1)
    sink_cols = []
    for hk in range(N_KV_HEADS):
        s = [sinks_ref[hk * Q_GROUP + g] for g in range(Q_GROUP)]
        sink_cols.append(jnp.where(row_i < CHUNK, s[0],
                                   jnp.where(row_i < 2 * CHUNK, s[1],
                                             jnp.where(row_i < 3 * CHUNK, s[2], s[3]))))
    low_half = lax.broadcasted_iota(I32, (CHUNK, LANES), 1) < HEAD_DIM
    for c in range(n_chunks):
        outs = []
        for hk in range(N_KV_HEADS):
            kc = kext_ref[hk, key_stride * c:key_stride * c + KEYS, :]
            vc = vext_ref[hk, key_stride * c:key_stride * c + KEYS, :]
            heads = []
            for pair in range(hk * Q_GROUP // 2, (hk + 1) * Q_GROUP // 2):
                both = z_ref[CHUNK * c:CHUNK * (c + 1), pair * LANES:(pair + 1) * LANES] * (HEAD_DIM ** -0.5)
                heads += [jnp.where(low_half, both, 0.0), jnp.where(low_half, 0.0, both)]
            qs = jnp.concatenate(heads, axis=0).astype(BF16)
            sc = lax.dot_general(qs, kc, (((1,), (1,)), ((), ())), preferred_element_type=F32)
            if is_prompt and c < WINDOW // CHUNK:
                valid = (key_i >= WINDOW - CHUNK * c) | (tile_in_seq > 0)
                sc = jnp.where(valid, sc, NEG_INF)
            sink = sink_cols[hk]
            m = jnp.maximum(jnp.max(sc, axis=-1, keepdims=True), sink)
            p = jnp.exp(sc - m)
            den = jnp.sum(p, axis=-1, keepdims=True) + jnp.exp(sink - m)
            o = _dot(p.astype(BF16), vc) / den
            for j in range(Q_GROUP // 2):
                even, odd = o[CHUNK * 2 * j:CHUNK * (2 * j + 1), :], o[CHUNK * (2 * j + 1):CHUNK * (2 * j + 2), :]
                outs.append(jnp.where(low_half, even, odd))
        att_ref[CHUNK * c:CHUNK * (c + 1), :] = jnp.concatenate(outs, axis=1).astype(BF16)

    if is_prompt:
        kext_ref[:, 0:WINDOW, :] = kext_ref[:, tt:tt + WINDOW, :]
        vext_ref[:, 0:WINDOW, :] = vext_ref[:, tt:tt + WINDOW, :]

    gc = GMLP_CHUNK if is_prompt else CHUNK
    ri = lax.broadcasted_iota(I32, (gc, gc), 0)
    ci = lax.broadcasted_iota(I32, (gc, gc), 1)
    causal = (ci // CHUNK) <= (ri // CHUNK)
    for g in range(GMLP_GROUPS):
        lo, hi = g * GMLP_GROUP_DIM, (g + 1) * GMLP_GROUP_DIM
        u = _gelu_tanh(z_ref[:, U_OFF + lo:U_OFF + hi])
        gv = _layer_norm(_gelu_tanh(z_ref[:, GV_OFF + lo:GV_OFF + hi]), lng_ref[:, lo:hi], lnb_ref[:, lo:hi])
        if not is_prompt:
            gvo_ref[:, lo:hi] = gv
        gvb = gv.astype(BF16)
        wm = jnp.where(causal, ws_ref[g, 0:gc, 0:gc], 0.0).astype(BF16)
        bcol = bs_ref[0:gc, g:g + 1]
        for n in range(tt // gc):
            sp = _dot(wm, gvb[gc * n:gc * (n + 1), :]) + bcol
            gm_ref[gc * n:gc * (n + 1), lo:hi] = (u[gc * n:gc * (n + 1), :] * sp).astype(BF16)

    y = _dot(att_ref[...], woa_ref[...]) + _dot(gm_ref[...], wog_ref[...]) + bo_ref[...]
    h = _layer_norm(DN_ALPHA * x + y, l1g_ref[...], l1b_ref[...])
    h_ref[...] = h
    hw_ref[...] = pltpu.pack_elementwise([h[:, :PACK_WIDTH], h[:, PACK_WIDTH:]], packed_dtype=BF16)

    h_hi = h.astype(BF16)
    h_lo = (h - h_hi.astype(F32)).astype(BF16)
    logits = _dot(h_hi, wrh_ref[...]) + _dot(h_lo, wrh_ref[...]) + _dot(h_hi, wrl_ref[...]) + br_ref[...]
    lane = lax.broadcasted_iota(I32, (tt, LANES), 1)
    lane_f = lane.astype(F32)
    l = jnp.where(lane < N_EXPERTS, logits, -jnp.inf)
    tops, idxs, hots = [], [], []
    for _ in range(TOP_K):
        m = jnp.max(l, axis=-1, keepdims=True)
        idx = jnp.min(jnp.where(l == m, lane_f, float(LANES)), axis=-1, keepdims=True)
        hot = lane_f == idx
        l = jnp.where(hot, -jnp.inf, l)
        tops.append(m)
        idxs.append(idx)
        hots.append(hot)
    es = [jnp.exp(t - tops[0]) for t in tops]
    esum = es[0] + es[1] + es[2] + es[3]
    chosen = jnp.where(hots[0] | hots[1] | hots[2] | hots[3], 1.0, 0.0)
    before = _dot(tri_ref[...], chosen.astype(BF16)) + base_ref[...]
    meta = jnp.zeros((tt, LANES), F32)
    for kk in range(TOP_K):
        rank = jnp.sum(jnp.where(hots[kk], before, 0.0), axis=-1, keepdims=True)
        meta = jnp.where(lane == META_E + kk, idxs[kk], meta)
        meta = jnp.where(lane == META_R + kk, rank, meta)
        gate_ref[:, kk * LANES:(kk + 1) * LANES] = jnp.broadcast_to(es[kk] / esum, (tt, LANES))
    meta_ref[...] = jnp.transpose(meta)[0:2 * TOP_K, :]
    base_ref[...] = base_ref[...] + jnp.sum(chosen, axis=0, keepdims=True)
    cnt_ref[...] = base_ref[...]


def _const_spec(shape):
    nd = len(shape)
    return pl.BlockSpec(shape, lambda *_: (0,) * nd, pipeline_mode=pl.Buffered(1))


def _mixer_weight_specs(tt):
    return [
        _const_spec((D_MODEL, Z_WIDTH)), _const_spec((1, Z_WIDTH)),
        _const_spec((1, GMLP_WIDTH)), _const_spec((1, GMLP_WIDTH)),
        _const_spec((GMLP_GROUPS, GMLP_CHUNK, GMLP_CHUNK)), _const_spec((GMLP_CHUNK, GMLP_GROUPS)),
        _const_spec((ATTN_WIDTH, D_MODEL)), _const_spec((GMLP_WIDTH, D_MODEL)), _const_spec((1, D_MODEL)),
        _const_spec((1, D_MODEL)), _const_spec((1, D_MODEL)),
        _const_spec((D_MODEL, LANES)), _const_spec((D_MODEL, LANES)), _const_spec((1, LANES)),
        _const_spec((tt, tt)),
    ]


def _mixer_scratch(tt, kext_rows):
    return [
        pltpu.VMEM((tt, Z_WIDTH), F32),
        pltpu.VMEM((N_KV_HEADS, kext_rows, KV_WIDTH), BF16), pltpu.VMEM((N_KV_HEADS, kext_rows, KV_WIDTH), BF16),
        pltpu.VMEM((tt, ATTN_WIDTH), BF16), pltpu.VMEM((tt, GMLP_WIDTH), BF16),
        pltpu.VMEM((1, LANES), F32),
    ]


def _mix_prompt(sinks, x, weights, first_batch, batch):
    seq = x.shape[1]
    tt = MIX_TILE
    n_tiles = seq // tt
    tok = batch * seq
    smem = pl.BlockSpec(memory_space=pltpu.SMEM)
    return pl.pallas_call(
        functools.partial(_mixer_kernel, is_prompt=True, tt=tt),
        grid=(batch, n_tiles),
        in_specs=[smem, pl.BlockSpec((None, tt, D_MODEL), lambda b, i: (b + first_batch, i, 0))]
        + _mixer_weight_specs(tt),
        out_specs=[
            pl.BlockSpec((tt, D_MODEL), lambda b, i: (b * n_tiles + i, 0)),
            pl.BlockSpec((tt, PACK_WIDTH), lambda b, i: (b * n_tiles + i, 0)),
            pl.BlockSpec((2 * TOP_K, tt), lambda b, i: (0, b * n_tiles + i)),
            pl.BlockSpec((tt, TOP_K * LANES), lambda b, i: (b * n_tiles + i, 0)),
            pl.BlockSpec((1, LANES), lambda b, i: (0, 0)),
            pl.BlockSpec((WINDOW, KV_WIDTH), lambda b, i: (b, 0)),
            pl.BlockSpec((WINDOW, KV_WIDTH), lambda b, i: (b, 0)),
        ],
        out_shape=[
            jax.ShapeDtypeStruct((tok, D_MODEL), F32), jax.ShapeDtypeStruct((tok, PACK_WIDTH), jnp.uint32),
            jax.ShapeDtypeStruct((2 * TOP_K, tok), F32), jax.ShapeDtypeStruct((tok, TOP_K * LANES), F32),
            jax.ShapeDtypeStruct((1, LANES), F32),
            jax.ShapeDtypeStruct((batch * WINDOW, KV_WIDTH), F32),
            jax.ShapeDtypeStruct((batch * WINDOW, KV_WIDTH), F32),
        ],
        scratch_shapes=_mixer_scratch(tt, WINDOW + tt),
        compiler_params=pltpu.CompilerParams(
            dimension_semantics=("arbitrary", "arbitrary"), vmem_limit_bytes=VMEM_LIMIT),
        name="mix_prompt",
    )(sinks, x, *weights)


def _mix_sample(sinks, x2, ck, cv, base0, weights):
    tok = x2.shape[0]
    tt = MIX_TILE
    n_chunks = tt // CHUNK
    cache_rows = n_chunks * WINDOW
    smem = pl.BlockSpec(memory_space=pltpu.SMEM)
    row = lambda w: pl.BlockSpec((tt, w), lambda i: (i, 0))
    return pl.pallas_call(
        functools.partial(_mixer_kernel, is_prompt=False, tt=tt),
        grid=(tok // tt,),
        in_specs=[smem, row(D_MODEL),
                  pl.BlockSpec((cache_rows, KV_WIDTH), lambda i: (i, 0)),
                  pl.BlockSpec((cache_rows, KV_WIDTH), lambda i: (i, 0)),
                  _const_spec((1, LANES))] + _mixer_weight_specs(tt),
        out_specs=[row(D_MODEL), row(PACK_WIDTH), pl.BlockSpec((2 * TOP_K, tt), lambda i: (0, i)),
                   row(TOP_K * LANES),
                   pl.BlockSpec((1, LANES), lambda i: (0, 0)),
                   row(KV_WIDTH), row(KV_WIDTH), row(GMLP_WIDTH)],
        out_shape=[
            jax.ShapeDtypeStruct((tok, D_MODEL), F32), jax.ShapeDtypeStruct((tok, PACK_WIDTH), jnp.uint32),
            jax.ShapeDtypeStruct((2 * TOP_K, tok), F32), jax.ShapeDtypeStruct((tok, TOP_K * LANES), F32),
            jax.ShapeDtypeStruct((1, LANES), F32),
            jax.ShapeDtypeStruct((tok, KV_WIDTH), F32), jax.ShapeDtypeStruct((tok, KV_WIDTH), F32),
            jax.ShapeDtypeStruct((tok, GMLP_WIDTH), F32),
        ],
        scratch_shapes=_mixer_scratch(tt, n_chunks * KEYS),
        compiler_params=pltpu.CompilerParams(
            dimension_semantics=("arbitrary",), vmem_limit_bytes=VMEM_LIMIT),
        name="mix_sample",
    )(sinks, x2, ck, cv, base0, *weights)


def _sc_pipeline(body, n_tok, in_specs, out_specs):
    return pltpu.emit_pipeline(
        body, grid=(n_tok // SC_WINDOW,), in_specs=in_specs, out_specs=out_specs,
        core_axis_name=("core", "subcore"), dimension_semantics=(pltpu.PARALLEL,))


def _index_specs():
    return [pl.BlockSpec((1, SC_WINDOW), lambda i: (0, i))] * TOP_K


def _sc_dispatch(token_sets, n_rows):
    win, wid = SC_WINDOW, SLAB_WIDTH
    per_set = 2 + TOP_K
    n_in = per_set * len(token_sets)
    mesh = plsc.VectorSubcoreMesh(core_axis_name="core", subcore_axis_name="subcore")
    out_type = ([jax.ShapeDtypeStruct((n_rows, wid), jnp.uint32)] * PACK_SLABS
                + [jax.ShapeDtypeStruct((n_rows, LANES), F32)])

    @functools.partial(pl.kernel, out_type=out_type, mesh=mesh, scratch_types=[], name="sc_dispatch")
    def run(*refs):
        xs_hbm, gs_hbm = refs[n_in:n_in + PACK_SLABS], refs[n_in + PACK_SLABS]

        for s in range(len(token_sets)):
            h_hbm, g_hbm = refs[per_set * s], refs[per_set * s + 1]
            i_hbm = refs[per_set * s + 2:per_set * (s + 1)]
            n_tok = h_hbm.shape[0]
            for q in range(PACK_SLABS):
                def rows_body(x_vmem, *i_vmem, q=q):
                    for kk in range(TOP_K):
                        pltpu.sync_copy(x_vmem, xs_hbm[q].at[i_vmem[kk].at[0]])

                _sc_pipeline(rows_body, n_tok, [pl.BlockSpec((win, wid), lambda i, q=q: (i, q))] + _index_specs(),
                             [])(h_hbm, *i_hbm)
            for kk in range(TOP_K):
                def gate_body(g_vmem, i_vmem):
                    pltpu.sync_copy(g_vmem, gs_hbm.at[i_vmem.at[0]])

                _sc_pipeline(gate_body, n_tok,
                             [pl.BlockSpec((win, LANES), lambda i, kk=kk: (i, kk)), _index_specs()[0]],
                             [])(g_hbm, i_hbm[kk])

    outs = run(*[a for hw, gates, dests in token_sets for a in (hw, gates, *dests)])
    return outs[:PACK_SLABS], outs[PACK_SLABS]


def _expert_kernel(te_ref, na_ref, par_ref, nxt_ref, units_ref, *refs, out_slabs):
    x_refs, (gs_ref, wgu_hbm, bgu_ref, wd_hbm, bd_ref) = refs[:PACK_SLABS], refs[PACK_SLABS:PACK_SLABS + 5]
    y_refs = refs[PACK_SLABS + 5:PACK_SLABS + 5 + out_slabs]
    wgu_land, wd_land, wgu_bf, wd_bf, sem = refs[PACK_SLABS + 5 + out_slabs:]
    out_width = PACK_WIDTH // out_slabs
    t = pl.program_id(0)
    do_tile = t < na_ref[0]

    def weight_copies(expert, slot):
        return (pltpu.make_async_copy(wgu_hbm.at[0, expert], wgu_land.at[slot], sem.at[slot, 0]),
                pltpu.make_async_copy(wd_hbm.at[0, expert], wd_land.at[slot], sem.at[slot, 1]))

    @pl.when(do_tile & ((t == 0) | (te_ref[t] != te_ref[jnp.maximum(t - 1, 0)])))
    def _():
        slot = par_ref[t]

        @pl.when(t == 0)
        def _():
            for cp in weight_copies(te_ref[0], slot):
                cp.start()

        for cp in weight_copies(te_ref[t], slot):
            cp.wait()
        wgu_bf[...] = wgu_land[slot].astype(BF16)
        wd_bf[...] = wd_land[slot].astype(BF16)

        @pl.when(nxt_ref[t] >= 0)
        def _():
            for cp in weight_copies(nxt_ref[t], 1 - slot):
                cp.start()

    def compute(n_rows):
        words = [r[0:n_rows, :] for r in x_refs]
        x = jnp.concatenate(
            [pltpu.unpack_elementwise(w, index=index, packed_dtype=BF16, unpacked_dtype=F32)
             for index in range(2) for w in words], axis=1).astype(BF16)
        hmid = _dot(x, wgu_bf[...]) + bgu_ref[0]
        gate = jnp.minimum(hmid[:, :D_FF], SWIGLU_LIMIT)
        up = jnp.clip(hmid[:, D_FF:], -SWIGLU_LIMIT, SWIGLU_LIMIT)
        act = (up + 1.0) * gate * jax.nn.sigmoid(SWIGLU_ALPHA * gate)
        y = _dot(act.astype(BF16), wd_bf[...]) + bd_ref[0]
        y = y * jnp.concatenate([gs_ref[0:n_rows, :]] * (D_MODEL // LANES), axis=1)
        packed = pltpu.pack_elementwise([y[:, :PACK_WIDTH], y[:, PACK_WIDTH:]], packed_dtype=BF16)
        for q in range(out_slabs):
            y_refs[q][0:n_rows, :] = packed[:, q * out_width:(q + 1) * out_width]

    for units in range(1, ROW_TILE // ROW_UNIT + 1):
        @pl.when(do_tile & (units_ref[t] == units))
        def _(units=units):
            compute(units * ROW_UNIT)


def _experts(tile_e, n_active, units, xs, gs, wgu, bgu, wd, bd, out_width):
    tm = ROW_TILE
    out_slabs = PACK_WIDTH // out_width
    n_tiles = gs.shape[0] // tm
    tile = jnp.arange(n_tiles, dtype=I32)
    opens = jnp.concatenate([jnp.ones((1,), bool), tile_e[1:] != tile_e[:-1]]) & (tile < n_active[0])
    parity = ((jnp.cumsum(opens.astype(I32)) - 1) % 2).astype(I32)
    next_open = jnp.concatenate([lax.cummin(jnp.where(opens, tile, n_tiles)[::-1])[::-1][1:],
                                 jnp.full((1,), n_tiles, I32)])
    nxt = jnp.where(next_open < n_tiles, tile_e[jnp.minimum(next_open, n_tiles - 1)], -1).astype(I32)
    row_map = lambda i, te, na, *_: (jnp.minimum(i, na[0] - 1), 0)
    w_map = lambda i, te, na, *_: (te[jnp.minimum(i, na[0] - 1)], 0, 0)
    in_hbm = pl.BlockSpec(memory_space=pl.ANY)
    return pl.pallas_call(
        functools.partial(_expert_kernel, out_slabs=out_slabs),
        grid_spec=pltpu.PrefetchScalarGridSpec(
            num_scalar_prefetch=5,
            grid=(n_tiles,),
            in_specs=[pl.BlockSpec((tm, SLAB_WIDTH), row_map)] * PACK_SLABS + [
                pl.BlockSpec((tm, LANES), row_map),
                in_hbm,
                pl.BlockSpec((1, 1, 2 * D_FF), w_map),
                in_hbm,
                pl.BlockSpec((1, 1, D_MODEL), w_map),
            ],
            out_specs=[pl.BlockSpec((tm, out_width), row_map)] * out_slabs,
            scratch_shapes=[pltpu.VMEM((2, D_MODEL, 2 * D_FF), F32), pltpu.VMEM((2, D_FF, D_MODEL), F32),
                            pltpu.VMEM((D_MODEL, 2 * D_FF), BF16), pltpu.VMEM((D_FF, D_MODEL), BF16),
                            pltpu.SemaphoreType.DMA((2, 2))],
        ),
        out_shape=[jax.ShapeDtypeStruct((n_tiles * tm, out_width), jnp.uint32)] * out_slabs,
        compiler_params=pltpu.CompilerParams(
            dimension_semantics=("arbitrary",), vmem_limit_bytes=VMEM_LIMIT),
        name="experts",
    )(tile_e, n_active, parity, nxt, units, *xs, gs, wgu, bgu, wd, bd)


def _sc_combine(ys, dest_sets):
    n_sets = len(dest_sets)
    toks = [dests[0].shape[1] for dests in dest_sets]
    mesh = plsc.VectorSubcoreMesh(core_axis_name="core", subcore_axis_name="subcore")
    out_type = [jax.ShapeDtypeStruct((t, 2 * COMBINE_WIDTH), F32) for t in toks for _ in range(COMBINE_SLABS)]
    lanes = plsc.get_sparse_core_info().num_lanes
    high = jnp.uint32(0xFFFF0000)

    @functools.partial(pl.kernel, out_type=out_type, mesh=mesh, name="sc_combine",
                       scratch_types=[pltpu.VMEM((SC_WINDOW, COMBINE_WIDTH), jnp.uint32)] * 2,
                       compiler_params=pltpu.CompilerParams(needs_layout_passes=False))
    def run(*refs):
        ys_hbm = refs[:COMBINE_SLABS]
        out0 = COMBINE_SLABS + TOP_K * n_sets
        buf_a, buf_b = refs[-2:]

        def add_pair(ys_q, i_a, i_b, o_vmem, first):
            pltpu.sync_copy(ys_q.at[i_a.at[0]], buf_a)
            pltpu.sync_copy(ys_q.at[i_b.at[0]], buf_b)

            @pl.loop(0, SC_WINDOW)
            def _(r):
                for c in range(COMBINE_WIDTH // lanes):
                    cols = pl.ds(c * lanes, lanes)
                    wa, wb = buf_a[r, cols], buf_b[r, cols]
                    lo = plsc.bitcast(wa << 16, F32) + plsc.bitcast(wb << 16, F32)
                    hi = plsc.bitcast(wa & high, F32) + plsc.bitcast(wb & high, F32)
                    lo_cols, hi_cols = cols, pl.ds(COMBINE_WIDTH + c * lanes, lanes)
                    if first:
                        o_vmem[r, lo_cols] = lo
                        o_vmem[r, hi_cols] = hi
                    else:
                        o_vmem[r, lo_cols] = o_vmem[r, lo_cols] + lo
                        o_vmem[r, hi_cols] = o_vmem[r, hi_cols] + hi

        for s in range(n_sets):
            i_hbm = refs[COMBINE_SLABS + TOP_K * s:COMBINE_SLABS + TOP_K * (s + 1)]
            f_hbm = refs[out0 + COMBINE_SLABS * s:out0 + COMBINE_SLABS * (s + 1)]
            for q in range(COMBINE_SLABS):
                def body(i0, i1, i2, i3, o_vmem, q=q):
                    add_pair(ys_hbm[q], i0, i1, o_vmem, True)
                    add_pair(ys_hbm[q], i2, i3, o_vmem, False)

                _sc_pipeline(body, toks[s], _index_specs(),
                             [pl.BlockSpec((SC_WINDOW, 2 * COMBINE_WIDTH), lambda i: (i, 0))])(*i_hbm, f_hbm[q])

    outs = run(*ys, *[d for dests in dest_sets for d in dests])
    return [outs[COMBINE_SLABS * s:COMBINE_SLABS * (s + 1)] for s in range(n_sets)]


def _sc_gather(ys, dest_sets):
    n_out = TOP_K * PACK_SLABS
    n_sets = len(dest_sets)
    toks = [dests[0].shape[1] for dests in dest_sets]
    mesh = plsc.VectorSubcoreMesh(core_axis_name="core", subcore_axis_name="subcore")
    out_type = [jax.ShapeDtypeStruct((t, SLAB_WIDTH), jnp.uint32) for t in toks for _ in range(n_out)]

    @functools.partial(pl.kernel, out_type=out_type, mesh=mesh, scratch_types=[], name="sc_gather")
    def run(*refs):
        ys_hbm = refs[:PACK_SLABS]
        out0 = PACK_SLABS + TOP_K * n_sets

        for s in range(n_sets):
            i_hbm = refs[PACK_SLABS + TOP_K * s:PACK_SLABS + TOP_K * (s + 1)]
            f_hbm = refs[out0 + n_out * s:out0 + n_out * (s + 1)]
            for kk in range(TOP_K):
                for q in range(PACK_SLABS):
                    def body(i_vmem, o_vmem, q=q):
                        pltpu.sync_copy(ys_hbm[q].at[i_vmem.at[0]], o_vmem)

                    _sc_pipeline(body, toks[s], [_index_specs()[0]],
                                 [pl.BlockSpec((SC_WINDOW, SLAB_WIDTH), lambda i: (i, 0))]
                                 )(i_hbm[kk], f_hbm[kk * PACK_SLABS + q])

    outs = run(*ys, *[d for dests in dest_sets for d in dests])
    return [outs[n_out * s:n_out * (s + 1)] for s in range(n_sets)]


def _final_kernel(*refs, summed):
    n_in = COMBINE_SLABS if summed else TOP_K * PACK_SLABS
    h_ref, f_refs, g_ref, b_ref, out_ref = refs[0], refs[1:1 + n_in], refs[1 + n_in], refs[2 + n_in], refs[-1]
    if summed:
        f = jnp.concatenate([r[:, :COMBINE_WIDTH] for r in f_refs] + [r[:, COMBINE_WIDTH:] for r in f_refs], axis=1)
    else:
        halves = []
        for index in range(2):
            for q in range(PACK_SLABS):
                parts = [pltpu.unpack_elementwise(f_refs[kk * PACK_SLABS + q][...], index=index,
                                                  packed_dtype=BF16, unpacked_dtype=F32) for kk in range(TOP_K)]
                halves.append((parts[0] + parts[1]) + (parts[2] + parts[3]))
        f = jnp.concatenate(halves, axis=1)
    out_ref[...] = _layer_norm(DN_ALPHA * h_ref[...] + f, g_ref[...], b_ref[...])


def _final_norm(h, combined, summed, ln_g, ln_b, out_rows, first_row, earlier=None):
    tt = FINAL_TILE
    tok = h.shape[0]
    first_tile = first_row // tt
    row = lambda w: pl.BlockSpec((tt, w), lambda i: (i, 0))
    in_specs = ([row(D_MODEL)] + [row(c.shape[1]) for c in combined]
                + [_const_spec((1, D_MODEL)), _const_spec((1, D_MODEL))])
    args = [h, *combined, ln_g, ln_b]
    aliases = {}
    if earlier is not None:
        in_specs.append(pl.BlockSpec(memory_space=pl.ANY))
        aliases = {len(args): 0}
        args.append(earlier)
    return pl.pallas_call(
        functools.partial(_final_kernel, summed=summed),
        grid=(tok // tt,),
        in_specs=in_specs,
        out_specs=pl.BlockSpec((tt, D_MODEL), lambda i: (i + first_tile, 0)),
        out_shape=jax.ShapeDtypeStruct((out_rows, D_MODEL), F32),
        input_output_aliases=aliases,
        compiler_params=pltpu.CompilerParams(dimension_semantics=("arbitrary",), vmem_limit_bytes=VMEM_LIMIT),
        name="final_norm",
    )(*args)


def _mixer_weights(w_in, b_in, ln_g, ln_b, w_s, b_s, w_o, b_o, ln1_g, ln1_b, w_router, b_router):
    win = w_in.astype(BF16)
    bin_ = b_in[None, :]
    woa = w_o[:ATTN_WIDTH].astype(BF16)
    wog = w_o[ATTN_WIDTH:].astype(BF16)
    wr = jnp.pad(w_router, ((0, 0), (0, LANES - N_EXPERTS)))
    wrh = wr.astype(BF16)
    wrl = (wr - wrh.astype(F32)).astype(BF16)
    br = jnp.pad(b_router, (0, LANES - N_EXPERTS))[None, :]
    tri = (lax.broadcasted_iota(I32, (MIX_TILE, MIX_TILE), 1)
           < lax.broadcasted_iota(I32, (MIX_TILE, MIX_TILE), 0)).astype(BF16)
    return (win, bin_, ln_g.reshape(1, GMLP_WIDTH), ln_b.reshape(1, GMLP_WIDTH), w_s, b_s.T,
            woa, wog, b_o[None, :], ln1_g[None, :], ln1_b[None, :], wrh, wrl, br, tri)


def _dest_rows(meta, pstart):
    e = meta[META_E:META_E + TOP_K].astype(I32)
    r = meta[META_R:META_R + TOP_K].astype(I32)
    hit = e[None] == jnp.arange(N_EXPERTS, dtype=I32)[:, None, None]
    return jnp.sum(jnp.where(hit, pstart[:, None, None], 0), axis=0) + r


def _moe(token_sets, cnt, experts, sum_on_sc):
    n_assign = sum(hw.shape[0] for hw, _, _ in token_sets) * TOP_K
    n_tiles = (n_assign + N_EXPERTS * (ROW_TILE - 1)) // ROW_TILE
    counts = cnt[0, :N_EXPERTS].astype(I32)
    padded = (counts + ROW_TILE - 1) // ROW_TILE * ROW_TILE
    pend = jnp.cumsum(padded)
    pstart = pend - padded
    n_active = (pend[-1:] // ROW_TILE).astype(I32)
    tile_start = jnp.arange(n_tiles, dtype=I32) * ROW_TILE
    tile_e = jnp.minimum(jnp.sum(pend[None, :] <= tile_start[:, None], axis=1), N_EXPERTS - 1).astype(I32)
    hit = tile_e[:, None] == jnp.arange(N_EXPERTS, dtype=I32)[None, :]
    valid_end = jnp.sum(jnp.where(hit, pstart + counts, 0), axis=1)
    units = jnp.clip((valid_end - tile_start + ROW_UNIT - 1) // ROW_UNIT, 1, ROW_TILE // ROW_UNIT).astype(I32)

    lists = lambda d: [d[kk][None, :] for kk in range(TOP_K)]
    dest_sets = [lists(_dest_rows(meta, pstart)) for _, _, meta in token_sets]
    xs, gs = _sc_dispatch([(hw, gates, dests) for (hw, gates, _), dests in zip(token_sets, dest_sets)],
                          n_tiles * ROW_TILE)
    if sum_on_sc:
        return _sc_combine(_experts(tile_e, n_active, units, xs, gs, *experts, COMBINE_WIDTH), dest_sets)
    return _sc_gather(_experts(tile_e, n_active, units, xs, gs, *experts, SLAB_WIDTH), dest_sets)


def kernel(x_prompt, x_sample, cache_k, cache_v, w_in, b_in, attn_sinks, gmlp_ln_g, gmlp_ln_b, w_spatial, b_spatial, w_o, b_o, ln1_g, ln1_b, w_router, b_router, w_gate_up, b_gate_up, w_down, b_down, ln2_g, ln2_b):
    assert w_in.shape[0] == DEPTH
    batch, seq, _ = x_prompt.shape
    dec_batch, dec_seq, _ = x_sample.shape
    tok_p, tok_s = batch * seq, dec_batch * dec_seq
    assert dec_seq == CHUNK and seq % MIX_TILE == 0 and batch >= 2
    assert all(t % MIX_TILE == 0 and t % FINAL_TILE == 0 and t % SC_WINDOW == 0 for t in (tok_p, tok_s))

    weights = _mixer_weights(w_in[0], b_in[0], gmlp_ln_g[0], gmlp_ln_b[0], w_spatial[0], b_spatial[0],
                             w_o[0], b_o[0], ln1_g[0], ln1_b[0], w_router[0], b_router[0])
    sinks = attn_sinks[0]
    experts = (w_gate_up, b_gate_up[0][:, None, :], w_down, b_down[0][:, None, :])

    batch_a = batch // 2
    h_a, hw_a, meta_a, gates_a, cnt_a, kt_a, vt_a = _mix_prompt(sinks, x_prompt, weights, 0, batch_a)
    h_s, hw_s, meta_s, gates_s, cnt_as, ks, vs, gvs = _mix_sample(
        sinks, x_sample.reshape(tok_s, D_MODEL),
        cache_k[0].reshape(dec_batch * WINDOW, KV_WIDTH), cache_v[0].reshape(dec_batch * WINDOW, KV_WIDTH),
        cnt_a, weights)
    summed_a, summed_s = _moe([(hw_a, gates_a, meta_a), (hw_s, gates_s, meta_s)], cnt_as, experts, sum_on_sc=True)
    h_b, hw_b, meta_b, gates_b, cnt_b, kt_b, vt_b = _mix_prompt(sinks, x_prompt, weights, batch_a, batch - batch_a)
    picked_b, = _moe([(hw_b, gates_b, meta_b)], cnt_b, experts, sum_on_sc=False)

    g2, b2 = ln2_g[0][None, :], ln2_b[0][None, :]
    y_p = _final_norm(h_a, summed_a, True, g2, b2, tok_p, 0)
    y_p = _final_norm(h_b, picked_b, False, g2, b2, tok_p, batch_a * seq, earlier=y_p).reshape(batch, seq, D_MODEL)
    y_s = _final_norm(h_s, summed_s, True, g2, b2, tok_s, 0).reshape(dec_batch, dec_seq, D_MODEL)
    kt, vt = jnp.concatenate([kt_a, kt_b]), jnp.concatenate([vt_a, vt_b])

    kv5 = lambda a, nb, rows: a.reshape(DEPTH, nb, rows, N_KV_HEADS, HEAD_DIM)
    return (y_p, y_s, kv5(kt, batch, WINDOW), kv5(vt, batch, WINDOW),
            kv5(ks, dec_batch, dec_seq), kv5(vs, dec_batch, dec_seq),
            gvs.reshape(DEPTH, dec_batch, dec_seq, GMLP_GROUPS, GMLP_GROUP_DIM))
```

```python
import functools

import jax
import jax.numpy as jnp
from jax import lax
from jax.experimental import pallas as pl
from jax.experimental.pallas import tpu as pltpu
from jax.experimental.pallas import tpu_sc as plsc

F32 = jnp.float32
BF16 = jnp.bfloat16
I32 = jnp.int32

D_MODEL = 1024
CHUNK = 64
N_HEADS = 8
N_KV_HEADS = 2
HEAD_DIM = 64
Q_GROUP = N_HEADS // N_KV_HEADS
KV_WIDTH = N_KV_HEADS * HEAD_DIM
WINDOW = 128
KEYS = WINDOW + CHUNK
GMLP_GROUPS = 4
GMLP_GROUP_DIM = 128
GMLP_WIDTH = GMLP_GROUPS * GMLP_GROUP_DIM
GMLP_CHUNK = 128
ATTN_WIDTH = N_HEADS * HEAD_DIM
N_EXPERTS = 32
TOP_K = 4
D_FF = 1024
SWIGLU_LIMIT = 7.0
SWIGLU_ALPHA = 1.702
DEPTH = 1
DN_ALPHA = (2 * DEPTH) ** 0.25
LN_EPS = 1e-5
NEG_INF = -1e30

LANES = 128
K_OFF = ATTN_WIDTH
V_OFF = K_OFF + KV_WIDTH
U_OFF = V_OFF + KV_WIDTH
GV_OFF = U_OFF + GMLP_WIDTH
Z_WIDTH = GV_OFF + GMLP_WIDTH

MIX_TILE = 512
ROW_TILE = 1024
ROW_UNIT = 128
FINAL_TILE = 1024
SC_WINDOW = 128
SC_SLABS = 4
SLAB_WIDTH = D_MODEL // SC_SLABS
PACK_WIDTH = D_MODEL // 2
PACK_SLABS = PACK_WIDTH // SLAB_WIDTH
COMBINE_WIDTH = LANES
COMBINE_SLABS = PACK_WIDTH // COMBINE_WIDTH
VMEM_LIMIT = 58 * 1024 * 1024

META_E, META_R = 0, 4


def _gelu_tanh(x):
    return 0.5 * x * (1.0 + jnp.tanh(0.7978845608028654 * (x + 0.044715 * x * x * x)))


def _layer_norm(x, g, b):
    mu = jnp.mean(x, axis=-1, keepdims=True)
    xc = x - mu
    var = jnp.mean(xc * xc, axis=-1, keepdims=True)
    return xc * lax.rsqrt(var + LN_EPS) * g + b


def _dot(a, b):
    return jnp.dot(a, b, preferred_element_type=F32)


def _mixer_kernel(*refs, is_prompt, tt):
    n_chunks = tt // CHUNK
    it = iter(refs)
    sinks_ref = next(it)
    x_ref = next(it)
    if not is_prompt:
        ck_ref, cv_ref, base0_ref = next(it), next(it), next(it)
    (win_ref, bin_ref, lng_ref, lnb_ref, ws_ref, bs_ref, woa_ref, wog_ref, bo_ref, l1g_ref, l1b_ref,
     wrh_ref, wrl_ref, br_ref, tri_ref) = (next(it) for _ in range(15))
    h_ref, hw_ref, meta_ref, gate_ref, cnt_ref = (next(it) for _ in range(5))
    if is_prompt:
        kt_ref, vt_ref = next(it), next(it)
    else:
        ko_ref, vo_ref, gvo_ref = next(it), next(it), next(it)
    z_ref, kext_ref, vext_ref, att_ref, gm_ref, base_ref = (next(it) for _ in range(6))

    if is_prompt:
        first = (pl.program_id(0) == 0) & (pl.program_id(1) == 0)
        tile_in_seq = pl.program_id(1)

        @pl.when(first)
        def _():
            base_ref[...] = jnp.zeros_like(base_ref)

        @pl.when(tile_in_seq == 0)
        def _():
            kext_ref[:, 0:WINDOW, :] = jnp.zeros((N_KV_HEADS, WINDOW, KV_WIDTH), BF16)
            vext_ref[:, 0:WINDOW, :] = jnp.zeros((N_KV_HEADS, WINDOW, KV_WIDTH), BF16)
    else:
        @pl.when(pl.program_id(0) == 0)
        def _():
            base_ref[...] = base0_ref[...]

    x = x_ref[...]
    z_ref[...] = _dot(x.astype(BF16), win_ref[...]) + bin_ref[...]

    def put_kv(ext_ref, row0, rows_f32):
        n = rows_f32.shape[0]
        swapped = pltpu.roll(rows_f32, HEAD_DIM, axis=1)
        low = lax.broadcasted_iota(I32, (n, KV_WIDTH), 1) < HEAD_DIM
        ext_ref[0, row0:row0 + n, :] = jnp.where(low, rows_f32, swapped).astype(BF16)
        ext_ref[1, row0:row0 + n, :] = jnp.where(low, swapped, rows_f32).astype(BF16)

    k = z_ref[:, K_OFF:K_OFF + KV_WIDTH]
    v = z_ref[:, V_OFF:V_OFF + KV_WIDTH]
    if is_prompt:
        put_kv(kext_ref, WINDOW, k)
        put_kv(vext_ref, WINDOW, v)
        kt_ref[...] = k[tt - WINDOW:, :]
        vt_ref[...] = v[tt - WINDOW:, :]
        key_stride = CHUNK
    else:
        ko_ref[...] = k
        vo_ref[...] = v
        for c in range(n_chunks):
            put_kv(kext_ref, KEYS * c, ck_ref[WINDOW * c:WINDOW * (c + 1), :])
            put_kv(vext_ref, KEYS * c, cv_ref[WINDOW * c:WINDOW * (c + 1), :])
            put_kv(kext_ref, KEYS * c + WINDOW, k[CHUNK * c:CHUNK * (c + 1), :])
            put_kv(vext_ref, KEYS * c + WINDOW, v[CHUNK * c:CHUNK * (c + 1), :])
        key_stride = KEYS

    rows = Q_GROUP * CHUNK
    row_i = lax.broadcasted_iota(I32, (rows, 1), 0)
    key_i = lax.broadcasted_iota(I32, (rows, KEYS), 1)
    sink_cols = []
    for hk in range(N_KV_HEADS):
        s = [sinks_ref[hk * Q_GROUP + g] for g in range(Q_GROUP)]
        sink_cols.append(jnp.where(row_i < CHUNK, s[0],
                                   jnp.where(row_i < 2 * CHUNK, s[1],
                                             jnp.where(row_i < 3 * CHUNK, s[2], s[3]))))
    low_half = lax.broadcasted_iota(I32, (CHUNK, LANES), 1) < HEAD_DIM
    for c in range(n_chunks):
        outs = []
        for hk in range(N_KV_HEADS):
            kc = kext_ref[hk, key_stride * c:key_stride * c + KEYS, :]
            vc = vext_ref[hk, key_stride * c:key_stride * c + KEYS, :]
            heads = []
            for pair in range(hk * Q_GROUP // 2, (hk + 1) * Q_GROUP // 2):
                both = z_ref[CHUNK * c:CHUNK * (c + 1), pair * LANES:(pair + 1) * LANES] * (HEAD_DIM ** -0.5)
                heads += [jnp.where(low_half, both, 0.0), jnp.where(low_half, 0.0, both)]
            qs = jnp.concatenate(heads, axis=0).astype(BF16)
            sc = lax.dot_general(qs, kc, (((1,), (1,)), ((), ())), preferred_element_type=F32)
            if is_prompt and c < WINDOW // CHUNK:
                valid = (key_i >= WINDOW - CHUNK * c) | (tile_in_seq > 0)
                sc = jnp.where(valid, sc, NEG_INF)
            sink = sink_cols[hk]
            m = jnp.maximum(jnp.max(sc, axis=-1, keepdims=True), sink)
            p = jnp.exp(sc - m)
            den = jnp.sum(p, axis=-1, keepdims=True) + jnp.exp(sink - m)
            o = _dot(p.astype(BF16), vc) / den
            for j in range(Q_GROUP // 2):
                even, odd = o[CHUNK * 2 * j:CHUNK * (2 * j + 1), :], o[CHUNK * (2 * j + 1):CHUNK * (2 * j + 2), :]
                outs.append(jnp.where(low_half, even, odd))
        att_ref[CHUNK * c:CHUNK * (c + 1), :] = jnp.concatenate(outs, axis=1).astype(BF16)

    if is_prompt:
        kext_ref[:, 0:WINDOW, :] = kext_ref[:, tt:tt + WINDOW, :]
        vext_ref[:, 0:WINDOW, :] = vext_ref[:, tt:tt + WINDOW, :]

    gc = GMLP_CHUNK if is_prompt else CHUNK
    ri = lax.broadcasted_iota(I32, (gc, gc), 0)
    ci = lax.broadcasted_iota(I32, (gc, gc), 1)
    causal = (ci // CHUNK) <= (ri // CHUNK)
    for g in range(GMLP_GROUPS):
        lo, hi = g * GMLP_GROUP_DIM, (g + 1) * GMLP_GROUP_DIM
        u = _gelu_tanh(z_ref[:, U_OFF + lo:U_OFF + hi])
        gv = _layer_norm(_gelu_tanh(z_ref[:, GV_OFF + lo:GV_OFF + hi]), lng_ref[:, lo:hi], lnb_ref[:, lo:hi])
        if not is_prompt:
            gvo_ref[:, lo:hi] = gv
        gvb = gv.astype(BF16)
        wm = jnp.where(causal, ws_ref[g, 0:gc, 0:gc], 0.0).astype(BF16)
        bcol = bs_ref[0:gc, g:g + 1]
        for n in range(tt // gc):
            sp = _dot(wm, gvb[gc * n:gc * (n + 1), :]) + bcol
            gm_ref[gc * n:gc * (n + 1), lo:hi] = (u[gc * n:gc * (n + 1), :] * sp).astype(BF16)

    y = _dot(att_ref[...], woa_ref[...]) + _dot(gm_ref[...], wog_ref[...]) + bo_ref[...]
    h = _layer_norm(DN_ALPHA * x + y, l1g_ref[...], l1b_ref[...])
    h_ref[...] = h
    hw_ref[...] = pltpu.pack_elementwise([h[:, :PACK_WIDTH], h[:, PACK_WIDTH:]], packed_dtype=BF16)

    h_hi = h.astype(BF16)
    h_lo = (h - h_hi.astype(F32)).astype(BF16)
    logits = _dot(h_hi, wrh_ref[...]) + _dot(h_lo, wrh_ref[...]) + _dot(h_hi, wrl_ref[...]) + br_ref[...]
    lane = lax.broadcasted_iota(I32, (tt, LANES), 1)
    lane_f = lane.astype(F32)
    l = jnp.where(lane < N_EXPERTS, logits, -jnp.inf)
    tops, idxs, hots = [], [], []
    for _ in range(TOP_K):
        m = jnp.max(l, axis=-1, keepdims=True)
        idx = jnp.min(jnp.where(l == m, lane_f, float(LANES)), axis=-1, keepdims=True)
        hot = lane_f == idx
        l = jnp.where(hot, -jnp.inf, l)
        tops.append(m)
        idxs.append(idx)
        hots.append(hot)
    es = [jnp.exp(t - tops[0]) for t in tops]
    esum = es[0] + es[1] + es[2] + es[3]
    chosen = jnp.where(hots[0] | hots[1] | hots[2] | hots[3], 1.0, 0.0)
    before = _dot(tri_ref[...], chosen.astype(BF16)) + base_ref[...]
    meta = jnp.zeros((tt, LANES), F32)
    for kk in range(TOP_K):
        rank = jnp.sum(jnp.where(hots[kk], before, 0.0), axis=-1, keepdims=True)
        meta = jnp.where(lane == META_E + kk, idxs[kk], meta)
        meta = jnp.where(lane == META_R + kk, rank, meta)
        gate_ref[:, kk * LANES:(kk + 1) * LANES] = jnp.broadcast_to(es[kk] / esum, (tt, LANES))
    meta_ref[...] = jnp.transpose(meta)[0:2 * TOP_K, :]
    base_ref[...] = base_ref[...] + jnp.sum(chosen, axis=0, keepdims=True)
    cnt_ref[...] = base_ref[...]


def _const_spec(shape):
    nd = len(shape)
    return pl.BlockSpec(shape, lambda *_: (0,) * nd, pipeline_mode=pl.Buffered(1))


def _mixer_weight_specs(tt):
    return [
        _const_spec((D_MODEL, Z_WIDTH)), _const_spec((1, Z_WIDTH)),
        _const_spec((1, GMLP_WIDTH)), _const_spec((1, GMLP_WIDTH)),
        _const_spec((GMLP_GROUPS, GMLP_CHUNK, GMLP_CHUNK)), _const_spec((GMLP_CHUNK, GMLP_GROUPS)),
        _const_spec((ATTN_WIDTH, D_MODEL)), _const_spec((GMLP_WIDTH, D_MODEL)), _const_spec((1, D_MODEL)),
        _const_spec((1, D_MODEL)), _const_spec((1, D_MODEL)),
        _const_spec((D_MODEL, LANES)), _const_spec((D_MODEL, LANES)), _const_spec((1, LANES)),
        _const_spec((tt, tt)),
    ]


def _mixer_scratch(tt, kext_rows):
    return [
        pltpu.VMEM((tt, Z_WIDTH), F32),
        pltpu.VMEM((N_KV_HEADS, kext_rows, KV_WIDTH), BF16), pltpu.VMEM((N_KV_HEADS, kext_rows, KV_WIDTH), BF16),
        pltpu.VMEM((tt, ATTN_WIDTH), BF16), pltpu.VMEM((tt, GMLP_WIDTH), BF16),
        pltpu.VMEM((1, LANES), F32),
    ]


def _mix_prompt(sinks, x, weights, first_batch, batch):
    seq = x.shape[1]
    tt = MIX_TILE
    n_tiles = seq // tt
    tok = batch * seq
    smem = pl.BlockSpec(memory_space=pltpu.SMEM)
    return pl.pallas_call(
        functools.partial(_mixer_kernel, is_prompt=True, tt=tt),
        grid=(batch, n_tiles),
        in_specs=[smem, pl.BlockSpec((None, tt, D_MODEL), lambda b, i: (b + first_batch, i, 0))]
        + _mixer_weight_specs(tt),
        out_specs=[
            pl.BlockSpec((tt, D_MODEL), lambda b, i: (b * n_tiles + i, 0)),
            pl.BlockSpec((tt, PACK_WIDTH), lambda b, i: (b * n_tiles + i, 0)),
            pl.BlockSpec((2 * TOP_K, tt), lambda b, i: (0, b * n_tiles + i)),
            pl.BlockSpec((tt, TOP_K * LANES), lambda b, i: (b * n_tiles + i, 0)),
            pl.BlockSpec((1, LANES), lambda b, i: (0, 0)),
            pl.BlockSpec((WINDOW, KV_WIDTH), lambda b, i: (b, 0)),
            pl.BlockSpec((WINDOW, KV_WIDTH), lambda b, i: (b, 0)),
        ],
        out_shape=[
            jax.ShapeDtypeStruct((tok, D_MODEL), F32), jax.ShapeDtypeStruct((tok, PACK_WIDTH), jnp.uint32),
            jax.ShapeDtypeStruct((2 * TOP_K, tok), F32), jax.ShapeDtypeStruct((tok, TOP_K * LANES), F32),
            jax.ShapeDtypeStruct((1, LANES), F32),
            jax.ShapeDtypeStruct((batch * WINDOW, KV_WIDTH), F32),
            jax.ShapeDtypeStruct((batch * WINDOW, KV_WIDTH), F32),
        ],
        scratch_shapes=_mixer_scratch(tt, WINDOW + tt),
        compiler_params=pltpu.CompilerParams(
            dimension_semantics=("arbitrary", "arbitrary"), vmem_limit_bytes=VMEM_LIMIT),
        name="mix_prompt",
    )(sinks, x, *weights)


def _mix_sample(sinks, x2, ck, cv, base0, weights):
    tok = x2.shape[0]
    tt = MIX_TILE
    n_chunks = tt // CHUNK
    cache_rows = n_chunks * WINDOW
    smem = pl.BlockSpec(memory_space=pltpu.SMEM)
    row = lambda w: pl.BlockSpec((tt, w), lambda i: (i, 0))
    return pl.pallas_call(
        functools.partial(_mixer_kernel, is_prompt=False, tt=tt),
        grid=(tok // tt,),
        in_specs=[smem, row(D_MODEL),
                  pl.BlockSpec((cache_rows, KV_WIDTH), lambda i: (i, 0)),
                  pl.BlockSpec((cache_rows, KV_WIDTH), lambda i: (i, 0)),
                  _const_spec((1, LANES))] + _mixer_weight_specs(tt),
        out_specs=[row(D_MODEL), row(PACK_WIDTH), pl.BlockSpec((2 * TOP_K, tt), lambda i: (0, i)),
                   row(TOP_K * LANES),
                   pl.BlockSpec((1, LANES), lambda i: (0, 0)),
                   row(KV_WIDTH), row(KV_WIDTH), row(GMLP_WIDTH)],
        out_shape=[
            jax.ShapeDtypeStruct((tok, D_MODEL), F32), jax.ShapeDtypeStruct((tok, PACK_WIDTH), jnp.uint32),
            jax.ShapeDtypeStruct((2 * TOP_K, tok), F32), jax.ShapeDtypeStruct((tok, TOP_K * LANES), F32),
            jax.ShapeDtypeStruct((1, LANES), F32),
            jax.ShapeDtypeStruct((tok, KV_WIDTH), F32), jax.ShapeDtypeStruct((tok, KV_WIDTH), F32),
            jax.ShapeDtypeStruct((tok, GMLP_WIDTH), F32),
        ],
        scratch_shapes=_mixer_scratch(tt, n_chunks * KEYS),
        compiler_params=pltpu.CompilerParams(
            dimension_semantics=("arbitrary",), vmem_limit_bytes=VMEM_LIMIT),
        name="mix_sample",
    )(sinks, x2, ck, cv, base0, *weights)


def _sc_pipeline(body, n_tok, in_specs, out_specs):
    return pltpu.emit_pipeline(
        body, grid=(n_tok // SC_WINDOW,), in_specs=in_specs, out_specs=out_specs,
        core_axis_name=("core", "subcore"), dimension_semantics=(pltpu.PARALLEL,))


def _index_specs():
    return [pl.BlockSpec((1, SC_WINDOW), lambda i: (0, i))] * TOP_K


def _sc_dispatch(token_sets, n_rows):
    win, wid = SC_WINDOW, SLAB_WIDTH
    per_set = 2 + TOP_K
    n_in = per_set * len(token_sets)
    mesh = plsc.VectorSubcoreMesh(core_axis_name="core", subcore_axis_name="subcore")
    out_type = ([jax.ShapeDtypeStruct((n_rows, wid), jnp.uint32)] * PACK_SLABS
                + [jax.ShapeDtypeStruct((n_rows, LANES), F32)])

    @functools.partial(pl.kernel, out_type=out_type, mesh=mesh, scratch_types=[], name="sc_dispatch")
    def run(*refs):
        xs_hbm, gs_hbm = refs[n_in:n_in + PACK_SLABS], refs[n_in + PACK_SLABS]

        for s in range(len(token_sets)):
            h_hbm, g_hbm = refs[per_set * s], refs[per_set * s + 1]
            i_hbm = refs[per_set * s + 2:per_set * (s + 1)]
            n_tok = h_hbm.shape[0]
            for q in range(PACK_SLABS):
                def rows_body(x_vmem, *i_vmem, q=q):
                    for kk in range(TOP_K):
                        pltpu.sync_copy(x_vmem, xs_hbm[q].at[i_vmem[kk].at[0]])

                _sc_pipeline(rows_body, n_tok, [pl.BlockSpec((win, wid), lambda i, q=q: (i, q))] + _index_specs(),
                             [])(h_hbm, *i_hbm)
            for kk in range(TOP_K):
                def gate_body(g_vmem, i_vmem):
                    pltpu.sync_copy(g_vmem, gs_hbm.at[i_vmem.at[0]])

                _sc_pipeline(gate_body, n_tok,
                             [pl.BlockSpec((win, LANES), lambda i, kk=kk: (i, kk)), _index_specs()[0]],
                             [])(g_hbm, i_hbm[kk])

    outs = run(*[a for hw, gates, dests in token_sets for a in (hw, gates, *dests)])
    return outs[:PACK_SLABS], outs[PACK_SLABS]


def _expert_kernel(te_ref, na_ref, par_ref, nxt_ref, units_ref, *refs, out_slabs):
    x_refs, (gs_ref, wgu_hbm, bgu_ref, wd_hbm, bd_ref) = refs[:PACK_SLABS], refs[PACK_SLABS:PACK_SLABS + 5]
    y_refs = refs[PACK_SLABS + 5:PACK_SLABS + 5 + out_slabs]
    wgu_land, wd_land, wgu_bf, wd_bf, sem = refs[PACK_SLABS + 5 + out_slabs:]
    out_width = PACK_WIDTH // out_slabs
    t = pl.program_id(0)
    do_tile = t < na_ref[0]

    def weight_copies(expert, slot):
        return (pltpu.make_async_copy(wgu_hbm.at[0, expert], wgu_land.at[slot], sem.at[slot, 0]),
                pltpu.make_async_copy(wd_hbm.at[0, expert], wd_land.at[slot], sem.at[slot, 1]))

    @pl.when(do_tile & ((t == 0) | (te_ref[t] != te_ref[jnp.maximum(t - 1, 0)])))
    def _():
        slot = par_ref[t]

        @pl.when(t == 0)
        def _():
            for cp in weight_copies(te_ref[0], slot):
                cp.start()

        for cp in weight_copies(te_ref[t], slot):
            cp.wait()
        wgu_bf[...] = wgu_land[slot].astype(BF16)
        wd_bf[...] = wd_land[slot].astype(BF16)

        @pl.when(nxt_ref[t] >= 0)
        def _():
            for cp in weight_copies(nxt_ref[t], 1 - slot):
                cp.start()

    def compute(n_rows):
        words = [r[0:n_rows, :] for r in x_refs]
        x = jnp.concatenate(
            [pltpu.unpack_elementwise(w, index=index, packed_dtype=BF16, unpacked_dtype=F32)
             for index in range(2) for w in words], axis=1).astype(BF16)
        hmid = _dot(x, wgu_bf[...]) + bgu_ref[0]
        gate = jnp.minimum(hmid[:, :D_FF], SWIGLU_LIMIT)
        up = jnp.clip(hmid[:, D_FF:], -SWIGLU_LIMIT, SWIGLU_LIMIT)
        act = (up + 1.0) * gate * jax.nn.sigmoid(SWIGLU_ALPHA * gate)
        y = _dot(act.astype(BF16), wd_bf[...]) + bd_ref[0]
        y = y * jnp.concatenate([gs_ref[0:n_rows, :]] * (D_MODEL // LANES), axis=1)
        packed = pltpu.pack_elementwise([y[:, :PACK_WIDTH], y[:, PACK_WIDTH:]], packed_dtype=BF16)
        for q in range(out_slabs):
            y_refs[q][0:n_rows, :] = packed[:, q * out_width:(q + 1) * out_width]

    for units in range(1, ROW_TILE // ROW_UNIT + 1):
        @pl.when(do_tile & (units_ref[t] == units))
        def _(units=units):
            compute(units * ROW_UNIT)


def _experts(tile_e, n_active, units, xs, gs, wgu, bgu, wd, bd, out_width):
    tm = ROW_TILE
    out_slabs = PACK_WIDTH // out_width
    n_tiles = gs.shape[0] // tm
    tile = jnp.arange(n_tiles, dtype=I32)
    opens = jnp.concatenate([jnp.ones((1,), bool), tile_e[1:] != tile_e[:-1]]) & (tile < n_active[0])
    parity = ((jnp.cumsum(opens.astype(I32)) - 1) % 2).astype(I32)
    next_open = jnp.concatenate([lax.cummin(jnp.where(opens, tile, n_tiles)[::-1])[::-1][1:],
                                 jnp.full((1,), n_tiles, I32)])
    nxt = jnp.where(next_open < n_tiles, tile_e[jnp.minimum(next_open, n_tiles - 1)], -1).astype(I32)
    row_map = lambda i, te, na, *_: (jnp.minimum(i, na[0] - 1), 0)
    w_map = lambda i, te, na, *_: (te[jnp.minimum(i, na[0] - 1)], 0, 0)
    in_hbm = pl.BlockSpec(memory_space=pl.ANY)
    return pl.pallas_call(
        functools.partial(_expert_kernel, out_slabs=out_slabs),
        grid_spec=pltpu.PrefetchScalarGridSpec(
            num_scalar_prefetch=5,
            grid=(n_tiles,),
            in_specs=[pl.BlockSpec((tm, SLAB_WIDTH), row_map)] * PACK_SLABS + [
                pl.BlockSpec((tm, LANES), row_map),
                in_hbm,
                pl.BlockSpec((1, 1, 2 * D_FF), w_map),
                in_hbm,
                pl.BlockSpec((1, 1, D_MODEL), w_map),
            ],
            out_specs=[pl.BlockSpec((tm, out_width), row_map)] * out_slabs,
            scratch_shapes=[pltpu.VMEM((2, D_MODEL, 2 * D_FF), F32), pltpu.VMEM((2, D_FF, D_MODEL), F32),
                            pltpu.VMEM((D_MODEL, 2 * D_FF), BF16), pltpu.VMEM((D_FF, D_MODEL), BF16),
                            pltpu.SemaphoreType.DMA((2, 2))],
        ),
        out_shape=[jax.ShapeDtypeStruct((n_tiles * tm, out_width), jnp.uint32)] * out_slabs,
        compiler_params=pltpu.CompilerParams(
            dimension_semantics=("arbitrary",), vmem_limit_bytes=VMEM_LIMIT),
        name="experts",
    )(tile_e, n_active, parity, nxt, units, *xs, gs, wgu, bgu, wd, bd)


def _sc_combine(ys, dest_sets):
    n_sets = len(dest_sets)
    toks = [dests[0].shape[1] for dests in dest_sets]
    mesh = plsc.VectorSubcoreMesh(core_axis_name="core", subcore_axis_name="subcore")
    out_type = [jax.ShapeDtypeStruct((t, 2 * COMBINE_WIDTH), F32) for t in toks for _ in range(COMBINE_SLABS)]
    lanes = plsc.get_sparse_core_info().num_lanes
    high = jnp.uint32(0xFFFF0000)

    @functools.partial(pl.kernel, out_type=out_type, mesh=mesh, name="sc_combine",
                       scratch_types=[pltpu.VMEM((SC_WINDOW, COMBINE_WIDTH), jnp.uint32)] * 2,
                       compiler_params=pltpu.CompilerParams(needs_layout_passes=False))
    def run(*refs):
        ys_hbm = refs[:COMBINE_SLABS]
        out0 = COMBINE_SLABS + TOP_K * n_sets
        buf_a, buf_b = refs[-2:]

        def add_pair(ys_q, i_a, i_b, o_vmem, first):
            pltpu.sync_copy(ys_q.at[i_a.at[0]], buf_a)
            pltpu.sync_copy(ys_q.at[i_b.at[0]], buf_b)

            @pl.loop(0, SC_WINDOW)
            def _(r):
                for c in range(COMBINE_WIDTH // lanes):
                    cols = pl.ds(c * lanes, lanes)
                    wa, wb = buf_a[r, cols], buf_b[r, cols]
                    lo = plsc.bitcast(wa << 16, F32) + plsc.bitcast(wb << 16, F32)
                    hi = plsc.bitcast(wa & high, F32) + plsc.bitcast(wb & high, F32)
                    lo_cols, hi_cols = cols, pl.ds(COMBINE_WIDTH + c * lanes, lanes)
                    if first:
                        o_vmem[r, lo_cols] = lo
                        o_vmem[r, hi_cols] = hi
                    else:
                        o_vmem[r, lo_cols] = o_vmem[r, lo_cols] + lo
                        o_vmem[r, hi_cols] = o_vmem[r, hi_cols] + hi

        for s in range(n_sets):
            i_hbm = refs[COMBINE_SLABS + TOP_K * s:COMBINE_SLABS + TOP_K * (s + 1)]
            f_hbm = refs[out0 + COMBINE_SLABS * s:out0 + COMBINE_SLABS * (s + 1)]
            for q in range(COMBINE_SLABS):
                def body(i0, i1, i2, i3, o_vmem, q=q):
                    add_pair(ys_hbm[q], i0, i1, o_vmem, True)
                    add_pair(ys_hbm[q], i2, i3, o_vmem, False)

                _sc_pipeline(body, toks[s], _index_specs(),
                             [pl.BlockSpec((SC_WINDOW, 2 * COMBINE_WIDTH), lambda i: (i, 0))])(*i_hbm, f_hbm[q])

    outs = run(*ys, *[d for dests in dest_sets for d in dests])
    return [outs[COMBINE_SLABS * s:COMBINE_SLABS * (s + 1)] for s in range(n_sets)]


def _sc_gather(ys, dest_sets):
    n_out = TOP_K * PACK_SLABS
    n_sets = len(dest_sets)
    toks = [dests[0].shape[1] for dests in dest_sets]
    mesh = plsc.VectorSubcoreMesh(core_axis_name="core", subcore_axis_name="subcore")
    out_type = [jax.ShapeDtypeStruct((t, SLAB_WIDTH), jnp.uint32) for t in toks for _ in range(n_out)]

    @functools.partial(pl.kernel, out_type=out_type, mesh=mesh, scratch_types=[], name="sc_gather")
    def run(*refs):
        ys_hbm = refs[:PACK_SLABS]
        out0 = PACK_SLABS + TOP_K * n_sets

        for s in range(n_sets):
            i_hbm = refs[PACK_SLABS + TOP_K * s:PACK_SLABS + TOP_K * (s + 1)]
            f_hbm = refs[out0 + n_out * s:out0 + n_out * (s + 1)]
            for kk in range(TOP_K):
                for q in range(PACK_SLABS):
                    def body(i_vmem, o_vmem, q=q):
                        pltpu.sync_copy(ys_hbm[q].at[i_vmem.at[0]], o_vmem)

                    _sc_pipeline(body, toks[s], [_index_specs()[0]],
                                 [pl.BlockSpec((SC_WINDOW, SLAB_WIDTH), lambda i: (i, 0))]
                                 )(i_hbm[kk], f_hbm[kk * PACK_SLABS + q])

    outs = run(*ys, *[d for dests in dest_sets for d in dests])
    return [outs[n_out * s:n_out * (s + 1)] for s in range(n_sets)]


def _final_kernel(*refs, summed):
    n_in = COMBINE_SLABS if summed else TOP_K * PACK_SLABS
    h_ref, f_refs, g_ref, b_ref, out_ref = refs[0], refs[1:1 + n_in], refs[1 + n_in], refs[2 + n_in], refs[-1]
    if summed:
        f = jnp.concatenate([r[:, :COMBINE_WIDTH] for r in f_refs] + [r[:, COMBINE_WIDTH:] for r in f_refs], axis=1)
    else:
        halves = []
        for index in range(2):
            for q in range(PACK_SLABS):
                parts = [pltpu.unpack_elementwise(f_refs[kk * PACK_SLABS + q][...], index=index,
                                                  packed_dtype=BF16, unpacked_dtype=F32) for kk in range(TOP_K)]
                halves.append((parts[0] + parts[1]) + (parts[2] + parts[3]))
        f = jnp.concatenate(halves, axis=1)
    out_ref[...] = _layer_norm(DN_ALPHA * h_ref[...] + f, g_ref[...], b_ref[...])


def _final_norm(h, combined, summed, ln_g, ln_b, out_rows, first_row, earlier=None):
    tt = FINAL_TILE
    tok = h.shape[0]
    first_tile = first_row // tt
    row = lambda w: pl.BlockSpec((tt, w), lambda i: (i, 0))
    in_specs = ([row(D_MODEL)] + [row(c.shape[1]) for c in combined]
                + [_const_spec((1, D_MODEL)), _const_spec((1, D_MODEL))])
    args = [h, *combined, ln_g, ln_b]
    aliases = {}
    if earlier is not None:
        in_specs.append(pl.BlockSpec(memory_space=pl.ANY))
        aliases = {len(args): 0}
        args.append(earlier)
    return pl.pallas_call(
        functools.partial(_final_kernel, summed=summed),
        grid=(tok // tt,),
        in_specs=in_specs,
        out_specs=pl.BlockSpec((tt, D_MODEL), lambda i: (i + first_tile, 0)),
        out_shape=jax.ShapeDtypeStruct((out_rows, D_MODEL), F32),
        input_output_aliases=aliases,
        compiler_params=pltpu.CompilerParams(dimension_semantics=("arbitrary",), vmem_limit_bytes=VMEM_LIMIT),
        name="final_norm",
    )(*args)


def _mixer_weights(w_in, b_in, ln_g, ln_b, w_s, b_s, w_o, b_o, ln1_g, ln1_b, w_router, b_router):
    win = w_in.astype(BF16)
    bin_ = b_in[None, :]
    woa = w_o[:ATTN_WIDTH].astype(BF16)
    wog = w_o[ATTN_WIDTH:].astype(BF16)
    wr = jnp.pad(w_router, ((0, 0), (0, LANES - N_EXPERTS)))
    wrh = wr.astype(BF16)
    wrl = (wr - wrh.astype(F32)).astype(BF16)
    br = jnp.pad(b_router, (0, LANES - N_EXPERTS))[None, :]
    tri = (lax.broadcasted_iota(I32, (MIX_TILE, MIX_TILE), 1)
           < lax.broadcasted_iota(I32, (MIX_TILE, MIX_TILE), 0)).astype(BF16)
    return (win, bin_, ln_g.reshape(1, GMLP_WIDTH), ln_b.reshape(1, GMLP_WIDTH), w_s, b_s.T,
            woa, wog, b_o[None, :], ln1_g[None, :], ln1_b[None, :], wrh, wrl, br, tri)


def _dest_rows(meta, pstart):
    e = meta[META_E:META_E + TOP_K].astype(I32)
    r = meta[META_R:META_R + TOP_K].astype(I32)
    hit = e[None] == jnp.arange(N_EXPERTS, dtype=I32)[:, None, None]
    return jnp.sum(jnp.where(hit, pstart[:, None, None], 0), axis=0) + r


def _moe(token_sets, cnt, experts, sum_on_sc):
    n_assign = sum(hw.shape[0] for hw, _, _ in token_sets) * TOP_K
    n_tiles = (n_assign + N_EXPERTS * (ROW_TILE - 1)) // ROW_TILE
    counts = cnt[0, :N_EXPERTS].astype(I32)
    padded = (counts + ROW_TILE - 1) // ROW_TILE * ROW_TILE
    pend = jnp.cumsum(padded)
    pstart = pend - padded
    n_active = (pend[-1:] // ROW_TILE).astype(I32)
    tile_start = jnp.arange(n_tiles, dtype=I32) * ROW_TILE
    tile_e = jnp.minimum(jnp.sum(pend[None, :] <= tile_start[:, None], axis=1), N_EXPERTS - 1).astype(I32)
    hit = tile_e[:, None] == jnp.arange(N_EXPERTS, dtype=I32)[None, :]
    valid_end = jnp.sum(jnp.where(hit, pstart + counts, 0), axis=1)
    units = jnp.clip((valid_end - tile_start + ROW_UNIT - 1) // ROW_UNIT, 1, ROW_TILE // ROW_UNIT).astype(I32)

    lists = lambda d: [d[kk][None, :] for kk in range(TOP_K)]
    dest_sets = [lists(_dest_rows(meta, pstart)) for _, _, meta in token_sets]
    xs, gs = _sc_dispatch([(hw, gates, dests) for (hw, gates, _), dests in zip(token_sets, dest_sets)],
                          n_tiles * ROW_TILE)
    if sum_on_sc:
        return _sc_combine(_experts(tile_e, n_active, units, xs, gs, *experts, COMBINE_WIDTH), dest_sets)
    return _sc_gather(_experts(tile_e, n_active, units, xs, gs, *experts, SLAB_WIDTH), dest_sets)


def kernel(x_prompt, x_sample, cache_k, cache_v, w_in, b_in, attn_sinks, gmlp_ln_g, gmlp_ln_b, w_spatial, b_spatial, w_o, b_o, ln1_g, ln1_b, w_router, b_router, w_gate_up, b_gate_up, w_down, b_down, ln2_g, ln2_b):
    assert w_in.shape[0] == DEPTH
    batch, seq, _ = x_prompt.shape
    dec_batch, dec_seq, _ = x_sample.shape
    tok_p, tok_s = batch * seq, dec_batch * dec_seq
    assert dec_seq == CHUNK and seq % MIX_TILE == 0 and batch >= 2
    assert all(t % MIX_TILE == 0 and t % FINAL_TILE == 0 and t % SC_WINDOW == 0 for t in (tok_p, tok_s))

    weights = _mixer_weights(w_in[0], b_in[0], gmlp_ln_g[0], gmlp_ln_b[0], w_spatial[0], b_spatial[0],
                             w_o[0], b_o[0], ln1_g[0], ln1_b[0], w_router[0], b_router[0])
    sinks = attn_sinks[0]
    experts = (w_gate_up, b_gate_up[0][:, None, :], w_down, b_down[0][:, None, :])

    batch_a = batch // 2
    h_a, hw_a, meta_a, gates_a, cnt_a, kt_a, vt_a = _mix_prompt(sinks, x_prompt, weights, 0, batch_a)
    h_s, hw_s, meta_s, gates_s, cnt_as, ks, vs, gvs = _mix_sample(
        sinks, x_sample.reshape(tok_s, D_MODEL),
        cache_k[0].reshape(dec_batch * WINDOW, KV_WIDTH), cache_v[0].reshape(dec_batch * WINDOW, KV_WIDTH),
        cnt_a, weights)
    summed_a, summed_s = _moe([(hw_a, gates_a, meta_a), (hw_s, gates_s, meta_s)], cnt_as, experts, sum_on_sc=True)
    h_b, hw_b, meta_b, gates_b, cnt_b, kt_b, vt_b = _mix_prompt(sinks, x_prompt, weights, batch_a, batch - batch_a)
    picked_b, = _moe([(hw_b, gates_b, meta_b)], cnt_b, experts, sum_on_sc=False)

    g2, b2 = ln2_g[0][None, :], ln2_b[0][None, :]
    y_p = _final_norm(h_a, summed_a, True, g2, b2, tok_p, 0)
    y_p = _final_norm(h_b, picked_b, False, g2, b2, tok_p, batch_a * seq, earlier=y_p).reshape(batch, seq, D_MODEL)
    y_s = _final_norm(h_s, summed_s, True, g2, b2, tok_s, 0).reshape(dec_batch, dec_seq, D_MODEL)
    kt, vt = jnp.concatenate([kt_a, kt_b]), jnp.concatenate([vt_a, vt_b])

    kv5 = lambda a, nb, rows: a.reshape(DEPTH, nb, rows, N_KV_HEADS, HEAD_DIM)
    return (y_p, y_s, kv5(kt, batch, WINDOW), kv5(vt, batch, WINDOW),
            kv5(ks, dec_batch, dec_seq), kv5(vs, dec_batch, dec_seq),
            gvs.reshape(DEPTH, dec_batch, dec_seq, GMLP_GROUPS, GMLP_GROUP_DIM))
```

```python
import functools

import jax
import jax.numpy as jnp
from jax import lax
from jax.experimental import pallas as pl
from jax.experimental.pallas import tpu as pltpu
from jax.experimental.pallas import tpu_sc as plsc

F32 = jnp.float32
BF16 = jnp.bfloat16
I32 = jnp.int32

D_MODEL = 1024
CHUNK = 64
N_HEADS = 8
N_KV_HEADS = 2
HEAD_DIM = 64
Q_GROUP = N_HEADS // N_KV_HEADS
KV_WIDTH = N_KV_HEADS * HEAD_DIM
WINDOW = 128
KEYS = WINDOW + CHUNK
GMLP_GROUPS = 4
GMLP_GROUP_DIM = 128
GMLP_WIDTH = GMLP_GROUPS * GMLP_GROUP_DIM
GMLP_CHUNK = 128
ATTN_WIDTH = N_HEADS * HEAD_DIM
N_EXPERTS = 32
TOP_K = 4
D_FF = 1024
SWIGLU_LIMIT = 7.0
SWIGLU_ALPHA = 1.702
DEPTH = 1
DN_ALPHA = (2 * DEPTH) ** 0.25
LN_EPS = 1e-5
NEG_INF = -1e30

LANES = 128
K_OFF = ATTN_WIDTH
V_OFF = K_OFF + KV_WIDTH
U_OFF = V_OFF + KV_WIDTH
GV_OFF = U_OFF + GMLP_WIDTH
Z_WIDTH = GV_OFF + GMLP_WIDTH

MIX_TILE = 512
ATTN_SPAN = 2
ROW_TILE = 768
ROW_UNIT = 128
FINAL_TILE = 1024
SC_WINDOW = 128
SC_SLABS = 4
SLAB_WIDTH = D_MODEL // SC_SLABS
PACK_WIDTH = D_MODEL // 2
PACK_SLABS = PACK_WIDTH // SLAB_WIDTH
COMBINE_WIDTH = LANES
COMBINE_SLABS = PACK_WIDTH // COMBINE_WIDTH
VMEM_LIMIT = 58 * 1024 * 1024

META_E, META_R = 0, 4


def _gelu_tanh(x):
    return 0.5 * x * (1.0 + jnp.tanh(0.7978845608028654 * (x + 0.044715 * x * x * x)))


def _layer_norm(x, g, b):
    mu = jnp.mean(x, axis=-1, keepdims=True)
    xc = x - mu
    var = jnp.mean(xc * xc, axis=-1, keepdims=True)
    return xc * lax.rsqrt(var + LN_EPS) * g + b


def _dot(a, b):
    return jnp.dot(a, b, preferred_element_type=F32)


def _mixer_kernel(*refs, is_prompt, tt):
    n_chunks = tt // CHUNK
    it = iter(refs)
    sinks_ref = next(it)
    x_ref = next(it)
    if not is_prompt:
        ck_ref, cv_ref, base0_ref = next(it), next(it), next(it)
    (win_ref, bin_ref, lng_ref, lnb_ref, ws_ref, bs_ref, woa_ref, wog_ref, bo_ref, l1g_ref, l1b_ref,
     wrh_ref, wrl_ref, br_ref, tri_ref) = (next(it) for _ in range(15))
    h_ref, hw_ref, meta_ref, gate_ref, cnt_ref = (next(it) for _ in range(5))
    if is_prompt:
        kt_ref, vt_ref = next(it), next(it)
    else:
        ko_ref, vo_ref, gvo_ref = next(it), next(it), next(it)
    z_ref, kext_ref, vext_ref, att_ref, gm_ref, base_ref = (next(it) for _ in range(6))

    if is_prompt:
        first = (pl.program_id(0) == 0) & (pl.program_id(1) == 0)
        tile_in_seq = pl.program_id(1)

        @pl.when(first)
        def _():
            base_ref[...] = jnp.zeros_like(base_ref)

        @pl.when(tile_in_seq == 0)
        def _():
            kext_ref[:, 0:WINDOW, :] = jnp.zeros((N_KV_HEADS, WINDOW, KV_WIDTH), BF16)
            vext_ref[:, 0:WINDOW, :] = jnp.zeros((N_KV_HEADS, WINDOW, KV_WIDTH), BF16)
    else:
        @pl.when(pl.program_id(0) == 0)
        def _():
            base_ref[...] = base0_ref[...]

    x = x_ref[...]
    z_ref[...] = _dot(x.astype(BF16), win_ref[...]) + bin_ref[...]

    def put_kv(ext_ref, row0, rows_f32):
        n = rows_f32.shape[0]
        swapped = pltpu.roll(rows_f32, HEAD_DIM, axis=1)
        low = lax.broadcasted_iota(I32, (n, KV_WIDTH), 1) < HEAD_DIM
        ext_ref[0, row0:row0 + n, :] = jnp.where(low, rows_f32, swapped).astype(BF16)
        ext_ref[1, row0:row0 + n, :] = jnp.where(low, swapped, rows_f32).astype(BF16)

    k = z_ref[:, K_OFF:K_OFF + KV_WIDTH]
    v = z_ref[:, V_OFF:V_OFF + KV_WIDTH]
    if is_prompt:
        put_kv(kext_ref, WINDOW, k)
        put_kv(vext_ref, WINDOW, v)
        kt_ref[...] = k[tt - WINDOW:, :]
        vt_ref[...] = v[tt - WINDOW:, :]
        key_stride = CHUNK
    else:
        ko_ref[...] = k
        vo_ref[...] = v
        for c in range(n_chunks):
            put_kv(kext_ref, KEYS * c, ck_ref[WINDOW * c:WINDOW * (c + 1), :])
            put_kv(vext_ref, KEYS * c, cv_ref[WINDOW * c:WINDOW * (c + 1), :])
            put_kv(kext_ref, KEYS * c + WINDOW, k[CHUNK * c:CHUNK * (c + 1), :])
            put_kv(vext_ref, KEYS * c + WINDOW, v[CHUNK * c:CHUNK * (c + 1), :])
        key_stride = KEYS

    span = ATTN_SPAN if is_prompt else 1
    qb, kb = span * CHUNK, WINDOW + span * CHUNK
    rows = Q_GROUP * qb
    row_i = lax.broadcasted_iota(I32, (rows, 1), 0)
    key_i = lax.broadcasted_iota(I32, (rows, kb), 1)
    first_key = CHUNK * ((row_i % qb) // CHUNK)
    band = (key_i >= first_key) & (key_i < first_key + KEYS)
    sink_cols = []
    for hk in range(N_KV_HEADS):
        s = [sinks_ref[hk * Q_GROUP + g] for g in range(Q_GROUP)]
        sink_cols.append(jnp.where(row_i < qb, s[0],
                                   jnp.where(row_i < 2 * qb, s[1], jnp.where(row_i < 3 * qb, s[2], s[3]))))
    low_half = lax.broadcasted_iota(I32, (qb, LANES), 1) < HEAD_DIM
    for c in range(0, n_chunks, span):
        outs = []
        for hk in range(N_KV_HEADS):
            kc = kext_ref[hk, key_stride * c:key_stride * c + kb, :]
            vc = vext_ref[hk, key_stride * c:key_stride * c + kb, :]
            heads = []
            for pair in range(hk * Q_GROUP // 2, (hk + 1) * Q_GROUP // 2):
                both = z_ref[CHUNK * c:CHUNK * c + qb, pair * LANES:(pair + 1) * LANES] * (HEAD_DIM ** -0.5)
                heads += [jnp.where(low_half, both, 0.0), jnp.where(low_half, 0.0, both)]
            qs = jnp.concatenate(heads, axis=0).astype(BF16)
            sc = lax.dot_general(qs, kc, (((1,), (1,)), ((), ())), preferred_element_type=F32)
            valid = band
            if is_prompt and c < WINDOW // CHUNK:
                valid = valid & ((key_i >= WINDOW - CHUNK * c) | (tile_in_seq > 0))
            if span > 1 or (is_prompt and c < WINDOW // CHUNK):
                sc = jnp.where(valid, sc, NEG_INF)
            sink = sink_cols[hk]
            m = jnp.maximum(jnp.max(sc, axis=-1, keepdims=True), sink)
            p = jnp.exp(sc - m)
            den = jnp.sum(p, axis=-1, keepdims=True) + jnp.exp(sink - m)
            o = _dot(p.astype(BF16), vc) / den
            for j in range(Q_GROUP // 2):
                even, odd = o[qb * 2 * j:qb * (2 * j + 1), :], o[qb * (2 * j + 1):qb * (2 * j + 2), :]
                outs.append(jnp.where(low_half, even, odd))
        att_ref[CHUNK * c:CHUNK * c + qb, :] = jnp.concatenate(outs, axis=1).astype(BF16)

    if is_prompt:
        kext_ref[:, 0:WINDOW, :] = kext_ref[:, tt:tt + WINDOW, :]
        vext_ref[:, 0:WINDOW, :] = vext_ref[:, tt:tt + WINDOW, :]

    gc = GMLP_CHUNK if is_prompt else CHUNK
    ri = lax.broadcasted_iota(I32, (gc, gc), 0)
    ci = lax.broadcasted_iota(I32, (gc, gc), 1)
    causal = (ci // CHUNK) <= (ri // CHUNK)
    for g in range(GMLP_GROUPS):
        lo, hi = g * GMLP_GROUP_DIM, (g + 1) * GMLP_GROUP_DIM
        u = _gelu_tanh(z_ref[:, U_OFF + lo:U_OFF + hi])
        gv = _layer_norm(_gelu_tanh(z_ref[:, GV_OFF + lo:GV_OFF + hi]), lng_ref[:, lo:hi], lnb_ref[:, lo:hi])
        if not is_prompt:
            gvo_ref[:, lo:hi] = gv
        gvb = gv.astype(BF16)
        wm = jnp.where(causal, ws_ref[g, 0:gc, 0:gc], 0.0).astype(BF16)
        bcol = bs_ref[0:gc, g:g + 1]
        for n in range(tt // gc):
            sp = _dot(wm, gvb[gc * n:gc * (n + 1), :]) + bcol
            gm_ref[gc * n:gc * (n + 1), lo:hi] = (u[gc * n:gc * (n + 1), :] * sp).astype(BF16)

    y = _dot(att_ref[...], woa_ref[...]) + _dot(gm_ref[...], wog_ref[...]) + bo_ref[...]
    h = _layer_norm(DN_ALPHA * x + y, l1g_ref[...], l1b_ref[...])
    h_ref[...] = h
    hw_ref[...] = pltpu.pack_elementwise([h[:, :PACK_WIDTH], h[:, PACK_WIDTH:]], packed_dtype=BF16)

    h_hi = h.astype(BF16)
    h_lo = (h - h_hi.astype(F32)).astype(BF16)
    logits = _dot(h_hi, wrh_ref[...]) + _dot(h_lo, wrh_ref[...]) + _dot(h_hi, wrl_ref[...]) + br_ref[...]
    lane = lax.broadcasted_iota(I32, (tt, LANES), 1)
    lane_f = lane.astype(F32)
    l = jnp.where(lane < N_EXPERTS, logits, -jnp.inf)
    tops, idxs, hots = [], [], []
    for _ in range(TOP_K):
        m = jnp.max(l, axis=-1, keepdims=True)
        idx = jnp.min(jnp.where(l == m, lane_f, float(LANES)), axis=-1, keepdims=True)
        hot = lane_f == idx
        l = jnp.where(hot, -jnp.inf, l)
        tops.append(m)
        idxs.append(idx)
        hots.append(hot)
    es = [jnp.exp(t - tops[0]) for t in tops]
    esum = es[0] + es[1] + es[2] + es[3]
    chosen = jnp.where(hots[0] | hots[1] | hots[2] | hots[3], 1.0, 0.0)
    before = _dot(tri_ref[...], chosen.astype(BF16)) + base_ref[...]
    meta = jnp.zeros((tt, LANES), F32)
    for kk in range(TOP_K):
        rank = jnp.sum(jnp.where(hots[kk], before, 0.0), axis=-1, keepdims=True)
        meta = jnp.where(lane == META_E + kk, idxs[kk], meta)
        meta = jnp.where(lane == META_R + kk, rank, meta)
        gate_ref[:, kk * LANES:(kk + 1) * LANES] = jnp.broadcast_to(es[kk] / esum, (tt, LANES))
    meta_ref[...] = jnp.transpose(meta)[0:2 * TOP_K, :]
    base_ref[...] = base_ref[...] + jnp.sum(chosen, axis=0, keepdims=True)
    cnt_ref[...] = base_ref[...]


def _const_spec(shape):
    nd = len(shape)
    return pl.BlockSpec(shape, lambda *_: (0,) * nd, pipeline_mode=pl.Buffered(1))


def _mixer_weight_specs(tt):
    return [
        _const_spec((D_MODEL, Z_WIDTH)), _const_spec((1, Z_WIDTH)),
        _const_spec((1, GMLP_WIDTH)), _const_spec((1, GMLP_WIDTH)),
        _const_spec((GMLP_GROUPS, GMLP_CHUNK, GMLP_CHUNK)), _const_spec((GMLP_CHUNK, GMLP_GROUPS)),
        _const_spec((ATTN_WIDTH, D_MODEL)), _const_spec((GMLP_WIDTH, D_MODEL)), _const_spec((1, D_MODEL)),
        _const_spec((1, D_MODEL)), _const_spec((1, D_MODEL)),
        _const_spec((D_MODEL, LANES)), _const_spec((D_MODEL, LANES)), _const_spec((1, LANES)),
        _const_spec((tt, tt)),
    ]


def _mixer_scratch(tt, kext_rows):
    return [
        pltpu.VMEM((tt, Z_WIDTH), F32),
        pltpu.VMEM((N_KV_HEADS, kext_rows, KV_WIDTH), BF16), pltpu.VMEM((N_KV_HEADS, kext_rows, KV_WIDTH), BF16),
        pltpu.VMEM((tt, ATTN_WIDTH), BF16), pltpu.VMEM((tt, GMLP_WIDTH), BF16),
        pltpu.VMEM((1, LANES), F32),
    ]


def _mix_prompt(sinks, x, weights, first_batch, batch):
    seq = x.shape[1]
    tt = MIX_TILE
    n_tiles = seq // tt
    tok = batch * seq
    smem = pl.BlockSpec(memory_space=pltpu.SMEM)
    return pl.pallas_call(
        functools.partial(_mixer_kernel, is_prompt=True, tt=tt),
        grid=(batch, n_tiles),
        in_specs=[smem, pl.BlockSpec((None, tt, D_MODEL), lambda b, i: (b + first_batch, i, 0))]
        + _mixer_weight_specs(tt),
        out_specs=[
            pl.BlockSpec((tt, D_MODEL), lambda b, i: (b * n_tiles + i, 0)),
            pl.BlockSpec((tt, PACK_WIDTH), lambda b, i: (b * n_tiles + i, 0)),
            pl.BlockSpec((2 * TOP_K, tt), lambda b, i: (0, b * n_tiles + i)),
            pl.BlockSpec((tt, TOP_K * LANES), lambda b, i: (b * n_tiles + i, 0)),
            pl.BlockSpec((1, LANES), lambda b, i: (0, 0)),
            pl.BlockSpec((WINDOW, KV_WIDTH), lambda b, i: (b, 0)),
            pl.BlockSpec((WINDOW, KV_WIDTH), lambda b, i: (b, 0)),
        ],
        out_shape=[
            jax.ShapeDtypeStruct((tok, D_MODEL), F32), jax.ShapeDtypeStruct((tok, PACK_WIDTH), jnp.uint32),
            jax.ShapeDtypeStruct((2 * TOP_K, tok), F32), jax.ShapeDtypeStruct((tok, TOP_K * LANES), F32),
            jax.ShapeDtypeStruct((1, LANES), F32),
            jax.ShapeDtypeStruct((batch * WINDOW, KV_WIDTH), F32),
            jax.ShapeDtypeStruct((batch * WINDOW, KV_WIDTH), F32),
        ],
        scratch_shapes=_mixer_scratch(tt, WINDOW + tt),
        compiler_params=pltpu.CompilerParams(
            dimension_semantics=("arbitrary", "arbitrary"), vmem_limit_bytes=VMEM_LIMIT),
        name="mix_prompt",
    )(sinks, x, *weights)


def _mix_sample(sinks, x2, ck, cv, base0, weights):
    tok = x2.shape[0]
    tt = MIX_TILE
    n_chunks = tt // CHUNK
    cache_rows = n_chunks * WINDOW
    smem = pl.BlockSpec(memory_space=pltpu.SMEM)
    row = lambda w: pl.BlockSpec((tt, w), lambda i: (i, 0))
    return pl.pallas_call(
        functools.partial(_mixer_kernel, is_prompt=False, tt=tt),
        grid=(tok // tt,),
        in_specs=[smem, row(D_MODEL),
                  pl.BlockSpec((cache_rows, KV_WIDTH), lambda i: (i, 0)),
                  pl.BlockSpec((cache_rows, KV_WIDTH), lambda i: (i, 0)),
                  _const_spec((1, LANES))] + _mixer_weight_specs(tt),
        out_specs=[row(D_MODEL), row(PACK_WIDTH), pl.BlockSpec((2 * TOP_K, tt), lambda i: (0, i)),
                   row(TOP_K * LANES),
                   pl.BlockSpec((1, LANES), lambda i: (0, 0)),
                   row(KV_WIDTH), row(KV_WIDTH), row(GMLP_WIDTH)],
        out_shape=[
            jax.ShapeDtypeStruct((tok, D_MODEL), F32), jax.ShapeDtypeStruct((tok, PACK_WIDTH), jnp.uint32),
            jax.ShapeDtypeStruct((2 * TOP_K, tok), F32), jax.ShapeDtypeStruct((tok, TOP_K * LANES), F32),
            jax.ShapeDtypeStruct((1, LANES), F32),
            jax.ShapeDtypeStruct((tok, KV_WIDTH), F32), jax.ShapeDtypeStruct((tok, KV_WIDTH), F32),
            jax.ShapeDtypeStruct((tok, GMLP_WIDTH), F32),
        ],
        scratch_shapes=_mixer_scratch(tt, n_chunks * KEYS),
        compiler_params=pltpu.CompilerParams(
            dimension_semantics=("arbitrary",), vmem_limit_bytes=VMEM_LIMIT),
        name="mix_sample",
    )(sinks, x2, ck, cv, base0, *weights)


def _sc_pipeline(body, n_tok, in_specs, out_specs):
    return pltpu.emit_pipeline(
        body, grid=(n_tok // SC_WINDOW,), in_specs=in_specs, out_specs=out_specs,
        core_axis_name=("core", "subcore"), dimension_semantics=(pltpu.PARALLEL,))


def _index_specs():
    return [pl.BlockSpec((1, SC_WINDOW), lambda i: (0, i))] * TOP_K


def _sc_dispatch(token_sets, n_rows):
    win, wid = SC_WINDOW, SLAB_WIDTH
    per_set = 2 + TOP_K
    n_in = per_set * len(token_sets)
    mesh = plsc.VectorSubcoreMesh(core_axis_name="core", subcore_axis_name="subcore")
    out_type = ([jax.ShapeDtypeStruct((n_rows, wid), jnp.uint32)] * PACK_SLABS
                + [jax.ShapeDtypeStruct((n_rows, LANES), F32)])

    @functools.partial(pl.kernel, out_type=out_type, mesh=mesh, scratch_types=[], name="sc_dispatch")
    def run(*refs):
        xs_hbm, gs_hbm = refs[n_in:n_in + PACK_SLABS], refs[n_in + PACK_SLABS]

        for s in range(len(token_sets)):
            h_hbm, g_hbm = refs[per_set * s], refs[per_set * s + 1]
            i_hbm = refs[per_set * s + 2:per_set * (s + 1)]
            n_tok = h_hbm.shape[0]
            for q in range(PACK_SLABS):
                def rows_body(x_vmem, *i_vmem, q=q):
                    for kk in range(TOP_K):
                        pltpu.sync_copy(x_vmem, xs_hbm[q].at[i_vmem[kk].at[0]])

                _sc_pipeline(rows_body, n_tok, [pl.BlockSpec((win, wid), lambda i, q=q: (i, q))] + _index_specs(),
                             [])(h_hbm, *i_hbm)
            for kk in range(TOP_K):
                def gate_body(g_vmem, i_vmem):
                    pltpu.sync_copy(g_vmem, gs_hbm.at[i_vmem.at[0]])

                _sc_pipeline(gate_body, n_tok,
                             [pl.BlockSpec((win, LANES), lambda i, kk=kk: (i, kk)), _index_specs()[0]],
                             [])(g_hbm, i_hbm[kk])

    outs = run(*[a for hw, gates, dests in token_sets for a in (hw, gates, *dests)])
    return outs[:PACK_SLABS], outs[PACK_SLABS]


def _expert_kernel(te_ref, na_ref, par_ref, nxt_ref, units_ref, *refs, out_slabs):
    x_refs, (gs_ref, wgu_hbm, bgu_ref, wd_hbm, bd_ref) = refs[:PACK_SLABS], refs[PACK_SLABS:PACK_SLABS + 5]
    y_refs = refs[PACK_SLABS + 5:PACK_SLABS + 5 + out_slabs]
    wgu_land, wd_land, wgu_bf, wd_bf, sem = refs[PACK_SLABS + 5 + out_slabs:]
    out_width = PACK_WIDTH // out_slabs
    t = pl.program_id(0)
    do_tile = t < na_ref[0]

    def weight_copies(expert, slot):
        return (pltpu.make_async_copy(wgu_hbm.at[0, expert], wgu_land.at[slot], sem.at[slot, 0]),
                pltpu.make_async_copy(wd_hbm.at[0, expert], wd_land.at[slot], sem.at[slot, 1]))

    @pl.when(do_tile & ((t == 0) | (te_ref[t] != te_ref[jnp.maximum(t - 1, 0)])))
    def _():
        slot = par_ref[t]

        @pl.when(t == 0)
        def _():
            for cp in weight_copies(te_ref[0], slot):
                cp.start()

        for cp in weight_copies(te_ref[t], slot):
            cp.wait()
        wgu_bf[...] = wgu_land[slot].astype(BF16)
        wd_bf[...] = wd_land[slot].astype(BF16)

        @pl.when(nxt_ref[t] >= 0)
        def _():
            for cp in weight_copies(nxt_ref[t], 1 - slot):
                cp.start()

    def compute(n_rows):
        words = [r[0:n_rows, :] for r in x_refs]
        x = jnp.concatenate(
            [pltpu.unpack_elementwise(w, index=index, packed_dtype=BF16, unpacked_dtype=F32)
             for index in range(2) for w in words], axis=1).astype(BF16)
        hmid = _dot(x, wgu_bf[...]) + bgu_ref[0]
        gate = jnp.minimum(hmid[:, :D_FF], SWIGLU_LIMIT)
        up = jnp.clip(hmid[:, D_FF:], -SWIGLU_LIMIT, SWIGLU_LIMIT)
        act = (up + 1.0) * gate * jax.nn.sigmoid(SWIGLU_ALPHA * gate)
        y = _dot(act.astype(BF16), wd_bf[...]) + bd_ref[0]
        y = y * jnp.concatenate([gs_ref[0:n_rows, :]] * (D_MODEL // LANES), axis=1)
        packed = pltpu.pack_elementwise([y[:, :PACK_WIDTH], y[:, PACK_WIDTH:]], packed_dtype=BF16)
        for q in range(out_slabs):
            y_refs[q][0:n_rows, :] = packed[:, q * out_width:(q + 1) * out_width]

    for units in range(1, ROW_TILE // ROW_UNIT + 1):
        @pl.when(do_tile & (units_ref[t] == units))
        def _(units=units):
            compute(units * ROW_UNIT)


def _experts(tile_e, n_active, units, xs, gs, wgu, bgu, wd, bd, out_width):
    tm = ROW_TILE
    out_slabs = PACK_WIDTH // out_width
    n_tiles = gs.shape[0] // tm
    tile = jnp.arange(n_tiles, dtype=I32)
    opens = jnp.concatenate([jnp.ones((1,), bool), tile_e[1:] != tile_e[:-1]]) & (tile < n_active[0])
    parity = ((jnp.cumsum(opens.astype(I32)) - 1) % 2).astype(I32)
    next_open = jnp.concatenate([lax.cummin(jnp.where(opens, tile, n_tiles)[::-1])[::-1][1:],
                                 jnp.full((1,), n_tiles, I32)])
    nxt = jnp.where(next_open < n_tiles, tile_e[jnp.minimum(next_open, n_tiles - 1)], -1).astype(I32)
    row_map = lambda i, te, na, *_: (jnp.minimum(i, na[0] - 1), 0)
    w_map = lambda i, te, na, *_: (te[jnp.minimum(i, na[0] - 1)], 0, 0)
    in_hbm = pl.BlockSpec(memory_space=pl.ANY)
    return pl.pallas_call(
        functools.partial(_expert_kernel, out_slabs=out_slabs),
        grid_spec=pltpu.PrefetchScalarGridSpec(
            num_scalar_prefetch=5,
            grid=(n_tiles,),
            in_specs=[pl.BlockSpec((tm, SLAB_WIDTH), row_map)] * PACK_SLABS + [
                pl.BlockSpec((tm, LANES), row_map),
                in_hbm,
                pl.BlockSpec((1, 1, 2 * D_FF), w_map),
                in_hbm,
                pl.BlockSpec((1, 1, D_MODEL), w_map),
            ],
            out_specs=[pl.BlockSpec((tm, out_width), row_map)] * out_slabs,
            scratch_shapes=[pltpu.VMEM((2, D_MODEL, 2 * D_FF), F32), pltpu.VMEM((2, D_FF, D_MODEL), F32),
                            pltpu.VMEM((D_MODEL, 2 * D_FF), BF16), pltpu.VMEM((D_FF, D_MODEL), BF16),
                            pltpu.SemaphoreType.DMA((2, 2))],
        ),
        out_shape=[jax.ShapeDtypeStruct((n_tiles * tm, out_width), jnp.uint32)] * out_slabs,
        compiler_params=pltpu.CompilerParams(
            dimension_semantics=("arbitrary",), vmem_limit_bytes=VMEM_LIMIT),
        name="experts",
    )(tile_e, n_active, parity, nxt, units, *xs, gs, wgu, bgu, wd, bd)


def _sc_combine(ys, dest_sets):
    n_sets = len(dest_sets)
    toks = [dests[0].shape[1] for dests in dest_sets]
    mesh = plsc.VectorSubcoreMesh(core_axis_name="core", subcore_axis_name="subcore")
    out_type = [jax.ShapeDtypeStruct((t, 2 * COMBINE_WIDTH), F32) for t in toks for _ in range(COMBINE_SLABS)]
    lanes = plsc.get_sparse_core_info().num_lanes
    high = jnp.uint32(0xFFFF0000)

    @functools.partial(pl.kernel, out_type=out_type, mesh=mesh, name="sc_combine",
                       scratch_types=[pltpu.VMEM((SC_WINDOW, COMBINE_WIDTH), jnp.uint32)] * 2,
                       compiler_params=pltpu.CompilerParams(needs_layout_passes=False))
    def run(*refs):
        ys_hbm = refs[:COMBINE_SLABS]
        out0 = COMBINE_SLABS + TOP_K * n_sets
        buf_a, buf_b = refs[-2:]

        def add_pair(ys_q, i_a, i_b, o_vmem, first):
            pltpu.sync_copy(ys_q.at[i_a.at[0]], buf_a)
            pltpu.sync_copy(ys_q.at[i_b.at[0]], buf_b)

            @pl.loop(0, SC_WINDOW)
            def _(r):
                for c in range(COMBINE_WIDTH // lanes):
                    cols = pl.ds(c * lanes, lanes)
                    wa, wb = buf_a[r, cols], buf_b[r, cols]
                    lo = plsc.bitcast(wa << 16, F32) + plsc.bitcast(wb << 16, F32)
                    hi = plsc.bitcast(wa & high, F32) + plsc.bitcast(wb & high, F32)
                    lo_cols, hi_cols = cols, pl.ds(COMBINE_WIDTH + c * lanes, lanes)
                    if first:
                        o_vmem[r, lo_cols] = lo
                        o_vmem[r, hi_cols] = hi
                    else:
                        o_vmem[r, lo_cols] = o_vmem[r, lo_cols] + lo
                        o_vmem[r, hi_cols] = o_vmem[r, hi_cols] + hi

        for s in range(n_sets):
            i_hbm = refs[COMBINE_SLABS + TOP_K * s:COMBINE_SLABS + TOP_K * (s + 1)]
            f_hbm = refs[out0 + COMBINE_SLABS * s:out0 + COMBINE_SLABS * (s + 1)]
            for q in range(COMBINE_SLABS):
                def body(i0, i1, i2, i3, o_vmem, q=q):
                    add_pair(ys_hbm[q], i0, i1, o_vmem, True)
                    add_pair(ys_hbm[q], i2, i3, o_vmem, False)

                _sc_pipeline(body, toks[s], _index_specs(),
                             [pl.BlockSpec((SC_WINDOW, 2 * COMBINE_WIDTH), lambda i: (i, 0))])(*i_hbm, f_hbm[q])

    outs = run(*ys, *[d for dests in dest_sets for d in dests])
    return [outs[COMBINE_SLABS * s:COMBINE_SLABS * (s + 1)] for s in range(n_sets)]


def _sc_gather(ys, dest_sets):
    n_out = TOP_K * PACK_SLABS
    n_sets = len(dest_sets)
    toks = [dests[0].shape[1] for dests in dest_sets]
    mesh = plsc.VectorSubcoreMesh(core_axis_name="core", subcore_axis_name="subcore")
    out_type = [jax.ShapeDtypeStruct((t, SLAB_WIDTH), jnp.uint32) for t in toks for _ in range(n_out)]

    @functools.partial(pl.kernel, out_type=out_type, mesh=mesh, scratch_types=[], name="sc_gather")
    def run(*refs):
        ys_hbm = refs[:PACK_SLABS]
        out0 = PACK_SLABS + TOP_K * n_sets

        for s in range(n_sets):
            i_hbm = refs[PACK_SLABS + TOP_K * s:PACK_SLABS + TOP_K * (s + 1)]
            f_hbm = refs[out0 + n_out * s:out0 + n_out * (s + 1)]
            for kk in range(TOP_K):
                for q in range(PACK_SLABS):
                    def body(i_vmem, o_vmem, q=q):
                        pltpu.sync_copy(ys_hbm[q].at[i_vmem.at[0]], o_vmem)

                    _sc_pipeline(body, toks[s], [_index_specs()[0]],
                                 [pl.BlockSpec((SC_WINDOW, SLAB_WIDTH), lambda i: (i, 0))]
                                 )(i_hbm[kk], f_hbm[kk * PACK_SLABS + q])

    outs = run(*ys, *[d for dests in dest_sets for d in dests])
    return [outs[n_out * s:n_out * (s + 1)] for s in range(n_sets)]


def _final_kernel(*refs, summed):
    n_in = COMBINE_SLABS if summed else TOP_K * PACK_SLABS
    h_ref, f_refs, g_ref, b_ref, out_ref = refs[0], refs[1:1 + n_in], refs[1 + n_in], refs[2 + n_in], refs[-1]
    if summed:
        f = jnp.concatenate([r[:, :COMBINE_WIDTH] for r in f_refs] + [r[:, COMBINE_WIDTH:] for r in f_refs], axis=1)
    else:
        halves = []
        for index in range(2):
            for q in range(PACK_SLABS):
                parts = [pltpu.unpack_elementwise(f_refs[kk * PACK_SLABS + q][...], index=index,
                                                  packed_dtype=BF16, unpacked_dtype=F32) for kk in range(TOP_K)]
                halves.append((parts[0] + parts[1]) + (parts[2] + parts[3]))
        f = jnp.concatenate(halves, axis=1)
    out_ref[...] = _layer_norm(DN_ALPHA * h_ref[...] + f, g_ref[...], b_ref[...])


def _final_norm(h, combined, summed, ln_g, ln_b, out_rows, first_row, earlier=None):
    tt = FINAL_TILE
    tok = h.shape[0]
    first_tile = first_row // tt
    row = lambda w: pl.BlockSpec((tt, w), lambda i: (i, 0))
    in_specs = ([row(D_MODEL)] + [row(c.shape[1]) for c in combined]
                + [_const_spec((1, D_MODEL)), _const_spec((1, D_MODEL))])
    args = [h, *combined, ln_g, ln_b]
    aliases = {}
    if earlier is not None:
        in_specs.append(pl.BlockSpec(memory_space=pl.ANY))
        aliases = {len(args): 0}
        args.append(earlier)
    return pl.pallas_call(
        functools.partial(_final_kernel, summed=summed),
        grid=(tok // tt,),
        in_specs=in_specs,
        out_specs=pl.BlockSpec((tt, D_MODEL), lambda i: (i + first_tile, 0)),
        out_shape=jax.ShapeDtypeStruct((out_rows, D_MODEL), F32),
        input_output_aliases=aliases,
        compiler_params=pltpu.CompilerParams(dimension_semantics=("arbitrary",), vmem_limit_bytes=VMEM_LIMIT),
        name="final_norm",
    )(*args)


def _mixer_weights(w_in, b_in, ln_g, ln_b, w_s, b_s, w_o, b_o, ln1_g, ln1_b, w_router, b_router):
    win = w_in.astype(BF16)
    bin_ = b_in[None, :]
    woa = w_o[:ATTN_WIDTH].astype(BF16)
    wog = w_o[ATTN_WIDTH:].astype(BF16)
    wr = jnp.pad(w_router, ((0, 0), (0, LANES - N_EXPERTS)))
    wrh = wr.astype(BF16)
    wrl = (wr - wrh.astype(F32)).astype(BF16)
    br = jnp.pad(b_router, (0, LANES - N_EXPERTS))[None, :]
    tri = (lax.broadcasted_iota(I32, (MIX_TILE, MIX_TILE), 1)
           < lax.broadcasted_iota(I32, (MIX_TILE, MIX_TILE), 0)).astype(BF16)
    return (win, bin_, ln_g.reshape(1, GMLP_WIDTH), ln_b.reshape(1, GMLP_WIDTH), w_s, b_s.T,
            woa, wog, b_o[None, :], ln1_g[None, :], ln1_b[None, :], wrh, wrl, br, tri)


def _dest_rows(meta, pstart):
    e = meta[META_E:META_E + TOP_K].astype(I32)
    r = meta[META_R:META_R + TOP_K].astype(I32)
    hit = e[None] == jnp.arange(N_EXPERTS, dtype=I32)[:, None, None]
    return jnp.sum(jnp.where(hit, pstart[:, None, None], 0), axis=0) + r


def _moe(token_sets, cnt, experts, sum_on_sc):
    n_assign = sum(hw.shape[0] for hw, _, _ in token_sets) * TOP_K
    n_tiles = (n_assign + N_EXPERTS * (ROW_TILE - 1)) // ROW_TILE
    counts = cnt[0, :N_EXPERTS].astype(I32)
    padded = (counts + ROW_TILE - 1) // ROW_TILE * ROW_TILE
    pend = jnp.cumsum(padded)
    pstart = pend - padded
    n_active = (pend[-1:] // ROW_TILE).astype(I32)
    tile_start = jnp.arange(n_tiles, dtype=I32) * ROW_TILE
    tile_e = jnp.minimum(jnp.sum(pend[None, :] <= tile_start[:, None], axis=1), N_EXPERTS - 1).astype(I32)
    hit = tile_e[:, None] == jnp.arange(N_EXPERTS, dtype=I32)[None, :]
    valid_end = jnp.sum(jnp.where(hit, pstart + counts, 0), axis=1)
    units = jnp.clip((valid_end - tile_start + ROW_UNIT - 1) // ROW_UNIT, 1, ROW_TILE // ROW_UNIT).astype(I32)

    lists = lambda d: [d[kk][None, :] for kk in range(TOP_K)]
    dest_sets = [lists(_dest_rows(meta, pstart)) for _, _, meta in token_sets]
    xs, gs = _sc_dispatch([(hw, gates, dests) for (hw, gates, _), dests in zip(token_sets, dest_sets)],
                          n_tiles * ROW_TILE)
    if sum_on_sc:
        return _sc_combine(_experts(tile_e, n_active, units, xs, gs, *experts, COMBINE_WIDTH), dest_sets)
    return _sc_gather(_experts(tile_e, n_active, units, xs, gs, *experts, SLAB_WIDTH), dest_sets)


def kernel(x_prompt, x_sample, cache_k, cache_v, w_in, b_in, attn_sinks, gmlp_ln_g, gmlp_ln_b, w_spatial, b_spatial, w_o, b_o, ln1_g, ln1_b, w_router, b_router, w_gate_up, b_gate_up, w_down, b_down, ln2_g, ln2_b):
    assert w_in.shape[0] == DEPTH
    batch, seq, _ = x_prompt.shape
    dec_batch, dec_seq, _ = x_sample.shape
    tok_p, tok_s = batch * seq, dec_batch * dec_seq
    assert dec_seq == CHUNK and seq % MIX_TILE == 0 and batch >= 2
    assert all(t % MIX_TILE == 0 and t % FINAL_TILE == 0 and t % SC_WINDOW == 0 for t in (tok_p, tok_s))

    weights = _mixer_weights(w_in[0], b_in[0], gmlp_ln_g[0], gmlp_ln_b[0], w_spatial[0], b_spatial[0],
                             w_o[0], b_o[0], ln1_g[0], ln1_b[0], w_router[0], b_router[0])
    sinks = attn_sinks[0]
    experts = (w_gate_up, b_gate_up[0][:, None, :], w_down, b_down[0][:, None, :])

    batch_a = batch // 2
    h_a, hw_a, meta_a, gates_a, cnt_a, kt_a, vt_a = _mix_prompt(sinks, x_prompt, weights, 0, batch_a)
    h_s, hw_s, meta_s, gates_s, cnt_as, ks, vs, gvs = _mix_sample(
        sinks, x_sample.reshape(tok_s, D_MODEL),
        cache_k[0].reshape(dec_batch * WINDOW, KV_WIDTH), cache_v[0].reshape(dec_batch * WINDOW, KV_WIDTH),
        cnt_a, weights)
    summed_a, summed_s = _moe([(hw_a, gates_a, meta_a), (hw_s, gates_s, meta_s)], cnt_as, experts, sum_on_sc=True)
    h_b, hw_b, meta_b, gates_b, cnt_b, kt_b, vt_b = _mix_prompt(sinks, x_prompt, weights, batch_a, batch - batch_a)
    picked_b, = _moe([(hw_b, gates_b, meta_b)], cnt_b, experts, sum_on_sc=False)

    g2, b2 = ln2_g[0][None, :], ln2_b[0][None, :]
    y_p = _final_norm(h_a, summed_a, True, g2, b2, tok_p, 0)
    y_p = _final_norm(h_b, picked_b, False, g2, b2, tok_p, batch_a * seq, earlier=y_p).reshape(batch, seq, D_MODEL)
    y_s = _final_norm(h_s, summed_s, True, g2, b2, tok_s, 0).reshape(dec_batch, dec_seq, D_MODEL)
    kt, vt = jnp.concatenate([kt_a, kt_b]), jnp.concatenate([vt_a, vt_b])

    kv5 = lambda a, nb, rows: a.reshape(DEPTH, nb, rows, N_KV_HEADS, HEAD_DIM)
    return (y_p, y_s, kv5(kt, batch, WINDOW), kv5(vt, batch, WINDOW),
            kv5(ks, dec_batch, dec_seq), kv5(vs, dec_batch, dec_seq),
            gvs.reshape(DEPTH, dec_batch, dec_seq, GMLP_GROUPS, GMLP_GROUP_DIM))
```
